```python
import functools
import jax, jax.numpy as jnp
from jax import lax
import numpy as np

D_MODEL = 1024
BATCH = 2
SEQ = 8192
DEPTH = 2
DEC_BATCH = 32
DEC_SEQ = 1
PAST_LEN = 8192
PAGE_SIZE = 128

BRANCH_W = 1024
N_BRANCH = 3
LRU_W = BRANCH_W
LRU_BLOCKS = 16
LRU_BD = LRU_W // LRU_BLOCKS
CONV_W = 4
LRU_C = 8.0
POOL_W = BRANCH_W
POOL_WINDOWS = (2, 4, 8, 16)
POOL_G = len(POOL_WINDOWS)
POOL_GD = POOL_W // POOL_G
POOL_BUF = max(POOL_WINDOWS) - 1
N_HEADS = 16
HEAD_DIM = 64
NSA_W = N_HEADS * HEAD_DIM
KV_HEADS = 4
Q_PER_KV = N_HEADS // KV_HEADS
KV_W = KV_HEADS * HEAD_DIM
CMP_STRIDE = 16
CMP_BLK = 2 * CMP_STRIDE
SEL_BLK = 64
N_SEL = 16
WINDOW = 512
Q_BLK = 128
FORCE = 1e4
NEG = -1e30
EPS = 1e-6

SPLIT_SIZES = (LRU_W, LRU_W, POOL_W, POOL_W, NSA_W, NSA_W, 6 * KV_W, N_BRANCH * N_HEADS, N_BRANCH * D_MODEL)
IN_W = sum(SPLIT_SIZES)
SPLIT_IDX = tuple(sum(SPLIT_SIZES[:i + 1]) for i in range(len(SPLIT_SIZES) - 1))

kernel_name = 'hybrid_rglru_pool_nsa_decoder_step'

f32 = jnp.float32


def _rmsnorm(x, g):
    xf = x.astype(f32)
    y = xf * lax.rsqrt(jnp.mean(xf * xf, axis=-1, keepdims=True) + EPS) * g.astype(f32)
    return y.astype(x.dtype)


def _alibi_slopes():
    s = 2.0 ** (-8.0 * np.arange(1, N_HEADS + 1) / N_HEADS)
    return jnp.asarray(s, dtype=f32).reshape(KV_HEADS, Q_PER_KV)


def _rglru(xb, conv_buf, h0, conv_w, conv_b, w_a, b_a, w_x, b_x, lam):
    B, T, W = xb.shape
    xp = jnp.concatenate([conv_buf.astype(xb.dtype), xb], axis=1)
    xc = conv_b
    for k in range(CONV_W):
        xc = xc + xp[:, k:k + T] * conv_w[k]
    xf = xc.astype(f32)
    xh = xf.reshape(B, T, LRU_BLOCKS, LRU_BD)
    r = jax.nn.sigmoid(jnp.einsum('btnd,nde->btne', xh, w_a.astype(f32)).reshape(B, T, W) + b_a.astype(f32))
    i = jax.nn.sigmoid(jnp.einsum('btnd,nde->btne', xh, w_x.astype(f32)).reshape(B, T, W) + b_x.astype(f32))
    log_a = -LRU_C * r * jax.nn.softplus(-lam.astype(f32))
    a = jnp.exp(log_a)
    b = jnp.sqrt(-jnp.expm1(2.0 * log_a)) * (i * xf)

    def step(h, ab):
        h = ab[0] * h + ab[1]
        return h, h

    hT, hs = lax.scan(step, h0.astype(f32), (a.swapaxes(0, 1), b.swapaxes(0, 1)))
    return hs.swapaxes(0, 1).astype(xb.dtype), xp[:, -(CONV_W - 1):], hT.astype(xb.dtype)


def _pool(xb, buf, pos0, w_pool, scale):
    B, T, W = xb.shape
    xcat = jnp.concatenate([buf.astype(xb.dtype), xb], axis=1)
    xf = xcat.astype(f32)
    cs = jnp.pad(jnp.cumsum(xf, axis=1), ((0, 0), (1, 0), (0, 0)))
    pos = pos0 + jnp.arange(T)
    outs = []
    for g, w in enumerate(POOL_WINDOWS):
        sl = slice(g * POOL_GD, (g + 1) * POOL_GD)
        wsum = cs[:, POOL_BUF + 1:POOL_BUF + 1 + T, sl] - cs[:, POOL_BUF + 1 - w:POOL_BUF + 1 - w + T, sl]
        cnt = jnp.minimum(w, pos + 1).astype(f32)[None, :, None]
        outs.append(wsum / cnt)
    pooled = jnp.concatenate(outs, axis=-1) - xf[:, POOL_BUF:]
    mixed = jnp.einsum('btgd,gde->btge', pooled.reshape(B, T, POOL_G, POOL_GD), w_pool.astype(f32)).reshape(B, T, W)
    y = (mixed * scale.astype(f32)).astype(xb.dtype)
    return y, xcat[:, -POOL_BUF:]


def _compress(rows, w_pos, w_phi):
    B, L = rows.shape[:2]
    nch = L // CMP_STRIDE
    ch = rows[:, :nch * CMP_STRIDE].reshape(B, nch, CMP_STRIDE, KV_HEADS, HEAD_DIM)
    first = jnp.einsum('bcjgd,jd->bcgd', ch, w_pos[:CMP_STRIDE])
    second = jnp.einsum('bcjgd,jd->bcgd', ch, w_pos[CMP_STRIDE:])
    blk = first[:, :-1] + second[:, 1:]
    return jnp.einsum('bngd,de->bnge', blk, w_phi)


def _nsa_attend(q, q_pos, kc, vc, c_end, gather_sel, n_blocks, kw, vw, w_pos, gates):
    B, T = q.shape[:2]
    slopes = _alibi_slopes()
    qg = q.astype(f32).reshape(B, T, KV_HEADS, Q_PER_KV, HEAD_DIM) * (HEAD_DIM ** -0.5)
    dist_c = q_pos[:, None] - c_end[None, :]
    ok_c = (dist_c >= 0)[:, None, None, :]
    s_c = jnp.einsum('btgrd,bngd->btgrn', qg, kc.astype(f32)) - slopes[:, :, None] * dist_c[:, None, None, :].astype(f32)
    p_c = jnp.where(ok_c, jax.nn.softmax(jnp.where(ok_c, s_c, NEG), axis=-1), 0.0)
    o_c = jnp.einsum('btgrn,bngd->btgrd', p_c, vc.astype(f32))
    per = SEL_BLK // CMP_STRIDE
    ps = p_c.sum(axis=3)
    ps = jnp.pad(ps, ((0, 0), (0, 0), (0, 0), (0, per * n_blocks - ps.shape[-1])))
    ps = ps.reshape(B, T, KV_HEADS, n_blocks, per).sum(-1)
    j = jnp.arange(n_blocks)[None, :]
    jb = (q_pos // SEL_BLK)[:, None]
    ok_b = (j * SEL_BLK) <= q_pos[:, None]
    forced = (j == 0) | (j == jb) | (j == jb - 1)
    score = jnp.where(ok_b[:, None, :], ps + FORCE * forced[:, None, :].astype(f32), NEG)
    _, idx = lax.top_k(score, min(N_SEL, n_blocks))
    ks, vs = gather_sel(idx)
    kpos = idx[..., None] * SEL_BLK + jnp.arange(SEL_BLK)
    dist_s = (q_pos[None, :, None, None, None] - kpos)[:, :, :, None]
    s_s = jnp.einsum('btgrd,btgnkd->btgrnk', qg, ks.astype(f32)) - slopes[None, None, :, :, None, None] * dist_s.astype(f32)
    s_s = jnp.where(dist_s >= 0, s_s, NEG)
    p_s = jax.nn.softmax(s_s.reshape(B, T, KV_HEADS, Q_PER_KV, -1), axis=-1).reshape(s_s.shape)
    o_s = jnp.einsum('btgrnk,btgnkd->btgrd', p_s, vs.astype(f32))
    dist_w = q_pos[:, None] - w_pos[None, :]
    ok_w = ((dist_w >= 0) & (dist_w < WINDOW) & (w_pos >= 0)[None, :])[:, None, None, :]
    s_w = jnp.einsum('btgrd,blgd->btgrl', qg, kw.astype(f32)) - slopes[:, :, None] * dist_w[:, None, None, :].astype(f32)
    p_w = jax.nn.softmax(jnp.where(ok_w, s_w, NEG), axis=-1)
    o_w = jnp.einsum('btgrl,blgd->btgrd', p_w, vw.astype(f32))
    o = jnp.stack([o_c, o_s, o_w], axis=-1).reshape(B, T, N_HEADS, HEAD_DIM, N_BRANCH)
    o = (o * gates.astype(f32)[:, :, :, None, :]).sum(-1)
    return o.reshape(B, T, NSA_W).astype(q.dtype)


def _nsa_prompt(q, k_c, v_c, k_s, v_s, k_w, v_w, gates, pk, fk, pv, fv):
    B, S = q.shape[:2]
    kc = _compress(k_c, pk, fk)
    vc = _compress(v_c, pv, fv)
    c_end = jnp.arange(kc.shape[1]) * CMP_STRIDE + CMP_BLK - 1
    nb = S // SEL_BLK
    ksb = k_s.reshape(B, nb, SEL_BLK, KV_HEADS, HEAD_DIM)
    vsb = v_s.reshape(B, nb, SEL_BLK, KV_HEADS, HEAD_DIM)
    bi = jnp.arange(B)[:, None, None, None]
    gi = jnp.arange(KV_HEADS)[None, None, :, None]

    def gather(idx):
        return ksb[bi, idx, :, gi], vsb[bi, idx, :, gi]

    pad = ((0, 0), (WINDOW, 0), (0, 0), (0, 0))
    kwp, vwp = jnp.pad(k_w, pad), jnp.pad(v_w, pad)
    nqb = S // Q_BLK
    qb = q.reshape(B, nqb, Q_BLK, N_HEADS, HEAD_DIM).swapaxes(0, 1)
    gb = gates.reshape(B, nqb, Q_BLK, N_HEADS, N_BRANCH).swapaxes(0, 1)

    def one(args):
        jq, qj, gj = args
        s0 = jq * Q_BLK
        kw = lax.dynamic_slice_in_dim(kwp, s0, WINDOW + Q_BLK, axis=1)
        vw = lax.dynamic_slice_in_dim(vwp, s0, WINDOW + Q_BLK, axis=1)
        w_pos = s0 - WINDOW + jnp.arange(WINDOW + Q_BLK)
        return _nsa_attend(qj, s0 + jnp.arange(Q_BLK), kc, vc, c_end, gather, nb, kw, vw, w_pos, gj)

    y = lax.map(one, (jnp.arange(nqb), qb, gb))
    y = y.swapaxes(0, 1).reshape(B, S, NSA_W)
    wb = min(WINDOW, S)
    return y, (k_c, v_c, k_s, v_s, k_w[:, -wb:], v_w[:, -wb:])


def _nsa_sample(q, k_c, v_c, k_s, v_s, k_w, v_w, gates, pk, fk, pv, fv,
                pool_ck, pool_cv, pool_sk, pool_sv, buf_k, buf_v, page_table):
    B, T = q.shape[:2]
    P = page_table.shape[1] * PAGE_SIZE

    def past_rows(pool):
        return pool[page_table].reshape(B, P, KV_HEADS, HEAD_DIM)

    kc = _compress(jnp.concatenate([past_rows(pool_ck), k_c.astype(pool_ck.dtype)], axis=1), pk, fk)
    vc = _compress(jnp.concatenate([past_rows(pool_cv), v_c.astype(pool_cv.dtype)], axis=1), pv, fv)
    c_end = jnp.arange(kc.shape[1]) * CMP_STRIDE + CMP_BLK - 1
    L = P + T
    nb = -(-L // SEL_BLK)
    npb = P // SEL_BLK
    nnew = nb - npb
    subs = PAGE_SIZE // SEL_BLK

    def new_blocks(rows):
        rows = jnp.pad(rows, ((0, 0), (0, nnew * SEL_BLK - T), (0, 0), (0, 0)))
        return rows.reshape(B, nnew, SEL_BLK, KV_HEADS, HEAD_DIM)

    ksn, vsn = new_blocks(k_s), new_blocks(v_s)
    ksp = pool_sk.reshape(pool_sk.shape[0], subs, SEL_BLK, KV_HEADS, HEAD_DIM)
    vsp = pool_sv.reshape(pool_sv.shape[0], subs, SEL_BLK, KV_HEADS, HEAD_DIM)
    bi = jnp.arange(B)[:, None, None, None]
    gi = jnp.arange(KV_HEADS)[None, None, :, None]

    def gather(idx):
        past = (idx < npb)[..., None, None]
        ip = jnp.minimum(idx, npb - 1)
        phys = page_table[bi, ip // subs]
        sub = ip % subs
        inew = jnp.clip(idx - npb, 0, nnew - 1)
        k = jnp.where(past, ksp[phys, sub, :, gi].astype(k_s.dtype), ksn[bi, inew, :, gi])
        v = jnp.where(past, vsp[phys, sub, :, gi].astype(v_s.dtype), vsn[bi, inew, :, gi])
        return k, v

    kw = jnp.concatenate([buf_k.astype(k_w.dtype), k_w], axis=1)
    vw = jnp.concatenate([buf_v.astype(v_w.dtype), v_w], axis=1)
    wb = buf_k.shape[1]
    w_pos = P - wb + jnp.arange(wb + T)
    y = _nsa_attend(q, P + jnp.arange(T), kc, vc, c_end, gather, nb, kw, vw, w_pos, gates)
    return y, (k_c, v_c, k_s, v_s, kw[:, -wb:], vw[:, -wb:])


def _layer(x, pos0, conv_buf, h0, pool_buf, nsa_fn, lw):
    (g_pre, g_post, w_in, conv_w, conv_b, w_a, b_a, w_x, b_x, lam, w_pool, pool_scale,
     pk, fk, pv, fv, w_branch, w_out) = lw
    B, T = x.shape[:2]
    u = _rmsnorm(x, g_pre)
    z = jnp.einsum('btd,de->bte', u, w_in)
    lru_x, lru_g, pool_x, pool_g, q, nsa_g, kv, bg, mg = jnp.split(z, SPLIT_IDX, axis=-1)
    y_lru, conv_new, h_new = _rglru(lru_x, conv_buf, h0, conv_w, conv_b, w_a, b_a, w_x, b_x, lam)
    y_pool, pool_new = _pool(pool_x, pool_buf, pos0, w_pool, pool_scale)
    k_c, v_c, k_s, v_s, k_w, v_w = [a.reshape(B, T, KV_HEADS, HEAD_DIM) for a in jnp.split(kv, 6, axis=-1)]
    gates = jax.nn.sigmoid(bg.reshape(B, T, N_HEADS, N_BRANCH))
    y_nsa, nsa_state = nsa_fn(q.reshape(B, T, N_HEADS, HEAD_DIM), k_c, v_c, k_s, v_s, k_w, v_w, gates, pk, fk, pv, fv)
    zb = jnp.stack([y_lru * jax.nn.silu(lru_g), y_pool * jax.nn.silu(pool_g), y_nsa * jax.nn.silu(nsa_g)], axis=2)
    br = jnp.einsum('btnw,nwd->btnd', zb, w_branch)
    m = jax.nn.sigmoid(mg.reshape(B, T, N_BRANCH, D_MODEL))
    out = jnp.einsum('btd,de->bte', (m * br).sum(axis=2), w_out)
    return x + _rmsnorm(out, g_post), nsa_state + (conv_new, h_new, pool_new)


def setup_inputs(seed: int = 0) -> dict:
    key = jax.random.key(seed)
    ks = jax.random.split(key, 32)
    n_pages = PAST_LEN // PAGE_SIZE
    used = DEC_BATCH * n_pages
    n_phys = used + max(1, used // 4)
    win_buf = min(WINDOW, PAST_LEN)

    def nrm(k, shape, s=1.0):
        return s * jax.random.normal(k, shape, f32)

    u = jax.random.uniform(ks[0], (DEPTH, LRU_W), f32, minval=0.9, maxval=0.999)
    sg = u ** (1.0 / LRU_C)
    lam = jnp.log(sg) - jnp.log1p(-sg)
    page_table = jax.random.permutation(ks[1], n_phys)[:used].reshape(DEC_BATCH, n_pages).astype(jnp.int32)
    pool_shape = (DEPTH, n_phys, PAGE_SIZE, KV_HEADS, HEAD_DIM)
    win_shape = (DEPTH, DEC_BATCH, win_buf, KV_HEADS, HEAD_DIM)
    return {
        'x_prompt': nrm(ks[2], (BATCH, SEQ, D_MODEL)),
        'x_sample': nrm(ks[3], (DEC_BATCH, DEC_SEQ, D_MODEL)),
        'cache_cmp_k': nrm(ks[4], pool_shape),
        'cache_cmp_v': nrm(ks[5], pool_shape),
        'cache_sel_k': nrm(ks[6], pool_shape),
        'cache_sel_v': nrm(ks[7], pool_shape),
        'cache_win_k': nrm(ks[8], win_shape),
        'cache_win_v': nrm(ks[9], win_shape),
        'state_conv': nrm(ks[10], (DEPTH, DEC_BATCH, CONV_W - 1, LRU_W)),
        'state_lru': nrm(ks[11], (DEPTH, DEC_BATCH, LRU_W), 0.3),
        'state_pool': nrm(ks[12], (DEPTH, DEC_BATCH, POOL_BUF, POOL_W)),
        'page_table': page_table,
        'g_pre': 1.0 + nrm(ks[13], (DEPTH, D_MODEL), 0.05),
        'g_post': 1.0 + nrm(ks[14], (DEPTH, D_MODEL), 0.05),
        'w_in': nrm(ks[15], (DEPTH, D_MODEL, IN_W), D_MODEL ** -0.5),
        'conv_w': nrm(ks[16], (DEPTH, CONV_W, LRU_W), CONV_W ** -0.5),
        'conv_b': nrm(ks[17], (DEPTH, LRU_W), 0.01),
        'w_rg_a': nrm(ks[18], (DEPTH, LRU_BLOCKS, LRU_BD, LRU_BD), LRU_BD ** -0.5),
        'b_rg_a': nrm(ks[19], (DEPTH, LRU_W), 0.1),
        'w_rg_x': nrm(ks[20], (DEPTH, LRU_BLOCKS, LRU_BD, LRU_BD), LRU_BD ** -0.5),
        'b_rg_x': nrm(ks[21], (DEPTH, LRU_W), 0.1),
        'lru_lambda': lam,
        'w_pool': nrm(ks[22], (DEPTH, POOL_G, POOL_GD, POOL_GD), POOL_GD ** -0.5),
        'pool_scale': 1.0 + nrm(ks[23], (DEPTH, POOL_W), 0.1),
        'cmp_pos_k': (1.0 + nrm(ks[24], (DEPTH, CMP_BLK, HEAD_DIM), 0.1)) * CMP_BLK ** -0.5,
        'cmp_phi_k': nrm(ks[25], (DEPTH, HEAD_DIM, HEAD_DIM), HEAD_DIM ** -0.5),
        'cmp_pos_v': (1.0 + nrm(ks[26], (DEPTH, CMP_BLK, HEAD_DIM), 0.1)) * CMP_BLK ** -0.5,
        'cmp_phi_v': nrm(ks[27], (DEPTH, HEAD_DIM, HEAD_DIM), HEAD_DIM ** -0.5),
        'w_branch': nrm(ks[28], (DEPTH, N_BRANCH, BRANCH_W, D_MODEL), BRANCH_W ** -0.5),
        'w_out': nrm(ks[29], (DEPTH, D_MODEL, D_MODEL), D_MODEL ** -0.5),
    }


def reference(x_prompt, x_sample, cache_cmp_k, cache_cmp_v, cache_sel_k, cache_sel_v, cache_win_k, cache_win_v,
              state_conv, state_lru, state_pool, page_table, g_pre, g_post, w_in, conv_w, conv_b,
              w_rg_a, b_rg_a, w_rg_x, b_rg_x, lru_lambda, w_pool, pool_scale,
              cmp_pos_k, cmp_phi_k, cmp_pos_v, cmp_phi_v, w_branch, w_out):
    past_len = page_table.shape[1] * PAGE_SIZE
    Bp = x_prompt.shape[0]
    dt = x_prompt.dtype
    xp, xs = x_prompt, x_sample
    pr = [[] for _ in range(9)]
    sm = [[] for _ in range(9)]
    for l in range(DEPTH):
        lw = (g_pre[l], g_post[l], w_in[l], conv_w[l], conv_b[l], w_rg_a[l], b_rg_a[l], w_rg_x[l], b_rg_x[l],
              lru_lambda[l], w_pool[l], pool_scale[l], cmp_pos_k[l], cmp_phi_k[l], cmp_pos_v[l], cmp_phi_v[l],
              w_branch[l], w_out[l])
        xp, st_p = _layer(xp, 0, jnp.zeros((Bp, CONV_W - 1, LRU_W), dt), jnp.zeros((Bp, LRU_W), dt),
                          jnp.zeros((Bp, POOL_BUF, POOL_W), dt), _nsa_prompt, lw)
        nsa_s = functools.partial(_nsa_sample, pool_ck=cache_cmp_k[l], pool_cv=cache_cmp_v[l],
                                  pool_sk=cache_sel_k[l], pool_sv=cache_sel_v[l],
                                  buf_k=cache_win_k[l], buf_v=cache_win_v[l], page_table=page_table)
        xs, st_s = _layer(xs, past_len, state_conv[l], state_lru[l], state_pool[l], nsa_s, lw)
        for i in range(9):
            pr[i].append(st_p[i])
            sm[i].append(st_s[i])
    return (xp, xs,
            jnp.stack(pr[0]), jnp.stack(sm[0]), jnp.stack(pr[1]), jnp.stack(sm[1]),
            jnp.stack(pr[2]), jnp.stack(sm[2]), jnp.stack(pr[3]), jnp.stack(sm[3]),
            jnp.stack(pr[4]), jnp.stack(sm[4]), jnp.stack(pr[5]), jnp.stack(sm[5]),
            jnp.stack(pr[6]), jnp.stack(sm[6]), jnp.stack(pr[7]), jnp.stack(sm[7]),
            jnp.stack(pr[8]), jnp.stack(sm[8]))
```

```python
import functools

import numpy as np
import jax
import jax.numpy as jnp
from jax import lax
from jax.experimental import pallas as pl
from jax.experimental.pallas import tpu as pltpu

f32 = jnp.float32
bf16 = jnp.bfloat16

D_MODEL = 1024
BRANCH_W = 1024
N_BRANCH = 3
LRU_BLOCKS = 16
LRU_BD = BRANCH_W // LRU_BLOCKS
CONV_W = 4
LRU_C = 8.0
POOL_WINDOWS = (2, 4, 8, 16)
POOL_GD = BRANCH_W // len(POOL_WINDOWS)
POOL_BUF = max(POOL_WINDOWS) - 1
N_HEADS = 16
HEAD_DIM = 64
KV_HEADS = 4
Q_PER_KV = N_HEADS // KV_HEADS
KV_W = KV_HEADS * HEAD_DIM
CMP_STRIDE = 16
CMP_BLK = 2 * CMP_STRIDE
SEL_BLK = 64
N_SEL = 16
WINDOW = 512
PAGE_SIZE = 128
FORCE = 1e4
NEG = -1e30
EPS = 1e-6

C_LRU_X, C_LRU_G, C_POOL_X, C_POOL_G, C_Q, C_NSA_G, C_MG = 0, 1024, 2048, 3072, 4096, 5120, 6144
C_KV = C_MG + N_BRANCH * D_MODEL
C_BG = C_KV + 6 * KV_W
BG_W = 128
IN_WP = 11264
IN_TN = 1024

MXU_W = 256
VMEM_LIMIT = 56 * 1024 * 1024

N_CMP = 512
N_BLK = 128
TQ = 128
TK = 512

SLOPES = [float(np.float32(2.0 ** (-8.0 * (h + 1) / N_HEADS))) for h in range(N_HEADS)]


def _nt(a, b):
    return lax.dot_general(a, b, (((1,), (1,)), ((), ())), preferred_element_type=f32)


def _dot(a, b):
    return jnp.dot(a, b, preferred_element_type=f32)


def _silu(x):
    return x * jax.nn.sigmoid(x)


def _params(sem):
    return pltpu.CompilerParams(dimension_semantics=sem, vmem_limit_bytes=VMEM_LIMIT)


def _inproj_kernel(x_ref, g_ref, w_ref, o_ref, u_ref):
    @pl.when(pl.program_id(1) == 0)
    def _():
        x = x_ref[...]
        ms = jnp.mean(x * x, axis=-1, keepdims=True)
        u_ref[...] = (x * lax.rsqrt(ms + EPS) * g_ref[...]).astype(bf16)

    o_ref[...] = _dot(u_ref[...], w_ref[...])


def _inproj(x2d, g_row, w_packed):
    n = x2d.shape[0]
    tm = min(n, 1024)
    return pl.pallas_call(
        _inproj_kernel,
        out_shape=jax.ShapeDtypeStruct((n, IN_WP), f32),
        grid=(n // tm, IN_WP // IN_TN),
        in_specs=[pl.BlockSpec((tm, D_MODEL), lambda i, j: (i, 0)),
                  pl.BlockSpec((1, D_MODEL), lambda i, j: (0, 0)),
                  pl.BlockSpec((D_MODEL, IN_TN), lambda i, j: (0, j))],
        out_specs=pl.BlockSpec((tm, IN_TN), lambda i, j: (i, j)),
        scratch_shapes=[pltpu.VMEM((tm, D_MODEL), bf16)],
        compiler_params=_params(("parallel", "arbitrary")),
        name="inproj",
    )(x2d, g_row, w_packed)


def _lru_gates(xc, wa_ref, ba_ref, wx_ref, bx_ref, lam_ref):
    xb = xc.astype(bf16)
    ra, ri = [], []
    for c in range(BRANCH_W // MXU_W):
        sl = slice(c * MXU_W, (c + 1) * MXU_W)
        ra.append(_dot(xb[:, sl], wa_ref[c]))
        ri.append(_dot(xb[:, sl], wx_ref[c]))
    r = jax.nn.sigmoid(jnp.concatenate(ra, axis=-1) + ba_ref[...])
    i = jax.nn.sigmoid(jnp.concatenate(ri, axis=-1) + bx_ref[...])
    nl = -lam_ref[...]
    softplus = jnp.maximum(nl, 0.0) + jnp.log1p(jnp.exp(-jnp.abs(nl)))
    log_a = -LRU_C * r * softplus
    a = jnp.exp(log_a)
    b = jnp.sqrt(1.0 - a * a) * (i * xc)
    return a, b


def _lru_kernel(x_ref, g_ref, cw_ref, cb_ref, wa_ref, ba_ref, wx_ref, bx_ref, lam_ref,
                zb_ref, tail_ref, h_ref, xs_ref, a_ref, b_ref, hc_ref):
    tt = x_ref.shape[1]

    @pl.when(pl.program_id(1) == 0)
    def _():
        xs_ref[0:8, :] = jnp.zeros((8, BRANCH_W), f32)
        hc_ref[...] = jnp.zeros_like(hc_ref)

    x = x_ref[0]
    xs_ref[8:, :] = x
    xc = cb_ref[...] + x * cw_ref[CONV_W - 1:CONV_W, :]
    for k in range(CONV_W - 1):
        xc = xc + xs_ref[pl.ds(8 - (CONV_W - 1 - k), tt), :] * cw_ref[k:k + 1, :]
    xs_ref[0:8, :] = x[tt - 8:, :]
    tail_ref[0] = x[tt - 8:, :]

    a, b = _lru_gates(xc, wa_ref, ba_ref, wx_ref, bx_ref, lam_ref)
    a_ref[...] = a
    b_ref[...] = b
    row = lax.broadcasted_iota(jnp.int32, (8, BRANCH_W), 0)

    def body(i, h):
        r0 = pl.multiple_of(i * 8, 8)
        av = a_ref[pl.ds(r0, 8), :]
        bv = b_ref[pl.ds(r0, 8), :]
        for d in (1, 2, 4):
            a_s = jnp.where(row >= d, pltpu.roll(av, d, 0), 1.0)
            b_s = jnp.where(row >= d, pltpu.roll(bv, d, 0), 0.0)
            bv = av * b_s + bv
            av = av * a_s
        hs = bv + av * h
        b_ref[pl.ds(r0, 8), :] = hs
        return hs[7:8, :]

    h = lax.fori_loop(0, tt // 8, body, hc_ref[...])
    hc_ref[...] = h
    h_ref[0] = h
    zb_ref[0] = (b_ref[...] * _silu(g_ref[0])).astype(bf16)


def _lru_prompt(z3, cw, cb, wa, ba, wx, bx, lam):
    b, s, _ = z3.shape
    tt = min(s, 512)
    row = lambda: pl.BlockSpec((1, BRANCH_W), lambda i, t: (0, 0))
    bd = lambda: pl.BlockSpec((BRANCH_W // MXU_W, MXU_W, MXU_W), lambda i, t: (0, 0, 0))
    return pl.pallas_call(
        _lru_kernel,
        out_shape=(jax.ShapeDtypeStruct((b, s, BRANCH_W), bf16),
                   jax.ShapeDtypeStruct((b, 8, BRANCH_W), f32),
                   jax.ShapeDtypeStruct((b, 1, BRANCH_W), f32)),
        grid=(b, s // tt),
        in_specs=[pl.BlockSpec((1, tt, BRANCH_W), lambda i, t: (i, t, C_LRU_X // BRANCH_W)),
                  pl.BlockSpec((1, tt, BRANCH_W), lambda i, t: (i, t, C_LRU_G // BRANCH_W)),
                  pl.BlockSpec((CONV_W, BRANCH_W), lambda i, t: (0, 0)),
                  row(), bd(), row(), bd(), row(), row()],
        out_specs=(pl.BlockSpec((1, tt, BRANCH_W), lambda i, t: (i, t, 0)),
                   pl.BlockSpec((1, 8, BRANCH_W), lambda i, t: (i, 0, 0)),
                   pl.BlockSpec((1, 1, BRANCH_W), lambda i, t: (i, 0, 0))),
        scratch_shapes=[pltpu.VMEM((tt + 8, BRANCH_W), f32), pltpu.VMEM((tt, BRANCH_W), f32),
                        pltpu.VMEM((tt, BRANCH_W), f32), pltpu.VMEM((1, BRANCH_W), f32)],
        compiler_params=_params(("parallel", "arbitrary")),
        name="lru_prompt",
    )(z3, z3, cw, cb, wa, ba, wx, bx, lam)


def _pool_kernel(x_ref, g_ref, wp_ref, sc_ref, zb_ref, tail_ref, xs_ref):
    tt = x_ref.shape[1]
    t = pl.program_id(1)

    @pl.when(t == 0)
    def _():
        xs_ref[0:16, :] = jnp.zeros((16, BRANCH_W), f32)

    x = x_ref[0]
    xs_ref[16:, :] = x
    pos1 = t * tt + 1 + lax.broadcasted_iota(jnp.int32, (tt, POOL_GD), 0)
    outs = []
    for gi, w in enumerate(POOL_WINDOWS):
        sl = slice(gi * POOL_GD, (gi + 1) * POOL_GD)
        s = xs_ref[:, sl]
        sh = 1
        while sh < w:
            s = s + pltpu.roll(s, sh, 0)
            sh *= 2
        cnt = jnp.minimum(w, pos1).astype(f32)
        pooled = s[16:, :] / cnt - x[:, sl]
        outs.append(_dot(pooled.astype(bf16), wp_ref[gi]))
    y = jnp.concatenate(outs, axis=-1) * sc_ref[...]
    zb_ref[0] = (y * _silu(g_ref[0])).astype(bf16)
    xs_ref[0:16, :] = x[tt - 16:, :]
    tail_ref[0] = x[tt - 16:, :]


def _pool_prompt(z3, wp, sc):
    b, s, _ = z3.shape
    tt = min(s, 512)
    return pl.pallas_call(
        _pool_kernel,
        out_shape=(jax.ShapeDtypeStruct((b, s, BRANCH_W), bf16),
                   jax.ShapeDtypeStruct((b, 16, BRANCH_W), f32)),
        grid=(b, s // tt),
        in_specs=[pl.BlockSpec((1, tt, BRANCH_W), lambda i, t: (i, t, C_POOL_X // BRANCH_W)),
                  pl.BlockSpec((1, tt, BRANCH_W), lambda i, t: (i, t, C_POOL_G // BRANCH_W)),
                  pl.BlockSpec((len(POOL_WINDOWS), POOL_GD, POOL_GD), lambda i, t: (0, 0, 0)),
                  pl.BlockSpec((1, BRANCH_W), lambda i, t: (0, 0))],
        out_specs=(pl.BlockSpec((1, tt, BRANCH_W), lambda i, t: (i, t, 0)),
                   pl.BlockSpec((1, 16, BRANCH_W), lambda i, t: (i, 0, 0))),
        scratch_shapes=[pltpu.VMEM((tt + 16, BRANCH_W), f32)],
        compiler_params=_params(("parallel", "arbitrary")),
        name="pool_prompt",
    )(z3, z3, wp, sc)


def _mix_sample_kernel(past_len, lx_ref, lg_ref, px_ref, pg_ref, conv_ref, h0_ref, pbuf_ref,
                       cw_ref, cb_ref, wa_ref, ba_ref, wx_ref, bx_ref, lam_ref, wp_ref, sc_ref,
                       zl_ref, zp_ref, h_ref):
    x = lx_ref[...]
    xc = cb_ref[...] + x * cw_ref[CONV_W - 1:CONV_W, :]
    for k in range(CONV_W - 1):
        xc = xc + conv_ref[k] * cw_ref[k:k + 1, :]
    a, b = _lru_gates(xc, wa_ref, ba_ref, wx_ref, bx_ref, lam_ref)
    h = a * h0_ref[...] + b
    h_ref[...] = h
    zl_ref[...] = (h * _silu(lg_ref[...])).astype(bf16)

    px = px_ref[...]
    outs = []
    for gi, w in enumerate(POOL_WINDOWS):
        sl = slice(gi * POOL_GD, (gi + 1) * POOL_GD)
        s = px[:, sl]
        for k in range(1, w):
            s = s + pbuf_ref[POOL_BUF - k][:, sl]
        cnt = float(min(w, past_len + 1))
        pooled = s / cnt - px[:, sl]
        outs.append(_dot(pooled.astype(bf16), wp_ref[gi]))
    y = jnp.concatenate(outs, axis=-1) * sc_ref[...]
    zp_ref[...] = (y * _silu(pg_ref[...])).astype(bf16)


def _mix_sample(z2, conv_t, h0, pbuf_t, cw, cb, wa, ba, wx, bx, lam, wp, sc, past_len):
    n = z2.shape[0]
    col = lambda c: pl.BlockSpec((n, BRANCH_W), lambda i: (0, c // BRANCH_W))
    full = lambda a: pl.BlockSpec(a.shape, lambda i: (0,) * a.ndim)
    args = (conv_t, h0, pbuf_t, cw, cb, wa, ba, wx, bx, lam, wp, sc)
    return pl.pallas_call(
        functools.partial(_mix_sample_kernel, past_len),
        out_shape=(jax.ShapeDtypeStruct((n, BRANCH_W), bf16), jax.ShapeDtypeStruct((n, BRANCH_W), bf16),
                   jax.ShapeDtypeStruct((n, BRANCH_W), f32)),
        grid=(1,),
        in_specs=[col(C_LRU_X), col(C_LRU_G), col(C_POOL_X), col(C_POOL_G)] + [full(a) for a in args],
        out_specs=(pl.BlockSpec((n, BRANCH_W), lambda i: (0, 0)),) * 3,
        compiler_params=_params(("arbitrary",)),
        name="mix_sample",
    )(z2, z2, z2, z2, *args)


def _compress_kernel(tab_ref, k_ref, v_ref, wk_ref, wv_ref, fk_ref, sk_ref, fv_ref, sv_ref):
    del tab_ref
    p = pl.program_id(1)
    chunks = PAGE_SIZE // CMP_STRIDE

    @pl.when(p == 0)
    def _():
        sk_ref[...] = jnp.zeros_like(sk_ref)
        sv_ref[...] = jnp.zeros_like(sv_ref)

    for src, w_ref, f_ref, s_ref in ((k_ref, wk_ref, fk_ref, sk_ref), (v_ref, wv_ref, fv_ref, sv_ref)):
        tile = src[0]
        first = jnp.sum((tile * w_ref[0]).reshape(chunks, CMP_STRIDE, KV_W), axis=1)
        second = jnp.sum((tile * w_ref[1]).reshape(chunks, CMP_STRIDE, KV_W), axis=1)
        for m in range(chunks):
            f_ref[0, m % 4, pl.ds(2 * p + m // 4, 1), :] = first[m:m + 1, :]
            if m >= 1:
                s_ref[0, (m - 1) % 4, pl.ds(2 * p + (m - 1) // 4, 1), :] = second[m:m + 1, :]
            else:
                @pl.when(p > 0)
                def _():
                    s_ref[0, 3, pl.ds(2 * p - 1, 1), :] = second[0:1, :]


def _compress(table, src_k, src_v, col_k, col_v, wk, wv, nb):
    npg = table.shape[0] // nb
    out = jax.ShapeDtypeStruct((nb, 4, N_CMP // 4, KV_W), f32)
    ospec = lambda: pl.BlockSpec((1, 4, N_CMP // 4, KV_W), lambda b, p, tab: (b, 0, 0, 0))
    return pl.pallas_call(
        _compress_kernel,
        out_shape=(out,) * 4,
        grid_spec=pltpu.PrefetchScalarGridSpec(
            num_scalar_prefetch=1,
            grid=(nb, npg),
            in_specs=[pl.BlockSpec((1, PAGE_SIZE, KV_W), lambda b, p, tab: (tab[b * npg + p], 0, col_k)),
                      pl.BlockSpec((1, PAGE_SIZE, KV_W), lambda b, p, tab: (tab[b * npg + p], 0, col_v)),
                      pl.BlockSpec((2, PAGE_SIZE, KV_W), lambda b, p, tab: (0, 0, 0)),
                      pl.BlockSpec((2, PAGE_SIZE, KV_W), lambda b, p, tab: (0, 0, 0))],
            out_specs=(ospec(), ospec(), ospec(), ospec())),
        compiler_params=_params(("parallel", "arbitrary")),
        name="compress",
    )(table, src_k, src_v, wk, wv)


def _finish_compress(f_ref, s_ref, phi_ref):
    blk = (f_ref[0] + s_ref[0]).reshape(N_CMP, KV_W)
    return _dot(blk.astype(bf16), phi_ref[...]).astype(bf16)


def _cmp_end(shape, axis):
    col = lax.broadcasted_iota(jnp.int32, shape, axis)
    n = ((col & (N_BLK - 1)) << 2) + (col >> 7)
    return n * CMP_STRIDE + (CMP_BLK - 1)


def _pick_blocks(score, n_pick):
    lane = lax.broadcasted_iota(jnp.int32, score.shape, 1).astype(f32)
    bias = jnp.full(score.shape, NEG, f32)
    for _ in range(n_pick):
        m = jnp.max(score, axis=-1, keepdims=True)
        first = jnp.min(jnp.where(score == m, lane, float(N_BLK)), axis=-1, keepdims=True)
        hit = lane == first
        bias = jnp.where(hit, 0.0, bias)
        score = jnp.where(hit, -jnp.inf, score)
    return bias


def _block_expand(first_block, n_keys):
    j = lax.broadcasted_iota(jnp.int32, (N_BLK, n_keys), 0)
    c = lax.broadcasted_iota(jnp.int32, (N_BLK, n_keys), 1)
    return jnp.where(j == first_block + (c >> 6), 1.0, 0.0).astype(bf16)


def _nsa_prompt_kernel(q_ref, bg_ref, ng_ref, fk_ref, sk_ref, fv_ref, sv_ref, phik_ref, phiv_ref,
                       ks_ref, vs_ref, kw_ref, vw_ref, o_ref,
                       kc_scr, vc_scr, q4_scr, sel_scr, oc_scr, m_scr, l_scr, acc_scr):
    qi = pl.program_id(1)
    s0 = qi * TQ

    @pl.when(qi == 0)
    def _():
        kc_scr[...] = _finish_compress(fk_ref, sk_ref, phik_ref)
        vc_scr[...] = _finish_compress(fv_ref, sv_ref, phiv_ref)

    t_col = s0 + lax.broadcasted_iota(jnp.int32, (TQ, 1), 0)

    dist_c = t_col - _cmp_end((TQ, N_CMP), 1)
    ok_c = dist_c >= 0
    dist_cf = dist_c.astype(f32)
    jl = lax.broadcasted_iota(jnp.int32, (TQ, N_BLK), 1)
    jb = t_col >> 6
    ok_b = (jl << 6) <= t_col
    forced = jnp.where((jl == 0) | (jl == jb) | (jl == jb - 1), FORCE, 0.0)
    for g in range(KV_HEADS):
        gs = slice(g * HEAD_DIM, (g + 1) * HEAD_DIM)
        q4 = jnp.concatenate([q_ref[0, :, (g * Q_PER_KV + r) * HEAD_DIM:(g * Q_PER_KV + r + 1) * HEAD_DIM]
                              for r in range(Q_PER_KV)], axis=0)
        q4_scr[g] = q4
        s_all = _nt(q4, kc_scr[:, gs])
        ps = jnp.zeros((TQ, N_BLK), f32)
        probs = []
        for r in range(Q_PER_KV):
            s = s_all[r * TQ:(r + 1) * TQ] - SLOPES[g * Q_PER_KV + r] * dist_cf
            s = jnp.where(ok_c, s, NEG)
            m = jnp.max(s, axis=-1, keepdims=True)
            e = jnp.exp(s - m)
            p = jnp.where(ok_c, e * (1.0 / jnp.sum(e, axis=-1, keepdims=True)), 0.0)
            ps = ps + ((p[:, 0:N_BLK] + p[:, N_BLK:2 * N_BLK]) + (p[:, 2 * N_BLK:3 * N_BLK] + p[:, 3 * N_BLK:]))
            probs.append(p.astype(bf16))
        oc_scr[g] = _dot(jnp.concatenate(probs, axis=0), vc_scr[:, gs])
        score = jnp.where(ok_b, ps + forced, NEG)
        sel_scr[g] = _pick_blocks(score, N_SEL).astype(bf16)

    m_scr[...] = jnp.full(m_scr.shape, NEG, f32)
    l_scr[...] = jnp.zeros_like(l_scr)
    acc_scr[...] = jnp.zeros_like(acc_scr)

    def sel_body(kt, carry):
        k0 = pl.multiple_of(kt * TK, TK)
        kpos = k0 + lax.broadcasted_iota(jnp.int32, (TQ, TK), 1)
        dist = t_col - kpos
        causal = dist >= 0
        dist_f = dist.astype(f32)
        expand = _block_expand(kt * (TK // SEL_BLK), TK)
        k_tile = ks_ref[0, pl.ds(k0, TK), :]
        v_tile = vs_ref[0, pl.ds(k0, TK), :]
        for g in range(KV_HEADS):
            gs = slice(g * HEAD_DIM, (g + 1) * HEAD_DIM)
            s_all = _nt(q4_scr[g], k_tile[:, gs])
            mask_bias = jnp.where(causal, _dot(sel_scr[g], expand), NEG)
            for r in range(Q_PER_KV):
                h = g * Q_PER_KV + r
                rows = slice(r * TQ, (r + 1) * TQ)
                s = s_all[rows] - SLOPES[h] * dist_f + mask_bias
                m_old = m_scr[g, rows]
                m_new = jnp.maximum(m_old, jnp.max(s, axis=-1, keepdims=True))
                alpha = jnp.exp(m_old - m_new)
                p = jnp.exp(s - m_new)
                l_scr[g, rows] = alpha * l_scr[g, rows] + jnp.sum(p, axis=-1, keepdims=True)
                acc_scr[g, rows] = alpha * acc_scr[g, rows] + _dot(p.astype(bf16), v_tile[:, gs])
                m_scr[g, rows] = m_new
        return carry

    lax.fori_loop(0, (s0 + TQ + TK - 1) // TK, sel_body, 0)

    w0 = pl.multiple_of(jnp.maximum(s0 - WINDOW, 0), TQ)
    dist_w = t_col - (w0 + lax.broadcasted_iota(jnp.int32, (TQ, WINDOW + TQ), 1))
    ok_w = (dist_w >= 0) & (dist_w < WINDOW)
    dist_wf = dist_w.astype(f32)
    kw = kw_ref[0, pl.ds(w0, WINDOW + TQ), :]
    vw = vw_ref[0, pl.ds(w0, WINDOW + TQ), :]
    gates = jax.nn.sigmoid(bg_ref[0])
    pieces = []
    for g in range(KV_HEADS):
        gs = slice(g * HEAD_DIM, (g + 1) * HEAD_DIM)
        s_all = _nt(q4_scr[g], kw[:, gs])
        probs = []
        for r in range(Q_PER_KV):
            s = jnp.where(ok_w, s_all[r * TQ:(r + 1) * TQ] - SLOPES[g * Q_PER_KV + r] * dist_wf, NEG)
            e = jnp.exp(s - jnp.max(s, axis=-1, keepdims=True))
            probs.append((e * (1.0 / jnp.sum(e, axis=-1, keepdims=True))).astype(bf16))
        o_w = _dot(jnp.concatenate(probs, axis=0), vw[:, gs])
        o_s = acc_scr[g] * (1.0 / l_scr[g])
        o_c = oc_scr[g]
        for r in range(Q_PER_KV):
            h = g * Q_PER_KV + r
            rows = slice(r * TQ, (r + 1) * TQ)
            pieces.append(gates[:, 3 * h:3 * h + 1] * o_c[rows] + gates[:, 3 * h + 1:3 * h + 2] * o_s[rows]
                          + gates[:, 3 * h + 2:3 * h + 3] * o_w[rows])
    y = jnp.concatenate(pieces, axis=-1)
    o_ref[0] = (y * _silu(ng_ref[0])).astype(bf16)


def _nsa_prompt(qb, z3, fk, sk, fv, sv, phik, phiv, ksb, vsb, kwb, vwb):
    b, s, _ = qb.shape
    seq = lambda: pl.BlockSpec((1, s, KV_W), lambda i, t: (i, 0, 0))
    cmp = lambda: pl.BlockSpec((1, 4, N_CMP // 4, KV_W), lambda i, t: (i, 0, 0, 0))
    phi = lambda: pl.BlockSpec((KV_W, KV_W), lambda i, t: (0, 0))
    rows4 = Q_PER_KV * TQ
    return pl.pallas_call(
        _nsa_prompt_kernel,
        out_shape=jax.ShapeDtypeStruct((b, s, BRANCH_W), bf16),
        grid=(b, s // TQ),
        in_specs=[pl.BlockSpec((1, TQ, BRANCH_W), lambda i, t: (i, t, 0)),
                  pl.BlockSpec((1, TQ, BG_W), lambda i, t: (i, t, C_BG // BG_W)),
                  pl.BlockSpec((1, TQ, BRANCH_W), lambda i, t: (i, t, C_NSA_G // BRANCH_W)),
                  cmp(), cmp(), cmp(), cmp(), phi(), phi(), seq(), seq(), seq(), seq()],
        out_specs=pl.BlockSpec((1, TQ, BRANCH_W), lambda i, t: (i, t, 0)),
        scratch_shapes=[pltpu.VMEM((N_CMP, KV_W), bf16), pltpu.VMEM((N_CMP, KV_W), bf16),
                        pltpu.VMEM((KV_HEADS, rows4, HEAD_DIM), bf16),
                        pltpu.VMEM((KV_HEADS, TQ, N_BLK), bf16),
                        pltpu.VMEM((KV_HEADS, rows4, HEAD_DIM), f32),
                        pltpu.VMEM((KV_HEADS, rows4, 1), f32),
                        pltpu.VMEM((KV_HEADS, rows4, 1), f32),
                        pltpu.VMEM((KV_HEADS, rows4, HEAD_DIM), f32)],
        compiler_params=_params(("parallel", "arbitrary")),
        name="nsa_prompt",
    )(qb, z3, z3, fk, sk, fv, sv, phik, phiv, ksb, vsb, kwb, vwb)


def _by_group(fn):
    hg = lax.broadcasted_iota(jnp.int32, (N_HEADS, 1), 0) >> 2
    out = fn(0)
    for g in range(1, KV_HEADS):
        out = jnp.where(hg == g, fn(g), out)
    return out


def _nsa_sample_kernel(past_len, tab_ref, q_ref, bg_ref, ng_ref, slope_ref,
                       fk_ref, sk_ref, fv_ref, sv_ref, phik_ref, phiv_ref,
                       ksn_ref, vsn_ref, kwn_ref, vwn_ref, bk_ref, bv_ref, kp_ref, vp_ref, o_ref,
                       q_scr, sel_scr, oc_scr, ow_scr, m_scr, l_scr, acc_scr):
    del tab_ref
    p = pl.program_id(1)
    slope = slope_ref[:, 0:1]
    gsl = lambda g: slice(g * HEAD_DIM, (g + 1) * HEAD_DIM)

    def per_head(row_ref):
        row = row_ref[0]
        return _by_group(lambda g: jnp.broadcast_to(row[:, gsl(g)], (N_HEADS, HEAD_DIM)))

    @pl.when(p == 0)
    def _():
        qrow = q_ref[0]
        q16 = jnp.concatenate([qrow[:, h * HEAD_DIM:(h + 1) * HEAD_DIM] for h in range(N_HEADS)], axis=0)
        q_scr[...] = q16
        q16f = q16.astype(f32)
        kc = _finish_compress(fk_ref, sk_ref, phik_ref)
        vc = _finish_compress(fv_ref, sv_ref, phiv_ref)

        dist_c = past_len - _cmp_end((1, N_CMP), 1)
        ok_c = dist_c >= 0
        s = _by_group(lambda g: _nt(q16, kc[:, gsl(g)])) - slope * dist_c.astype(f32)
        s = jnp.where(ok_c, s, NEG)
        e = jnp.exp(s - jnp.max(s, axis=-1, keepdims=True))
        pc = jnp.where(ok_c, e * (1.0 / jnp.sum(e, axis=-1, keepdims=True)), 0.0)
        pcb = pc.astype(bf16)
        oc_scr[...] = _by_group(lambda g: _dot(pcb, vc[:, gsl(g)]))
        ps16 = (pc[:, 0:N_BLK] + pc[:, N_BLK:2 * N_BLK]) + (pc[:, 2 * N_BLK:3 * N_BLK] + pc[:, 3 * N_BLK:])
        ps = jnp.concatenate([jnp.sum(ps16[g * Q_PER_KV:(g + 1) * Q_PER_KV], axis=0, keepdims=True)
                              for g in range(KV_HEADS)] + [jnp.zeros((8 - KV_HEADS, N_BLK), f32)], axis=0)
        jl = lax.broadcasted_iota(jnp.int32, (8, N_BLK), 1)
        jb = past_len // SEL_BLK
        forced = jnp.where((jl == 0) | (jl == jb) | (jl == jb - 1), FORCE, 0.0)
        bias8 = _pick_blocks(ps + forced, N_SEL - 1)
        sel_scr[...] = jnp.concatenate(
            [jnp.broadcast_to(bias8[g:g + 1], (Q_PER_KV, N_BLK)) for g in range(KV_HEADS)], axis=0).astype(bf16)

        wb = bk_ref.shape[1]
        dist_w = wb - lax.broadcasted_iota(jnp.int32, (1, wb), 1)
        ok_w = (dist_w >= 0) & (dist_w < WINDOW)
        bk = bk_ref[0].astype(bf16)
        bv = bv_ref[0].astype(bf16)
        s_buf = _by_group(lambda g: _nt(q16, bk[:, gsl(g)])) - slope * dist_w.astype(f32)
        s_buf = jnp.where(ok_w, s_buf, NEG)
        s_new = jnp.sum(q16f * per_head(kwn_ref), axis=-1, keepdims=True)
        m_w = jnp.maximum(jnp.max(s_buf, axis=-1, keepdims=True), s_new)
        e_buf = jnp.exp(s_buf - m_w)
        e_new = jnp.exp(s_new - m_w)
        ebb = e_buf.astype(bf16)
        num = _by_group(lambda g: _dot(ebb, bv[:, gsl(g)])) + e_new * per_head(vwn_ref)
        ow_scr[...] = num * (1.0 / (jnp.sum(e_buf, axis=-1, keepdims=True) + e_new))

        m_scr[...] = jnp.sum(q16f * per_head(ksn_ref), axis=-1, keepdims=True)
        l_scr[...] = jnp.ones_like(l_scr)
        acc_scr[...] = per_head(vsn_ref)

    q16 = q_scr[...]
    kp = kp_ref[0].astype(bf16)
    vp = vp_ref[0].astype(bf16)
    dist = past_len - (p * PAGE_SIZE + lax.broadcasted_iota(jnp.int32, (1, PAGE_SIZE), 1))
    expand = _block_expand(p * (PAGE_SIZE // SEL_BLK), PAGE_SIZE)
    s = _by_group(lambda g: _nt(q16, kp[:, gsl(g)])) - slope * dist.astype(f32) + _dot(sel_scr[...], expand)
    m_old = m_scr[...]
    m_new = jnp.maximum(m_old, jnp.max(s, axis=-1, keepdims=True))
    alpha = jnp.exp(m_old - m_new)
    pr = jnp.exp(s - m_new)
    prb = pr.astype(bf16)
    l_scr[...] = alpha * l_scr[...] + jnp.sum(pr, axis=-1, keepdims=True)
    acc_scr[...] = alpha * acc_scr[...] + _by_group(lambda g: _dot(prb, vp[:, gsl(g)]))
    m_scr[...] = m_new

    @pl.when(p == pl.num_programs(1) - 1)
    def _():
        o_s = acc_scr[...] * (1.0 / l_scr[...])
        gates = jax.nn.sigmoid(bg_ref[0])
        lane = lax.broadcasted_iota(jnp.int32, (N_HEADS, BG_W), 1)
        h3 = 3 * lax.broadcasted_iota(jnp.int32, (N_HEADS, BG_W), 0)
        gate = lambda n: jnp.sum(jnp.where(lane == h3 + n, gates, 0.0), axis=-1, keepdims=True)
        y16 = gate(0) * oc_scr[...] + gate(1) * o_s + gate(2) * ow_scr[...]
        y = jnp.concatenate([y16[h:h + 1, :] for h in range(N_HEADS)], axis=-1)
        o_ref[0] = (y * _silu(ng_ref[0])).astype(bf16)


def _nsa_sample(table, qb, z2, slopes, fk, sk, fv, sv, phik, phiv, pool_k, pool_v, buf_k, buf_v, past_len):
    nb = qb.shape[0]
    npg = table.shape[0] // nb
    z3 = z2.reshape(nb, 1, IN_WP)
    tok = lambda w, c: pl.BlockSpec((1, 1, w), lambda b, p, tab: (b, 0, c))
    cmp = lambda: pl.BlockSpec((1, 4, N_CMP // 4, KV_W), lambda b, p, tab: (b, 0, 0, 0))
    phi = lambda: pl.BlockSpec((KV_W, KV_W), lambda b, p, tab: (0, 0))
    wb = buf_k.shape[1]
    win = lambda: pl.BlockSpec((1, wb, KV_W), lambda b, p, tab: (b, 0, 0))
    page = lambda: pl.BlockSpec((1, PAGE_SIZE, KV_W), lambda b, p, tab: (tab[b * npg + p], 0, 0))
    kvc = C_KV // KV_W
    return pl.pallas_call(
        functools.partial(_nsa_sample_kernel, past_len),
        out_shape=jax.ShapeDtypeStruct((nb, 1, BRANCH_W), bf16),
        grid_spec=pltpu.PrefetchScalarGridSpec(
            num_scalar_prefetch=1,
            grid=(nb, npg),
            in_specs=[tok(BRANCH_W, 0), tok(BG_W, C_BG // BG_W), tok(BRANCH_W, C_NSA_G // BRANCH_W),
                      pl.BlockSpec((N_HEADS, 128), lambda b, p, tab: (0, 0)),
                      cmp(), cmp(), cmp(), cmp(), phi(), phi(),
                      tok(KV_W, kvc + 2), tok(KV_W, kvc + 3), tok(KV_W, kvc + 4), tok(KV_W, kvc + 5),
                      win(), win(), page(), page()],
            out_specs=pl.BlockSpec((1, 1, BRANCH_W), lambda b, p, tab: (b, 0, 0)),
            scratch_shapes=[pltpu.VMEM((N_HEADS, HEAD_DIM), bf16), pltpu.VMEM((N_HEADS, N_BLK), bf16),
                            pltpu.VMEM((N_HEADS, HEAD_DIM), f32), pltpu.VMEM((N_HEADS, HEAD_DIM), f32),
                            pltpu.VMEM((N_HEADS, 1), f32), pltpu.VMEM((N_HEADS, 1), f32),
                            pltpu.VMEM((N_HEADS, HEAD_DIM), f32)]),
        compiler_params=_params(("parallel", "arbitrary")),
        name="nsa_sample",
    )(table, qb.reshape(nb, 1, BRANCH_W), z3, z3, slopes, fk, sk, fv, sv, phik, phiv,
      z3, z3, z3, z3, buf_k, buf_v, pool_k, pool_v)


def _merge_kernel(zl_ref, zp_ref, zn_ref, m0_ref, m1_ref, m2_ref, wb_ref, wo_ref, g_ref, x_ref, y_ref):
    acc = None
    for n, (zz, mg) in enumerate(((zl_ref, m0_ref), (zp_ref, m1_ref), (zn_ref, m2_ref))):
        term = jax.nn.sigmoid(mg[...]) * _dot(zz[...], wb_ref[n])
        acc = term if acc is None else acc + term
    out = _dot(acc.astype(bf16), wo_ref[...])
    ms = jnp.mean(out * out, axis=-1, keepdims=True)
    y_ref[...] = x_ref[...] + out * lax.rsqrt(ms + EPS) * g_ref[...]


def _merge(zl, zp, zn, z2, wb, wo, g_row, x2d):
    n = x2d.shape[0]
    tm = min(n, 256)
    rowblk = lambda c: pl.BlockSpec((tm, D_MODEL), lambda i: (i, c))
    return pl.pallas_call(
        _merge_kernel,
        out_shape=jax.ShapeDtypeStruct((n, D_MODEL), f32),
        grid=(n // tm,),
        in_specs=[rowblk(0), rowblk(0), rowblk(0),
                  rowblk(C_MG // D_MODEL), rowblk(C_MG // D_MODEL + 1), rowblk(C_MG // D_MODEL + 2),
                  pl.BlockSpec((N_BRANCH, BRANCH_W, D_MODEL), lambda i: (0, 0, 0)),
                  pl.BlockSpec((D_MODEL, D_MODEL), lambda i: (0, 0)),
                  pl.BlockSpec((1, D_MODEL), lambda i: (0, 0)),
                  rowblk(0)],
        out_specs=rowblk(0),
        compiler_params=_params(("parallel",)),
        name="merge",
    )(zl, zp, zn, z2, z2, z2, wb, wo, g_row, x2d)


def _block_diag(w, per):
    n, d, _ = w.shape
    eye = jnp.eye(per, dtype=w.dtype)
    t = jnp.einsum('cpde,pq->cpdqe', w.reshape(n // per, per, d, d), eye)
    return t.reshape(n // per, per * d, per * d)


def _pack_w_in(w):
    old_kv = 6 * BRANCH_W
    old_bg = old_kv + 6 * KV_W
    old_mg = old_bg + N_BRANCH * N_HEADS
    parts = [w[:, :old_kv], w[:, old_mg:], w[:, old_kv:old_bg], w[:, old_bg:old_mg]]
    packed = jnp.concatenate(parts, axis=1)
    return jnp.pad(packed, ((0, 0), (0, IN_WP - packed.shape[1]))).astype(bf16)


def _tile_wpos(w_pos):
    halves = w_pos.reshape(2, CMP_STRIDE, HEAD_DIM)
    return jnp.tile(halves, (1, PAGE_SIZE // CMP_STRIDE, KV_HEADS))


def _kv_seg(z, i):
    return z[..., C_KV + i * KV_W:C_KV + (i + 1) * KV_W]


def kernel(x_prompt, x_sample, cache_cmp_k, cache_cmp_v, cache_sel_k, cache_sel_v, cache_win_k, cache_win_v, state_conv, state_lru, state_pool, page_table, g_pre, g_post, w_in, conv_w, conv_b, w_rg_a, b_rg_a, w_rg_x, b_rg_x, lru_lambda, w_pool, pool_scale, cmp_pos_k, cmp_phi_k, cmp_pos_v, cmp_phi_v, w_branch, w_out):
    depth = w_in.shape[0]
    bp, seq, _ = x_prompt.shape
    bs = x_sample.shape[0]
    n_pages = page_table.shape[1]
    past_len = n_pages * PAGE_SIZE
    n_phys = cache_cmp_k.shape[1]
    assert seq == N_BLK * SEL_BLK and past_len == N_BLK * SEL_BLK and x_sample.shape[1] == 1
    wb = cache_win_k.shape[2]

    table_s = page_table.reshape(-1).astype(jnp.int32)
    table_p = jnp.arange(bp * (seq // PAGE_SIZE), dtype=jnp.int32)
    slopes = jnp.broadcast_to(jnp.asarray(SLOPES, f32)[:, None], (N_HEADS, 128))
    row = lambda v: v.reshape(1, -1)

    xp = x_prompt.reshape(bp * seq, D_MODEL)
    xs = x_sample.reshape(bs, D_MODEL)
    pr = [[] for _ in range(9)]
    sm = [[] for _ in range(9)]
    for l in range(depth):
        w_packed = _pack_w_in(w_in[l])
        wa = _block_diag(w_rg_a[l], MXU_W // LRU_BD).astype(bf16)
        wx = _block_diag(w_rg_x[l], MXU_W // LRU_BD).astype(bf16)
        wp = w_pool[l].astype(bf16)
        phik = _block_diag(jnp.broadcast_to(cmp_phi_k[l], (KV_HEADS, HEAD_DIM, HEAD_DIM)), KV_HEADS)[0].astype(bf16)
        phiv = _block_diag(jnp.broadcast_to(cmp_phi_v[l], (KV_HEADS, HEAD_DIM, HEAD_DIM)), KV_HEADS)[0].astype(bf16)
        wpos_k = _tile_wpos(cmp_pos_k[l])
        wpos_v = _tile_wpos(cmp_pos_v[l])
        wbr = w_branch[l].astype(bf16)
        wo = w_out[l].astype(bf16)
        lru_w = (conv_w[l], row(conv_b[l]), wa, row(b_rg_a[l]), wx, row(b_rg_x[l]), row(lru_lambda[l]))

        z = _inproj(xp, row(g_pre[l]), w_packed)
        z3 = z.reshape(bp, seq, IN_WP)
        zl, conv_tail, h_p = _lru_prompt(z3, *lru_w)
        zpool, pool_tail = _pool_prompt(z3, wp, row(pool_scale[l]))
        zr = z.reshape(bp * seq // PAGE_SIZE, PAGE_SIZE, IN_WP)
        fk, sk, fv, sv = _compress(table_p, zr, zr, C_KV // KV_W, C_KV // KV_W + 1, wpos_k, wpos_v, bp)
        qb = (z3[..., C_Q:C_Q + BRANCH_W] * (HEAD_DIM ** -0.5)).astype(bf16)
        kvb = [_kv_seg(z3, i).astype(bf16) for i in range(2, 6)]
        zn = _nsa_prompt(qb, z3, fk, sk, fv, sv, phik, phiv, *kvb)
        xp = _merge(zl.reshape(bp * seq, BRANCH_W), zpool.reshape(bp * seq, BRANCH_W),
                    zn.reshape(bp * seq, BRANCH_W), z, wbr, wo, row(g_post[l]), xp)
        kv_rows = [_kv_seg(z3, i).reshape(bp, seq, KV_HEADS, HEAD_DIM) for i in range(6)]
        wlen = min(WINDOW, seq)
        st_p = kv_rows[:4] + [kv_rows[4][:, -wlen:], kv_rows[5][:, -wlen:],
                              conv_tail[:, -(CONV_W - 1):], h_p[:, 0], pool_tail[:, -POOL_BUF:]]

        zs = _inproj(xs, row(g_pre[l]), w_packed)
        zls, zps, h_s = _mix_sample(zs, state_conv[l].swapaxes(0, 1), state_lru[l], state_pool[l].swapaxes(0, 1),
                                    *lru_w, wp, row(pool_scale[l]), past_len)
        pool_ck = cache_cmp_k[l].reshape(n_phys, PAGE_SIZE, KV_W)
        pool_cv = cache_cmp_v[l].reshape(n_phys, PAGE_SIZE, KV_W)
        fk, sk, fv, sv = _compress(table_s, pool_ck, pool_cv, 0, 0, wpos_k, wpos_v, bs)
        qs = (zs[:, C_Q:C_Q + BRANCH_W] * (HEAD_DIM ** -0.5)).astype(bf16)
        zns = _nsa_sample(table_s, qs, zs, slopes, fk, sk, fv, sv, phik, phiv,
                          cache_sel_k[l].reshape(n_phys, PAGE_SIZE, KV_W), cache_sel_v[l].reshape(n_phys, PAGE_SIZE, KV_W),
                          cache_win_k[l].reshape(bs, wb, KV_W), cache_win_v[l].reshape(bs, wb, KV_W), past_len)
        xs = _merge(zls, zps, zns.reshape(bs, BRANCH_W), zs, wbr, wo, row(g_post[l]), xs)
        kv_new = [_kv_seg(zs, i).reshape(bs, 1, KV_HEADS, HEAD_DIM) for i in range(6)]
        st_s = kv_new[:4] + [jnp.concatenate([cache_win_k[l], kv_new[4]], axis=1)[:, -wb:],
                             jnp.concatenate([cache_win_v[l], kv_new[5]], axis=1)[:, -wb:],
                             jnp.concatenate([state_conv[l], zs[:, None, C_LRU_X:C_LRU_X + BRANCH_W]], axis=1)[:, -(CONV_W - 1):],
                             h_s,
                             jnp.concatenate([state_pool[l], zs[:, None, C_POOL_X:C_POOL_X + BRANCH_W]], axis=1)[:, -POOL_BUF:]]
        for i in range(9):
            pr[i].append(st_p[i])
            sm[i].append(st_s[i])

    out = [xp.reshape(bp, seq, D_MODEL), xs.reshape(bs, 1, D_MODEL)]
    for i in range(9):
        out += [jnp.stack(pr[i]), jnp.stack(sm[i])]
    return tuple(out)
```

```python
import functools

import numpy as np
import jax
import jax.numpy as jnp
from jax import lax
from jax.experimental import pallas as pl
from jax.experimental.pallas import tpu as pltpu

f32 = jnp.float32
bf16 = jnp.bfloat16

D_MODEL = 1024
BRANCH_W = 1024
N_BRANCH = 3
LRU_BLOCKS = 16
LRU_BD = BRANCH_W // LRU_BLOCKS
CONV_W = 4
LRU_C = 8.0
POOL_WINDOWS = (2, 4, 8, 16)
POOL_GD = BRANCH_W // len(POOL_WINDOWS)
POOL_BUF = max(POOL_WINDOWS) - 1
N_HEADS = 16
HEAD_DIM = 64
KV_HEADS = 4
Q_PER_KV = N_HEADS // KV_HEADS
KV_W = KV_HEADS * HEAD_DIM
CMP_STRIDE = 16
CMP_BLK = 2 * CMP_STRIDE
SEL_BLK = 64
N_SEL = 16
WINDOW = 512
PAGE_SIZE = 128
FORCE = 1e4
NEG = -1e30
EPS = 1e-6

C_LRU_X, C_LRU_G, C_POOL_X, C_POOL_G, C_Q, C_NSA_G, C_MG = 0, 1024, 2048, 3072, 4096, 5120, 6144
C_KV = C_MG + N_BRANCH * D_MODEL
C_BG = C_KV + 6 * KV_W
BG_W = 128
IN_WP = 11264
IN_TN = 1024

MXU_W = 256
VMEM_LIMIT = 56 * 1024 * 1024

N_CMP = 512
N_BLK = 128
TQ = 128
TK = 512
PP = 8
POS_ROWS = 64
QA_W = HEAD_DIM + POS_ROWS

SLOPES = [float(np.float32(2.0 ** (-8.0 * (h + 1) / N_HEADS))) for h in range(N_HEADS)]


def _nt(a, b):
    return lax.dot_general(a, b, (((1,), (1,)), ((), ())), preferred_element_type=f32)


def _dot(a, b):
    return jnp.dot(a, b, preferred_element_type=f32)


def _silu(x):
    return x * jax.nn.sigmoid(x)


def _params(sem):
    return pltpu.CompilerParams(dimension_semantics=sem, vmem_limit_bytes=VMEM_LIMIT)


def _inproj_kernel(x_ref, g_ref, w_ref, o_ref, u_ref):
    @pl.when(pl.program_id(1) == 0)
    def _():
        x = x_ref[...]
        ms = jnp.mean(x * x, axis=-1, keepdims=True)
        u_ref[...] = (x * lax.rsqrt(ms + EPS) * g_ref[...]).astype(bf16)

    o_ref[...] = _dot(u_ref[...], w_ref[...])


def _inproj(x2d, g_row, w_packed):
    n = x2d.shape[0]
    tm = min(n, 1024)
    return pl.pallas_call(
        _inproj_kernel,
        out_shape=jax.ShapeDtypeStruct((n, IN_WP), f32),
        grid=(n // tm, IN_WP // IN_TN),
        in_specs=[pl.BlockSpec((tm, D_MODEL), lambda i, j: (i, 0)),
                  pl.BlockSpec((1, D_MODEL), lambda i, j: (0, 0)),
                  pl.BlockSpec((D_MODEL, IN_TN), lambda i, j: (0, j))],
        out_specs=pl.BlockSpec((tm, IN_TN), lambda i, j: (i, j)),
        scratch_shapes=[pltpu.VMEM((tm, D_MODEL), bf16)],
        compiler_params=_params(("parallel", "arbitrary")),
        name="inproj",
    )(x2d, g_row, w_packed)


def _lru_gates(xc, wa_ref, ba_ref, wx_ref, bx_ref, lam_ref):
    xb = xc.astype(bf16)
    ra, ri = [], []
    for c in range(BRANCH_W // MXU_W):
        sl = slice(c * MXU_W, (c + 1) * MXU_W)
        ra.append(_dot(xb[:, sl], wa_ref[c]))
        ri.append(_dot(xb[:, sl], wx_ref[c]))
    r = jax.nn.sigmoid(jnp.concatenate(ra, axis=-1) + ba_ref[...])
    i = jax.nn.sigmoid(jnp.concatenate(ri, axis=-1) + bx_ref[...])
    nl = -lam_ref[...]
    softplus = jnp.maximum(nl, 0.0) + jnp.log1p(jnp.exp(-jnp.abs(nl)))
    log_a = -LRU_C * r * softplus
    a = jnp.exp(log_a)
    b = jnp.sqrt(1.0 - a * a) * (i * xc)
    return a, b


def _lru_kernel(x_ref, g_ref, cw_ref, cb_ref, wa_ref, ba_ref, wx_ref, bx_ref, lam_ref,
                zb_ref, tail_ref, h_ref, xs_ref, a_ref, b_ref, hc_ref):
    tt = x_ref.shape[1]

    @pl.when(pl.program_id(1) == 0)
    def _():
        xs_ref[0:8, :] = jnp.zeros((8, BRANCH_W), f32)
        hc_ref[...] = jnp.zeros_like(hc_ref)

    x = x_ref[0]
    xs_ref[8:, :] = x
    xc = cb_ref[...] + x * cw_ref[CONV_W - 1:CONV_W, :]
    for k in range(CONV_W - 1):
        xc = xc + xs_ref[pl.ds(8 - (CONV_W - 1 - k), tt), :] * cw_ref[k:k + 1, :]
    xs_ref[0:8, :] = x[tt - 8:, :]
    tail_ref[0] = x[tt - 8:, :]

    a, b = _lru_gates(xc, wa_ref, ba_ref, wx_ref, bx_ref, lam_ref)
    a_ref[...] = a
    b_ref[...] = b
    row = lax.broadcasted_iota(jnp.int32, (8, BRANCH_W), 0)

    def body(i, h):
        r0 = pl.multiple_of(i * 8, 8)
        av = a_ref[pl.ds(r0, 8), :]
        bv = b_ref[pl.ds(r0, 8), :]
        for d in (1, 2, 4):
            a_s = jnp.where(row >= d, pltpu.roll(av, d, 0), 1.0)
            b_s = jnp.where(row >= d, pltpu.roll(bv, d, 0), 0.0)
            bv = av * b_s + bv
            av = av * a_s
        hs = bv + av * h
        b_ref[pl.ds(r0, 8), :] = hs
        return hs[7:8, :]

    h = lax.fori_loop(0, tt // 8, body, hc_ref[...])
    hc_ref[...] = h
    h_ref[0] = h
    zb_ref[0] = (b_ref[...] * _silu(g_ref[0])).astype(bf16)


def _lru_prompt(z3, cw, cb, wa, ba, wx, bx, lam):
    b, s, _ = z3.shape
    tt = min(s, 512)
    row = lambda: pl.BlockSpec((1, BRANCH_W), lambda i, t: (0, 0))
    bd = lambda: pl.BlockSpec((BRANCH_W // MXU_W, MXU_W, MXU_W), lambda i, t: (0, 0, 0))
    return pl.pallas_call(
        _lru_kernel,
        out_shape=(jax.ShapeDtypeStruct((b, s, BRANCH_W), bf16),
                   jax.ShapeDtypeStruct((b, 8, BRANCH_W), f32),
                   jax.ShapeDtypeStruct((b, 1, BRANCH_W), f32)),
        grid=(b, s // tt),
        in_specs=[pl.BlockSpec((1, tt, BRANCH_W), lambda i, t: (i, t, C_LRU_X // BRANCH_W)),
                  pl.BlockSpec((1, tt, BRANCH_W), lambda i, t: (i, t, C_LRU_G // BRANCH_W)),
                  pl.BlockSpec((CONV_W, BRANCH_W), lambda i, t: (0, 0)),
                  row(), bd(), row(), bd(), row(), row()],
        out_specs=(pl.BlockSpec((1, tt, BRANCH_W), lambda i, t: (i, t, 0)),
                   pl.BlockSpec((1, 8, BRANCH_W), lambda i, t: (i, 0, 0)),
                   pl.BlockSpec((1, 1, BRANCH_W), lambda i, t: (i, 0, 0))),
        scratch_shapes=[pltpu.VMEM((tt + 8, BRANCH_W), f32), pltpu.VMEM((tt, BRANCH_W), f32),
                        pltpu.VMEM((tt, BRANCH_W), f32), pltpu.VMEM((1, BRANCH_W), f32)],
        compiler_params=_params(("parallel", "arbitrary")),
        name="lru_prompt",
    )(z3, z3, cw, cb, wa, ba, wx, bx, lam)


def _pool_kernel(x_ref, g_ref, wp_ref, sc_ref, zb_ref, tail_ref, xs_ref):
    tt = x_ref.shape[1]
    t = pl.program_id(1)

    @pl.when(t == 0)
    def _():
        xs_ref[0:16, :] = jnp.zeros((16, BRANCH_W), f32)

    x = x_ref[0]
    xs_ref[16:, :] = x
    pos1 = t * tt + 1 + lax.broadcasted_iota(jnp.int32, (tt, POOL_GD), 0)
    outs = []
    for gi, w in enumerate(POOL_WINDOWS):
        sl = slice(gi * POOL_GD, (gi + 1) * POOL_GD)
        s = xs_ref[:, sl]
        sh = 1
        while sh < w:
            s = s + pltpu.roll(s, sh, 0)
            sh *= 2
        cnt = jnp.minimum(w, pos1).astype(f32)
        pooled = s[16:, :] / cnt - x[:, sl]
        outs.append(_dot(pooled.astype(bf16), wp_ref[gi]))
    y = jnp.concatenate(outs, axis=-1) * sc_ref[...]
    zb_ref[0] = (y * _silu(g_ref[0])).astype(bf16)
    xs_ref[0:16, :] = x[tt - 16:, :]
    tail_ref[0] = x[tt - 16:, :]


def _pool_prompt(z3, wp, sc):
    b, s, _ = z3.shape
    tt = min(s, 512)
    return pl.pallas_call(
        _pool_kernel,
        out_shape=(jax.ShapeDtypeStruct((b, s, BRANCH_W), bf16),
                   jax.ShapeDtypeStruct((b, 16, BRANCH_W), f32)),
        grid=(b, s // tt),
        in_specs=[pl.BlockSpec((1, tt, BRANCH_W), lambda i, t: (i, t, C_POOL_X // BRANCH_W)),
                  pl.BlockSpec((1, tt, BRANCH_W), lambda i, t: (i, t, C_POOL_G // BRANCH_W)),
                  pl.BlockSpec((len(POOL_WINDOWS), POOL_GD, POOL_GD), lambda i, t: (0, 0, 0)),
                  pl.BlockSpec((1, BRANCH_W), lambda i, t: (0, 0))],
        out_specs=(pl.BlockSpec((1, tt, BRANCH_W), lambda i, t: (i, t, 0)),
                   pl.BlockSpec((1, 16, BRANCH_W), lambda i, t: (i, 0, 0))),
        scratch_shapes=[pltpu.VMEM((tt + 16, BRANCH_W), f32)],
        compiler_params=_params(("parallel", "arbitrary")),
        name="pool_prompt",
    )(z3, z3, wp, sc)


def _mix_sample_kernel(past_len, lx_ref, lg_ref, px_ref, pg_ref, conv_ref, h0_ref, pbuf_ref,
                       cw_ref, cb_ref, wa_ref, ba_ref, wx_ref, bx_ref, lam_ref, wp_ref, sc_ref,
                       zl_ref, zp_ref, h_ref):
    x = lx_ref[...]
    xc = cb_ref[...] + x * cw_ref[CONV_W - 1:CONV_W, :]
    for k in range(CONV_W - 1):
        xc = xc + conv_ref[k] * cw_ref[k:k + 1, :]
    a, b = _lru_gates(xc, wa_ref, ba_ref, wx_ref, bx_ref, lam_ref)
    h = a * h0_ref[...] + b
    h_ref[...] = h
    zl_ref[...] = (h * _silu(lg_ref[...])).astype(bf16)

    px = px_ref[...]
    outs = []
    for gi, w in enumerate(POOL_WINDOWS):
        sl = slice(gi * POOL_GD, (gi + 1) * POOL_GD)
        s = px[:, sl]
        for k in range(1, w):
            s = s + pbuf_ref[POOL_BUF - k][:, sl]
        cnt = float(min(w, past_len + 1))
        pooled = s / cnt - px[:, sl]
        outs.append(_dot(pooled.astype(bf16), wp_ref[gi]))
    y = jnp.concatenate(outs, axis=-1) * sc_ref[...]
    zp_ref[...] = (y * _silu(pg_ref[...])).astype(bf16)


def _mix_sample(z2, conv_t, h0, pbuf_t, cw, cb, wa, ba, wx, bx, lam, wp, sc, past_len):
    n = z2.shape[0]
    col = lambda c: pl.BlockSpec((n, BRANCH_W), lambda i: (0, c // BRANCH_W))
    full = lambda a: pl.BlockSpec(a.shape, lambda i: (0,) * a.ndim)
    args = (conv_t, h0, pbuf_t, cw, cb, wa, ba, wx, bx, lam, wp, sc)
    return pl.pallas_call(
        functools.partial(_mix_sample_kernel, past_len),
        out_shape=(jax.ShapeDtypeStruct((n, BRANCH_W), bf16), jax.ShapeDtypeStruct((n, BRANCH_W), bf16),
                   jax.ShapeDtypeStruct((n, BRANCH_W), f32)),
        grid=(1,),
        in_specs=[col(C_LRU_X), col(C_LRU_G), col(C_POOL_X), col(C_POOL_G)] + [full(a) for a in args],
        out_specs=(pl.BlockSpec((n, BRANCH_W), lambda i: (0, 0)),) * 3,
        compiler_params=_params(("arbitrary",)),
        name="mix_sample",
    )(z2, z2, z2, z2, *args)


def _compress_kernel(tab_ref, k_ref, v_ref, wk_ref, wv_ref, fk_ref, sk_ref, fv_ref, sv_ref):
    del tab_ref
    p = pl.program_id(1)
    chunks = PAGE_SIZE // CMP_STRIDE

    @pl.when(p == 0)
    def _():
        sk_ref[...] = jnp.zeros_like(sk_ref)
        sv_ref[...] = jnp.zeros_like(sv_ref)

    for src, w_ref, f_ref, s_ref in ((k_ref, wk_ref, fk_ref, sk_ref), (v_ref, wv_ref, fv_ref, sv_ref)):
        tile = src[0]
        first = jnp.sum((tile * w_ref[0]).reshape(chunks, CMP_STRIDE, KV_W), axis=1)
        second = jnp.sum((tile * w_ref[1]).reshape(chunks, CMP_STRIDE, KV_W), axis=1)
        for m in range(chunks):
            f_ref[0, m % 4, pl.ds(2 * p + m // 4, 1), :] = first[m:m + 1, :]
            if m >= 1:
                s_ref[0, (m - 1) % 4, pl.ds(2 * p + (m - 1) // 4, 1), :] = second[m:m + 1, :]
            else:
                @pl.when(p > 0)
                def _():
                    s_ref[0, 3, pl.ds(2 * p - 1, 1), :] = second[0:1, :]


def _compress(table, src_k, src_v, col_k, col_v, wk, wv, nb):
    npg = table.shape[0] // nb
    out = jax.ShapeDtypeStruct((nb, 4, N_CMP // 4, KV_W), f32)
    ospec = lambda: pl.BlockSpec((1, 4, N_CMP // 4, KV_W), lambda b, p, tab: (b, 0, 0, 0))
    return pl.pallas_call(
        _compress_kernel,
        out_shape=(out,) * 4,
        grid_spec=pltpu.PrefetchScalarGridSpec(
            num_scalar_prefetch=1,
            grid=(nb, npg),
            in_specs=[pl.BlockSpec((1, PAGE_SIZE, KV_W), lambda b, p, tab: (tab[b * npg + p], 0, col_k)),
                      pl.BlockSpec((1, PAGE_SIZE, KV_W), lambda b, p, tab: (tab[b * npg + p], 0, col_v)),
                      pl.BlockSpec((2, PAGE_SIZE, KV_W), lambda b, p, tab: (0, 0, 0)),
                      pl.BlockSpec((2, PAGE_SIZE, KV_W), lambda b, p, tab: (0, 0, 0))],
            out_specs=(ospec(), ospec(), ospec(), ospec())),
        compiler_params=_params(("parallel", "arbitrary")),
        name="compress",
    )(table, src_k, src_v, wk, wv)


def _chunk_maps():
    n_chunk = PP * PAGE_SIZE // CMP_STRIDE
    rows = n_chunk // 4
    chunk_of = np.arange(PP * PAGE_SIZE) // CMP_STRIDE
    first = np.zeros((n_chunk, PP * PAGE_SIZE), np.float32)
    second = np.zeros((n_chunk + 8, PP * PAGE_SIZE), np.float32)
    for i in range(4):
        for jj in range(rows):
            first[i * rows + jj] = chunk_of == 4 * jj + i
            second[i * rows + jj] = chunk_of == 4 * jj + i + 1
    second[n_chunk] = chunk_of == 0
    return jnp.asarray(first, bf16), jnp.asarray(second, bf16)


def _compress_t_kernel(tab_ref, *refs):
    del tab_ref
    k_refs, v_refs = refs[:PP], refs[PP:2 * PP]
    wk_ref, wv_ref, ea_ref, eb_ref, fk_ref, sk_ref, fv_ref, sv_ref = refs[2 * PP:]
    ps = pl.program_id(1)
    rows = PP * PAGE_SIZE // CMP_STRIDE // 4
    r0 = pl.multiple_of(ps * rows, rows)
    for pages, w_ref, f_ref, s_ref in ((k_refs, wk_ref, fk_ref, sk_ref), (v_refs, wv_ref, fv_ref, sv_ref)):
        a1 = jnp.concatenate([(r[0, 0] * w_ref[0]).astype(bf16) for r in pages], axis=1)
        a2 = jnp.concatenate([(r[0, 0] * w_ref[1]).astype(bf16) for r in pages], axis=1)
        first = _nt(ea_ref[...], a1)
        second = _nt(eb_ref[...], a2)
        for i in range(4):
            f_ref[0, i, pl.ds(r0, rows), :] = first[i * rows:(i + 1) * rows]
            s_ref[0, i, pl.ds(r0, rows), :] = second[i * rows:(i + 1) * rows]

        @pl.when(ps > 0)
        def _():
            s_ref[0, 3, pl.ds(r0 - 1, 1), :] = second[4 * rows:4 * rows + 1]


def _compress_t(table, layer, cache_k, cache_v, wk, wv, nb):
    npg = table.shape[0] // nb
    ea, eb = _chunk_maps()
    out = jax.ShapeDtypeStruct((nb, 4, N_CMP // 4, KV_W), f32)
    ospec = lambda: pl.BlockSpec((1, 4, N_CMP // 4, KV_W), lambda b, p, tab: (b, 0, 0, 0))
    page = lambda i: pl.BlockSpec((1, 1, KV_W, PAGE_SIZE),
                                  lambda b, p, tab: (layer, tab[b * npg + p * PP + i], 0, 0))
    full = lambda a: pl.BlockSpec(a.shape, lambda b, p, tab: (0,) * a.ndim)
    return pl.pallas_call(
        _compress_t_kernel,
        out_shape=(out,) * 4,
        grid_spec=pltpu.PrefetchScalarGridSpec(
            num_scalar_prefetch=1,
            grid=(nb, npg // PP),
            in_specs=[page(i) for i in range(PP)] * 2 + [full(wk), full(wv), full(ea), full(eb)],
            out_specs=(ospec(), ospec(), ospec(), ospec())),
        compiler_params=_params(("parallel", "arbitrary")),
        name="compress_t",
    )(table, *([cache_k] * PP), *([cache_v] * PP), wk, wv, ea, eb)


def _finish_compress(f_ref, s_ref, phi_ref):
    blk = (f_ref[0] + s_ref[0]).reshape(N_CMP, KV_W)
    return _dot(blk.astype(bf16), phi_ref[...]).astype(bf16)


def _cmp_end(shape, axis):
    col = lax.broadcasted_iota(jnp.int32, shape, axis)
    n = ((col & (N_BLK - 1)) << 2) + (col >> 7)
    return n * CMP_STRIDE + (CMP_BLK - 1)


def _pick_blocks(score, n_pick):
    lane = lax.broadcasted_iota(jnp.int32, score.shape, 1).astype(f32)
    bias = jnp.full(score.shape, NEG, f32)
    for _ in range(n_pick):
        m = jnp.max(score, axis=-1, keepdims=True)
        first = jnp.min(jnp.where(score == m, lane, float(N_BLK)), axis=-1, keepdims=True)
        hit = lane == first
        bias = jnp.where(hit, 0.0, bias)
        score = jnp.where(hit, -jnp.inf, score)
    return bias


def _block_expand(first_block, n_keys):
    j = lax.broadcasted_iota(jnp.int32, (N_BLK, n_keys), 0)
    c = lax.broadcasted_iota(jnp.int32, (N_BLK, n_keys), 1)
    return jnp.where(j == first_block + (c >> 6), 1.0, 0.0).astype(bf16)


def _nsa_prompt_kernel(q_ref, bg_ref, ng_ref, fk_ref, sk_ref, fv_ref, sv_ref, phik_ref, phiv_ref,
                       kst_ref, vs_ref, kw_ref, vw_ref, pos_ref, o_ref,
                       kc_scr, vc_scr, qa_scr, oc_scr, m_scr, l_scr, acc_scr):
    qi = pl.program_id(1)
    s0 = qi * TQ

    @pl.when(qi == 0)
    def _():
        kc_scr[...] = _finish_compress(fk_ref, sk_ref, phik_ref)
        vc_scr[...] = _finish_compress(fv_ref, sv_ref, phiv_ref)

    t_col = s0 + lax.broadcasted_iota(jnp.int32, (TQ, 1), 0)

    dist_c = t_col - _cmp_end((TQ, N_CMP), 1)
    ok_c = dist_c >= 0
    dist_cf = dist_c.astype(f32)
    jl = lax.broadcasted_iota(jnp.int32, (TQ, N_BLK), 1)
    jb = t_col >> 6
    ok_b = (jl << 6) <= t_col
    forced = jnp.where((jl == 0) | (jl == jb) | (jl == jb - 1), FORCE, 0.0)
    for g in range(KV_HEADS):
        gs = slice(g * HEAD_DIM, (g + 1) * HEAD_DIM)
        qa = jnp.concatenate([q_ref[0, :, (g * Q_PER_KV + r) * QA_W:(g * Q_PER_KV + r + 1) * QA_W]
                              for r in range(Q_PER_KV)], axis=0)
        qa_scr[g, :, 0:QA_W] = qa
        q4 = qa[:, 0:HEAD_DIM]
        s_all = _nt(q4, kc_scr[:, gs])
        ps = jnp.zeros((TQ, N_BLK), f32)
        probs = []
        for r in range(Q_PER_KV):
            s = s_all[r * TQ:(r + 1) * TQ] - SLOPES[g * Q_PER_KV + r] * dist_cf
            s = jnp.where(ok_c, s, NEG)
            m = jnp.max(s, axis=-1, keepdims=True)
            e = jnp.exp(s - m)
            p = jnp.where(ok_c, e * (1.0 / jnp.sum(e, axis=-1, keepdims=True)), 0.0)
            ps = ps + ((p[:, 0:N_BLK] + p[:, N_BLK:2 * N_BLK]) + (p[:, 2 * N_BLK:3 * N_BLK] + p[:, 3 * N_BLK:]))
            probs.append(p.astype(bf16))
        oc_scr[g] = _dot(jnp.concatenate(probs, axis=0), vc_scr[:, gs])
        score = jnp.where(ok_b, ps + forced, NEG)
        bias = _pick_blocks(score, N_SEL).astype(bf16)
        qa_scr[g, :, QA_W:] = jnp.concatenate([bias] * Q_PER_KV, axis=0)

    m_scr[...] = jnp.full(m_scr.shape, NEG, f32)
    l_scr[...] = jnp.zeros_like(l_scr)
    acc_scr[...] = jnp.zeros_like(acc_scr)

    def sel_tile(kt, diagonal):
        k0 = pl.multiple_of(kt * TK, TK)
        tile_off = (k0 - s0).astype(f32)
        expand = _block_expand(kt * (TK // SEL_BLK), TK)
        kt_tile = kst_ref[0, kt]
        v_tile = vs_ref[0, pl.ds(k0, TK), :]
        if diagonal:
            causal = (k0 + lax.broadcasted_iota(jnp.int32, (TQ, TK), 1)) <= t_col
        for g in range(KV_HEADS):
            gs = slice(g * HEAD_DIM, (g + 1) * HEAD_DIM)
            k_aug = jnp.concatenate([kt_tile[gs], pos_ref[...], expand], axis=0)
            s_all = _dot(qa_scr[g], k_aug)
            for r in range(Q_PER_KV):
                rows = slice(r * TQ, (r + 1) * TQ)
                s = s_all[rows]
                if diagonal:
                    s = jnp.where(causal, s, NEG)
                c = SLOPES[g * Q_PER_KV + r] * tile_off
                m_old = m_scr[g, rows]
                m_new = jnp.maximum(m_old, jnp.max(s, axis=-1, keepdims=True) + c)
                alpha = jnp.exp(m_old - m_new)
                p = jnp.exp(s - (m_new - c))
                l_scr[g, rows] = alpha * l_scr[g, rows] + jnp.sum(p, axis=-1, keepdims=True)
                acc_scr[g, rows] = alpha * acc_scr[g, rows] + _dot(p.astype(bf16), v_tile[:, gs])
                m_scr[g, rows] = m_new

    def sel_body(kt, carry):
        sel_tile(kt, False)
        return carry

    last = s0 // TK
    lax.fori_loop(0, last, sel_body, 0)
    sel_tile(last, True)

    w0 = pl.multiple_of(jnp.maximum(s0 - WINDOW, 0), TQ)
    dist_w = t_col - (w0 + lax.broadcasted_iota(jnp.int32, (TQ, WINDOW + TQ), 1))
    ok_w = (dist_w >= 0) & (dist_w < WINDOW)
    dist_wf = dist_w.astype(f32)
    kw = kw_ref[0, pl.ds(w0, WINDOW + TQ), :]
    vw = vw_ref[0, pl.ds(w0, WINDOW + TQ), :]
    gates = jax.nn.sigmoid(bg_ref[0])
    pieces = []
    for g in range(KV_HEADS):
        gs = slice(g * HEAD_DIM, (g + 1) * HEAD_DIM)
        s_all = _nt(qa_scr[g, :, 0:HEAD_DIM], kw[:, gs])
        probs = []
        for r in range(Q_PER_KV):
            s = jnp.where(ok_w, s_all[r * TQ:(r + 1) * TQ] - SLOPES[g * Q_PER_KV + r] * dist_wf, NEG)
            e = jnp.exp(s - jnp.max(s, axis=-1, keepdims=True))
            probs.append((e * (1.0 / jnp.sum(e, axis=-1, keepdims=True))).astype(bf16))
        o_w = _dot(jnp.concatenate(probs, axis=0), vw[:, gs])
        o_s = acc_scr[g] * (1.0 / l_scr[g])
        o_c = oc_scr[g]
        for r in range(Q_PER_KV):
            h = g * Q_PER_KV + r
            rows = slice(r * TQ, (r + 1) * TQ)
            pieces.append(gates[:, 3 * h:3 * h + 1] * o_c[rows] + gates[:, 3 * h + 1:3 * h + 2] * o_s[rows]
                          + gates[:, 3 * h + 2:3 * h + 3] * o_w[rows])
    y = jnp.concatenate(pieces, axis=-1)
    o_ref[0] = (y * _silu(ng_ref[0])).astype(bf16)


def _nsa_prompt(qa, z3, fk, sk, fv, sv, phik, phiv, kst, vsb, kwb, vwb):
    b, s, _ = qa.shape
    seq = lambda: pl.BlockSpec((1, s, KV_W), lambda i, t: (i, 0, 0))
    cmp = lambda: pl.BlockSpec((1, 4, N_CMP // 4, KV_W), lambda i, t: (i, 0, 0, 0))
    phi = lambda: pl.BlockSpec((KV_W, KV_W), lambda i, t: (0, 0))
    rows4 = Q_PER_KV * TQ
    pos = _pos_rows()
    return pl.pallas_call(
        _nsa_prompt_kernel,
        out_shape=jax.ShapeDtypeStruct((b, s, BRANCH_W), bf16),
        grid=(b, s // TQ),
        in_specs=[pl.BlockSpec((1, TQ, N_HEADS * QA_W), lambda i, t: (i, t, 0)),
                  pl.BlockSpec((1, TQ, BG_W), lambda i, t: (i, t, C_BG // BG_W)),
                  pl.BlockSpec((1, TQ, BRANCH_W), lambda i, t: (i, t, C_NSA_G // BRANCH_W)),
                  cmp(), cmp(), cmp(), cmp(), phi(), phi(),
                  pl.BlockSpec((1, s // TK, KV_W, TK), lambda i, t: (i, 0, 0, 0)),
                  seq(), seq(), seq(),
                  pl.BlockSpec(pos.shape, lambda i, t: (0, 0))],
        out_specs=pl.BlockSpec((1, TQ, BRANCH_W), lambda i, t: (i, t, 0)),
        scratch_shapes=[pltpu.VMEM((N_CMP, KV_W), bf16), pltpu.VMEM((N_CMP, KV_W), bf16),
                        pltpu.VMEM((KV_HEADS, rows4, QA_W + N_BLK), bf16),
                        pltpu.VMEM((KV_HEADS, rows4, HEAD_DIM), f32),
                        pltpu.VMEM((KV_HEADS, rows4, 1), f32),
                        pltpu.VMEM((KV_HEADS, rows4, 1), f32),
                        pltpu.VMEM((KV_HEADS, rows4, HEAD_DIM), f32)],
        compiler_params=_params(("parallel", "arbitrary")),
        name="nsa_prompt",
    )(qa, z3, z3, fk, sk, fv, sv, phik, phiv, kst, vsb, kwb, vwb, pos)


def _by_group(fn):
    hg = lax.broadcasted_iota(jnp.int32, (N_HEADS, 1), 0) >> 2
    out = fn(0)
    for g in range(1, KV_HEADS):
        out = jnp.where(hg == g, fn(g), out)
    return out


def _nsa_sample_kernel(past_len, tab_ref, q_ref, bg_ref, ng_ref, slope_ref,
                       fk_ref, sk_ref, fv_ref, sv_ref, phik_ref, phiv_ref,
                       ksn_ref, vsn_ref, kwn_ref, vwn_ref, bk_ref, bv_ref, *refs):
    del tab_ref
    kp_refs, vp_refs = refs[:PP], refs[PP:2 * PP]
    o_ref, q_scr, sel_scr, oc_scr, ow_scr, m_scr, l_scr, acc_scr = refs[2 * PP:]
    p = pl.program_id(1)
    slope = slope_ref[:, 0:1]
    gsl = lambda g: slice(g * HEAD_DIM, (g + 1) * HEAD_DIM)

    def per_head(row_ref):
        row = row_ref[0]
        return _by_group(lambda g: jnp.broadcast_to(row[:, gsl(g)], (N_HEADS, HEAD_DIM)))

    @pl.when(p == 0)
    def _():
        qrow = q_ref[0]
        q16 = jnp.concatenate([qrow[:, h * HEAD_DIM:(h + 1) * HEAD_DIM] for h in range(N_HEADS)], axis=0)
        q_scr[...] = q16
        q16f = q16.astype(f32)
        kc = _finish_compress(fk_ref, sk_ref, phik_ref)
        vc = _finish_compress(fv_ref, sv_ref, phiv_ref)

        dist_c = past_len - _cmp_end((1, N_CMP), 1)
        ok_c = dist_c >= 0
        s = _by_group(lambda g: _nt(q16, kc[:, gsl(g)])) - slope * dist_c.astype(f32)
        s = jnp.where(ok_c, s, NEG)
        e = jnp.exp(s - jnp.max(s, axis=-1, keepdims=True))
        pc = jnp.where(ok_c, e * (1.0 / jnp.sum(e, axis=-1, keepdims=True)), 0.0)
        pcb = pc.astype(bf16)
        oc_scr[...] = _by_group(lambda g: _dot(pcb, vc[:, gsl(g)]))
        ps16 = (pc[:, 0:N_BLK] + pc[:, N_BLK:2 * N_BLK]) + (pc[:, 2 * N_BLK:3 * N_BLK] + pc[:, 3 * N_BLK:])
        ps = jnp.concatenate([jnp.sum(ps16[g * Q_PER_KV:(g + 1) * Q_PER_KV], axis=0, keepdims=True)
                              for g in range(KV_HEADS)] + [jnp.zeros((8 - KV_HEADS, N_BLK), f32)], axis=0)
        jl = lax.broadcasted_iota(jnp.int32, (8, N_BLK), 1)
        jb = past_len // SEL_BLK
        forced = jnp.where((jl == 0) | (jl == jb) | (jl == jb - 1), FORCE, 0.0)
        bias8 = _pick_blocks(ps + forced, N_SEL - 1)
        sel_scr[...] = jnp.concatenate(
            [jnp.broadcast_to(bias8[g:g + 1], (Q_PER_KV, N_BLK)) for g in range(KV_HEADS)], axis=0).astype(bf16)

        wb = bk_ref.shape[-1]
        dist_w = wb - lax.broadcasted_iota(jnp.int32, (1, wb), 1)
        ok_w = (dist_w >= 0) & (dist_w < WINDOW)
        bk = bk_ref[0, 0].astype(bf16)
        bv = bv_ref[0, 0].astype(bf16)
        s_buf = _by_group(lambda g: _dot(q16, bk[gsl(g)])) - slope * dist_w.astype(f32)
        s_buf = jnp.where(ok_w, s_buf, NEG)
        s_new = jnp.sum(q16f * per_head(kwn_ref), axis=-1, keepdims=True)
        m_w = jnp.maximum(jnp.max(s_buf, axis=-1, keepdims=True), s_new)
        e_buf = jnp.exp(s_buf - m_w)
        e_new = jnp.exp(s_new - m_w)
        ebb = e_buf.astype(bf16)
        num = _by_group(lambda g: _nt(ebb, bv[gsl(g)])) + e_new * per_head(vwn_ref)
        ow_scr[...] = num * (1.0 / (jnp.sum(e_buf, axis=-1, keepdims=True) + e_new))

        m_scr[...] = jnp.sum(q16f * per_head(ksn_ref), axis=-1, keepdims=True)
        l_scr[...] = jnp.ones_like(l_scr)
        acc_scr[...] = per_head(vsn_ref)

    q16 = q_scr[...]
    n_keys = PP * PAGE_SIZE
    kp = jnp.concatenate([r[0, 0].astype(bf16) for r in kp_refs], axis=1)
    vp = jnp.concatenate([r[0, 0].astype(bf16) for r in vp_refs], axis=1)
    dist = past_len - (p * n_keys + lax.broadcasted_iota(jnp.int32, (1, n_keys), 1))
    expand = _block_expand(p * (n_keys // SEL_BLK), n_keys)
    s = _by_group(lambda g: _dot(q16, kp[gsl(g)])) - slope * dist.astype(f32) + _dot(sel_scr[...], expand)
    m_old = m_scr[...]
    m_new = jnp.maximum(m_old, jnp.max(s, axis=-1, keepdims=True))
    alpha = jnp.exp(m_old - m_new)
    pr = jnp.exp(s - m_new)
    prb = pr.astype(bf16)
    l_scr[...] = alpha * l_scr[...] + jnp.sum(pr, axis=-1, keepdims=True)
    acc_scr[...] = alpha * acc_scr[...] + _by_group(lambda g: _nt(prb, vp[gsl(g)]))
    m_scr[...] = m_new

    @pl.when(p == pl.num_programs(1) - 1)
    def _():
        o_s = acc_scr[...] * (1.0 / l_scr[...])
        gates = jax.nn.sigmoid(bg_ref[0])
        lane = lax.broadcasted_iota(jnp.int32, (N_HEADS, BG_W), 1)
        h3 = 3 * lax.broadcasted_iota(jnp.int32, (N_HEADS, BG_W), 0)
        gate = lambda n: jnp.sum(jnp.where(lane == h3 + n, gates, 0.0), axis=-1, keepdims=True)
        y16 = gate(0) * oc_scr[...] + gate(1) * o_s + gate(2) * ow_scr[...]
        y = jnp.concatenate([y16[h:h + 1, :] for h in range(N_HEADS)], axis=-1)
        o_ref[0] = (y * _silu(ng_ref[0])).astype(bf16)


def _nsa_sample(table, layer, qb, z2, slopes, fk, sk, fv, sv, phik, phiv, pool_k, pool_v, buf_k, buf_v, past_len):
    nb = qb.shape[0]
    npg = table.shape[0] // nb
    z3 = z2.reshape(nb, 1, IN_WP)
    tok = lambda w, c: pl.BlockSpec((1, 1, w), lambda b, p, tab: (b, 0, c))
    cmp = lambda: pl.BlockSpec((1, 4, N_CMP // 4, KV_W), lambda b, p, tab: (b, 0, 0, 0))
    phi = lambda: pl.BlockSpec((KV_W, KV_W), lambda b, p, tab: (0, 0))
    wb = buf_k.shape[-1]
    win = lambda: pl.BlockSpec((1, 1, KV_W, wb), lambda b, p, tab: (layer, b, 0, 0))
    page = lambda i: pl.BlockSpec((1, 1, KV_W, PAGE_SIZE),
                                  lambda b, p, tab: (layer, tab[b * npg + p * PP + i], 0, 0))
    kvc = C_KV // KV_W
    return pl.pallas_call(
        functools.partial(_nsa_sample_kernel, past_len),
        out_shape=jax.ShapeDtypeStruct((nb, 1, BRANCH_W), bf16),
        grid_spec=pltpu.PrefetchScalarGridSpec(
            num_scalar_prefetch=1,
            grid=(nb, npg // PP),
            in_specs=[tok(BRANCH_W, 0), tok(BG_W, C_BG // BG_W), tok(BRANCH_W, C_NSA_G // BRANCH_W),
                      pl.BlockSpec((N_HEADS, 128), lambda b, p, tab: (0, 0)),
                      cmp(), cmp(), cmp(), cmp(), phi(), phi(),
                      tok(KV_W, kvc + 2), tok(KV_W, kvc + 3), tok(KV_W, kvc + 4), tok(KV_W, kvc + 5),
                      win(), win()] + [page(i) for i in range(PP)] * 2,
            out_specs=pl.BlockSpec((1, 1, BRANCH_W), lambda b, p, tab: (b, 0, 0)),
            scratch_shapes=[pltpu.VMEM((N_HEADS, HEAD_DIM), bf16), pltpu.VMEM((N_HEADS, N_BLK), bf16),
                            pltpu.VMEM((N_HEADS, HEAD_DIM), f32), pltpu.VMEM((N_HEADS, HEAD_DIM), f32),
                            pltpu.VMEM((N_HEADS, 1), f32), pltpu.VMEM((N_HEADS, 1), f32),
                            pltpu.VMEM((N_HEADS, HEAD_DIM), f32)]),
        compiler_params=_params(("parallel", "arbitrary")),
        name="nsa_sample",
    )(table, qb.reshape(nb, 1, BRANCH_W), z3, z3, slopes, fk, sk, fv, sv, phik, phiv,
      z3, z3, z3, z3, buf_k, buf_v, *([pool_k] * PP), *([pool_v] * PP))


def _merge_kernel(zl_ref, zp_ref, zn_ref, m0_ref, m1_ref, m2_ref, wb_ref, wo_ref, g_ref, x_ref, y_ref):
    acc = None
    for n, (zz, mg) in enumerate(((zl_ref, m0_ref), (zp_ref, m1_ref), (zn_ref, m2_ref))):
        term = jax.nn.sigmoid(mg[...]) * _dot(zz[...], wb_ref[n])
        acc = term if acc is None else acc + term
    out = _dot(acc.astype(bf16), wo_ref[...])
    ms = jnp.mean(out * out, axis=-1, keepdims=True)
    y_ref[...] = x_ref[...] + out * lax.rsqrt(ms + EPS) * g_ref[...]


def _merge(zl, zp, zn, z2, wb, wo, g_row, x2d):
    n = x2d.shape[0]
    tm = min(n, 256)
    rowblk = lambda c: pl.BlockSpec((tm, D_MODEL), lambda i: (i, c))
    return pl.pallas_call(
        _merge_kernel,
        out_shape=jax.ShapeDtypeStruct((n, D_MODEL), f32),
        grid=(n // tm,),
        in_specs=[rowblk(0), rowblk(0), rowblk(0),
                  rowblk(C_MG // D_MODEL), rowblk(C_MG // D_MODEL + 1), rowblk(C_MG // D_MODEL + 2),
                  pl.BlockSpec((N_BRANCH, BRANCH_W, D_MODEL), lambda i: (0, 0, 0)),
                  pl.BlockSpec((D_MODEL, D_MODEL), lambda i: (0, 0)),
                  pl.BlockSpec((1, D_MODEL), lambda i: (0, 0)),
                  rowblk(0)],
        out_specs=rowblk(0),
        compiler_params=_params(("parallel",)),
        name="merge",
    )(zl, zp, zn, z2, z2, z2, wb, wo, g_row, x2d)


def _block_diag(w, per):
    n, d, _ = w.shape
    eye = jnp.eye(per, dtype=w.dtype)
    t = jnp.einsum('cpde,pq->cpdqe', w.reshape(n // per, per, d, d), eye)
    return t.reshape(n // per, per * d, per * d)


def _pack_w_in(w):
    old_kv = 6 * BRANCH_W
    old_bg = old_kv + 6 * KV_W
    old_mg = old_bg + N_BRANCH * N_HEADS
    parts = [w[:, :old_kv], w[:, old_mg:], w[:, old_kv:old_bg], w[:, old_bg:old_mg]]
    packed = jnp.concatenate(parts, axis=1)
    return jnp.pad(packed, ((0, 0), (0, IN_WP - packed.shape[1]))).astype(bf16)


def _tile_wpos(w_pos):
    halves = w_pos.reshape(2, CMP_STRIDE, HEAD_DIM)
    return jnp.tile(halves, (1, PAGE_SIZE // CMP_STRIDE, KV_HEADS))


def _tile_wpos_t(w_pos):
    halves = w_pos.reshape(2, CMP_STRIDE, HEAD_DIM).swapaxes(1, 2)
    return jnp.tile(halves, (1, KV_HEADS, PAGE_SIZE // CMP_STRIDE))


def _lanes_last(cache):
    d, n, rows = cache.shape[:3]
    return jnp.transpose(cache, (0, 1, 3, 4, 2)).reshape(d, n, KV_W, rows)


def _slope_parts():
    cols = np.zeros((N_HEADS, POS_ROWS), np.float32)
    rnd = lambda v: np.float32(np.float32(v).astype(bf16))
    for h, s in enumerate(SLOPES):
        s1 = rnd(s)
        s2 = rnd(np.float32(s) - s1)
        s3 = rnd(np.float32(s) - s1 - s2)
        cols[h, 0:6] = [s1, s2, s3, s1, s2, s3]
    return cols


def _pos_rows():
    rel = np.arange(TK)
    rows = np.zeros((POS_ROWS, TK), np.float32)
    rows[0:3] = (rel >> 4) << 4
    rows[3:6] = rel & 15
    return jnp.asarray(rows, bf16)


def _kv_seg(z, i):
    return z[..., C_KV + i * KV_W:C_KV + (i + 1) * KV_W]


def kernel(x_prompt, x_sample, cache_cmp_k, cache_cmp_v, cache_sel_k, cache_sel_v, cache_win_k, cache_win_v, state_conv, state_lru, state_pool, page_table, g_pre, g_post, w_in, conv_w, conv_b, w_rg_a, b_rg_a, w_rg_x, b_rg_x, lru_lambda, w_pool, pool_scale, cmp_pos_k, cmp_phi_k, cmp_pos_v, cmp_phi_v, w_branch, w_out):
    depth = w_in.shape[0]
    bp, seq, _ = x_prompt.shape
    bs = x_sample.shape[0]
    n_pages = page_table.shape[1]
    past_len = n_pages * PAGE_SIZE
    n_phys = cache_cmp_k.shape[1]
    assert seq == N_BLK * SEL_BLK and past_len == N_BLK * SEL_BLK and x_sample.shape[1] == 1
    wb = cache_win_k.shape[2]

    table_s = page_table.reshape(-1).astype(jnp.int32)
    table_p = jnp.arange(bp * (seq // PAGE_SIZE), dtype=jnp.int32)
    slopes = jnp.broadcast_to(jnp.asarray(SLOPES, f32)[:, None], (N_HEADS, 128))
    row = lambda v: v.reshape(1, -1)
    slope_cols = jnp.asarray(_slope_parts())
    cmp_kt, cmp_vt = _lanes_last(cache_cmp_k), _lanes_last(cache_cmp_v)
    sel_kt, sel_vt = _lanes_last(cache_sel_k), _lanes_last(cache_sel_v)
    win_kt, win_vt = _lanes_last(cache_win_k), _lanes_last(cache_win_v)

    xp = x_prompt.reshape(bp * seq, D_MODEL)
    xs = x_sample.reshape(bs, D_MODEL)
    pr = [[] for _ in range(9)]
    sm = [[] for _ in range(9)]
    for l in range(depth):
        w_packed = _pack_w_in(w_in[l])
        wa = _block_diag(w_rg_a[l], MXU_W // LRU_BD).astype(bf16)
        wx = _block_diag(w_rg_x[l], MXU_W // LRU_BD).astype(bf16)
        wp = w_pool[l].astype(bf16)
        phik = _block_diag(jnp.broadcast_to(cmp_phi_k[l], (KV_HEADS, HEAD_DIM, HEAD_DIM)), KV_HEADS)[0].astype(bf16)
        phiv = _block_diag(jnp.broadcast_to(cmp_phi_v[l], (KV_HEADS, HEAD_DIM, HEAD_DIM)), KV_HEADS)[0].astype(bf16)
        wpos_k = _tile_wpos(cmp_pos_k[l])
        wpos_v = _tile_wpos(cmp_pos_v[l])
        wbr = w_branch[l].astype(bf16)
        wo = w_out[l].astype(bf16)
        lru_w = (conv_w[l], row(conv_b[l]), wa, row(b_rg_a[l]), wx, row(b_rg_x[l]), row(lru_lambda[l]))

        z = _inproj(xp, row(g_pre[l]), w_packed)
        z3 = z.reshape(bp, seq, IN_WP)
        zl, conv_tail, h_p = _lru_prompt(z3, *lru_w)
        zpool, pool_tail = _pool_prompt(z3, wp, row(pool_scale[l]))
        zr = z.reshape(bp * seq // PAGE_SIZE, PAGE_SIZE, IN_WP)
        fk, sk, fv, sv = _compress(table_p, zr, zr, C_KV // KV_W, C_KV // KV_W + 1, wpos_k, wpos_v, bp)
        qh = (z3[..., C_Q:C_Q + BRANCH_W] * (HEAD_DIM ** -0.5)).reshape(bp, seq, N_HEADS, HEAD_DIM)
        qa = jnp.concatenate([qh, jnp.broadcast_to(slope_cols, (bp, seq, N_HEADS, POS_ROWS))], axis=-1)
        qa = qa.astype(bf16).reshape(bp, seq, N_HEADS * QA_W)
        kst = _kv_seg(z3, 2).astype(bf16).reshape(bp, seq // TK, TK, KV_W).swapaxes(2, 3)
        kvb = [_kv_seg(z3, i).astype(bf16) for i in range(3, 6)]
        zn = _nsa_prompt(qa, z3, fk, sk, fv, sv, phik, phiv, kst, *kvb)
        xp = _merge(zl.reshape(bp * seq, BRANCH_W), zpool.reshape(bp * seq, BRANCH_W),
                    zn.reshape(bp * seq, BRANCH_W), z, wbr, wo, row(g_post[l]), xp)
        kv_rows = [_kv_seg(z3, i).reshape(bp, seq, KV_HEADS, HEAD_DIM) for i in range(6)]
        wlen = min(WINDOW, seq)
        st_p = kv_rows[:4] + [kv_rows[4][:, -wlen:], kv_rows[5][:, -wlen:],
                              conv_tail[:, -(CONV_W - 1):], h_p[:, 0], pool_tail[:, -POOL_BUF:]]

        zs = _inproj(xs, row(g_pre[l]), w_packed)
        zls, zps, h_s = _mix_sample(zs, state_conv[l].swapaxes(0, 1), state_lru[l], state_pool[l].swapaxes(0, 1),
                                    *lru_w, wp, row(pool_scale[l]), past_len)
        fk, sk, fv, sv = _compress_t(table_s, l, cmp_kt, cmp_vt, _tile_wpos_t(cmp_pos_k[l]),
                                     _tile_wpos_t(cmp_pos_v[l]), bs)
        qs = (zs[:, C_Q:C_Q + BRANCH_W] * (HEAD_DIM ** -0.5)).astype(bf16)
        zns = _nsa_sample(table_s, l, qs, zs, slopes, fk, sk, fv, sv, phik, phiv,
                          sel_kt, sel_vt, win_kt, win_vt, past_len)
        xs = _merge(zls, zps, zns.reshape(bs, BRANCH_W), zs, wbr, wo, row(g_post[l]), xs)
        kv_new = [_kv_seg(zs, i).reshape(bs, 1, KV_HEADS, HEAD_DIM) for i in range(6)]
        st_s = kv_new[:4] + [jnp.concatenate([cache_win_k[l], kv_new[4]], axis=1)[:, -wb:],
                             jnp.concatenate([cache_win_v[l], kv_new[5]], axis=1)[:, -wb:],
                             jnp.concatenate([state_conv[l], zs[:, None, C_LRU_X:C_LRU_X + BRANCH_W]], axis=1)[:, -(CONV_W - 1):],
                             h_s,
                             jnp.concatenate([state_pool[l], zs[:, None, C_POOL_X:C_POOL_X + BRANCH_W]], axis=1)[:, -POOL_BUF:]]
        for i in range(9):
            pr[i].append(st_p[i])
            sm[i].append(st_s[i])

    out = [xp.reshape(bp, seq, D_MODEL), xs.reshape(bs, 1, D_MODEL)]
    for i in range(9):
        out += [jnp.stack(pr[i]), jnp.stack(sm[i])]
    return tuple(out)
```

```python
import functools

import numpy as np
import jax
import jax.numpy as jnp
from jax import lax
from jax.experimental import pallas as pl
from jax.experimental.pallas import tpu as pltpu

f32 = jnp.float32
bf16 = jnp.bfloat16

D_MODEL = 1024
BRANCH_W = 1024
N_BRANCH = 3
LRU_BLOCKS = 16
LRU_BD = BRANCH_W // LRU_BLOCKS
CONV_W = 4
LRU_C = 8.0
POOL_WINDOWS = (2, 4, 8, 16)
POOL_GD = BRANCH_W // len(POOL_WINDOWS)
POOL_BUF = max(POOL_WINDOWS) - 1
N_HEADS = 16
HEAD_DIM = 64
KV_HEADS = 4
Q_PER_KV = N_HEADS // KV_HEADS
KV_W = KV_HEADS * HEAD_DIM
CMP_STRIDE = 16
CMP_BLK = 2 * CMP_STRIDE
SEL_BLK = 64
N_SEL = 16
WINDOW = 512
PAGE_SIZE = 128
FORCE = 1e4
NEG = -1e30
EPS = 1e-6

C_LRU_X, C_LRU_G, C_POOL_X, C_POOL_G, C_Q, C_NSA_G, C_MG = 0, 1024, 2048, 3072, 4096, 5120, 6144
C_KV = C_MG + N_BRANCH * D_MODEL
C_BG = C_KV + 6 * KV_W
BG_W = 128
IN_WP = 11264
IN_TN = 1024

MXU_W = 256
VMEM_LIMIT = 56 * 1024 * 1024

N_CMP = 512
N_BLK = 128
TQ = 128
TK = 512
PP = 8
POS_ROWS = 64
QA_W = HEAD_DIM + POS_ROWS

SLOPES = [float(np.float32(2.0 ** (-8.0 * (h + 1) / N_HEADS))) for h in range(N_HEADS)]


def _nt(a, b):
    return lax.dot_general(a, b, (((1,), (1,)), ((), ())), preferred_element_type=f32)


def _dot(a, b):
    return jnp.dot(a, b, preferred_element_type=f32)


def _silu(x):
    return x * jax.nn.sigmoid(x)


def _params(sem):
    return pltpu.CompilerParams(dimension_semantics=sem, vmem_limit_bytes=VMEM_LIMIT)


def _inproj_kernel(x_ref, g_ref, w_ref, o_ref, u_ref):
    @pl.when(pl.program_id(1) == 0)
    def _():
        x = x_ref[...]
        ms = jnp.mean(x * x, axis=-1, keepdims=True)
        u_ref[...] = (x * lax.rsqrt(ms + EPS) * g_ref[...]).astype(bf16)

    o_ref[...] = _dot(u_ref[...], w_ref[...])


def _inproj(x2d, g_row, w_packed):
    n = x2d.shape[0]
    tm = min(n, 1024)
    return pl.pallas_call(
        _inproj_kernel,
        out_shape=jax.ShapeDtypeStruct((n, IN_WP), f32),
        grid=(n // tm, IN_WP // IN_TN),
        in_specs=[pl.BlockSpec((tm, D_MODEL), lambda i, j: (i, 0)),
                  pl.BlockSpec((1, D_MODEL), lambda i, j: (0, 0)),
                  pl.BlockSpec((D_MODEL, IN_TN), lambda i, j: (0, j))],
        out_specs=pl.BlockSpec((tm, IN_TN), lambda i, j: (i, j)),
        scratch_shapes=[pltpu.VMEM((tm, D_MODEL), bf16)],
        compiler_params=_params(("parallel", "arbitrary")),
        name="inproj",
    )(x2d, g_row, w_packed)


def _lru_gates(xc, wa_ref, ba_ref, wx_ref, bx_ref, lam_ref):
    xb = xc.astype(bf16)
    ra, ri = [], []
    for c in range(BRANCH_W // MXU_W):
        sl = slice(c * MXU_W, (c + 1) * MXU_W)
        ra.append(_dot(xb[:, sl], wa_ref[c]))
        ri.append(_dot(xb[:, sl], wx_ref[c]))
    r = jax.nn.sigmoid(jnp.concatenate(ra, axis=-1) + ba_ref[...])
    i = jax.nn.sigmoid(jnp.concatenate(ri, axis=-1) + bx_ref[...])
    nl = -lam_ref[...]
    softplus = jnp.maximum(nl, 0.0) + jnp.log1p(jnp.exp(-jnp.abs(nl)))
    log_a = -LRU_C * r * softplus
    a = jnp.exp(log_a)
    b = jnp.sqrt(1.0 - a * a) * (i * xc)
    return a, b


def _lru_kernel(x_ref, g_ref, cw_ref, cb_ref, wa_ref, ba_ref, wx_ref, bx_ref, lam_ref,
                zb_ref, tail_ref, h_ref, xs_ref, a_ref, b_ref, hc_ref):
    tt = x_ref.shape[1]

    @pl.when(pl.program_id(1) == 0)
    def _():
        xs_ref[0:8, :] = jnp.zeros((8, BRANCH_W), f32)
        hc_ref[...] = jnp.zeros_like(hc_ref)

    x = x_ref[0]
    xs_ref[8:, :] = x
    xc = cb_ref[...] + x * cw_ref[CONV_W - 1:CONV_W, :]
    for k in range(CONV_W - 1):
        xc = xc + xs_ref[pl.ds(8 - (CONV_W - 1 - k), tt), :] * cw_ref[k:k + 1, :]
    xs_ref[0:8, :] = x[tt - 8:, :]
    tail_ref[0] = x[tt - 8:, :]

    a, b = _lru_gates(xc, wa_ref, ba_ref, wx_ref, bx_ref, lam_ref)
    a_ref[...] = a
    b_ref[...] = b
    row = lax.broadcasted_iota(jnp.int32, (8, BRANCH_W), 0)

    def body(i, h):
        r0 = pl.multiple_of(i * 8, 8)
        av = a_ref[pl.ds(r0, 8), :]
        bv = b_ref[pl.ds(r0, 8), :]
        for d in (1, 2, 4):
            a_s = jnp.where(row >= d, pltpu.roll(av, d, 0), 1.0)
            b_s = jnp.where(row >= d, pltpu.roll(bv, d, 0), 0.0)
            bv = av * b_s + bv
            av = av * a_s
        hs = bv + av * h
        b_ref[pl.ds(r0, 8), :] = hs
        return hs[7:8, :]

    h = lax.fori_loop(0, tt // 8, body, hc_ref[...])
    hc_ref[...] = h
    h_ref[0] = h
    zb_ref[0] = (b_ref[...] * _silu(g_ref[0])).astype(bf16)


def _lru_prompt(z3, cw, cb, wa, ba, wx, bx, lam):
    b, s, _ = z3.shape
    tt = min(s, 512)
    row = lambda: pl.BlockSpec((1, BRANCH_W), lambda i, t: (0, 0))
    bd = lambda: pl.BlockSpec((BRANCH_W // MXU_W, MXU_W, MXU_W), lambda i, t: (0, 0, 0))
    return pl.pallas_call(
        _lru_kernel,
        out_shape=(jax.ShapeDtypeStruct((b, s, BRANCH_W), bf16),
                   jax.ShapeDtypeStruct((b, 8, BRANCH_W), f32),
                   jax.ShapeDtypeStruct((b, 1, BRANCH_W), f32)),
        grid=(b, s // tt),
        in_specs=[pl.BlockSpec((1, tt, BRANCH_W), lambda i, t: (i, t, C_LRU_X // BRANCH_W)),
                  pl.BlockSpec((1, tt, BRANCH_W), lambda i, t: (i, t, C_LRU_G // BRANCH_W)),
                  pl.BlockSpec((CONV_W, BRANCH_W), lambda i, t: (0, 0)),
                  row(), bd(), row(), bd(), row(), row()],
        out_specs=(pl.BlockSpec((1, tt, BRANCH_W), lambda i, t: (i, t, 0)),
                   pl.BlockSpec((1, 8, BRANCH_W), lambda i, t: (i, 0, 0)),
                   pl.BlockSpec((1, 1, BRANCH_W), lambda i, t: (i, 0, 0))),
        scratch_shapes=[pltpu.VMEM((tt + 8, BRANCH_W), f32), pltpu.VMEM((tt, BRANCH_W), f32),
                        pltpu.VMEM((tt, BRANCH_W), f32), pltpu.VMEM((1, BRANCH_W), f32)],
        compiler_params=_params(("parallel", "arbitrary")),
        name="lru_prompt",
    )(z3, z3, cw, cb, wa, ba, wx, bx, lam)


def _pool_kernel(x_ref, g_ref, wp_ref, sc_ref, zb_ref, tail_ref, xs_ref):
    tt = x_ref.shape[1]
    t = pl.program_id(1)

    @pl.when(t == 0)
    def _():
        xs_ref[0:16, :] = jnp.zeros((16, BRANCH_W), f32)

    x = x_ref[0]
    xs_ref[16:, :] = x
    pos1 = t * tt + 1 + lax.broadcasted_iota(jnp.int32, (tt, POOL_GD), 0)
    outs = []
    for gi, w in enumerate(POOL_WINDOWS):
        sl = slice(gi * POOL_GD, (gi + 1) * POOL_GD)
        s = xs_ref[:, sl]
        sh = 1
        while sh < w:
            s = s + pltpu.roll(s, sh, 0)
            sh *= 2
        cnt = jnp.minimum(w, pos1).astype(f32)
        pooled = s[16:, :] / cnt - x[:, sl]
        outs.append(_dot(pooled.astype(bf16), wp_ref[gi]))
    y = jnp.concatenate(outs, axis=-1) * sc_ref[...]
    zb_ref[0] = (y * _silu(g_ref[0])).astype(bf16)
    xs_ref[0:16, :] = x[tt - 16:, :]
    tail_ref[0] = x[tt - 16:, :]


def _pool_prompt(z3, wp, sc):
    b, s, _ = z3.shape
    tt = min(s, 512)
    return pl.pallas_call(
        _pool_kernel,
        out_shape=(jax.ShapeDtypeStruct((b, s, BRANCH_W), bf16),
                   jax.ShapeDtypeStruct((b, 16, BRANCH_W), f32)),
        grid=(b, s // tt),
        in_specs=[pl.BlockSpec((1, tt, BRANCH_W), lambda i, t: (i, t, C_POOL_X // BRANCH_W)),
                  pl.BlockSpec((1, tt, BRANCH_W), lambda i, t: (i, t, C_POOL_G // BRANCH_W)),
                  pl.BlockSpec((len(POOL_WINDOWS), POOL_GD, POOL_GD), lambda i, t: (0, 0, 0)),
                  pl.BlockSpec((1, BRANCH_W), lambda i, t: (0, 0))],
        out_specs=(pl.BlockSpec((1, tt, BRANCH_W), lambda i, t: (i, t, 0)),
                   pl.BlockSpec((1, 16, BRANCH_W), lambda i, t: (i, 0, 0))),
        scratch_shapes=[pltpu.VMEM((tt + 16, BRANCH_W), f32)],
        compiler_params=_params(("parallel", "arbitrary")),
        name="pool_prompt",
    )(z3, z3, wp, sc)


def _mix_sample_kernel(past_len, lx_ref, lg_ref, px_ref, pg_ref, conv_ref, h0_ref, pbuf_ref,
                       cw_ref, cb_ref, wa_ref, ba_ref, wx_ref, bx_ref, lam_ref, wp_ref, sc_ref,
                       zl_ref, zp_ref, h_ref):
    x = lx_ref[...]
    xc = cb_ref[...] + x * cw_ref[CONV_W - 1:CONV_W, :]
    for k in range(CONV_W - 1):
        xc = xc + conv_ref[k] * cw_ref[k:k + 1, :]
    a, b = _lru_gates(xc, wa_ref, ba_ref, wx_ref, bx_ref, lam_ref)
    h = a * h0_ref[...] + b
    h_ref[...] = h
    zl_ref[...] = (h * _silu(lg_ref[...])).astype(bf16)

    px = px_ref[...]
    outs = []
    for gi, w in enumerate(POOL_WINDOWS):
        sl = slice(gi * POOL_GD, (gi + 1) * POOL_GD)
        s = px[:, sl]
        for k in range(1, w):
            s = s + pbuf_ref[POOL_BUF - k][:, sl]
        cnt = float(min(w, past_len + 1))
        pooled = s / cnt - px[:, sl]
        outs.append(_dot(pooled.astype(bf16), wp_ref[gi]))
    y = jnp.concatenate(outs, axis=-1) * sc_ref[...]
    zp_ref[...] = (y * _silu(pg_ref[...])).astype(bf16)


def _mix_sample(z2, conv_t, h0, pbuf_t, cw, cb, wa, ba, wx, bx, lam, wp, sc, past_len):
    n = z2.shape[0]
    col = lambda c: pl.BlockSpec((n, BRANCH_W), lambda i: (0, c // BRANCH_W))
    full = lambda a: pl.BlockSpec(a.shape, lambda i: (0,) * a.ndim)
    args = (conv_t, h0, pbuf_t, cw, cb, wa, ba, wx, bx, lam, wp, sc)
    return pl.pallas_call(
        functools.partial(_mix_sample_kernel, past_len),
        out_shape=(jax.ShapeDtypeStruct((n, BRANCH_W), bf16), jax.ShapeDtypeStruct((n, BRANCH_W), bf16),
                   jax.ShapeDtypeStruct((n, BRANCH_W), f32)),
        grid=(1,),
        in_specs=[col(C_LRU_X), col(C_LRU_G), col(C_POOL_X), col(C_POOL_G)] + [full(a) for a in args],
        out_specs=(pl.BlockSpec((n, BRANCH_W), lambda i: (0, 0)),) * 3,
        compiler_params=_params(("arbitrary",)),
        name="mix_sample",
    )(z2, z2, z2, z2, *args)


def _compress_kernel(tab_ref, k_ref, v_ref, wk_ref, wv_ref, fk_ref, sk_ref, fv_ref, sv_ref):
    del tab_ref
    p = pl.program_id(1)
    chunks = PAGE_SIZE // CMP_STRIDE

    @pl.when(p == 0)
    def _():
        sk_ref[...] = jnp.zeros_like(sk_ref)
        sv_ref[...] = jnp.zeros_like(sv_ref)

    for src, w_ref, f_ref, s_ref in ((k_ref, wk_ref, fk_ref, sk_ref), (v_ref, wv_ref, fv_ref, sv_ref)):
        tile = src[0]
        first = jnp.sum((tile * w_ref[0]).reshape(chunks, CMP_STRIDE, KV_W), axis=1)
        second = jnp.sum((tile * w_ref[1]).reshape(chunks, CMP_STRIDE, KV_W), axis=1)
        for m in range(chunks):
            f_ref[0, m % 4, pl.ds(2 * p + m // 4, 1), :] = first[m:m + 1, :]
            if m >= 1:
                s_ref[0, (m - 1) % 4, pl.ds(2 * p + (m - 1) // 4, 1), :] = second[m:m + 1, :]
            else:
                @pl.when(p > 0)
                def _():
                    s_ref[0, 3, pl.ds(2 * p - 1, 1), :] = second[0:1, :]


def _compress(table, src_k, src_v, col_k, col_v, wk, wv, nb):
    npg = table.shape[0] // nb
    out = jax.ShapeDtypeStruct((nb, 4, N_CMP // 4, KV_W), f32)
    ospec = lambda: pl.BlockSpec((1, 4, N_CMP // 4, KV_W), lambda b, p, tab: (b, 0, 0, 0))
    return pl.pallas_call(
        _compress_kernel,
        out_shape=(out,) * 4,
        grid_spec=pltpu.PrefetchScalarGridSpec(
            num_scalar_prefetch=1,
            grid=(nb, npg),
            in_specs=[pl.BlockSpec((1, PAGE_SIZE, KV_W), lambda b, p, tab: (tab[b * npg + p], 0, col_k)),
                      pl.BlockSpec((1, PAGE_SIZE, KV_W), lambda b, p, tab: (tab[b * npg + p], 0, col_v)),
                      pl.BlockSpec((2, PAGE_SIZE, KV_W), lambda b, p, tab: (0, 0, 0)),
                      pl.BlockSpec((2, PAGE_SIZE, KV_W), lambda b, p, tab: (0, 0, 0))],
            out_specs=(ospec(), ospec(), ospec(), ospec())),
        compiler_params=_params(("parallel", "arbitrary")),
        name="compress",
    )(table, src_k, src_v, wk, wv)


def _chunk_maps():
    n_chunk = PP * PAGE_SIZE // CMP_STRIDE
    rows = n_chunk // 4
    chunk_of = np.arange(PP * PAGE_SIZE) // CMP_STRIDE
    first = np.zeros((n_chunk, PP * PAGE_SIZE), np.float32)
    second = np.zeros((n_chunk + 8, PP * PAGE_SIZE), np.float32)
    for i in range(4):
        for jj in range(rows):
            first[i * rows + jj] = chunk_of == 4 * jj + i
            second[i * rows + jj] = chunk_of == 4 * jj + i + 1
    second[n_chunk] = chunk_of == 0
    return jnp.asarray(first, bf16), jnp.asarray(second, bf16)


def _compress_t_kernel(tab_ref, *refs):
    del tab_ref
    k_refs, v_refs = refs[:PP], refs[PP:2 * PP]
    wk_ref, wv_ref, ea_ref, eb_ref, fk_ref, sk_ref, fv_ref, sv_ref = refs[2 * PP:]
    ps = pl.program_id(1)
    rows = PP * PAGE_SIZE // CMP_STRIDE // 4
    r0 = pl.multiple_of(ps * rows, rows)
    for pages, w_ref, f_ref, s_ref in ((k_refs, wk_ref, fk_ref, sk_ref), (v_refs, wv_ref, fv_ref, sv_ref)):
        a1 = jnp.concatenate([(r[0, 0] * w_ref[0]).astype(bf16) for r in pages], axis=1)
        a2 = jnp.concatenate([(r[0, 0] * w_ref[1]).astype(bf16) for r in pages], axis=1)
        first = _nt(ea_ref[...], a1)
        second = _nt(eb_ref[...], a2)
        for i in range(4):
            f_ref[0, i, pl.ds(r0, rows), :] = first[i * rows:(i + 1) * rows]
            s_ref[0, i, pl.ds(r0, rows), :] = second[i * rows:(i + 1) * rows]

        @pl.when(ps > 0)
        def _():
            s_ref[0, 3, pl.ds(r0 - 1, 1), :] = second[4 * rows:4 * rows + 1]


def _compress_t(table, layer, cache_k, cache_v, wk, wv, nb):
    npg = table.shape[0] // nb
    ea, eb = _chunk_maps()
    out = jax.ShapeDtypeStruct((nb, 4, N_CMP // 4, KV_W), f32)
    ospec = lambda: pl.BlockSpec((1, 4, N_CMP // 4, KV_W), lambda b, p, tab: (b, 0, 0, 0))
    page = lambda i: pl.BlockSpec((1, 1, KV_W, PAGE_SIZE),
                                  lambda b, p, tab: (layer, tab[b * npg + p * PP + i], 0, 0))
    full = lambda a: pl.BlockSpec(a.shape, lambda b, p, tab: (0,) * a.ndim)
    return pl.pallas_call(
        _compress_t_kernel,
        out_shape=(out,) * 4,
        grid_spec=pltpu.PrefetchScalarGridSpec(
            num_scalar_prefetch=1,
            grid=(nb, npg // PP),
            in_specs=[page(i) for i in range(PP)] * 2 + [full(wk), full(wv), full(ea), full(eb)],
            out_specs=(ospec(), ospec(), ospec(), ospec())),
        compiler_params=_params(("parallel", "arbitrary")),
        name="compress_t",
    )(table, *([cache_k] * PP), *([cache_v] * PP), wk, wv, ea, eb)


def _finish_compress(f_ref, s_ref, phi_ref):
    blk = (f_ref[0] + s_ref[0]).reshape(N_CMP, KV_W)
    return _dot(blk.astype(bf16), phi_ref[...]).astype(bf16)


def _cmp_end(shape, axis):
    col = lax.broadcasted_iota(jnp.int32, shape, axis)
    n = ((col & (N_BLK - 1)) << 2) + (col >> 7)
    return n * CMP_STRIDE + (CMP_BLK - 1)


def _pick_blocks(score, n_pick):
    lane = lax.broadcasted_iota(jnp.int32, score.shape, 1).astype(f32)
    bias = jnp.full(score.shape, NEG, f32)
    for _ in range(n_pick):
        m = jnp.max(score, axis=-1, keepdims=True)
        first = jnp.min(jnp.where(score == m, lane, float(N_BLK)), axis=-1, keepdims=True)
        hit = lane == first
        bias = jnp.where(hit, 0.0, bias)
        score = jnp.where(hit, -jnp.inf, score)
    return bias


def _block_expand(first_block, n_keys):
    j = lax.broadcasted_iota(jnp.int32, (N_BLK, n_keys), 0)
    c = lax.broadcasted_iota(jnp.int32, (N_BLK, n_keys), 1)
    return jnp.where(j == first_block + (c >> 6), 1.0, 0.0).astype(bf16)


CQ = Q_PER_KV * TQ
WIN_T = WINDOW // TQ + 1
N_PARTS = 9
TILE_COL = HEAD_DIM + 6


def _pick_blocks_t(score, n_pick):
    jrow = lax.broadcasted_iota(jnp.int32, score.shape, 0).astype(f32)
    bias = jnp.full(score.shape, NEG, f32)
    for _ in range(n_pick):
        m = jnp.max(score, axis=0, keepdims=True)
        first = jnp.min(jnp.where(score == m, jrow, float(N_BLK)), axis=0, keepdims=True)
        hit = jrow == first
        bias = jnp.where(hit, 0.0, bias)
        score = jnp.where(hit, -jnp.inf, score)
    return bias


def _nsa_prompt_kernel(qt_ref, srow_ref, slope_ref, bgt_ref, ng_ref, fk_ref, sk_ref, fv_ref, sv_ref,
                       phik_ref, phivt_ref, cpos_ref, ksa_ref, vst_ref, kwa_ref, vwt_ref, o_ref,
                       kca_scr, vct_scr, qa_scr, oc_scr, m_scr, l_scr, acc_scr):
    qi = pl.program_id(1)
    s0 = qi * TQ

    @pl.when(qi == 0)
    def _():
        kc = _finish_compress(fk_ref, sk_ref, phik_ref)
        for g in range(KV_HEADS):
            kca_scr[g] = jnp.concatenate([kc[:, g * HEAD_DIM:(g + 1) * HEAD_DIM], cpos_ref[...]], axis=1)
        blk_v = (fv_ref[0] + sv_ref[0]).reshape(N_CMP, KV_W).astype(bf16)
        vct_scr[...] = _nt(phivt_ref[...], blk_v).astype(bf16)

    t_of = lambda shape: s0 + (lax.broadcasted_iota(jnp.int32, shape, 1) & (TQ - 1))

    ok_c = _cmp_end((N_CMP, CQ), 0) <= t_of((N_CMP, CQ))
    t_q = t_of((N_BLK, TQ))
    jr = lax.broadcasted_iota(jnp.int32, (N_BLK, TQ), 0)
    jb = t_q >> 6
    ok_b = (jr << 6) <= t_q
    forced = jnp.where((jr == 0) | (jr == jb) | (jr == jb - 1), FORCE, 0.0)
    tile_any = jnp.full((N_BLK // 8, 1), NEG, f32)
    for g in range(KV_HEADS):
        qa = jnp.concatenate([qt_ref[0, 0, g], srow_ref[g]], axis=0)
        st = jnp.where(ok_c, _dot(kca_scr[g], qa), NEG)
        e = jnp.exp(st - jnp.max(st, axis=0, keepdims=True))
        pt = jnp.where(ok_c, e * (1.0 / jnp.sum(e, axis=0, keepdims=True)), 0.0)
        oc_scr[g] = _dot(vct_scr[g * HEAD_DIM:(g + 1) * HEAD_DIM], pt.astype(bf16))
        ps = None
        for i in range(4):
            for r in range(Q_PER_KV):
                slab = pt[i * N_BLK:(i + 1) * N_BLK, r * TQ:(r + 1) * TQ]
                ps = slab if ps is None else ps + slab
        bias = _pick_blocks_t(jnp.where(ok_b, ps + forced, NEG), N_SEL)
        qa_scr[g] = jnp.concatenate([qa, jnp.concatenate([bias.astype(bf16)] * Q_PER_KV, axis=1)], axis=0)
        any_t = jnp.max(bias.reshape(N_BLK // 8, 8, TQ), axis=1)
        tile_any = jnp.maximum(tile_any, jnp.max(any_t, axis=1, keepdims=True))
    kt_row = lax.broadcasted_iota(jnp.int32, tile_any.shape, 0)
    tile_bits = jnp.sum(jnp.where(tile_any == 0.0, 1 << kt_row, 0))

    m_scr[...] = jnp.full(m_scr.shape, NEG, f32)
    l_scr[...] = jnp.zeros_like(l_scr)
    acc_scr[...] = jnp.zeros_like(acc_scr)

    def sel_tile(kt, diagonal):
        k0 = pl.multiple_of(kt * TK, TK)
        tile_off = (k0 - s0).astype(f32)
        key_r = lax.broadcasted_iota(jnp.int32, (TK, N_BLK), 0)
        blk_c = lax.broadcasted_iota(jnp.int32, (TK, N_BLK), 1)
        expand = jnp.where(blk_c == kt * (TK // SEL_BLK) + (key_r >> 6), 1.0, 0.0).astype(bf16)
        k_aug = ksa_ref[0, pl.ds(k0, TK), :]
        v_t = vst_ref[0, kt]
        if diagonal:
            causal = (k0 + lax.broadcasted_iota(jnp.int32, (TK, CQ), 0)) <= t_of((TK, CQ))
        for g in range(KV_HEADS):
            lhs = jnp.concatenate([k_aug[:, g * QA_W:(g + 1) * QA_W], expand], axis=1)
            st = _dot(lhs, qa_scr[g])
            if diagonal:
                st = jnp.where(causal, st, NEG)
            c = slope_ref[g] * tile_off
            m_old = m_scr[g]
            m_new = jnp.maximum(m_old, jnp.max(st, axis=0, keepdims=True) + c)
            alpha = jnp.exp(m_old - m_new)
            pt = jnp.exp(st - (m_new - c))
            l_scr[g] = alpha * l_scr[g] + jnp.sum(pt, axis=0, keepdims=True)
            acc_scr[g] = alpha * acc_scr[g] + _dot(v_t[g * HEAD_DIM:(g + 1) * HEAD_DIM], pt.astype(bf16))
            m_scr[g] = m_new

    def sel_body(kt, carry):
        @pl.when(((tile_bits >> kt) & 1) == 1)
        def _():
            sel_tile(kt, False)
        return carry

    last = s0 // TK
    lax.fori_loop(0, last, sel_body, 0)
    sel_tile(last, True)

    w_tile = jnp.maximum(qi - WINDOW // TQ, 0)
    n_win = WIN_T * TQ
    dist_w = t_of((n_win, CQ)) - (w_tile * TQ + lax.broadcasted_iota(jnp.int32, (n_win, CQ), 0))
    ok_w = (dist_w >= 0) & (dist_w < WINDOW)
    lane_w = lax.broadcasted_iota(jnp.int32, (n_win, QA_W), 1)
    slab_off = ((lax.broadcasted_iota(jnp.int32, (n_win, QA_W), 0) >> 7) << 7).astype(f32).astype(bf16)
    in_tile_col = (lane_w >= TILE_COL) & (lane_w < TILE_COL + 3)
    kw_all = kwa_ref[0, pl.ds(pl.multiple_of(w_tile * TQ, TQ), n_win), :]
    vw_t = jnp.concatenate([vwt_ref[0, w_tile + i] for i in range(WIN_T)], axis=1)
    gates = jax.nn.sigmoid(bgt_ref[0])
    blocks = []
    for g in range(KV_HEADS):
        lhs = jnp.where(in_tile_col, slab_off, kw_all[:, g * QA_W:(g + 1) * QA_W])
        st = jnp.where(ok_w, _dot(lhs, qa_scr[g, 0:QA_W]), NEG)
        e = jnp.exp(st - jnp.max(st, axis=0, keepdims=True))
        pt = (e * (1.0 / jnp.sum(e, axis=0, keepdims=True))).astype(bf16)
        o_w = _dot(vw_t[g * HEAD_DIM:(g + 1) * HEAD_DIM], pt)
        o_s = acc_scr[g] * (1.0 / l_scr[g])
        o_c = oc_scr[g]
        heads = []
        for r in range(Q_PER_KV):
            h = g * Q_PER_KV + r
            cols = slice(r * TQ, (r + 1) * TQ)
            heads.append(gates[3 * h:3 * h + 1] * o_c[:, cols] + gates[3 * h + 1:3 * h + 2] * o_s[:, cols]
                         + gates[3 * h + 2:3 * h + 3] * o_w[:, cols])
        for r in range(0, Q_PER_KV, 2):
            blocks.append(jnp.concatenate(heads[r:r + 2], axis=0).T)
    y = jnp.concatenate(blocks, axis=-1)
    o_ref[0] = (y * _silu(ng_ref[0])).astype(bf16)


def _nsa_prompt(z3, fk, sk, fv, sv, phik, phivt):
    b, s, _ = z3.shape
    nq = s // TQ
    qh = (z3[..., C_Q:C_Q + BRANCH_W] * (HEAD_DIM ** -0.5)).astype(bf16)
    qt = qh.reshape(b, nq, TQ, KV_HEADS, Q_PER_KV, HEAD_DIM).transpose(0, 1, 3, 5, 4, 2).reshape(
        b, nq, KV_HEADS, HEAD_DIM, CQ)
    parts = _slope_parts().reshape(KV_HEADS, Q_PER_KV, POS_ROWS)
    srow = jnp.asarray(np.repeat(parts.transpose(0, 2, 1), TQ, axis=2), bf16)
    slope = jnp.asarray(np.repeat(np.asarray(SLOPES, np.float32).reshape(KV_HEADS, 1, Q_PER_KV), TQ, axis=2))
    bgt = z3[..., C_BG:C_BG + BG_W].swapaxes(1, 2)
    ksa = _key_aug(_kv_seg(z3, 2), TK)
    kwa = _key_aug(_kv_seg(z3, 4), TQ)
    vst = _kv_seg(z3, 3).astype(bf16).reshape(b, s // TK, TK, KV_W).swapaxes(2, 3)
    vwt = _kv_seg(z3, 5).astype(bf16).reshape(b, nq, TQ, KV_W).swapaxes(2, 3)
    cpos = _cmp_pos_cols()

    once = pl.Buffered(1)
    cmp = lambda: pl.BlockSpec((1, 4, N_CMP // 4, KV_W), lambda i, t: (i, 0, 0, 0))
    full = lambda a: pl.BlockSpec(a.shape, lambda i, t: (0,) * a.ndim)
    return pl.pallas_call(
        _nsa_prompt_kernel,
        out_shape=jax.ShapeDtypeStruct((b, s, BRANCH_W), bf16),
        grid=(b, nq),
        in_specs=[pl.BlockSpec((1, 1, KV_HEADS, HEAD_DIM, CQ), lambda i, t: (i, t, 0, 0, 0)),
                  full(srow), full(slope),
                  pl.BlockSpec((1, BG_W, TQ), lambda i, t: (i, 0, t)),
                  pl.BlockSpec((1, TQ, BRANCH_W), lambda i, t: (i, t, C_NSA_G // BRANCH_W)),
                  cmp(), cmp(), cmp(), cmp(), full(phik), full(phivt), full(cpos),
                  pl.BlockSpec((1, s, KV_HEADS * QA_W), lambda i, t: (i, 0, 0), pipeline_mode=once),
                  pl.BlockSpec((1, s // TK, KV_W, TK), lambda i, t: (i, 0, 0, 0), pipeline_mode=once),
                  pl.BlockSpec((1, s, KV_HEADS * QA_W), lambda i, t: (i, 0, 0), pipeline_mode=once),
                  pl.BlockSpec((1, nq, KV_W, TQ), lambda i, t: (i, 0, 0, 0), pipeline_mode=once)],
        out_specs=pl.BlockSpec((1, TQ, BRANCH_W), lambda i, t: (i, t, 0)),
        scratch_shapes=[pltpu.VMEM((KV_HEADS, N_CMP, QA_W), bf16), pltpu.VMEM((KV_W, N_CMP), bf16),
                        pltpu.VMEM((KV_HEADS, QA_W + N_BLK, CQ), bf16),
                        pltpu.VMEM((KV_HEADS, HEAD_DIM, CQ), f32),
                        pltpu.VMEM((KV_HEADS, 1, CQ), f32), pltpu.VMEM((KV_HEADS, 1, CQ), f32),
                        pltpu.VMEM((KV_HEADS, HEAD_DIM, CQ), f32)],
        compiler_params=_params(("parallel", "arbitrary")),
        name="nsa_prompt",
    )(qt, srow, slope, bgt, z3, fk, sk, fv, sv, phik, phivt, cpos, ksa, vst, kwa, vwt)


def _by_group(fn):
    hg = lax.broadcasted_iota(jnp.int32, (N_HEADS, 1), 0) >> 2
    out = fn(0)
    for g in range(1, KV_HEADS):
        out = jnp.where(hg == g, fn(g), out)
    return out


def _nsa_sample_kernel(past_len, tab_ref, q_ref, bg_ref, ng_ref, slope_ref,
                       fk_ref, sk_ref, fv_ref, sv_ref, phik_ref, phiv_ref,
                       ksn_ref, vsn_ref, kwn_ref, vwn_ref, bk_ref, bv_ref, *refs):
    del tab_ref
    kp_refs, vp_refs = refs[:PP], refs[PP:2 * PP]
    o_ref, q_scr, sel_scr, oc_scr, ow_scr, m_scr, l_scr, acc_scr = refs[2 * PP:]
    p = pl.program_id(1)
    slope = slope_ref[:, 0:1]
    gsl = lambda g: slice(g * HEAD_DIM, (g + 1) * HEAD_DIM)

    def per_head(row_ref):
        row = row_ref[0]
        return _by_group(lambda g: jnp.broadcast_to(row[:, gsl(g)], (N_HEADS, HEAD_DIM)))

    @pl.when(p == 0)
    def _():
        qrow = q_ref[0]
        q16 = jnp.concatenate([qrow[:, h * HEAD_DIM:(h + 1) * HEAD_DIM] for h in range(N_HEADS)], axis=0)
        q_scr[...] = q16
        q16f = q16.astype(f32)
        kc = _finish_compress(fk_ref, sk_ref, phik_ref)
        vc = _finish_compress(fv_ref, sv_ref, phiv_ref)

        dist_c = past_len - _cmp_end((1, N_CMP), 1)
        ok_c = dist_c >= 0
        s = _by_group(lambda g: _nt(q16, kc[:, gsl(g)])) - slope * dist_c.astype(f32)
        s = jnp.where(ok_c, s, NEG)
        e = jnp.exp(s - jnp.max(s, axis=-1, keepdims=True))
        pc = jnp.where(ok_c, e * (1.0 / jnp.sum(e, axis=-1, keepdims=True)), 0.0)
        pcb = pc.astype(bf16)
        oc_scr[...] = _by_group(lambda g: _dot(pcb, vc[:, gsl(g)]))
        ps16 = (pc[:, 0:N_BLK] + pc[:, N_BLK:2 * N_BLK]) + (pc[:, 2 * N_BLK:3 * N_BLK] + pc[:, 3 * N_BLK:])
        ps = jnp.concatenate([jnp.sum(ps16[g * Q_PER_KV:(g + 1) * Q_PER_KV], axis=0, keepdims=True)
                              for g in range(KV_HEADS)] + [jnp.zeros((8 - KV_HEADS, N_BLK), f32)], axis=0)
        jl = lax.broadcasted_iota(jnp.int32, (8, N_BLK), 1)
        jb = past_len // SEL_BLK
        forced = jnp.where((jl == 0) | (jl == jb) | (jl == jb - 1), FORCE, 0.0)
        bias8 = _pick_blocks(ps + forced, N_SEL - 1)
        sel_scr[...] = jnp.concatenate(
            [jnp.broadcast_to(bias8[g:g + 1], (Q_PER_KV, N_BLK)) for g in range(KV_HEADS)], axis=0).astype(bf16)

        wb = bk_ref.shape[-1]
        dist_w = wb - lax.broadcasted_iota(jnp.int32, (1, wb), 1)
        ok_w = (dist_w >= 0) & (dist_w < WINDOW)
        bk = bk_ref[0, 0].astype(bf16)
        bv = bv_ref[0, 0].astype(bf16)
        s_buf = _by_group(lambda g: _dot(q16, bk[gsl(g)])) - slope * dist_w.astype(f32)
        s_buf = jnp.where(ok_w, s_buf, NEG)
        s_new = jnp.sum(q16f * per_head(kwn_ref), axis=-1, keepdims=True)
        m_w = jnp.maximum(jnp.max(s_buf, axis=-1, keepdims=True), s_new)
        e_buf = jnp.exp(s_buf - m_w)
        e_new = jnp.exp(s_new - m_w)
        ebb = e_buf.astype(bf16)
        num = _by_group(lambda g: _nt(ebb, bv[gsl(g)])) + e_new * per_head(vwn_ref)
        ow_scr[...] = num * (1.0 / (jnp.sum(e_buf, axis=-1, keepdims=True) + e_new))

        m_scr[...] = jnp.sum(q16f * per_head(ksn_ref), axis=-1, keepdims=True)
        l_scr[...] = jnp.ones_like(l_scr)
        acc_scr[...] = per_head(vsn_ref)

    q16 = q_scr[...]
    n_keys = PP * PAGE_SIZE
    kp = jnp.concatenate([r[0, 0].astype(bf16) for r in kp_refs], axis=1)
    vp = jnp.concatenate([r[0, 0].astype(bf16) for r in vp_refs], axis=1)
    dist = past_len - (p * n_keys + lax.broadcasted_iota(jnp.int32, (1, n_keys), 1))
    expand = _block_expand(p * (n_keys // SEL_BLK), n_keys)
    s = _by_group(lambda g: _dot(q16, kp[gsl(g)])) - slope * dist.astype(f32) + _dot(sel_scr[...], expand)
    m_old = m_scr[...]
    m_new = jnp.maximum(m_old, jnp.max(s, axis=-1, keepdims=True))
    alpha = jnp.exp(m_old - m_new)
    pr = jnp.exp(s - m_new)
    prb = pr.astype(bf16)
    l_scr[...] = alpha * l_scr[...] + jnp.sum(pr, axis=-1, keepdims=True)
    acc_scr[...] = alpha * acc_scr[...] + _by_group(lambda g: _nt(prb, vp[gsl(g)]))
    m_scr[...] = m_new

    @pl.when(p == pl.num_programs(1) - 1)
    def _():
        o_s = acc_scr[...] * (1.0 / l_scr[...])
        gates = jax.nn.sigmoid(bg_ref[0])
        lane = lax.broadcasted_iota(jnp.int32, (N_HEADS, BG_W), 1)
        h3 = 3 * lax.broadcasted_iota(jnp.int32, (N_HEADS, BG_W), 0)
        gate = lambda n: jnp.sum(jnp.where(lane == h3 + n, gates, 0.0), axis=-1, keepdims=True)
        y16 = gate(0) * oc_scr[...] + gate(1) * o_s + gate(2) * ow_scr[...]
        y = jnp.concatenate([y16[h:h + 1, :] for h in range(N_HEADS)], axis=-1)
        o_ref[0] = (y * _silu(ng_ref[0])).astype(bf16)


def _nsa_sample(table, layer, qb, z2, slopes, fk, sk, fv, sv, phik, phiv, pool_k, pool_v, buf_k, buf_v, past_len):
    nb = qb.shape[0]
    npg = table.shape[0] // nb
    z3 = z2.reshape(nb, 1, IN_WP)
    tok = lambda w, c: pl.BlockSpec((1, 1, w), lambda b, p, tab: (b, 0, c))
    cmp = lambda: pl.BlockSpec((1, 4, N_CMP // 4, KV_W), lambda b, p, tab: (b, 0, 0, 0))
    phi = lambda: pl.BlockSpec((KV_W, KV_W), lambda b, p, tab: (0, 0))
    wb = buf_k.shape[-1]
    win = lambda: pl.BlockSpec((1, 1, KV_W, wb), lambda b, p, tab: (layer, b, 0, 0))
    page = lambda i: pl.BlockSpec((1, 1, KV_W, PAGE_SIZE),
                                  lambda b, p, tab: (layer, tab[b * npg + p * PP + i], 0, 0))
    kvc = C_KV // KV_W
    return pl.pallas_call(
        functools.partial(_nsa_sample_kernel, past_len),
        out_shape=jax.ShapeDtypeStruct((nb, 1, BRANCH_W), bf16),
        grid_spec=pltpu.PrefetchScalarGridSpec(
            num_scalar_prefetch=1,
            grid=(nb, npg // PP),
            in_specs=[tok(BRANCH_W, 0), tok(BG_W, C_BG // BG_W), tok(BRANCH_W, C_NSA_G // BRANCH_W),
                      pl.BlockSpec((N_HEADS, 128), lambda b, p, tab: (0, 0)),
                      cmp(), cmp(), cmp(), cmp(), phi(), phi(),
                      tok(KV_W, kvc + 2), tok(KV_W, kvc + 3), tok(KV_W, kvc + 4), tok(KV_W, kvc + 5),
                      win(), win()] + [page(i) for i in range(PP)] * 2,
            out_specs=pl.BlockSpec((1, 1, BRANCH_W), lambda b, p, tab: (b, 0, 0)),
            scratch_shapes=[pltpu.VMEM((N_HEADS, HEAD_DIM), bf16), pltpu.VMEM((N_HEADS, N_BLK), bf16),
                            pltpu.VMEM((N_HEADS, HEAD_DIM), f32), pltpu.VMEM((N_HEADS, HEAD_DIM), f32),
                            pltpu.VMEM((N_HEADS, 1), f32), pltpu.VMEM((N_HEADS, 1), f32),
                            pltpu.VMEM((N_HEADS, HEAD_DIM), f32)]),
        compiler_params=_params(("parallel", "arbitrary")),
        name="nsa_sample",
    )(table, qb.reshape(nb, 1, BRANCH_W), z3, z3, slopes, fk, sk, fv, sv, phik, phiv,
      z3, z3, z3, z3, buf_k, buf_v, *([pool_k] * PP), *([pool_v] * PP))


def _merge_kernel(zl_ref, zp_ref, zn_ref, m0_ref, m1_ref, m2_ref, wb_ref, wo_ref, g_ref, x_ref, y_ref):
    acc = None
    for n, (zz, mg) in enumerate(((zl_ref, m0_ref), (zp_ref, m1_ref), (zn_ref, m2_ref))):
        term = jax.nn.sigmoid(mg[...]) * _dot(zz[...], wb_ref[n])
        acc = term if acc is None else acc + term
    out = _dot(acc.astype(bf16), wo_ref[...])
    ms = jnp.mean(out * out, axis=-1, keepdims=True)
    y_ref[...] = x_ref[...] + out * lax.rsqrt(ms + EPS) * g_ref[...]


def _merge(zl, zp, zn, z2, wb, wo, g_row, x2d):
    n = x2d.shape[0]
    tm = min(n, 256)
    rowblk = lambda c: pl.BlockSpec((tm, D_MODEL), lambda i: (i, c))
    return pl.pallas_call(
        _merge_kernel,
        out_shape=jax.ShapeDtypeStruct((n, D_MODEL), f32),
        grid=(n // tm,),
        in_specs=[rowblk(0), rowblk(0), rowblk(0),
                  rowblk(C_MG // D_MODEL), rowblk(C_MG // D_MODEL + 1), rowblk(C_MG // D_MODEL + 2),
                  pl.BlockSpec((N_BRANCH, BRANCH_W, D_MODEL), lambda i: (0, 0, 0)),
                  pl.BlockSpec((D_MODEL, D_MODEL), lambda i: (0, 0)),
                  pl.BlockSpec((1, D_MODEL), lambda i: (0, 0)),
                  rowblk(0)],
        out_specs=rowblk(0),
        compiler_params=_params(("parallel",)),
        name="merge",
    )(zl, zp, zn, z2, z2, z2, wb, wo, g_row, x2d)


def _block_diag(w, per):
    n, d, _ = w.shape
    eye = jnp.eye(per, dtype=w.dtype)
    t = jnp.einsum('cpde,pq->cpdqe', w.reshape(n // per, per, d, d), eye)
    return t.reshape(n // per, per * d, per * d)


def _pack_w_in(w):
    old_kv = 6 * BRANCH_W
    old_bg = old_kv + 6 * KV_W
    old_mg = old_bg + N_BRANCH * N_HEADS
    parts = [w[:, :old_kv], w[:, old_mg:], w[:, old_kv:old_bg], w[:, old_bg:old_mg]]
    packed = jnp.concatenate(parts, axis=1)
    return jnp.pad(packed, ((0, 0), (0, IN_WP - packed.shape[1]))).astype(bf16)


def _tile_wpos(w_pos):
    halves = w_pos.reshape(2, CMP_STRIDE, HEAD_DIM)
    return jnp.tile(halves, (1, PAGE_SIZE // CMP_STRIDE, KV_HEADS))


def _tile_wpos_t(w_pos):
    halves = w_pos.reshape(2, CMP_STRIDE, HEAD_DIM).swapaxes(1, 2)
    return jnp.tile(halves, (1, KV_HEADS, PAGE_SIZE // CMP_STRIDE))


def _lanes_last(cache):
    d, n, rows = cache.shape[:3]
    return jnp.transpose(cache, (0, 1, 3, 4, 2)).reshape(d, n, KV_W, rows)


def _slope_parts():
    cols = np.zeros((N_HEADS, POS_ROWS), np.float32)
    rnd = lambda v: np.float32(np.float32(v).astype(bf16))
    for h, s in enumerate(SLOPES):
        s1 = rnd(s)
        s2 = rnd(np.float32(s) - s1)
        s3 = rnd(np.float32(s) - s1 - s2)
        cols[h, 0:N_PARTS] = [s1, s2, s3] * 3
    return cols


def _split_pos(pos, shift):
    cols = np.zeros((pos.shape[0], POS_ROWS), np.float32)
    cols[:, 0:3] = ((pos >> shift) << shift)[:, None]
    cols[:, 3:6] = (pos & ((1 << shift) - 1))[:, None]
    return cols


def _key_aug(k_rows, tile):
    b, s, _ = k_rows.shape
    pos = jnp.asarray(_split_pos(np.arange(s) % tile, 4), bf16)
    kg = k_rows.astype(bf16).reshape(b, s, KV_HEADS, HEAD_DIM)
    posb = jnp.broadcast_to(pos[None, :, None, :], (b, s, KV_HEADS, POS_ROWS))
    return jnp.concatenate([kg, posb], axis=-1).reshape(b, s, KV_HEADS * QA_W)


def _cmp_pos_cols():
    slot = np.arange(N_CMP)
    n = ((slot & (N_BLK - 1)) << 2) + (slot >> 7)
    return jnp.asarray(_split_pos(n * CMP_STRIDE + CMP_BLK - 1, 8), bf16)


def _kv_seg(z, i):
    return z[..., C_KV + i * KV_W:C_KV + (i + 1) * KV_W]


def kernel(x_prompt, x_sample, cache_cmp_k, cache_cmp_v, cache_sel_k, cache_sel_v, cache_win_k, cache_win_v, state_conv, state_lru, state_pool, page_table, g_pre, g_post, w_in, conv_w, conv_b, w_rg_a, b_rg_a, w_rg_x, b_rg_x, lru_lambda, w_pool, pool_scale, cmp_pos_k, cmp_phi_k, cmp_pos_v, cmp_phi_v, w_branch, w_out):
    depth = w_in.shape[0]
    bp, seq, _ = x_prompt.shape
    bs = x_sample.shape[0]
    n_pages = page_table.shape[1]
    past_len = n_pages * PAGE_SIZE
    n_phys = cache_cmp_k.shape[1]
    assert seq == N_BLK * SEL_BLK and past_len == N_BLK * SEL_BLK and x_sample.shape[1] == 1
    wb = cache_win_k.shape[2]

    table_s = page_table.reshape(-1).astype(jnp.int32)
    table_p = jnp.arange(bp * (seq // PAGE_SIZE), dtype=jnp.int32)
    slopes = jnp.broadcast_to(jnp.asarray(SLOPES, f32)[:, None], (N_HEADS, 128))
    row = lambda v: v.reshape(1, -1)
    cmp_kt, cmp_vt = _lanes_last(cache_cmp_k), _lanes_last(cache_cmp_v)
    sel_kt, sel_vt = _lanes_last(cache_sel_k), _lanes_last(cache_sel_v)
    win_kt, win_vt = _lanes_last(cache_win_k), _lanes_last(cache_win_v)

    xp = x_prompt.reshape(bp * seq, D_MODEL)
    xs = x_sample.reshape(bs, D_MODEL)
    pr = [[] for _ in range(9)]
    sm = [[] for _ in range(9)]
    for l in range(depth):
        w_packed = _pack_w_in(w_in[l])
        wa = _block_diag(w_rg_a[l], MXU_W // LRU_BD).astype(bf16)
        wx = _block_diag(w_rg_x[l], MXU_W // LRU_BD).astype(bf16)
        wp = w_pool[l].astype(bf16)
        phik = _block_diag(jnp.broadcast_to(cmp_phi_k[l], (KV_HEADS, HEAD_DIM, HEAD_DIM)), KV_HEADS)[0].astype(bf16)
        phiv = _block_diag(jnp.broadcast_to(cmp_phi_v[l], (KV_HEADS, HEAD_DIM, HEAD_DIM)), KV_HEADS)[0].astype(bf16)
        wpos_k = _tile_wpos(cmp_pos_k[l])
        wpos_v = _tile_wpos(cmp_pos_v[l])
        wbr = w_branch[l].astype(bf16)
        wo = w_out[l].astype(bf16)
        lru_w = (conv_w[l], row(conv_b[l]), wa, row(b_rg_a[l]), wx, row(b_rg_x[l]), row(lru_lambda[l]))

        z = _inproj(xp, row(g_pre[l]), w_packed)
        z3 = z.reshape(bp, seq, IN_WP)
        zl, conv_tail, h_p = _lru_prompt(z3, *lru_w)
        zpool, pool_tail = _pool_prompt(z3, wp, row(pool_scale[l]))
        zr = z.reshape(bp * seq // PAGE_SIZE, PAGE_SIZE, IN_WP)
        fk, sk, fv, sv = _compress(table_p, zr, zr, C_KV // KV_W, C_KV // KV_W + 1, wpos_k, wpos_v, bp)
        zn = _nsa_prompt(z3, fk, sk, fv, sv, phik, phiv.T)
        xp = _merge(zl.reshape(bp * seq, BRANCH_W), zpool.reshape(bp * seq, BRANCH_W),
                    zn.reshape(bp * seq, BRANCH_W), z, wbr, wo, row(g_post[l]), xp)
        kv_rows = [_kv_seg(z3, i).reshape(bp, seq, KV_HEADS, HEAD_DIM) for i in range(6)]
        wlen = min(WINDOW, seq)
        st_p = kv_rows[:4] + [kv_rows[4][:, -wlen:], kv_rows[5][:, -wlen:],
                              conv_tail[:, -(CONV_W - 1):], h_p[:, 0], pool_tail[:, -POOL_BUF:]]

        zs = _inproj(xs, row(g_pre[l]), w_packed)
        zls, zps, h_s = _mix_sample(zs, state_conv[l].swapaxes(0, 1), state_lru[l], state_pool[l].swapaxes(0, 1),
                                    *lru_w, wp, row(pool_scale[l]), past_len)
        fk, sk, fv, sv = _compress_t(table_s, l, cmp_kt, cmp_vt, _tile_wpos_t(cmp_pos_k[l]),
                                     _tile_wpos_t(cmp_pos_v[l]), bs)
        qs = (zs[:, C_Q:C_Q + BRANCH_W] * (HEAD_DIM ** -0.5)).astype(bf16)
        zns = _nsa_sample(table_s, l, qs, zs, slopes, fk, sk, fv, sv, phik, phiv,
                          sel_kt, sel_vt, win_kt, win_vt, past_len)
        xs = _merge(zls, zps, zns.reshape(bs, BRANCH_W), zs, wbr, wo, row(g_post[l]), xs)
        kv_new = [_kv_seg(zs, i).reshape(bs, 1, KV_HEADS, HEAD_DIM) for i in range(6)]
        st_s = kv_new[:4] + [jnp.concatenate([cache_win_k[l], kv_new[4]], axis=1)[:, -wb:],
                             jnp.concatenate([cache_win_v[l], kv_new[5]], axis=1)[:, -wb:],
                             jnp.concatenate([state_conv[l], zs[:, None, C_LRU_X:C_LRU_X + BRANCH_W]], axis=1)[:, -(CONV_W - 1):],
                             h_s,
                             jnp.concatenate([state_pool[l], zs[:, None, C_POOL_X:C_POOL_X + BRANCH_W]], axis=1)[:, -POOL_BUF:]]
        for i in range(9):
            pr[i].append(st_p[i])
            sm[i].append(st_s[i])

    out = [xp.reshape(bp, seq, D_MODEL), xs.reshape(bs, 1, D_MODEL)]
    for i in range(9):
        out += [jnp.stack(pr[i]), jnp.stack(sm[i])]
    return tuple(out)
```

```python
import functools

import numpy as np
import jax
import jax.numpy as jnp
from jax import lax
from jax.experimental import pallas as pl
from jax.experimental.pallas import tpu as pltpu

f32 = jnp.float32
bf16 = jnp.bfloat16

D_MODEL = 1024
BRANCH_W = 1024
N_BRANCH = 3
LRU_BLOCKS = 16
LRU_BD = BRANCH_W // LRU_BLOCKS
CONV_W = 4
LRU_C = 8.0
POOL_WINDOWS = (2, 4, 8, 16)
POOL_GD = BRANCH_W // len(POOL_WINDOWS)
POOL_BUF = max(POOL_WINDOWS) - 1
N_HEADS = 16
HEAD_DIM = 64
KV_HEADS = 4
Q_PER_KV = N_HEADS // KV_HEADS
KV_W = KV_HEADS * HEAD_DIM
CMP_STRIDE = 16
CMP_BLK = 2 * CMP_STRIDE
SEL_BLK = 64
N_SEL = 16
WINDOW = 512
PAGE_SIZE = 128
FORCE = 1e4
NEG = -1e30
EPS = 1e-6

C_LRU_X, C_LRU_G, C_POOL_X, C_POOL_G, C_Q, C_NSA_G, C_MG = 0, 1024, 2048, 3072, 4096, 5120, 6144
C_KV = C_MG + N_BRANCH * D_MODEL
C_BG = C_KV + 6 * KV_W
BG_W = 128
IN_WP = 11264
IN_TN = 1024

MXU_W = 256
VMEM_LIMIT = 56 * 1024 * 1024

N_CMP = 512
N_BLK = 128
TQ = 256
TK = 512
PP = 8
PP_SEL = 16
POS_ROWS = 64
QA_W = HEAD_DIM + POS_ROWS

SLOPES = [float(np.float32(2.0 ** (-8.0 * (h + 1) / N_HEADS))) for h in range(N_HEADS)]


def _nt(a, b):
    return lax.dot_general(a, b, (((1,), (1,)), ((), ())), preferred_element_type=f32)


def _dot(a, b):
    return jnp.dot(a, b, preferred_element_type=f32)


def _silu(x):
    return x * jax.nn.sigmoid(x)


def _params(sem):
    return pltpu.CompilerParams(dimension_semantics=sem, vmem_limit_bytes=VMEM_LIMIT)


def _inproj_kernel(x_ref, g_ref, w_ref, o_ref, u_ref):
    @pl.when(pl.program_id(1) == 0)
    def _():
        x = x_ref[...]
        ms = jnp.mean(x * x, axis=-1, keepdims=True)
        u_ref[...] = (x * lax.rsqrt(ms + EPS) * g_ref[...]).astype(bf16)

    o_ref[...] = _dot(u_ref[...], w_ref[...])


def _inproj(x2d, g_row, w_packed):
    n = x2d.shape[0]
    tm = min(n, 1024)
    return pl.pallas_call(
        _inproj_kernel,
        out_shape=jax.ShapeDtypeStruct((n, IN_WP), f32),
        grid=(n // tm, IN_WP // IN_TN),
        in_specs=[pl.BlockSpec((tm, D_MODEL), lambda i, j: (i, 0)),
                  pl.BlockSpec((1, D_MODEL), lambda i, j: (0, 0)),
                  pl.BlockSpec((D_MODEL, IN_TN), lambda i, j: (0, j))],
        out_specs=pl.BlockSpec((tm, IN_TN), lambda i, j: (i, j)),
        scratch_shapes=[pltpu.VMEM((tm, D_MODEL), bf16)],
        compiler_params=_params(("parallel", "arbitrary")),
        name="inproj",
    )(x2d, g_row, w_packed)


def _lru_gates(xc, wa_ref, ba_ref, wx_ref, bx_ref, lam_ref):
    xb = xc.astype(bf16)
    ra, ri = [], []
    for c in range(BRANCH_W // MXU_W):
        sl = slice(c * MXU_W, (c + 1) * MXU_W)
        ra.append(_dot(xb[:, sl], wa_ref[c]))
        ri.append(_dot(xb[:, sl], wx_ref[c]))
    r = jax.nn.sigmoid(jnp.concatenate(ra, axis=-1) + ba_ref[...])
    i = jax.nn.sigmoid(jnp.concatenate(ri, axis=-1) + bx_ref[...])
    nl = -lam_ref[...]
    softplus = jnp.maximum(nl, 0.0) + jnp.log1p(jnp.exp(-jnp.abs(nl)))
    log_a = -LRU_C * r * softplus
    a = jnp.exp(log_a)
    b = jnp.sqrt(1.0 - a * a) * (i * xc)
    return a, b


def _lru_kernel(x_ref, g_ref, cw_ref, cb_ref, wa_ref, ba_ref, wx_ref, bx_ref, lam_ref,
                zb_ref, tail_ref, h_ref, xs_ref, a_ref, b_ref, hc_ref):
    tt = x_ref.shape[1]

    @pl.when(pl.program_id(1) == 0)
    def _():
        xs_ref[0:8, :] = jnp.zeros((8, BRANCH_W), f32)
        hc_ref[...] = jnp.zeros_like(hc_ref)

    x = x_ref[0]
    xs_ref[8:, :] = x
    xc = cb_ref[...] + x * cw_ref[CONV_W - 1:CONV_W, :]
    for k in range(CONV_W - 1):
        xc = xc + xs_ref[pl.ds(8 - (CONV_W - 1 - k), tt), :] * cw_ref[k:k + 1, :]
    xs_ref[0:8, :] = x[tt - 8:, :]
    tail_ref[0] = x[tt - 8:, :]

    a, b = _lru_gates(xc, wa_ref, ba_ref, wx_ref, bx_ref, lam_ref)
    a_ref[...] = a
    b_ref[...] = b
    row = lax.broadcasted_iota(jnp.int32, (8, BRANCH_W), 0)

    def body(i, h):
        r0 = pl.multiple_of(i * 8, 8)
        av = a_ref[pl.ds(r0, 8), :]
        bv = b_ref[pl.ds(r0, 8), :]
        for d in (1, 2, 4):
            a_s = jnp.where(row >= d, pltpu.roll(av, d, 0), 1.0)
            b_s = jnp.where(row >= d, pltpu.roll(bv, d, 0), 0.0)
            bv = av * b_s + bv
            av = av * a_s
        hs = bv + av * h
        b_ref[pl.ds(r0, 8), :] = hs
        return hs[7:8, :]

    h = lax.fori_loop(0, tt // 8, body, hc_ref[...])
    hc_ref[...] = h
    h_ref[0] = h
    zb_ref[0] = (b_ref[...] * _silu(g_ref[0])).astype(bf16)


def _lru_prompt(z3, cw, cb, wa, ba, wx, bx, lam):
    b, s, _ = z3.shape
    tt = min(s, 512)
    row = lambda: pl.BlockSpec((1, BRANCH_W), lambda i, t: (0, 0))
    bd = lambda: pl.BlockSpec((BRANCH_W // MXU_W, MXU_W, MXU_W), lambda i, t: (0, 0, 0))
    return pl.pallas_call(
        _lru_kernel,
        out_shape=(jax.ShapeDtypeStruct((b, s, BRANCH_W), bf16),
                   jax.ShapeDtypeStruct((b, 8, BRANCH_W), f32),
                   jax.ShapeDtypeStruct((b, 1, BRANCH_W), f32)),
        grid=(b, s // tt),
        in_specs=[pl.BlockSpec((1, tt, BRANCH_W), lambda i, t: (i, t, C_LRU_X // BRANCH_W)),
                  pl.BlockSpec((1, tt, BRANCH_W), lambda i, t: (i, t, C_LRU_G // BRANCH_W)),
                  pl.BlockSpec((CONV_W, BRANCH_W), lambda i, t: (0, 0)),
                  row(), bd(), row(), bd(), row(), row()],
        out_specs=(pl.BlockSpec((1, tt, BRANCH_W), lambda i, t: (i, t, 0)),
                   pl.BlockSpec((1, 8, BRANCH_W), lambda i, t: (i, 0, 0)),
                   pl.BlockSpec((1, 1, BRANCH_W), lambda i, t: (i, 0, 0))),
        scratch_shapes=[pltpu.VMEM((tt + 8, BRANCH_W), f32), pltpu.VMEM((tt, BRANCH_W), f32),
                        pltpu.VMEM((tt, BRANCH_W), f32), pltpu.VMEM((1, BRANCH_W), f32)],
        compiler_params=_params(("parallel", "arbitrary")),
        name="lru_prompt",
    )(z3, z3, cw, cb, wa, ba, wx, bx, lam)


def _pool_kernel(x_ref, g_ref, wp_ref, sc_ref, zb_ref, tail_ref, xs_ref):
    tt = x_ref.shape[1]
    t = pl.program_id(1)

    @pl.when(t == 0)
    def _():
        xs_ref[0:16, :] = jnp.zeros((16, BRANCH_W), f32)

    x = x_ref[0]
    xs_ref[16:, :] = x
    pos1 = t * tt + 1 + lax.broadcasted_iota(jnp.int32, (tt, POOL_GD), 0)
    outs = []
    for gi, w in enumerate(POOL_WINDOWS):
        sl = slice(gi * POOL_GD, (gi + 1) * POOL_GD)
        s = xs_ref[:, sl]
        sh = 1
        while sh < w:
            s = s + pltpu.roll(s, sh, 0)
            sh *= 2
        cnt = jnp.minimum(w, pos1).astype(f32)
        pooled = s[16:, :] / cnt - x[:, sl]
        outs.append(_dot(pooled.astype(bf16), wp_ref[gi]))
    y = jnp.concatenate(outs, axis=-1) * sc_ref[...]
    zb_ref[0] = (y * _silu(g_ref[0])).astype(bf16)
    xs_ref[0:16, :] = x[tt - 16:, :]
    tail_ref[0] = x[tt - 16:, :]


def _pool_prompt(z3, wp, sc):
    b, s, _ = z3.shape
    tt = min(s, 512)
    return pl.pallas_call(
        _pool_kernel,
        out_shape=(jax.ShapeDtypeStruct((b, s, BRANCH_W), bf16),
                   jax.ShapeDtypeStruct((b, 16, BRANCH_W), f32)),
        grid=(b, s // tt),
        in_specs=[pl.BlockSpec((1, tt, BRANCH_W), lambda i, t: (i, t, C_POOL_X // BRANCH_W)),
                  pl.BlockSpec((1, tt, BRANCH_W), lambda i, t: (i, t, C_POOL_G // BRANCH_W)),
                  pl.BlockSpec((len(POOL_WINDOWS), POOL_GD, POOL_GD), lambda i, t: (0, 0, 0)),
                  pl.BlockSpec((1, BRANCH_W), lambda i, t: (0, 0))],
        out_specs=(pl.BlockSpec((1, tt, BRANCH_W), lambda i, t: (i, t, 0)),
                   pl.BlockSpec((1, 16, BRANCH_W), lambda i, t: (i, 0, 0))),
        scratch_shapes=[pltpu.VMEM((tt + 16, BRANCH_W), f32)],
        compiler_params=_params(("parallel", "arbitrary")),
        name="pool_prompt",
    )(z3, z3, wp, sc)


def _mix_sample_kernel(past_len, lx_ref, lg_ref, px_ref, pg_ref, conv_ref, h0_ref, pbuf_ref,
                       cw_ref, cb_ref, wa_ref, ba_ref, wx_ref, bx_ref, lam_ref, wp_ref, sc_ref,
                       zl_ref, zp_ref, h_ref):
    x = lx_ref[...]
    xc = cb_ref[...] + x * cw_ref[CONV_W - 1:CONV_W, :]
    for k in range(CONV_W - 1):
        xc = xc + conv_ref[k] * cw_ref[k:k + 1, :]
    a, b = _lru_gates(xc, wa_ref, ba_ref, wx_ref, bx_ref, lam_ref)
    h = a * h0_ref[...] + b
    h_ref[...] = h
    zl_ref[...] = (h * _silu(lg_ref[...])).astype(bf16)

    px = px_ref[...]
    outs = []
    for gi, w in enumerate(POOL_WINDOWS):
        sl = slice(gi * POOL_GD, (gi + 1) * POOL_GD)
        s = px[:, sl]
        for k in range(1, w):
            s = s + pbuf_ref[POOL_BUF - k][:, sl]
        cnt = float(min(w, past_len + 1))
        pooled = s / cnt - px[:, sl]
        outs.append(_dot(pooled.astype(bf16), wp_ref[gi]))
    y = jnp.concatenate(outs, axis=-1) * sc_ref[...]
    zp_ref[...] = (y * _silu(pg_ref[...])).astype(bf16)


def _mix_sample(z2, conv_t, h0, pbuf_t, cw, cb, wa, ba, wx, bx, lam, wp, sc, past_len):
    n = z2.shape[0]
    col = lambda c: pl.BlockSpec((n, BRANCH_W), lambda i: (0, c // BRANCH_W))
    full = lambda a: pl.BlockSpec(a.shape, lambda i: (0,) * a.ndim)
    args = (conv_t, h0, pbuf_t, cw, cb, wa, ba, wx, bx, lam, wp, sc)
    return pl.pallas_call(
        functools.partial(_mix_sample_kernel, past_len),
        out_shape=(jax.ShapeDtypeStruct((n, BRANCH_W), bf16), jax.ShapeDtypeStruct((n, BRANCH_W), bf16),
                   jax.ShapeDtypeStruct((n, BRANCH_W), f32)),
        grid=(1,),
        in_specs=[col(C_LRU_X), col(C_LRU_G), col(C_POOL_X), col(C_POOL_G)] + [full(a) for a in args],
        out_specs=(pl.BlockSpec((n, BRANCH_W), lambda i: (0, 0)),) * 3,
        compiler_params=_params(("arbitrary",)),
        name="mix_sample",
    )(z2, z2, z2, z2, *args)


def _compress_kernel(tab_ref, k_ref, v_ref, wk_ref, wv_ref, fk_ref, sk_ref, fv_ref, sv_ref):
    del tab_ref
    p = pl.program_id(1)
    chunks = PAGE_SIZE // CMP_STRIDE

    @pl.when(p == 0)
    def _():
        sk_ref[...] = jnp.zeros_like(sk_ref)
        sv_ref[...] = jnp.zeros_like(sv_ref)

    for src, w_ref, f_ref, s_ref in ((k_ref, wk_ref, fk_ref, sk_ref), (v_ref, wv_ref, fv_ref, sv_ref)):
        tile = src[0]
        first = jnp.sum((tile * w_ref[0]).reshape(chunks, CMP_STRIDE, KV_W), axis=1)
        second = jnp.sum((tile * w_ref[1]).reshape(chunks, CMP_STRIDE, KV_W), axis=1)
        for m in range(chunks):
            f_ref[0, m % 4, pl.ds(2 * p + m // 4, 1), :] = first[m:m + 1, :]
            if m >= 1:
                s_ref[0, (m - 1) % 4, pl.ds(2 * p + (m - 1) // 4, 1), :] = second[m:m + 1, :]
            else:
                @pl.when(p > 0)
                def _():
                    s_ref[0, 3, pl.ds(2 * p - 1, 1), :] = second[0:1, :]


def _compress(table, src_k, src_v, col_k, col_v, wk, wv, nb):
    npg = table.shape[0] // nb
    out = jax.ShapeDtypeStruct((nb, 4, N_CMP // 4, KV_W), f32)
    ospec = lambda: pl.BlockSpec((1, 4, N_CMP // 4, KV_W), lambda b, p, tab: (b, 0, 0, 0))
    return pl.pallas_call(
        _compress_kernel,
        out_shape=(out,) * 4,
        grid_spec=pltpu.PrefetchScalarGridSpec(
            num_scalar_prefetch=1,
            grid=(nb, npg),
            in_specs=[pl.BlockSpec((1, PAGE_SIZE, KV_W), lambda b, p, tab: (tab[b * npg + p], 0, col_k)),
                      pl.BlockSpec((1, PAGE_SIZE, KV_W), lambda b, p, tab: (tab[b * npg + p], 0, col_v)),
                      pl.BlockSpec((2, PAGE_SIZE, KV_W), lambda b, p, tab: (0, 0, 0)),
                      pl.BlockSpec((2, PAGE_SIZE, KV_W), lambda b, p, tab: (0, 0, 0))],
            out_specs=(ospec(), ospec(), ospec(), ospec())),
        compiler_params=_params(("parallel", "arbitrary")),
        name="compress",
    )(table, src_k, src_v, wk, wv)


def _chunk_maps():
    n_chunk = PP * PAGE_SIZE // CMP_STRIDE
    rows = n_chunk // 4
    chunk_of = np.arange(PP * PAGE_SIZE) // CMP_STRIDE
    first = np.zeros((n_chunk, PP * PAGE_SIZE), np.float32)
    second = np.zeros((n_chunk + 8, PP * PAGE_SIZE), np.float32)
    for i in range(4):
        for jj in range(rows):
            first[i * rows + jj] = chunk_of == 4 * jj + i
            second[i * rows + jj] = chunk_of == 4 * jj + i + 1
    second[n_chunk] = chunk_of == 0
    return jnp.asarray(first, bf16), jnp.asarray(second, bf16)


def _compress_t_kernel(tab_ref, *refs):
    del tab_ref
    k_refs, v_refs = refs[:PP], refs[PP:2 * PP]
    wk_ref, wv_ref, ea_ref, eb_ref, fk_ref, sk_ref, fv_ref, sv_ref = refs[2 * PP:]
    ps = pl.program_id(1)
    rows = PP * PAGE_SIZE // CMP_STRIDE // 4
    r0 = pl.multiple_of(ps * rows, rows)
    for pages, w_ref, f_ref, s_ref in ((k_refs, wk_ref, fk_ref, sk_ref), (v_refs, wv_ref, fv_ref, sv_ref)):
        a1 = jnp.concatenate([(r[0, 0] * w_ref[0]).astype(bf16) for r in pages], axis=1)
        a2 = jnp.concatenate([(r[0, 0] * w_ref[1]).astype(bf16) for r in pages], axis=1)
        first = _nt(ea_ref[...], a1)
        second = _nt(eb_ref[...], a2)
        for i in range(4):
            f_ref[0, i, pl.ds(r0, rows), :] = first[i * rows:(i + 1) * rows]
            s_ref[0, i, pl.ds(r0, rows), :] = second[i * rows:(i + 1) * rows]

        @pl.when(ps > 0)
        def _():
            s_ref[0, 3, pl.ds(r0 - 1, 1), :] = second[4 * rows:4 * rows + 1]


def _compress_t(table, layer, cache_k, cache_v, wk, wv, nb):
    npg = table.shape[0] // nb
    ea, eb = _chunk_maps()
    out = jax.ShapeDtypeStruct((nb, 4, N_CMP // 4, KV_W), f32)
    ospec = lambda: pl.BlockSpec((1, 4, N_CMP // 4, KV_W), lambda b, p, tab: (b, 0, 0, 0))
    page = lambda i: pl.BlockSpec((1, 1, KV_W, PAGE_SIZE),
                                  lambda b, p, tab: (layer, tab[b * npg + p * PP + i], 0, 0))
    full = lambda a: pl.BlockSpec(a.shape, lambda b, p, tab: (0,) * a.ndim)
    return pl.pallas_call(
        _compress_t_kernel,
        out_shape=(out,) * 4,
        grid_spec=pltpu.PrefetchScalarGridSpec(
            num_scalar_prefetch=1,
            grid=(nb, npg // PP),
            in_specs=[page(i) for i in range(PP)] * 2 + [full(wk), full(wv), full(ea), full(eb)],
            out_specs=(ospec(), ospec(), ospec(), ospec())),
        compiler_params=_params(("parallel", "arbitrary")),
        name="compress_t",
    )(table, *([cache_k] * PP), *([cache_v] * PP), wk, wv, ea, eb)


def _finish_compress(f_ref, s_ref, phi_ref):
    blk = (f_ref[0] + s_ref[0]).reshape(N_CMP, KV_W)
    return _dot(blk.astype(bf16), phi_ref[...]).astype(bf16)


def _cmp_end(shape, axis):
    col = lax.broadcasted_iota(jnp.int32, shape, axis)
    n = ((col & (N_BLK - 1)) << 2) + (col >> 7)
    return n * CMP_STRIDE + (CMP_BLK - 1)


def _pick_blocks(score, n_pick):
    lane = lax.broadcasted_iota(jnp.int32, score.shape, 1).astype(f32)
    bias = jnp.full(score.shape, NEG, f32)
    for _ in range(n_pick):
        m = jnp.max(score, axis=-1, keepdims=True)
        first = jnp.min(jnp.where(score == m, lane, float(N_BLK)), axis=-1, keepdims=True)
        hit = lane == first
        bias = jnp.where(hit, 0.0, bias)
        score = jnp.where(hit, -jnp.inf, score)
    return bias


def _block_expand(first_block, n_keys):
    j = lax.broadcasted_iota(jnp.int32, (N_BLK, n_keys), 0)
    c = lax.broadcasted_iota(jnp.int32, (N_BLK, n_keys), 1)
    return jnp.where(j == first_block + (c >> 6), 1.0, 0.0).astype(bf16)


CQ = Q_PER_KV * TQ
WIN_T = WINDOW // TQ + 1
N_PARTS = 9
TILE_COL = HEAD_DIM + 6


def _pick_blocks_t(score, n_pick):
    jrow = lax.broadcasted_iota(jnp.int32, score.shape, 0).astype(f32)
    bias = jnp.full(score.shape, NEG, f32)
    for _ in range(n_pick):
        m = jnp.max(score, axis=0, keepdims=True)
        first = jnp.min(jnp.where(score == m, jrow, float(N_BLK)), axis=0, keepdims=True)
        hit = jrow == first
        bias = jnp.where(hit, 0.0, bias)
        score = jnp.where(hit, -jnp.inf, score)
    return bias


def _nsa_prompt_kernel(qt_ref, srow_ref, slope_ref, bgt_ref, ng_ref, fk_ref, sk_ref, fv_ref, sv_ref,
                       phik_ref, phivt_ref, cpos_ref, ksa_ref, vst_ref, kwa_ref, vwt_ref, o_ref,
                       kca_scr, vct_scr, qa_scr, oc_scr, m_scr, l_scr, acc_scr):
    qi = pl.program_id(1)
    s0 = qi * TQ

    @pl.when(qi == 0)
    def _():
        kc = _finish_compress(fk_ref, sk_ref, phik_ref)
        for g in range(KV_HEADS):
            kca_scr[g] = jnp.concatenate([kc[:, g * HEAD_DIM:(g + 1) * HEAD_DIM], cpos_ref[...]], axis=1)
        blk_v = (fv_ref[0] + sv_ref[0]).reshape(N_CMP, KV_W).astype(bf16)
        vct_scr[...] = _nt(phivt_ref[...], blk_v).astype(bf16)

    t_of = lambda shape: s0 + (lax.broadcasted_iota(jnp.int32, shape, 1) & (TQ - 1))

    ok_c = _cmp_end((N_CMP, CQ), 0) <= t_of((N_CMP, CQ))
    t_q = t_of((N_BLK, TQ))
    jr = lax.broadcasted_iota(jnp.int32, (N_BLK, TQ), 0)
    jb = t_q >> 6
    ok_b = (jr << 6) <= t_q
    forced = jnp.where((jr == 0) | (jr == jb) | (jr == jb - 1), FORCE, 0.0)
    tile_any = jnp.full((N_BLK // 8, 1), NEG, f32)
    for g in range(KV_HEADS):
        qa = jnp.concatenate([qt_ref[0, 0, g], srow_ref[g]], axis=0)
        st = jnp.where(ok_c, _dot(kca_scr[g], qa), NEG)
        e = jnp.exp(st - jnp.max(st, axis=0, keepdims=True))
        pt = jnp.where(ok_c, e * (1.0 / jnp.sum(e, axis=0, keepdims=True)), 0.0)
        oc_scr[g] = _dot(vct_scr[g * HEAD_DIM:(g + 1) * HEAD_DIM], pt.astype(bf16))
        ps = None
        for i in range(4):
            for r in range(Q_PER_KV):
                slab = pt[i * N_BLK:(i + 1) * N_BLK, r * TQ:(r + 1) * TQ]
                ps = slab if ps is None else ps + slab
        bias = _pick_blocks_t(jnp.where(ok_b, ps + forced, NEG), N_SEL)
        qa_scr[g] = jnp.concatenate([qa, jnp.concatenate([bias.astype(bf16)] * Q_PER_KV, axis=1)], axis=0)
        any_t = jnp.max(bias.reshape(N_BLK // 8, 8, TQ), axis=1)
        tile_any = jnp.maximum(tile_any, jnp.max(any_t, axis=1, keepdims=True))
    kt_row = lax.broadcasted_iota(jnp.int32, tile_any.shape, 0)
    tile_bits = jnp.sum(jnp.where(tile_any == 0.0, 1 << kt_row, 0))

    m_scr[...] = jnp.full(m_scr.shape, NEG, f32)
    l_scr[...] = jnp.zeros_like(l_scr)
    acc_scr[...] = jnp.zeros_like(acc_scr)

    def sel_tile(kt, diagonal):
        k0 = pl.multiple_of(kt * TK, TK)
        tile_off = (k0 - s0).astype(f32)
        key_r = lax.broadcasted_iota(jnp.int32, (TK, N_BLK), 0)
        blk_c = lax.broadcasted_iota(jnp.int32, (TK, N_BLK), 1)
        expand = jnp.where(blk_c == kt * (TK // SEL_BLK) + (key_r >> 6), 1.0, 0.0).astype(bf16)
        k_aug = ksa_ref[0, pl.ds(k0, TK), :]
        v_t = vst_ref[0, kt]
        if diagonal:
            causal = (k0 + lax.broadcasted_iota(jnp.int32, (TK, CQ), 0)) <= t_of((TK, CQ))
        for g in range(KV_HEADS):
            lhs = jnp.concatenate([k_aug[:, g * QA_W:(g + 1) * QA_W], expand], axis=1)
            st = _dot(lhs, qa_scr[g])
            if diagonal:
                st = jnp.where(causal, st, NEG)
            c = slope_ref[g] * tile_off
            m_old = m_scr[g]
            m_new = jnp.maximum(m_old, jnp.max(st, axis=0, keepdims=True) + c)
            alpha = jnp.exp(m_old - m_new)
            pt = jnp.exp(st - (m_new - c))
            l_scr[g] = alpha * l_scr[g] + jnp.sum(pt, axis=0, keepdims=True)
            acc_scr[g] = alpha * acc_scr[g] + _dot(v_t[g * HEAD_DIM:(g + 1) * HEAD_DIM], pt.astype(bf16))
            m_scr[g] = m_new

    def sel_body(kt, carry):
        @pl.when(((tile_bits >> kt) & 1) == 1)
        def _():
            sel_tile(kt, False)
        return carry

    last = s0 // TK
    lax.fori_loop(0, last, sel_body, 0)
    sel_tile(last, True)

    w_tile = jnp.maximum(qi - WINDOW // TQ, 0)
    n_win = WIN_T * TQ
    dist_w = t_of((n_win, CQ)) - (w_tile * TQ + lax.broadcasted_iota(jnp.int32, (n_win, CQ), 0))
    ok_w = (dist_w >= 0) & (dist_w < WINDOW)
    lane_w = lax.broadcasted_iota(jnp.int32, (n_win, QA_W), 1)
    slab_off = (lax.broadcasted_iota(jnp.int32, (n_win, QA_W), 0) // TQ * TQ).astype(f32).astype(bf16)
    in_tile_col = (lane_w >= TILE_COL) & (lane_w < TILE_COL + 3)
    kw_all = kwa_ref[0, pl.ds(pl.multiple_of(w_tile * TQ, TQ), n_win), :]
    vw_t = jnp.concatenate([vwt_ref[0, w_tile + i] for i in range(WIN_T)], axis=1)
    gates = jax.nn.sigmoid(bgt_ref[0])
    blocks = []
    for g in range(KV_HEADS):
        lhs = jnp.where(in_tile_col, slab_off, kw_all[:, g * QA_W:(g + 1) * QA_W])
        st = jnp.where(ok_w, _dot(lhs, qa_scr[g, 0:QA_W]), NEG)
        e = jnp.exp(st - jnp.max(st, axis=0, keepdims=True))
        pt = (e * (1.0 / jnp.sum(e, axis=0, keepdims=True))).astype(bf16)
        o_w = _dot(vw_t[g * HEAD_DIM:(g + 1) * HEAD_DIM], pt)
        o_s = acc_scr[g] * (1.0 / l_scr[g])
        o_c = oc_scr[g]
        heads = []
        for r in range(Q_PER_KV):
            h = g * Q_PER_KV + r
            cols = slice(r * TQ, (r + 1) * TQ)
            heads.append(gates[3 * h:3 * h + 1] * o_c[:, cols] + gates[3 * h + 1:3 * h + 2] * o_s[:, cols]
                         + gates[3 * h + 2:3 * h + 3] * o_w[:, cols])
        for r in range(0, Q_PER_KV, 2):
            blocks.append(jnp.concatenate(heads[r:r + 2], axis=0).T)
    y = jnp.concatenate(blocks, axis=-1)
    o_ref[0] = (y * _silu(ng_ref[0])).astype(bf16)


def _nsa_prompt(z3, fk, sk, fv, sv, phik, phivt):
    b, s, _ = z3.shape
    nq = s // TQ
    qh = (z3[..., C_Q:C_Q + BRANCH_W] * (HEAD_DIM ** -0.5)).astype(bf16)
    qt = qh.reshape(b, nq, TQ, KV_HEADS, Q_PER_KV, HEAD_DIM).transpose(0, 1, 3, 5, 4, 2).reshape(
        b, nq, KV_HEADS, HEAD_DIM, CQ)
    parts = _slope_parts().reshape(KV_HEADS, Q_PER_KV, POS_ROWS)
    srow = jnp.asarray(np.repeat(parts.transpose(0, 2, 1), TQ, axis=2), bf16)
    slope = jnp.asarray(np.repeat(np.asarray(SLOPES, np.float32).reshape(KV_HEADS, 1, Q_PER_KV), TQ, axis=2))
    bgt = z3[..., C_BG:C_BG + BG_W].swapaxes(1, 2)
    ksa = _key_aug(_kv_seg(z3, 2), TK)
    kwa = _key_aug(_kv_seg(z3, 4), TQ)
    vst = _kv_seg(z3, 3).astype(bf16).reshape(b, s // TK, TK, KV_W).swapaxes(2, 3)
    vwt = _kv_seg(z3, 5).astype(bf16).reshape(b, nq, TQ, KV_W).swapaxes(2, 3)
    cpos = _cmp_pos_cols()

    once = pl.Buffered(1)
    cmp = lambda: pl.BlockSpec((1, 4, N_CMP // 4, KV_W), lambda i, t: (i, 0, 0, 0))
    full = lambda a: pl.BlockSpec(a.shape, lambda i, t: (0,) * a.ndim)
    return pl.pallas_call(
        _nsa_prompt_kernel,
        out_shape=jax.ShapeDtypeStruct((b, s, BRANCH_W), bf16),
        grid=(b, nq),
        in_specs=[pl.BlockSpec((1, 1, KV_HEADS, HEAD_DIM, CQ), lambda i, t: (i, t, 0, 0, 0)),
                  full(srow), full(slope),
                  pl.BlockSpec((1, BG_W, TQ), lambda i, t: (i, 0, t)),
                  pl.BlockSpec((1, TQ, BRANCH_W), lambda i, t: (i, t, C_NSA_G // BRANCH_W)),
                  cmp(), cmp(), cmp(), cmp(), full(phik), full(phivt), full(cpos),
                  pl.BlockSpec((1, s, KV_HEADS * QA_W), lambda i, t: (i, 0, 0), pipeline_mode=once),
                  pl.BlockSpec((1, s // TK, KV_W, TK), lambda i, t: (i, 0, 0, 0), pipeline_mode=once),
                  pl.BlockSpec((1, s, KV_HEADS * QA_W), lambda i, t: (i, 0, 0), pipeline_mode=once),
                  pl.BlockSpec((1, nq, KV_W, TQ), lambda i, t: (i, 0, 0, 0), pipeline_mode=once)],
        out_specs=pl.BlockSpec((1, TQ, BRANCH_W), lambda i, t: (i, t, 0)),
        scratch_shapes=[pltpu.VMEM((KV_HEADS, N_CMP, QA_W), bf16), pltpu.VMEM((KV_W, N_CMP), bf16),
                        pltpu.VMEM((KV_HEADS, QA_W + N_BLK, CQ), bf16),
                        pltpu.VMEM((KV_HEADS, HEAD_DIM, CQ), f32),
                        pltpu.VMEM((KV_HEADS, 1, CQ), f32), pltpu.VMEM((KV_HEADS, 1, CQ), f32),
                        pltpu.VMEM((KV_HEADS, HEAD_DIM, CQ), f32)],
        compiler_params=_params(("parallel", "arbitrary")),
        name="nsa_prompt",
    )(qt, srow, slope, bgt, z3, fk, sk, fv, sv, phik, phivt, cpos, ksa, vst, kwa, vwt)


def _by_group(fn):
    hg = lax.broadcasted_iota(jnp.int32, (N_HEADS, 1), 0) >> 2
    out = fn(0)
    for g in range(1, KV_HEADS):
        out = jnp.where(hg == g, fn(g), out)
    return out


def _nsa_sample_kernel(past_len, tab_ref, q_ref, bg_ref, ng_ref, slope_ref,
                       fk_ref, sk_ref, fv_ref, sv_ref, phik_ref, phiv_ref,
                       ksn_ref, vsn_ref, kwn_ref, vwn_ref, bk_ref, bv_ref, *refs):
    del tab_ref
    kp_refs, vp_refs = refs[:PP_SEL], refs[PP_SEL:2 * PP_SEL]
    o_ref, q_scr, sel_scr, oc_scr, ow_scr, m_scr, l_scr, acc_scr = refs[2 * PP_SEL:]
    p = pl.program_id(1)
    slope = slope_ref[:, 0:1]
    gsl = lambda g: slice(g * HEAD_DIM, (g + 1) * HEAD_DIM)

    def per_head(row_ref):
        row = row_ref[0]
        return _by_group(lambda g: jnp.broadcast_to(row[:, gsl(g)], (N_HEADS, HEAD_DIM)))

    @pl.when(p == 0)
    def _():
        qrow = q_ref[0]
        q16 = jnp.concatenate([qrow[:, h * HEAD_DIM:(h + 1) * HEAD_DIM] for h in range(N_HEADS)], axis=0)
        q_scr[...] = q16
        q16f = q16.astype(f32)
        kc = _finish_compress(fk_ref, sk_ref, phik_ref)
        vc = _finish_compress(fv_ref, sv_ref, phiv_ref)

        dist_c = past_len - _cmp_end((1, N_CMP), 1)
        ok_c = dist_c >= 0
        s = _by_group(lambda g: _nt(q16, kc[:, gsl(g)])) - slope * dist_c.astype(f32)
        s = jnp.where(ok_c, s, NEG)
        e = jnp.exp(s - jnp.max(s, axis=-1, keepdims=True))
        pc = jnp.where(ok_c, e * (1.0 / jnp.sum(e, axis=-1, keepdims=True)), 0.0)
        pcb = pc.astype(bf16)
        oc_scr[...] = _by_group(lambda g: _dot(pcb, vc[:, gsl(g)]))
        ps16 = (pc[:, 0:N_BLK] + pc[:, N_BLK:2 * N_BLK]) + (pc[:, 2 * N_BLK:3 * N_BLK] + pc[:, 3 * N_BLK:])
        ps = jnp.concatenate([jnp.sum(ps16[g * Q_PER_KV:(g + 1) * Q_PER_KV], axis=0, keepdims=True)
                              for g in range(KV_HEADS)] + [jnp.zeros((8 - KV_HEADS, N_BLK), f32)], axis=0)
        jl = lax.broadcasted_iota(jnp.int32, (8, N_BLK), 1)
        jb = past_len // SEL_BLK
        forced = jnp.where((jl == 0) | (jl == jb) | (jl == jb - 1), FORCE, 0.0)
        bias8 = _pick_blocks(ps + forced, N_SEL - 1)
        sel_scr[...] = jnp.concatenate(
            [jnp.broadcast_to(bias8[g:g + 1], (Q_PER_KV, N_BLK)) for g in range(KV_HEADS)], axis=0).astype(bf16)

        wb = bk_ref.shape[-1]
        dist_w = wb - lax.broadcasted_iota(jnp.int32, (1, wb), 1)
        ok_w = (dist_w >= 0) & (dist_w < WINDOW)
        bk = bk_ref[0, 0].astype(bf16)
        bv = bv_ref[0, 0].astype(bf16)
        s_buf = _by_group(lambda g: _dot(q16, bk[gsl(g)])) - slope * dist_w.astype(f32)
        s_buf = jnp.where(ok_w, s_buf, NEG)
        s_new = jnp.sum(q16f * per_head(kwn_ref), axis=-1, keepdims=True)
        m_w = jnp.maximum(jnp.max(s_buf, axis=-1, keepdims=True), s_new)
        e_buf = jnp.exp(s_buf - m_w)
        e_new = jnp.exp(s_new - m_w)
        ebb = e_buf.astype(bf16)
        num = _by_group(lambda g: _nt(ebb, bv[gsl(g)])) + e_new * per_head(vwn_ref)
        ow_scr[...] = num * (1.0 / (jnp.sum(e_buf, axis=-1, keepdims=True) + e_new))

        m_scr[...] = jnp.sum(q16f * per_head(ksn_ref), axis=-1, keepdims=True)
        l_scr[...] = jnp.ones_like(l_scr)
        acc_scr[...] = per_head(vsn_ref)

    q16 = q_scr[...]
    n_keys = PP_SEL * PAGE_SIZE
    kp = jnp.concatenate([r[0, 0].astype(bf16) for r in kp_refs], axis=1)
    vp = jnp.concatenate([r[0, 0].astype(bf16) for r in vp_refs], axis=1)
    dist = past_len - (p * n_keys + lax.broadcasted_iota(jnp.int32, (1, n_keys), 1))
    expand = _block_expand(p * (n_keys // SEL_BLK), n_keys)
    s = _by_group(lambda g: _dot(q16, kp[gsl(g)])) - slope * dist.astype(f32) + _dot(sel_scr[...], expand)
    m_old = m_scr[...]
    m_new = jnp.maximum(m_old, jnp.max(s, axis=-1, keepdims=True))
    alpha = jnp.exp(m_old - m_new)
    pr = jnp.exp(s - m_new)
    prb = pr.astype(bf16)
    l_scr[...] = alpha * l_scr[...] + jnp.sum(pr, axis=-1, keepdims=True)
    acc_scr[...] = alpha * acc_scr[...] + _by_group(lambda g: _nt(prb, vp[gsl(g)]))
    m_scr[...] = m_new

    @pl.when(p == pl.num_programs(1) - 1)
    def _():
        o_s = acc_scr[...] * (1.0 / l_scr[...])
        gates = jax.nn.sigmoid(bg_ref[0])
        lane = lax.broadcasted_iota(jnp.int32, (N_HEADS, BG_W), 1)
        h3 = 3 * lax.broadcasted_iota(jnp.int32, (N_HEADS, BG_W), 0)
        gate = lambda n: jnp.sum(jnp.where(lane == h3 + n, gates, 0.0), axis=-1, keepdims=True)
        y16 = gate(0) * oc_scr[...] + gate(1) * o_s + gate(2) * ow_scr[...]
        y = jnp.concatenate([y16[h:h + 1, :] for h in range(N_HEADS)], axis=-1)
        o_ref[0] = (y * _silu(ng_ref[0])).astype(bf16)


def _nsa_sample(table, layer, qb, z2, slopes, fk, sk, fv, sv, phik, phiv, pool_k, pool_v, buf_k, buf_v, past_len):
    nb = qb.shape[0]
    npg = table.shape[0] // nb
    z3 = z2.reshape(nb, 1, IN_WP)
    tok = lambda w, c: pl.BlockSpec((1, 1, w), lambda b, p, tab: (b, 0, c))
    cmp = lambda: pl.BlockSpec((1, 4, N_CMP // 4, KV_W), lambda b, p, tab: (b, 0, 0, 0))
    phi = lambda: pl.BlockSpec((KV_W, KV_W), lambda b, p, tab: (0, 0))
    wb = buf_k.shape[-1]
    win = lambda: pl.BlockSpec((1, 1, KV_W, wb), lambda b, p, tab: (layer, b, 0, 0))
    page = lambda i: pl.BlockSpec((1, 1, KV_W, PAGE_SIZE),
                                  lambda b, p, tab: (layer, tab[b * npg + p * PP_SEL + i], 0, 0))
    kvc = C_KV // KV_W
    return pl.pallas_call(
        functools.partial(_nsa_sample_kernel, past_len),
        out_shape=jax.ShapeDtypeStruct((nb, 1, BRANCH_W), bf16),
        grid_spec=pltpu.PrefetchScalarGridSpec(
            num_scalar_prefetch=1,
            grid=(nb, npg // PP_SEL),
            in_specs=[tok(BRANCH_W, 0), tok(BG_W, C_BG // BG_W), tok(BRANCH_W, C_NSA_G // BRANCH_W),
                      pl.BlockSpec((N_HEADS, 128), lambda b, p, tab: (0, 0)),
                      cmp(), cmp(), cmp(), cmp(), phi(), phi(),
                      tok(KV_W, kvc + 2), tok(KV_W, kvc + 3), tok(KV_W, kvc + 4), tok(KV_W, kvc + 5),
                      win(), win()] + [page(i) for i in range(PP_SEL)] * 2,
            out_specs=pl.BlockSpec((1, 1, BRANCH_W), lambda b, p, tab: (b, 0, 0)),
            scratch_shapes=[pltpu.VMEM((N_HEADS, HEAD_DIM), bf16), pltpu.VMEM((N_HEADS, N_BLK), bf16),
                            pltpu.VMEM((N_HEADS, HEAD_DIM), f32), pltpu.VMEM((N_HEADS, HEAD_DIM), f32),
                            pltpu.VMEM((N_HEADS, 1), f32), pltpu.VMEM((N_HEADS, 1), f32),
                            pltpu.VMEM((N_HEADS, HEAD_DIM), f32)]),
        compiler_params=_params(("parallel", "arbitrary")),
        name="nsa_sample",
    )(table, qb.reshape(nb, 1, BRANCH_W), z3, z3, slopes, fk, sk, fv, sv, phik, phiv,
      z3, z3, z3, z3, buf_k, buf_v, *([pool_k] * PP_SEL), *([pool_v] * PP_SEL))


def _merge_kernel(zl_ref, zp_ref, zn_ref, m0_ref, m1_ref, m2_ref, wb_ref, wo_ref, g_ref, x_ref, y_ref):
    acc = None
    for n, (zz, mg) in enumerate(((zl_ref, m0_ref), (zp_ref, m1_ref), (zn_ref, m2_ref))):
        term = jax.nn.sigmoid(mg[...]) * _dot(zz[...], wb_ref[n])
        acc = term if acc is None else acc + term
    out = _dot(acc.astype(bf16), wo_ref[...])
    ms = jnp.mean(out * out, axis=-1, keepdims=True)
    y_ref[...] = x_ref[...] + out * lax.rsqrt(ms + EPS) * g_ref[...]


def _merge(zl, zp, zn, z2, wb, wo, g_row, x2d):
    n = x2d.shape[0]
    tm = min(n, 256)
    rowblk = lambda c: pl.BlockSpec((tm, D_MODEL), lambda i: (i, c))
    return pl.pallas_call(
        _merge_kernel,
        out_shape=jax.ShapeDtypeStruct((n, D_MODEL), f32),
        grid=(n // tm,),
        in_specs=[rowblk(0), rowblk(0), rowblk(0),
                  rowblk(C_MG // D_MODEL), rowblk(C_MG // D_MODEL + 1), rowblk(C_MG // D_MODEL + 2),
                  pl.BlockSpec((N_BRANCH, BRANCH_W, D_MODEL), lambda i: (0, 0, 0)),
                  pl.BlockSpec((D_MODEL, D_MODEL), lambda i: (0, 0)),
                  pl.BlockSpec((1, D_MODEL), lambda i: (0, 0)),
                  rowblk(0)],
        out_specs=rowblk(0),
        compiler_params=_params(("parallel",)),
        name="merge",
    )(zl, zp, zn, z2, z2, z2, wb, wo, g_row, x2d)


def _block_diag(w, per):
    n, d, _ = w.shape
    eye = jnp.eye(per, dtype=w.dtype)
    t = jnp.einsum('cpde,pq->cpdqe', w.reshape(n // per, per, d, d), eye)
    return t.reshape(n // per, per * d, per * d)


def _pack_w_in(w):
    old_kv = 6 * BRANCH_W
    old_bg = old_kv + 6 * KV_W
    old_mg = old_bg + N_BRANCH * N_HEADS
    parts = [w[:, :old_kv], w[:, old_mg:], w[:, old_kv:old_bg], w[:, old_bg:old_mg]]
    packed = jnp.concatenate(parts, axis=1)
    return jnp.pad(packed, ((0, 0), (0, IN_WP - packed.shape[1]))).astype(bf16)


def _tile_wpos(w_pos):
    halves = w_pos.reshape(2, CMP_STRIDE, HEAD_DIM)
    return jnp.tile(halves, (1, PAGE_SIZE // CMP_STRIDE, KV_HEADS))


def _tile_wpos_t(w_pos):
    halves = w_pos.reshape(2, CMP_STRIDE, HEAD_DIM).swapaxes(1, 2)
    return jnp.tile(halves, (1, KV_HEADS, PAGE_SIZE // CMP_STRIDE))


def _lanes_last(cache):
    d, n, rows = cache.shape[:3]
    return jnp.transpose(cache, (0, 1, 3, 4, 2)).reshape(d, n, KV_W, rows)


def _slope_parts():
    cols = np.zeros((N_HEADS, POS_ROWS), np.float32)
    rnd = lambda v: np.float32(np.float32(v).astype(bf16))
    for h, s in enumerate(SLOPES):
        s1 = rnd(s)
        s2 = rnd(np.float32(s) - s1)
        s3 = rnd(np.float32(s) - s1 - s2)
        cols[h, 0:N_PARTS] = [s1, s2, s3] * 3
    return cols


def _split_pos(pos, shift):
    cols = np.zeros((pos.shape[0], POS_ROWS), np.float32)
    cols[:, 0:3] = ((pos >> shift) << shift)[:, None]
    cols[:, 3:6] = (pos & ((1 << shift) - 1))[:, None]
    return cols


def _key_aug(k_rows, tile):
    b, s, _ = k_rows.shape
    pos = jnp.asarray(_split_pos(np.arange(s) % tile, 4), bf16)
    kg = k_rows.astype(bf16).reshape(b, s, KV_HEADS, HEAD_DIM)
    posb = jnp.broadcast_to(pos[None, :, None, :], (b, s, KV_HEADS, POS_ROWS))
    return jnp.concatenate([kg, posb], axis=-1).reshape(b, s, KV_HEADS * QA_W)


def _cmp_pos_cols():
    slot = np.arange(N_CMP)
    n = ((slot & (N_BLK - 1)) << 2) + (slot >> 7)
    return jnp.asarray(_split_pos(n * CMP_STRIDE + CMP_BLK - 1, 8), bf16)


def _kv_seg(z, i):
    return z[..., C_KV + i * KV_W:C_KV + (i + 1) * KV_W]


def kernel(x_prompt, x_sample, cache_cmp_k, cache_cmp_v, cache_sel_k, cache_sel_v, cache_win_k, cache_win_v, state_conv, state_lru, state_pool, page_table, g_pre, g_post, w_in, conv_w, conv_b, w_rg_a, b_rg_a, w_rg_x, b_rg_x, lru_lambda, w_pool, pool_scale, cmp_pos_k, cmp_phi_k, cmp_pos_v, cmp_phi_v, w_branch, w_out):
    depth = w_in.shape[0]
    bp, seq, _ = x_prompt.shape
    bs = x_sample.shape[0]
    n_pages = page_table.shape[1]
    past_len = n_pages * PAGE_SIZE
    n_phys = cache_cmp_k.shape[1]
    assert seq == N_BLK * SEL_BLK and past_len == N_BLK * SEL_BLK and x_sample.shape[1] == 1
    wb = cache_win_k.shape[2]

    table_s = page_table.reshape(-1).astype(jnp.int32)
    table_p = jnp.arange(bp * (seq // PAGE_SIZE), dtype=jnp.int32)
    slopes = jnp.broadcast_to(jnp.asarray(SLOPES, f32)[:, None], (N_HEADS, 128))
    row = lambda v: v.reshape(1, -1)
    cmp_kt, cmp_vt = _lanes_last(cache_cmp_k), _lanes_last(cache_cmp_v)
    sel_kt, sel_vt = _lanes_last(cache_sel_k), _lanes_last(cache_sel_v)
    win_kt, win_vt = _lanes_last(cache_win_k), _lanes_last(cache_win_v)

    xp = x_prompt.reshape(bp * seq, D_MODEL)
    xs = x_sample.reshape(bs, D_MODEL)
    pr = [[] for _ in range(9)]
    sm = [[] for _ in range(9)]
    for l in range(depth):
        w_packed = _pack_w_in(w_in[l])
        wa = _block_diag(w_rg_a[l], MXU_W // LRU_BD).astype(bf16)
        wx = _block_diag(w_rg_x[l], MXU_W // LRU_BD).astype(bf16)
        wp = w_pool[l].astype(bf16)
        phik = _block_diag(jnp.broadcast_to(cmp_phi_k[l], (KV_HEADS, HEAD_DIM, HEAD_DIM)), KV_HEADS)[0].astype(bf16)
        phiv = _block_diag(jnp.broadcast_to(cmp_phi_v[l], (KV_HEADS, HEAD_DIM, HEAD_DIM)), KV_HEADS)[0].astype(bf16)
        wpos_k = _tile_wpos(cmp_pos_k[l])
        wpos_v = _tile_wpos(cmp_pos_v[l])
        wbr = w_branch[l].astype(bf16)
        wo = w_out[l].astype(bf16)
        lru_w = (conv_w[l], row(conv_b[l]), wa, row(b_rg_a[l]), wx, row(b_rg_x[l]), row(lru_lambda[l]))

        z = _inproj(xp, row(g_pre[l]), w_packed)
        z3 = z.reshape(bp, seq, IN_WP)
        zl, conv_tail, h_p = _lru_prompt(z3, *lru_w)
        zpool, pool_tail = _pool_prompt(z3, wp, row(pool_scale[l]))
        zr = z.reshape(bp * seq // PAGE_SIZE, PAGE_SIZE, IN_WP)
        fk, sk, fv, sv = _compress(table_p, zr, zr, C_KV // KV_W, C_KV // KV_W + 1, wpos_k, wpos_v, bp)
        zn = _nsa_prompt(z3, fk, sk, fv, sv, phik, phiv.T)
        xp = _merge(zl.reshape(bp * seq, BRANCH_W), zpool.reshape(bp * seq, BRANCH_W),
                    zn.reshape(bp * seq, BRANCH_W), z, wbr, wo, row(g_post[l]), xp)
        kv_rows = [_kv_seg(z3, i).reshape(bp, seq, KV_HEADS, HEAD_DIM) for i in range(6)]
        wlen = min(WINDOW, seq)
        st_p = kv_rows[:4] + [kv_rows[4][:, -wlen:], kv_rows[5][:, -wlen:],
                              conv_tail[:, -(CONV_W - 1):], h_p[:, 0], pool_tail[:, -POOL_BUF:]]

        zs = _inproj(xs, row(g_pre[l]), w_packed)
        zls, zps, h_s = _mix_sample(zs, state_conv[l].swapaxes(0, 1), state_lru[l], state_pool[l].swapaxes(0, 1),
                                    *lru_w, wp, row(pool_scale[l]), past_len)
        fk, sk, fv, sv = _compress_t(table_s, l, cmp_kt, cmp_vt, _tile_wpos_t(cmp_pos_k[l]),
                                     _tile_wpos_t(cmp_pos_v[l]), bs)
        qs = (zs[:, C_Q:C_Q + BRANCH_W] * (HEAD_DIM ** -0.5)).astype(bf16)
        zns = _nsa_sample(table_s, l, qs, zs, slopes, fk, sk, fv, sv, phik, phiv,
                          sel_kt, sel_vt, win_kt, win_vt, past_len)
        xs = _merge(zls, zps, zns.reshape(bs, BRANCH_W), zs, wbr, wo, row(g_post[l]), xs)
        kv_new = [_kv_seg(zs, i).reshape(bs, 1, KV_HEADS, HEAD_DIM) for i in range(6)]
        st_s = kv_new[:4] + [jnp.concatenate([cache_win_k[l], kv_new[4]], axis=1)[:, -wb:],
                             jnp.concatenate([cache_win_v[l], kv_new[5]], axis=1)[:, -wb:],
                             jnp.concatenate([state_conv[l], zs[:, None, C_LRU_X:C_LRU_X + BRANCH_W]], axis=1)[:, -(CONV_W - 1):],
                             h_s,
                             jnp.concatenate([state_pool[l], zs[:, None, C_POOL_X:C_POOL_X + BRANCH_W]], axis=1)[:, -POOL_BUF:]]
        for i in range(9):
            pr[i].append(st_p[i])
            sm[i].append(st_s[i])

    out = [xp.reshape(bp, seq, D_MODEL), xs.reshape(bs, 1, D_MODEL)]
    for i in range(9):
        out += [jnp.stack(pr[i]), jnp.stack(sm[i])]
    return tuple(out)
```

```python
import functools

import numpy as np
import jax
import jax.numpy as jnp
from jax import lax
from jax.experimental import pallas as pl
from jax.experimental.pallas import tpu as pltpu

f32 = jnp.float32
bf16 = jnp.bfloat16

D_MODEL = 1024
BRANCH_W = 1024
N_BRANCH = 3
LRU_BLOCKS = 16
LRU_BD = BRANCH_W // LRU_BLOCKS
CONV_W = 4
LRU_C = 8.0
POOL_WINDOWS = (2, 4, 8, 16)
POOL_GD = BRANCH_W // len(POOL_WINDOWS)
POOL_BUF = max(POOL_WINDOWS) - 1
N_HEADS = 16
HEAD_DIM = 64
KV_HEADS = 4
Q_PER_KV = N_HEADS // KV_HEADS
KV_W = KV_HEADS * HEAD_DIM
CMP_STRIDE = 16
CMP_BLK = 2 * CMP_STRIDE
SEL_BLK = 64
N_SEL = 16
WINDOW = 512
PAGE_SIZE = 128
FORCE = 1e4
NEG = -1e30
EPS = 1e-6

C_LRU_X, C_LRU_G, C_POOL_X, C_POOL_G, C_Q, C_NSA_G, C_MG = 0, 1024, 2048, 3072, 4096, 5120, 6144
C_KV = C_MG + N_BRANCH * D_MODEL
C_BG = C_KV + 6 * KV_W
BG_W = 128
IN_WP = 11264
IN_TN = 1024

MXU_W = 256
VMEM_LIMIT = 56 * 1024 * 1024

N_CMP = 512
N_BLK = 128
TQ = 256
TK = 512
PP = 8
PP_SEL = 16
POS_ROWS = 64
QA_W = HEAD_DIM + POS_ROWS

SLOPES = [float(np.float32(2.0 ** (-8.0 * (h + 1) / N_HEADS))) for h in range(N_HEADS)]
LOG2E = float(np.log2(np.e))
SLOPES_LOG2 = [float(np.float32(s * LOG2E)) for s in SLOPES]
Q_SCALE = float(np.float32(HEAD_DIM ** -0.5 * LOG2E))
LOOP_GROUPS = ((0, 1), (2, 3))


def _nt(a, b):
    return lax.dot_general(a, b, (((1,), (1,)), ((), ())), preferred_element_type=f32)


def _dot(a, b):
    return jnp.dot(a, b, preferred_element_type=f32)


def _silu(x):
    return x * jax.nn.sigmoid(x)


def _params(sem):
    return pltpu.CompilerParams(dimension_semantics=sem, vmem_limit_bytes=VMEM_LIMIT)


def _inproj_kernel(x_ref, g_ref, w_ref, o_ref, u_ref):
    @pl.when(pl.program_id(1) == 0)
    def _():
        x = x_ref[...]
        ms = jnp.mean(x * x, axis=-1, keepdims=True)
        u_ref[...] = (x * lax.rsqrt(ms + EPS) * g_ref[...]).astype(bf16)

    o_ref[...] = _dot(u_ref[...], w_ref[...])


def _inproj(x2d, g_row, w_packed):
    n = x2d.shape[0]
    tm = min(n, 1024)
    return pl.pallas_call(
        _inproj_kernel,
        out_shape=jax.ShapeDtypeStruct((n, IN_WP), f32),
        grid=(n // tm, IN_WP // IN_TN),
        in_specs=[pl.BlockSpec((tm, D_MODEL), lambda i, j: (i, 0)),
                  pl.BlockSpec((1, D_MODEL), lambda i, j: (0, 0)),
                  pl.BlockSpec((D_MODEL, IN_TN), lambda i, j: (0, j))],
        out_specs=pl.BlockSpec((tm, IN_TN), lambda i, j: (i, j)),
        scratch_shapes=[pltpu.VMEM((tm, D_MODEL), bf16)],
        compiler_params=_params(("parallel", "arbitrary")),
        name="inproj",
    )(x2d, g_row, w_packed)


def _lru_gates(xc, wa_ref, ba_ref, wx_ref, bx_ref, lam_ref):
    xb = xc.astype(bf16)
    ra, ri = [], []
    for c in range(BRANCH_W // MXU_W):
        sl = slice(c * MXU_W, (c + 1) * MXU_W)
        ra.append(_dot(xb[:, sl], wa_ref[c]))
        ri.append(_dot(xb[:, sl], wx_ref[c]))
    r = jax.nn.sigmoid(jnp.concatenate(ra, axis=-1) + ba_ref[...])
    i = jax.nn.sigmoid(jnp.concatenate(ri, axis=-1) + bx_ref[...])
    nl = -lam_ref[...]
    softplus = jnp.maximum(nl, 0.0) + jnp.log1p(jnp.exp(-jnp.abs(nl)))
    log_a = -LRU_C * r * softplus
    a = jnp.exp(log_a)
    b = jnp.sqrt(1.0 - a * a) * (i * xc)
    return a, b


def _lru_kernel(x_ref, g_ref, cw_ref, cb_ref, wa_ref, ba_ref, wx_ref, bx_ref, lam_ref,
                zb_ref, tail_ref, h_ref, xs_ref, a_ref, b_ref, hc_ref):
    tt = x_ref.shape[1]

    @pl.when(pl.program_id(1) == 0)
    def _():
        xs_ref[0:8, :] = jnp.zeros((8, BRANCH_W), f32)
        hc_ref[...] = jnp.zeros_like(hc_ref)

    x = x_ref[0]
    xs_ref[8:, :] = x
    xc = cb_ref[...] + x * cw_ref[CONV_W - 1:CONV_W, :]
    for k in range(CONV_W - 1):
        xc = xc + xs_ref[pl.ds(8 - (CONV_W - 1 - k), tt), :] * cw_ref[k:k + 1, :]
    xs_ref[0:8, :] = x[tt - 8:, :]
    tail_ref[0] = x[tt - 8:, :]

    a, b = _lru_gates(xc, wa_ref, ba_ref, wx_ref, bx_ref, lam_ref)
    a_ref[...] = a
    b_ref[...] = b
    row = lax.broadcasted_iota(jnp.int32, (8, BRANCH_W), 0)

    def body(i, h):
        r0 = pl.multiple_of(i * 8, 8)
        av = a_ref[pl.ds(r0, 8), :]
        bv = b_ref[pl.ds(r0, 8), :]
        for d in (1, 2, 4):
            a_s = jnp.where(row >= d, pltpu.roll(av, d, 0), 1.0)
            b_s = jnp.where(row >= d, pltpu.roll(bv, d, 0), 0.0)
            bv = av * b_s + bv
            av = av * a_s
        hs = bv + av * h
        b_ref[pl.ds(r0, 8), :] = hs
        return hs[7:8, :]

    h = lax.fori_loop(0, tt // 8, body, hc_ref[...])
    hc_ref[...] = h
    h_ref[0] = h
    zb_ref[0] = (b_ref[...] * _silu(g_ref[0])).astype(bf16)


def _lru_prompt(z3, cw, cb, wa, ba, wx, bx, lam):
    b, s, _ = z3.shape
    tt = min(s, 512)
    row = lambda: pl.BlockSpec((1, BRANCH_W), lambda i, t: (0, 0))
    bd = lambda: pl.BlockSpec((BRANCH_W // MXU_W, MXU_W, MXU_W), lambda i, t: (0, 0, 0))
    return pl.pallas_call(
        _lru_kernel,
        out_shape=(jax.ShapeDtypeStruct((b, s, BRANCH_W), bf16),
                   jax.ShapeDtypeStruct((b, 8, BRANCH_W), f32),
                   jax.ShapeDtypeStruct((b, 1, BRANCH_W), f32)),
        grid=(b, s // tt),
        in_specs=[pl.BlockSpec((1, tt, BRANCH_W), lambda i, t: (i, t, C_LRU_X // BRANCH_W)),
                  pl.BlockSpec((1, tt, BRANCH_W), lambda i, t: (i, t, C_LRU_G // BRANCH_W)),
                  pl.BlockSpec((CONV_W, BRANCH_W), lambda i, t: (0, 0)),
                  row(), bd(), row(), bd(), row(), row()],
        out_specs=(pl.BlockSpec((1, tt, BRANCH_W), lambda i, t: (i, t, 0)),
                   pl.BlockSpec((1, 8, BRANCH_W), lambda i, t: (i, 0, 0)),
                   pl.BlockSpec((1, 1, BRANCH_W), lambda i, t: (i, 0, 0))),
        scratch_shapes=[pltpu.VMEM((tt + 8, BRANCH_W), f32), pltpu.VMEM((tt, BRANCH_W), f32),
                        pltpu.VMEM((tt, BRANCH_W), f32), pltpu.VMEM((1, BRANCH_W), f32)],
        compiler_params=_params(("parallel", "arbitrary")),
        name="lru_prompt",
    )(z3, z3, cw, cb, wa, ba, wx, bx, lam)


def _pool_kernel(x_ref, g_ref, wp_ref, sc_ref, zb_ref, tail_ref, xs_ref):
    tt = x_ref.shape[1]
    t = pl.program_id(1)

    @pl.when(t == 0)
    def _():
        xs_ref[0:16, :] = jnp.zeros((16, BRANCH_W), f32)

    x = x_ref[0]
    xs_ref[16:, :] = x
    pos1 = t * tt + 1 + lax.broadcasted_iota(jnp.int32, (tt, POOL_GD), 0)
    outs = []
    for gi, w in enumerate(POOL_WINDOWS):
        sl = slice(gi * POOL_GD, (gi + 1) * POOL_GD)
        s = xs_ref[:, sl]
        sh = 1
        while sh < w:
            s = s + pltpu.roll(s, sh, 0)
            sh *= 2
        cnt = jnp.minimum(w, pos1).astype(f32)
        pooled = s[16:, :] / cnt - x[:, sl]
        outs.append(_dot(pooled.astype(bf16), wp_ref[gi]))
    y = jnp.concatenate(outs, axis=-1) * sc_ref[...]
    zb_ref[0] = (y * _silu(g_ref[0])).astype(bf16)
    xs_ref[0:16, :] = x[tt - 16:, :]
    tail_ref[0] = x[tt - 16:, :]


def _pool_prompt(z3, wp, sc):
    b, s, _ = z3.shape
    tt = min(s, 512)
    return pl.pallas_call(
        _pool_kernel,
        out_shape=(jax.ShapeDtypeStruct((b, s, BRANCH_W), bf16),
                   jax.ShapeDtypeStruct((b, 16, BRANCH_W), f32)),
        grid=(b, s // tt),
        in_specs=[pl.BlockSpec((1, tt, BRANCH_W), lambda i, t: (i, t, C_POOL_X // BRANCH_W)),
                  pl.BlockSpec((1, tt, BRANCH_W), lambda i, t: (i, t, C_POOL_G // BRANCH_W)),
                  pl.BlockSpec((len(POOL_WINDOWS), POOL_GD, POOL_GD), lambda i, t: (0, 0, 0)),
                  pl.BlockSpec((1, BRANCH_W), lambda i, t: (0, 0))],
        out_specs=(pl.BlockSpec((1, tt, BRANCH_W), lambda i, t: (i, t, 0)),
                   pl.BlockSpec((1, 16, BRANCH_W), lambda i, t: (i, 0, 0))),
        scratch_shapes=[pltpu.VMEM((tt + 16, BRANCH_W), f32)],
        compiler_params=_params(("parallel", "arbitrary")),
        name="pool_prompt",
    )(z3, z3, wp, sc)


def _mix_sample_kernel(past_len, lx_ref, lg_ref, px_ref, pg_ref, conv_ref, h0_ref, pbuf_ref,
                       cw_ref, cb_ref, wa_ref, ba_ref, wx_ref, bx_ref, lam_ref, wp_ref, sc_ref,
                       zl_ref, zp_ref, h_ref):
    x = lx_ref[...]
    xc = cb_ref[...] + x * cw_ref[CONV_W - 1:CONV_W, :]
    for k in range(CONV_W - 1):
        xc = xc + conv_ref[k] * cw_ref[k:k + 1, :]
    a, b = _lru_gates(xc, wa_ref, ba_ref, wx_ref, bx_ref, lam_ref)
    h = a * h0_ref[...] + b
    h_ref[...] = h
    zl_ref[...] = (h * _silu(lg_ref[...])).astype(bf16)

    px = px_ref[...]
    outs = []
    for gi, w in enumerate(POOL_WINDOWS):
        sl = slice(gi * POOL_GD, (gi + 1) * POOL_GD)
        s = px[:, sl]
        for k in range(1, w):
            s = s + pbuf_ref[POOL_BUF - k][:, sl]
        cnt = float(min(w, past_len + 1))
        pooled = s / cnt - px[:, sl]
        outs.append(_dot(pooled.astype(bf16), wp_ref[gi]))
    y = jnp.concatenate(outs, axis=-1) * sc_ref[...]
    zp_ref[...] = (y * _silu(pg_ref[...])).astype(bf16)


def _mix_sample(z2, conv_t, h0, pbuf_t, cw, cb, wa, ba, wx, bx, lam, wp, sc, past_len):
    n = z2.shape[0]
    col = lambda c: pl.BlockSpec((n, BRANCH_W), lambda i: (0, c // BRANCH_W))
    full = lambda a: pl.BlockSpec(a.shape, lambda i: (0,) * a.ndim)
    args = (conv_t, h0, pbuf_t, cw, cb, wa, ba, wx, bx, lam, wp, sc)
    return pl.pallas_call(
        functools.partial(_mix_sample_kernel, past_len),
        out_shape=(jax.ShapeDtypeStruct((n, BRANCH_W), bf16), jax.ShapeDtypeStruct((n, BRANCH_W), bf16),
                   jax.ShapeDtypeStruct((n, BRANCH_W), f32)),
        grid=(1,),
        in_specs=[col(C_LRU_X), col(C_LRU_G), col(C_POOL_X), col(C_POOL_G)] + [full(a) for a in args],
        out_specs=(pl.BlockSpec((n, BRANCH_W), lambda i: (0, 0)),) * 3,
        compiler_params=_params(("arbitrary",)),
        name="mix_sample",
    )(z2, z2, z2, z2, *args)


def _compress_kernel(tab_ref, k_ref, v_ref, wk_ref, wv_ref, fk_ref, sk_ref, fv_ref, sv_ref):
    del tab_ref
    p = pl.program_id(1)
    chunks = PAGE_SIZE // CMP_STRIDE

    @pl.when(p == 0)
    def _():
        sk_ref[...] = jnp.zeros_like(sk_ref)
        sv_ref[...] = jnp.zeros_like(sv_ref)

    for src, w_ref, f_ref, s_ref in ((k_ref, wk_ref, fk_ref, sk_ref), (v_ref, wv_ref, fv_ref, sv_ref)):
        tile = src[0]
        first = jnp.sum((tile * w_ref[0]).reshape(chunks, CMP_STRIDE, KV_W), axis=1)
        second = jnp.sum((tile * w_ref[1]).reshape(chunks, CMP_STRIDE, KV_W), axis=1)
        for m in range(chunks):
            f_ref[0, m % 4, pl.ds(2 * p + m // 4, 1), :] = first[m:m + 1, :]
            if m >= 1:
                s_ref[0, (m - 1) % 4, pl.ds(2 * p + (m - 1) // 4, 1), :] = second[m:m + 1, :]
            else:
                @pl.when(p > 0)
                def _():
                    s_ref[0, 3, pl.ds(2 * p - 1, 1), :] = second[0:1, :]


def _compress(table, src_k, src_v, col_k, col_v, wk, wv, nb):
    npg = table.shape[0] // nb
    out = jax.ShapeDtypeStruct((nb, 4, N_CMP // 4, KV_W), f32)
    ospec = lambda: pl.BlockSpec((1, 4, N_CMP // 4, KV_W), lambda b, p, tab: (b, 0, 0, 0))
    return pl.pallas_call(
        _compress_kernel,
        out_shape=(out,) * 4,
        grid_spec=pltpu.PrefetchScalarGridSpec(
            num_scalar_prefetch=1,
            grid=(nb, npg),
            in_specs=[pl.BlockSpec((1, PAGE_SIZE, KV_W), lambda b, p, tab: (tab[b * npg + p], 0, col_k)),
                      pl.BlockSpec((1, PAGE_SIZE, KV_W), lambda b, p, tab: (tab[b * npg + p], 0, col_v)),
                      pl.BlockSpec((2, PAGE_SIZE, KV_W), lambda b, p, tab: (0, 0, 0)),
                      pl.BlockSpec((2, PAGE_SIZE, KV_W), lambda b, p, tab: (0, 0, 0))],
            out_specs=(ospec(), ospec(), ospec(), ospec())),
        compiler_params=_params(("parallel", "arbitrary")),
        name="compress",
    )(table, src_k, src_v, wk, wv)


def _chunk_maps():
    n_chunk = PP * PAGE_SIZE // CMP_STRIDE
    rows = n_chunk // 4
    chunk_of = np.arange(PP * PAGE_SIZE) // CMP_STRIDE
    first = np.zeros((n_chunk, PP * PAGE_SIZE), np.float32)
    second = np.zeros((n_chunk + 8, PP * PAGE_SIZE), np.float32)
    for i in range(4):
        for jj in range(rows):
            first[i * rows + jj] = chunk_of == 4 * jj + i
            second[i * rows + jj] = chunk_of == 4 * jj + i + 1
    second[n_chunk] = chunk_of == 0
    return jnp.asarray(first, bf16), jnp.asarray(second, bf16)


def _compress_t_kernel(tab_ref, *refs):
    del tab_ref
    k_refs, v_refs = refs[:PP], refs[PP:2 * PP]
    wk_ref, wv_ref, ea_ref, eb_ref, fk_ref, sk_ref, fv_ref, sv_ref = refs[2 * PP:]
    ps = pl.program_id(1)
    rows = PP * PAGE_SIZE // CMP_STRIDE // 4
    r0 = pl.multiple_of(ps * rows, rows)
    for pages, w_ref, f_ref, s_ref in ((k_refs, wk_ref, fk_ref, sk_ref), (v_refs, wv_ref, fv_ref, sv_ref)):
        a1 = jnp.concatenate([(r[0, 0] * w_ref[0]).astype(bf16) for r in pages], axis=1)
        a2 = jnp.concatenate([(r[0, 0] * w_ref[1]).astype(bf16) for r in pages], axis=1)
        first = _nt(ea_ref[...], a1)
        second = _nt(eb_ref[...], a2)
        for i in range(4):
            f_ref[0, i, pl.ds(r0, rows), :] = first[i * rows:(i + 1) * rows]
            s_ref[0, i, pl.ds(r0, rows), :] = second[i * rows:(i + 1) * rows]

        @pl.when(ps > 0)
        def _():
            s_ref[0, 3, pl.ds(r0 - 1, 1), :] = second[4 * rows:4 * rows + 1]


def _compress_t(table, layer, cache_k, cache_v, wk, wv, nb):
    npg = table.shape[0] // nb
    ea, eb = _chunk_maps()
    out = jax.ShapeDtypeStruct((nb, 4, N_CMP // 4, KV_W), f32)
    ospec = lambda: pl.BlockSpec((1, 4, N_CMP // 4, KV_W), lambda b, p, tab: (b, 0, 0, 0))
    page = lambda i: pl.BlockSpec((1, 1, KV_W, PAGE_SIZE),
                                  lambda b, p, tab: (layer, tab[b * npg + p * PP + i], 0, 0))
    full = lambda a: pl.BlockSpec(a.shape, lambda b, p, tab: (0,) * a.ndim)
    return pl.pallas_call(
        _compress_t_kernel,
        out_shape=(out,) * 4,
        grid_spec=pltpu.PrefetchScalarGridSpec(
            num_scalar_prefetch=1,
            grid=(nb, npg // PP),
            in_specs=[page(i) for i in range(PP)] * 2 + [full(wk), full(wv), full(ea), full(eb)],
            out_specs=(ospec(), ospec(), ospec(), ospec())),
        compiler_params=_params(("parallel", "arbitrary")),
        name="compress_t",
    )(table, *([cache_k] * PP), *([cache_v] * PP), wk, wv, ea, eb)


def _finish_compress(f_ref, s_ref, phi_ref):
    blk = (f_ref[0] + s_ref[0]).reshape(N_CMP, KV_W)
    return _dot(blk.astype(bf16), phi_ref[...]).astype(bf16)


def _cmp_end(shape, axis):
    col = lax.broadcasted_iota(jnp.int32, shape, axis)
    n = ((col & (N_BLK - 1)) << 2) + (col >> 7)
    return n * CMP_STRIDE + (CMP_BLK - 1)


def _pick_blocks(score, n_pick):
    lane = lax.broadcasted_iota(jnp.int32, score.shape, 1).astype(f32)
    bias = jnp.full(score.shape, NEG, f32)
    for _ in range(n_pick):
        m = jnp.max(score, axis=-1, keepdims=True)
        first = jnp.min(jnp.where(score == m, lane, float(N_BLK)), axis=-1, keepdims=True)
        hit = lane == first
        bias = jnp.where(hit, 0.0, bias)
        score = jnp.where(hit, -jnp.inf, score)
    return bias


def _block_expand(first_block, n_keys):
    j = lax.broadcasted_iota(jnp.int32, (N_BLK, n_keys), 0)
    c = lax.broadcasted_iota(jnp.int32, (N_BLK, n_keys), 1)
    return jnp.where(j == first_block + (c >> 6), 1.0, 0.0).astype(bf16)


CQ = Q_PER_KV * TQ
WIN_T = WINDOW // TQ + 1
N_PARTS = 9
TILE_COL = HEAD_DIM + 6


def _pick_blocks_t(score, n_pick):
    jrow = lax.broadcasted_iota(jnp.int32, score.shape, 0).astype(f32)
    bias = jnp.full(score.shape, NEG, f32)
    for _ in range(n_pick):
        m = jnp.max(score, axis=0, keepdims=True)
        first = jnp.min(jnp.where(score == m, jrow, float(N_BLK)), axis=0, keepdims=True)
        hit = jrow == first
        bias = jnp.where(hit, 0.0, bias)
        score = jnp.where(hit, -jnp.inf, score)
    return bias


def _nsa_prompt_kernel(q_ref, srow_ref, slope_ref, bg_ref, ng_ref, fk_ref, sk_ref, fv_ref, sv_ref,
                       phik_ref, phivt_ref, cpos_ref, ksa_ref, vst_ref, kwa_ref, vwt_ref, o_ref,
                       kca_scr, vct_scr, qa_scr, oc_scr, m_scr, l_scr, acc_scr):
    qi = pl.program_id(1)
    s0 = qi * TQ

    @pl.when(qi == 0)
    def _():
        kc = _finish_compress(fk_ref, sk_ref, phik_ref)
        for g in range(KV_HEADS):
            kca_scr[g] = jnp.concatenate([kc[:, g * HEAD_DIM:(g + 1) * HEAD_DIM], cpos_ref[...]], axis=1)
        blk_v = (fv_ref[0] + sv_ref[0]).reshape(N_CMP, KV_W).astype(bf16)
        vct_scr[...] = _nt(phivt_ref[...], blk_v).astype(bf16)

    t_of = lambda shape: s0 + (lax.broadcasted_iota(jnp.int32, shape, 1) & (TQ - 1))
    q_t = (q_ref[0] * Q_SCALE).T.astype(bf16)

    ok_c = _cmp_end((N_CMP, CQ), 0) <= t_of((N_CMP, CQ))
    t_q = t_of((N_BLK, TQ))
    jr = lax.broadcasted_iota(jnp.int32, (N_BLK, TQ), 0)
    jb = t_q >> 6
    ok_b = (jr << 6) <= t_q
    forced = jnp.where((jr == 0) | (jr == jb) | (jr == jb - 1), FORCE, 0.0)
    tile_any = [jnp.full((N_BLK // 8, 1), NEG, f32) for _ in LOOP_GROUPS]
    for g in range(KV_HEADS):
        q_g = jnp.concatenate([q_t[(g * Q_PER_KV + r) * HEAD_DIM:(g * Q_PER_KV + r + 1) * HEAD_DIM]
                               for r in range(Q_PER_KV)], axis=1)
        qa = jnp.concatenate([q_g, srow_ref[g]], axis=0)
        st = jnp.where(ok_c, _dot(kca_scr[g], qa), NEG)
        e = jnp.exp2(st - jnp.max(st, axis=0, keepdims=True))
        pt = jnp.where(ok_c, e * (1.0 / jnp.sum(e, axis=0, keepdims=True)), 0.0)
        oc_scr[g] = _dot(vct_scr[g * HEAD_DIM:(g + 1) * HEAD_DIM], pt.astype(bf16))
        ps = None
        for i in range(4):
            for r in range(Q_PER_KV):
                slab = pt[i * N_BLK:(i + 1) * N_BLK, r * TQ:(r + 1) * TQ]
                ps = slab if ps is None else ps + slab
        bias = _pick_blocks_t(jnp.where(ok_b, ps + forced, NEG), N_SEL)
        qa_scr[g] = jnp.concatenate([qa, jnp.concatenate([bias.astype(bf16)] * Q_PER_KV, axis=1)], axis=0)
        any_t = jnp.max(bias.reshape(N_BLK // 8, 8, TQ), axis=1)
        li = [g in grp for grp in LOOP_GROUPS].index(True)
        tile_any[li] = jnp.maximum(tile_any[li], jnp.max(any_t, axis=1, keepdims=True))
    kt_row = lax.broadcasted_iota(jnp.int32, tile_any[0].shape, 0)
    tile_bits = [jnp.sum(jnp.where(ta == 0.0, 1 << kt_row, 0)) for ta in tile_any]

    m_scr[...] = jnp.full(m_scr.shape, NEG, f32)
    l_scr[...] = jnp.zeros_like(l_scr)
    acc_scr[...] = jnp.zeros_like(acc_scr)

    def sel_tile(kt, diagonal, groups):
        k0 = pl.multiple_of(kt * TK, TK)
        tile_off = (k0 - s0).astype(f32)
        key_r = lax.broadcasted_iota(jnp.int32, (TK, N_BLK), 0)
        blk_c = lax.broadcasted_iota(jnp.int32, (TK, N_BLK), 1)
        expand = jnp.where(blk_c == kt * (TK // SEL_BLK) + (key_r >> 6), 1.0, 0.0).astype(bf16)
        k_aug = ksa_ref[0, pl.ds(k0, TK), :]
        v_t = vst_ref[0, kt]
        if diagonal:
            causal = (k0 + lax.broadcasted_iota(jnp.int32, (TK, CQ), 0)) <= t_of((TK, CQ))
        for g in groups:
            lhs = jnp.concatenate([k_aug[:, g * QA_W:(g + 1) * QA_W], expand], axis=1)
            st = _dot(lhs, qa_scr[g])
            if diagonal:
                st = jnp.where(causal, st, NEG)
            c = slope_ref[g] * tile_off
            m_old = m_scr[g]
            m_new = jnp.maximum(m_old, jnp.max(st, axis=0, keepdims=True) + c)
            alpha = jnp.exp2(m_old - m_new)
            pt = jnp.exp2(st - (m_new - c))
            l_scr[g] = alpha * l_scr[g] + jnp.sum(pt, axis=0, keepdims=True)
            acc_scr[g] = alpha * acc_scr[g] + _dot(v_t[g * HEAD_DIM:(g + 1) * HEAD_DIM], pt.astype(bf16))
            m_scr[g] = m_new

    last = s0 // TK
    for groups, bits in zip(LOOP_GROUPS, tile_bits):
        def sel_body(kt, carry, groups=groups, bits=bits):
            @pl.when(((bits >> kt) & 1) == 1)
            def _():
                sel_tile(kt, False, groups)
            return carry

        lax.fori_loop(0, last, sel_body, 0)
    sel_tile(last, True, range(KV_HEADS))

    w_tile = jnp.maximum(qi - WINDOW // TQ, 0)
    n_win = WIN_T * TQ
    dist_w = t_of((n_win, CQ)) - (w_tile * TQ + lax.broadcasted_iota(jnp.int32, (n_win, CQ), 0))
    ok_w = (dist_w >= 0) & (dist_w < WINDOW)
    lane_w = lax.broadcasted_iota(jnp.int32, (n_win, QA_W), 1)
    slab_off = (lax.broadcasted_iota(jnp.int32, (n_win, QA_W), 0) // TQ * TQ).astype(f32).astype(bf16)
    in_tile_col = (lane_w >= TILE_COL) & (lane_w < TILE_COL + 3)
    kw_all = kwa_ref[0, pl.ds(pl.multiple_of(w_tile * TQ, TQ), n_win), :]
    vw_t = jnp.concatenate([vwt_ref[0, w_tile + i] for i in range(WIN_T)], axis=1)
    gates = jax.nn.sigmoid(bg_ref[0].T)
    blocks = []
    for g in range(KV_HEADS):
        lhs = jnp.where(in_tile_col, slab_off, kw_all[:, g * QA_W:(g + 1) * QA_W])
        st = jnp.where(ok_w, _dot(lhs, qa_scr[g, 0:QA_W]), NEG)
        e = jnp.exp2(st - jnp.max(st, axis=0, keepdims=True))
        pt = (e * (1.0 / jnp.sum(e, axis=0, keepdims=True))).astype(bf16)
        o_w = _dot(vw_t[g * HEAD_DIM:(g + 1) * HEAD_DIM], pt)
        o_s = acc_scr[g] * (1.0 / l_scr[g])
        o_c = oc_scr[g]
        heads = []
        for r in range(Q_PER_KV):
            h = g * Q_PER_KV + r
            cols = slice(r * TQ, (r + 1) * TQ)
            heads.append(gates[3 * h:3 * h + 1] * o_c[:, cols] + gates[3 * h + 1:3 * h + 2] * o_s[:, cols]
                         + gates[3 * h + 2:3 * h + 3] * o_w[:, cols])
        for r in range(0, Q_PER_KV, 2):
            blocks.append(jnp.concatenate(heads[r:r + 2], axis=0).T)
    y = jnp.concatenate(blocks, axis=-1)
    o_ref[0] = (y * _silu(ng_ref[0])).astype(bf16)


def _nsa_prompt(z3, fk, sk, fv, sv, phik, phivt):
    b, s, _ = z3.shape
    nq = s // TQ
    parts = _slope_parts().reshape(KV_HEADS, Q_PER_KV, POS_ROWS)
    srow = jnp.asarray(np.repeat(parts.transpose(0, 2, 1), TQ, axis=2), bf16)
    slope = jnp.asarray(np.repeat(np.asarray(SLOPES_LOG2, np.float32).reshape(KV_HEADS, 1, Q_PER_KV), TQ, axis=2))
    ksa = _key_aug(_kv_seg(z3, 2), TK)
    kwa = _key_aug(_kv_seg(z3, 4), TQ)
    vst = _kv_seg(z3, 3).astype(bf16).reshape(b, s // TK, TK, KV_W).swapaxes(2, 3)
    vwt = _kv_seg(z3, 5).astype(bf16).reshape(b, nq, TQ, KV_W).swapaxes(2, 3)
    cpos = _cmp_pos_cols()

    once = pl.Buffered(1)
    cmp = lambda: pl.BlockSpec((1, 4, N_CMP // 4, KV_W), lambda i, t: (i, 0, 0, 0))
    full = lambda a: pl.BlockSpec(a.shape, lambda i, t: (0,) * a.ndim)
    return pl.pallas_call(
        _nsa_prompt_kernel,
        out_shape=jax.ShapeDtypeStruct((b, s, BRANCH_W), bf16),
        grid=(b, nq),
        in_specs=[pl.BlockSpec((1, TQ, BRANCH_W), lambda i, t: (i, t, C_Q // BRANCH_W)),
                  full(srow), full(slope),
                  pl.BlockSpec((1, TQ, BG_W), lambda i, t: (i, t, C_BG // BG_W)),
                  pl.BlockSpec((1, TQ, BRANCH_W), lambda i, t: (i, t, C_NSA_G // BRANCH_W)),
                  cmp(), cmp(), cmp(), cmp(), full(phik), full(phivt), full(cpos),
                  pl.BlockSpec((1, s, KV_HEADS * QA_W), lambda i, t: (i, 0, 0), pipeline_mode=once),
                  pl.BlockSpec((1, s // TK, KV_W, TK), lambda i, t: (i, 0, 0, 0), pipeline_mode=once),
                  pl.BlockSpec((1, s, KV_HEADS * QA_W), lambda i, t: (i, 0, 0), pipeline_mode=once),
                  pl.BlockSpec((1, nq, KV_W, TQ), lambda i, t: (i, 0, 0, 0), pipeline_mode=once)],
        out_specs=pl.BlockSpec((1, TQ, BRANCH_W), lambda i, t: (i, t, 0)),
        scratch_shapes=[pltpu.VMEM((KV_HEADS, N_CMP, QA_W), bf16), pltpu.VMEM((KV_W, N_CMP), bf16),
                        pltpu.VMEM((KV_HEADS, QA_W + N_BLK, CQ), bf16),
                        pltpu.VMEM((KV_HEADS, HEAD_DIM, CQ), f32),
                        pltpu.VMEM((KV_HEADS, 1, CQ), f32), pltpu.VMEM((KV_HEADS, 1, CQ), f32),
                        pltpu.VMEM((KV_HEADS, HEAD_DIM, CQ), f32)],
        compiler_params=_params(("parallel", "arbitrary")),
        name="nsa_prompt",
    )(z3, srow, slope, z3, z3, fk, sk, fv, sv, phik, phivt, cpos, ksa, vst, kwa, vwt)


def _by_group(fn):
    hg = lax.broadcasted_iota(jnp.int32, (N_HEADS, 1), 0) >> 2
    out = fn(0)
    for g in range(1, KV_HEADS):
        out = jnp.where(hg == g, fn(g), out)
    return out


def _nsa_sample_kernel(past_len, tab_ref, q_ref, bg_ref, ng_ref, slope_ref,
                       fk_ref, sk_ref, fv_ref, sv_ref, phik_ref, phiv_ref,
                       ksn_ref, vsn_ref, kwn_ref, vwn_ref, bk_ref, bv_ref, *refs):
    del tab_ref
    kp_refs, vp_refs = refs[:PP_SEL], refs[PP_SEL:2 * PP_SEL]
    o_ref, q_scr, sel_scr, oc_scr, ow_scr, m_scr, l_scr, acc_scr = refs[2 * PP_SEL:]
    p = pl.program_id(1)
    slope = slope_ref[:, 0:1]
    gsl = lambda g: slice(g * HEAD_DIM, (g + 1) * HEAD_DIM)

    def per_head(row_ref):
        row = row_ref[0]
        return _by_group(lambda g: jnp.broadcast_to(row[:, gsl(g)], (N_HEADS, HEAD_DIM)))

    @pl.when(p == 0)
    def _():
        qrow = q_ref[0]
        q16 = jnp.concatenate([qrow[:, h * HEAD_DIM:(h + 1) * HEAD_DIM] for h in range(N_HEADS)], axis=0)
        q_scr[...] = q16
        q16f = q16.astype(f32)
        kc = _finish_compress(fk_ref, sk_ref, phik_ref)
        vc = _finish_compress(fv_ref, sv_ref, phiv_ref)

        dist_c = past_len - _cmp_end((1, N_CMP), 1)
        ok_c = dist_c >= 0
        s = _by_group(lambda g: _nt(q16, kc[:, gsl(g)])) - slope * dist_c.astype(f32)
        s = jnp.where(ok_c, s, NEG)
        e = jnp.exp(s - jnp.max(s, axis=-1, keepdims=True))
        pc = jnp.where(ok_c, e * (1.0 / jnp.sum(e, axis=-1, keepdims=True)), 0.0)
        pcb = pc.astype(bf16)
        oc_scr[...] = _by_group(lambda g: _dot(pcb, vc[:, gsl(g)]))
        ps16 = (pc[:, 0:N_BLK] + pc[:, N_BLK:2 * N_BLK]) + (pc[:, 2 * N_BLK:3 * N_BLK] + pc[:, 3 * N_BLK:])
        ps = jnp.concatenate([jnp.sum(ps16[g * Q_PER_KV:(g + 1) * Q_PER_KV], axis=0, keepdims=True)
                              for g in range(KV_HEADS)] + [jnp.zeros((8 - KV_HEADS, N_BLK), f32)], axis=0)
        jl = lax.broadcasted_iota(jnp.int32, (8, N_BLK), 1)
        jb = past_len // SEL_BLK
        forced = jnp.where((jl == 0) | (jl == jb) | (jl == jb - 1), FORCE, 0.0)
        bias8 = _pick_blocks(ps + forced, N_SEL - 1)
        sel_scr[...] = jnp.concatenate(
            [jnp.broadcast_to(bias8[g:g + 1], (Q_PER_KV, N_BLK)) for g in range(KV_HEADS)], axis=0).astype(bf16)

        wb = bk_ref.shape[-1]
        dist_w = wb - lax.broadcasted_iota(jnp.int32, (1, wb), 1)
        ok_w = (dist_w >= 0) & (dist_w < WINDOW)
        bk = bk_ref[0, 0].astype(bf16)
        bv = bv_ref[0, 0].astype(bf16)
        s_buf = _by_group(lambda g: _dot(q16, bk[gsl(g)])) - slope * dist_w.astype(f32)
        s_buf = jnp.where(ok_w, s_buf, NEG)
        s_new = jnp.sum(q16f * per_head(kwn_ref), axis=-1, keepdims=True)
        m_w = jnp.maximum(jnp.max(s_buf, axis=-1, keepdims=True), s_new)
        e_buf = jnp.exp(s_buf - m_w)
        e_new = jnp.exp(s_new - m_w)
        ebb = e_buf.astype(bf16)
        num = _by_group(lambda g: _nt(ebb, bv[gsl(g)])) + e_new * per_head(vwn_ref)
        ow_scr[...] = num * (1.0 / (jnp.sum(e_buf, axis=-1, keepdims=True) + e_new))

        m_scr[...] = jnp.sum(q16f * per_head(ksn_ref), axis=-1, keepdims=True)
        l_scr[...] = jnp.ones_like(l_scr)
        acc_scr[...] = per_head(vsn_ref)

    q16 = q_scr[...]
    n_keys = PP_SEL * PAGE_SIZE
    kp = jnp.concatenate([r[0, 0].astype(bf16) for r in kp_refs], axis=1)
    vp = jnp.concatenate([r[0, 0].astype(bf16) for r in vp_refs], axis=1)
    dist = past_len - (p * n_keys + lax.broadcasted_iota(jnp.int32, (1, n_keys), 1))
    expand = _block_expand(p * (n_keys // SEL_BLK), n_keys)
    s = _by_group(lambda g: _dot(q16, kp[gsl(g)])) - slope * dist.astype(f32) + _dot(sel_scr[...], expand)
    m_old = m_scr[...]
    m_new = jnp.maximum(m_old, jnp.max(s, axis=-1, keepdims=True))
    alpha = jnp.exp(m_old - m_new)
    pr = jnp.exp(s - m_new)
    prb = pr.astype(bf16)
    l_scr[...] = alpha * l_scr[...] + jnp.sum(pr, axis=-1, keepdims=True)
    acc_scr[...] = alpha * acc_scr[...] + _by_group(lambda g: _nt(prb, vp[gsl(g)]))
    m_scr[...] = m_new

    @pl.when(p == pl.num_programs(1) - 1)
    def _():
        o_s = acc_scr[...] * (1.0 / l_scr[...])
        gates = jax.nn.sigmoid(bg_ref[0])
        lane = lax.broadcasted_iota(jnp.int32, (N_HEADS, BG_W), 1)
        h3 = 3 * lax.broadcasted_iota(jnp.int32, (N_HEADS, BG_W), 0)
        gate = lambda n: jnp.sum(jnp.where(lane == h3 + n, gates, 0.0), axis=-1, keepdims=True)
        y16 = gate(0) * oc_scr[...] + gate(1) * o_s + gate(2) * ow_scr[...]
        y = jnp.concatenate([y16[h:h + 1, :] for h in range(N_HEADS)], axis=-1)
        o_ref[0] = (y * _silu(ng_ref[0])).astype(bf16)


def _nsa_sample(table, layer, qb, z2, slopes, fk, sk, fv, sv, phik, phiv, pool_k, pool_v, buf_k, buf_v, past_len):
    nb = qb.shape[0]
    npg = table.shape[0] // nb
    z3 = z2.reshape(nb, 1, IN_WP)
    tok = lambda w, c: pl.BlockSpec((1, 1, w), lambda b, p, tab: (b, 0, c))
    cmp = lambda: pl.BlockSpec((1, 4, N_CMP // 4, KV_W), lambda b, p, tab: (b, 0, 0, 0))
    phi = lambda: pl.BlockSpec((KV_W, KV_W), lambda b, p, tab: (0, 0))
    wb = buf_k.shape[-1]
    win = lambda: pl.BlockSpec((1, 1, KV_W, wb), lambda b, p, tab: (layer, b, 0, 0))
    page = lambda i: pl.BlockSpec((1, 1, KV_W, PAGE_SIZE),
                                  lambda b, p, tab: (layer, tab[b * npg + p * PP_SEL + i], 0, 0))
    kvc = C_KV // KV_W
    return pl.pallas_call(
        functools.partial(_nsa_sample_kernel, past_len),
        out_shape=jax.ShapeDtypeStruct((nb, 1, BRANCH_W), bf16),
        grid_spec=pltpu.PrefetchScalarGridSpec(
            num_scalar_prefetch=1,
            grid=(nb, npg // PP_SEL),
            in_specs=[tok(BRANCH_W, 0), tok(BG_W, C_BG // BG_W), tok(BRANCH_W, C_NSA_G // BRANCH_W),
                      pl.BlockSpec((N_HEADS, 128), lambda b, p, tab: (0, 0)),
                      cmp(), cmp(), cmp(), cmp(), phi(), phi(),
                      tok(KV_W, kvc + 2), tok(KV_W, kvc + 3), tok(KV_W, kvc + 4), tok(KV_W, kvc + 5),
                      win(), win()] + [page(i) for i in range(PP_SEL)] * 2,
            out_specs=pl.BlockSpec((1, 1, BRANCH_W), lambda b, p, tab: (b, 0, 0)),
            scratch_shapes=[pltpu.VMEM((N_HEADS, HEAD_DIM), bf16), pltpu.VMEM((N_HEADS, N_BLK), bf16),
                            pltpu.VMEM((N_HEADS, HEAD_DIM), f32), pltpu.VMEM((N_HEADS, HEAD_DIM), f32),
                            pltpu.VMEM((N_HEADS, 1), f32), pltpu.VMEM((N_HEADS, 1), f32),
                            pltpu.VMEM((N_HEADS, HEAD_DIM), f32)]),
        compiler_params=_params(("parallel", "arbitrary")),
        name="nsa_sample",
    )(table, qb.reshape(nb, 1, BRANCH_W), z3, z3, slopes, fk, sk, fv, sv, phik, phiv,
      z3, z3, z3, z3, buf_k, buf_v, *([pool_k] * PP_SEL), *([pool_v] * PP_SEL))


def _merge_kernel(zl_ref, zp_ref, zn_ref, m0_ref, m1_ref, m2_ref, wb_ref, wo_ref, g_ref, x_ref, y_ref):
    acc = None
    for n, (zz, mg) in enumerate(((zl_ref, m0_ref), (zp_ref, m1_ref), (zn_ref, m2_ref))):
        term = jax.nn.sigmoid(mg[...]) * _dot(zz[...], wb_ref[n])
        acc = term if acc is None else acc + term
    out = _dot(acc.astype(bf16), wo_ref[...])
    ms = jnp.mean(out * out, axis=-1, keepdims=True)
    y_ref[...] = x_ref[...] + out * lax.rsqrt(ms + EPS) * g_ref[...]


def _merge(zl, zp, zn, z2, wb, wo, g_row, x2d):
    n = x2d.shape[0]
    tm = min(n, 256)
    rowblk = lambda c: pl.BlockSpec((tm, D_MODEL), lambda i: (i, c))
    return pl.pallas_call(
        _merge_kernel,
        out_shape=jax.ShapeDtypeStruct((n, D_MODEL), f32),
        grid=(n // tm,),
        in_specs=[rowblk(0), rowblk(0), rowblk(0),
                  rowblk(C_MG // D_MODEL), rowblk(C_MG // D_MODEL + 1), rowblk(C_MG // D_MODEL + 2),
                  pl.BlockSpec((N_BRANCH, BRANCH_W, D_MODEL), lambda i: (0, 0, 0)),
                  pl.BlockSpec((D_MODEL, D_MODEL), lambda i: (0, 0)),
                  pl.BlockSpec((1, D_MODEL), lambda i: (0, 0)),
                  rowblk(0)],
        out_specs=rowblk(0),
        compiler_params=_params(("parallel",)),
        name="merge",
    )(zl, zp, zn, z2, z2, z2, wb, wo, g_row, x2d)


def _block_diag(w, per):
    n, d, _ = w.shape
    eye = jnp.eye(per, dtype=w.dtype)
    t = jnp.einsum('cpde,pq->cpdqe', w.reshape(n // per, per, d, d), eye)
    return t.reshape(n // per, per * d, per * d)


def _pack_w_in(w):
    old_kv = 6 * BRANCH_W
    old_bg = old_kv + 6 * KV_W
    old_mg = old_bg + N_BRANCH * N_HEADS
    parts = [w[:, :old_kv], w[:, old_mg:], w[:, old_kv:old_bg], w[:, old_bg:old_mg]]
    packed = jnp.concatenate(parts, axis=1)
    return jnp.pad(packed, ((0, 0), (0, IN_WP - packed.shape[1]))).astype(bf16)


def _tile_wpos(w_pos):
    halves = w_pos.reshape(2, CMP_STRIDE, HEAD_DIM)
    return jnp.tile(halves, (1, PAGE_SIZE // CMP_STRIDE, KV_HEADS))


def _tile_wpos_t(w_pos):
    halves = w_pos.reshape(2, CMP_STRIDE, HEAD_DIM).swapaxes(1, 2)
    return jnp.tile(halves, (1, KV_HEADS, PAGE_SIZE // CMP_STRIDE))


def _lanes_last(cache):
    d, n, rows = cache.shape[:3]
    return jnp.transpose(cache, (0, 1, 3, 4, 2)).reshape(d, n, KV_W, rows)


def _slope_parts():
    cols = np.zeros((N_HEADS, POS_ROWS), np.float32)
    rnd = lambda v: np.float32(np.float32(v).astype(bf16))
    for h, s in enumerate(SLOPES_LOG2):
        s1 = rnd(s)
        s2 = rnd(np.float32(s) - s1)
        s3 = rnd(np.float32(s) - s1 - s2)
        cols[h, 0:N_PARTS] = [s1, s2, s3] * 3
    return cols


def _split_pos(pos, shift):
    cols = np.zeros((pos.shape[0], POS_ROWS), np.float32)
    cols[:, 0:3] = ((pos >> shift) << shift)[:, None]
    cols[:, 3:6] = (pos & ((1 << shift) - 1))[:, None]
    return cols


def _key_aug(k_rows, tile):
    b, s, _ = k_rows.shape
    pos = jnp.asarray(_split_pos(np.arange(s) % tile, 4), bf16)
    kg = k_rows.astype(bf16).reshape(b, s, KV_HEADS, HEAD_DIM)
    posb = jnp.broadcast_to(pos[None, :, None, :], (b, s, KV_HEADS, POS_ROWS))
    return jnp.concatenate([kg, posb], axis=-1).reshape(b, s, KV_HEADS * QA_W)


def _cmp_pos_cols():
    slot = np.arange(N_CMP)
    n = ((slot & (N_BLK - 1)) << 2) + (slot >> 7)
    return jnp.asarray(_split_pos(n * CMP_STRIDE + CMP_BLK - 1, 8), bf16)


def _kv_seg(z, i):
    return z[..., C_KV + i * KV_W:C_KV + (i + 1) * KV_W]


def kernel(x_prompt, x_sample, cache_cmp_k, cache_cmp_v, cache_sel_k, cache_sel_v, cache_win_k, cache_win_v, state_conv, state_lru, state_pool, page_table, g_pre, g_post, w_in, conv_w, conv_b, w_rg_a, b_rg_a, w_rg_x, b_rg_x, lru_lambda, w_pool, pool_scale, cmp_pos_k, cmp_phi_k, cmp_pos_v, cmp_phi_v, w_branch, w_out):
    depth = w_in.shape[0]
    bp, seq, _ = x_prompt.shape
    bs = x_sample.shape[0]
    n_pages = page_table.shape[1]
    past_len = n_pages * PAGE_SIZE
    n_phys = cache_cmp_k.shape[1]
    assert seq == N_BLK * SEL_BLK and past_len == N_BLK * SEL_BLK and x_sample.shape[1] == 1
    wb = cache_win_k.shape[2]

    table_s = page_table.reshape(-1).astype(jnp.int32)
    table_p = jnp.arange(bp * (seq // PAGE_SIZE), dtype=jnp.int32)
    slopes = jnp.broadcast_to(jnp.asarray(SLOPES, f32)[:, None], (N_HEADS, 128))
    row = lambda v: v.reshape(1, -1)
    cmp_kt, cmp_vt = _lanes_last(cache_cmp_k), _lanes_last(cache_cmp_v)
    sel_kt, sel_vt = _lanes_last(cache_sel_k), _lanes_last(cache_sel_v)
    win_kt, win_vt = _lanes_last(cache_win_k), _lanes_last(cache_win_v)

    xp = x_prompt.reshape(bp * seq, D_MODEL)
    xs = x_sample.reshape(bs, D_MODEL)
    pr = [[] for _ in range(9)]
    sm = [[] for _ in range(9)]
    for l in range(depth):
        w_packed = _pack_w_in(w_in[l])
        wa = _block_diag(w_rg_a[l], MXU_W // LRU_BD).astype(bf16)
        wx = _block_diag(w_rg_x[l], MXU_W // LRU_BD).astype(bf16)
        wp = w_pool[l].astype(bf16)
        phik = _block_diag(jnp.broadcast_to(cmp_phi_k[l], (KV_HEADS, HEAD_DIM, HEAD_DIM)), KV_HEADS)[0].astype(bf16)
        phiv = _block_diag(jnp.broadcast_to(cmp_phi_v[l], (KV_HEADS, HEAD_DIM, HEAD_DIM)), KV_HEADS)[0].astype(bf16)
        wpos_k = _tile_wpos(cmp_pos_k[l])
        wpos_v = _tile_wpos(cmp_pos_v[l])
        wbr = w_branch[l].astype(bf16)
        wo = w_out[l].astype(bf16)
        lru_w = (conv_w[l], row(conv_b[l]), wa, row(b_rg_a[l]), wx, row(b_rg_x[l]), row(lru_lambda[l]))

        z = _inproj(xp, row(g_pre[l]), w_packed)
        z3 = z.reshape(bp, seq, IN_WP)
        zl, conv_tail, h_p = _lru_prompt(z3, *lru_w)
        zpool, pool_tail = _pool_prompt(z3, wp, row(pool_scale[l]))
        zr = z.reshape(bp * seq // PAGE_SIZE, PAGE_SIZE, IN_WP)
        fk, sk, fv, sv = _compress(table_p, zr, zr, C_KV // KV_W, C_KV // KV_W + 1, wpos_k, wpos_v, bp)
        zn = _nsa_prompt(z3, fk, sk, fv, sv, phik, phiv.T)
        xp = _merge(zl.reshape(bp * seq, BRANCH_W), zpool.reshape(bp * seq, BRANCH_W),
                    zn.reshape(bp * seq, BRANCH_W), z, wbr, wo, row(g_post[l]), xp)
        kv_rows = [_kv_seg(z3, i).reshape(bp, seq, KV_HEADS, HEAD_DIM) for i in range(6)]
        wlen = min(WINDOW, seq)
        st_p = kv_rows[:4] + [kv_rows[4][:, -wlen:], kv_rows[5][:, -wlen:],
                              conv_tail[:, -(CONV_W - 1):], h_p[:, 0], pool_tail[:, -POOL_BUF:]]

        zs = _inproj(xs, row(g_pre[l]), w_packed)
        zls, zps, h_s = _mix_sample(zs, state_conv[l].swapaxes(0, 1), state_lru[l], state_pool[l].swapaxes(0, 1),
                                    *lru_w, wp, row(pool_scale[l]), past_len)
        fk, sk, fv, sv = _compress_t(table_s, l, cmp_kt, cmp_vt, _tile_wpos_t(cmp_pos_k[l]),
                                     _tile_wpos_t(cmp_pos_v[l]), bs)
        qs = (zs[:, C_Q:C_Q + BRANCH_W] * (HEAD_DIM ** -0.5)).astype(bf16)
        zns = _nsa_sample(table_s, l, qs, zs, slopes, fk, sk, fv, sv, phik, phiv,
                          sel_kt, sel_vt, win_kt, win_vt, past_len)
        xs = _merge(zls, zps, zns.reshape(bs, BRANCH_W), zs, wbr, wo, row(g_post[l]), xs)
        kv_new = [_kv_seg(zs, i).reshape(bs, 1, KV_HEADS, HEAD_DIM) for i in range(6)]
        st_s = kv_new[:4] + [jnp.concatenate([cache_win_k[l], kv_new[4]], axis=1)[:, -wb:],
                             jnp.concatenate([cache_win_v[l], kv_new[5]], axis=1)[:, -wb:],
                             jnp.concatenate([state_conv[l], zs[:, None, C_LRU_X:C_LRU_X + BRANCH_W]], axis=1)[:, -(CONV_W - 1):],
                             h_s,
                             jnp.concatenate([state_pool[l], zs[:, None, C_POOL_X:C_POOL_X + BRANCH_W]], axis=1)[:, -POOL_BUF:]]
        for i in range(9):
            pr[i].append(st_p[i])
            sm[i].append(st_s[i])

    out = [xp.reshape(bp, seq, D_MODEL), xs.reshape(bs, 1, D_MODEL)]
    for i in range(9):
        out += [jnp.stack(pr[i]), jnp.stack(sm[i])]
    return tuple(out)
```

```python
import functools

import numpy as np
import jax
import jax.numpy as jnp
from jax import lax
from jax.experimental import pallas as pl
from jax.experimental.pallas import tpu as pltpu

f32 = jnp.float32
bf16 = jnp.bfloat16

D_MODEL = 1024
BRANCH_W = 1024
N_BRANCH = 3
LRU_BLOCKS = 16
LRU_BD = BRANCH_W // LRU_BLOCKS
CONV_W = 4
LRU_C = 8.0
POOL_WINDOWS = (2, 4, 8, 16)
POOL_GD = BRANCH_W // len(POOL_WINDOWS)
POOL_BUF = max(POOL_WINDOWS) - 1
N_HEADS = 16
HEAD_DIM = 64
KV_HEADS = 4
Q_PER_KV = N_HEADS // KV_HEADS
KV_W = KV_HEADS * HEAD_DIM
CMP_STRIDE = 16
CMP_BLK = 2 * CMP_STRIDE
SEL_BLK = 64
N_SEL = 16
WINDOW = 512
PAGE_SIZE = 128
FORCE = 1e4
NEG = -1e30
EPS = 1e-6

C_LRU_X, C_LRU_G, C_POOL_X, C_POOL_G, C_Q, C_NSA_G, C_MG = 0, 1024, 2048, 3072, 4096, 5120, 6144
C_KV = C_MG + N_BRANCH * D_MODEL
C_BG = C_KV + 6 * KV_W
BG_W = 128
IN_WP = 11264
IN_TN = 1024

MXU_W = 256
VMEM_LIMIT = 56 * 1024 * 1024

N_CMP = 512
N_BLK = 128
TQ = 256
TK = 512
PP = 8
PP_SEL = 16
POS_ROWS = 64
QA_W = HEAD_DIM + POS_ROWS

SLOPES = [float(np.float32(2.0 ** (-8.0 * (h + 1) / N_HEADS))) for h in range(N_HEADS)]
LOG2E = float(np.log2(np.e))
SLOPES_LOG2 = [float(np.float32(s * LOG2E)) for s in SLOPES]
Q_SCALE = float(np.float32(HEAD_DIM ** -0.5 * LOG2E))
LOOP_GROUPS = ((0,), (1,), (2,), (3,))


def _nt(a, b):
    return lax.dot_general(a, b, (((1,), (1,)), ((), ())), preferred_element_type=f32)


def _dot(a, b):
    return jnp.dot(a, b, preferred_element_type=f32)


def _silu(x):
    return x * jax.nn.sigmoid(x)


def _params(sem):
    return pltpu.CompilerParams(dimension_semantics=sem, vmem_limit_bytes=VMEM_LIMIT)


def _inproj_kernel(x_ref, g_ref, w_ref, o_ref, u_ref):
    @pl.when(pl.program_id(1) == 0)
    def _():
        x = x_ref[...]
        ms = jnp.mean(x * x, axis=-1, keepdims=True)
        u_ref[...] = (x * lax.rsqrt(ms + EPS) * g_ref[...]).astype(bf16)

    o_ref[...] = _dot(u_ref[...], w_ref[...])


def _inproj(x2d, g_row, w_packed):
    n = x2d.shape[0]
    tm = min(n, 1024)
    return pl.pallas_call(
        _inproj_kernel,
        out_shape=jax.ShapeDtypeStruct((n, IN_WP), f32),
        grid=(n // tm, IN_WP // IN_TN),
        in_specs=[pl.BlockSpec((tm, D_MODEL), lambda i, j: (i, 0)),
                  pl.BlockSpec((1, D_MODEL), lambda i, j: (0, 0)),
                  pl.BlockSpec((D_MODEL, IN_TN), lambda i, j: (0, j))],
        out_specs=pl.BlockSpec((tm, IN_TN), lambda i, j: (i, j)),
        scratch_shapes=[pltpu.VMEM((tm, D_MODEL), bf16)],
        compiler_params=_params(("parallel", "arbitrary")),
        name="inproj",
    )(x2d, g_row, w_packed)


def _lru_gates(xc, wa_ref, ba_ref, wx_ref, bx_ref, lam_ref):
    xb = xc.astype(bf16)
    ra, ri = [], []
    for c in range(BRANCH_W // MXU_W):
        sl = slice(c * MXU_W, (c + 1) * MXU_W)
        ra.append(_dot(xb[:, sl], wa_ref[c]))
        ri.append(_dot(xb[:, sl], wx_ref[c]))
    r = jax.nn.sigmoid(jnp.concatenate(ra, axis=-1) + ba_ref[...])
    i = jax.nn.sigmoid(jnp.concatenate(ri, axis=-1) + bx_ref[...])
    nl = -lam_ref[...]
    softplus = jnp.maximum(nl, 0.0) + jnp.log1p(jnp.exp(-jnp.abs(nl)))
    log_a = -LRU_C * r * softplus
    a = jnp.exp(log_a)
    b = jnp.sqrt(1.0 - a * a) * (i * xc)
    return a, b


def _lru_kernel(x_ref, g_ref, cw_ref, cb_ref, wa_ref, ba_ref, wx_ref, bx_ref, lam_ref,
                zb_ref, tail_ref, h_ref, xs_ref, a_ref, b_ref, hc_ref):
    tt = x_ref.shape[1]

    @pl.when(pl.program_id(1) == 0)
    def _():
        xs_ref[0:8, :] = jnp.zeros((8, BRANCH_W), f32)
        hc_ref[...] = jnp.zeros_like(hc_ref)

    x = x_ref[0]
    xs_ref[8:, :] = x
    xc = cb_ref[...] + x * cw_ref[CONV_W - 1:CONV_W, :]
    for k in range(CONV_W - 1):
        xc = xc + xs_ref[pl.ds(8 - (CONV_W - 1 - k), tt), :] * cw_ref[k:k + 1, :]
    xs_ref[0:8, :] = x[tt - 8:, :]
    tail_ref[0] = x[tt - 8:, :]

    a, b = _lru_gates(xc, wa_ref, ba_ref, wx_ref, bx_ref, lam_ref)
    a_ref[...] = a
    b_ref[...] = b
    row = lax.broadcasted_iota(jnp.int32, (8, BRANCH_W), 0)

    def body(i, h):
        r0 = pl.multiple_of(i * 8, 8)
        av = a_ref[pl.ds(r0, 8), :]
        bv = b_ref[pl.ds(r0, 8), :]
        for d in (1, 2, 4):
            a_s = jnp.where(row >= d, pltpu.roll(av, d, 0), 1.0)
            b_s = jnp.where(row >= d, pltpu.roll(bv, d, 0), 0.0)
            bv = av * b_s + bv
            av = av * a_s
        hs = bv + av * h
        b_ref[pl.ds(r0, 8), :] = hs
        return hs[7:8, :]

    h = lax.fori_loop(0, tt // 8, body, hc_ref[...])
    hc_ref[...] = h
    h_ref[0] = h
    zb_ref[0] = (b_ref[...] * _silu(g_ref[0])).astype(bf16)


def _lru_prompt(z3, cw, cb, wa, ba, wx, bx, lam):
    b, s, _ = z3.shape
    tt = min(s, 512)
    row = lambda: pl.BlockSpec((1, BRANCH_W), lambda i, t: (0, 0))
    bd = lambda: pl.BlockSpec((BRANCH_W // MXU_W, MXU_W, MXU_W), lambda i, t: (0, 0, 0))
    return pl.pallas_call(
        _lru_kernel,
        out_shape=(jax.ShapeDtypeStruct((b, s, BRANCH_W), bf16),
                   jax.ShapeDtypeStruct((b, 8, BRANCH_W), f32),
                   jax.ShapeDtypeStruct((b, 1, BRANCH_W), f32)),
        grid=(b, s // tt),
        in_specs=[pl.BlockSpec((1, tt, BRANCH_W), lambda i, t: (i, t, C_LRU_X // BRANCH_W)),
                  pl.BlockSpec((1, tt, BRANCH_W), lambda i, t: (i, t, C_LRU_G // BRANCH_W)),
                  pl.BlockSpec((CONV_W, BRANCH_W), lambda i, t: (0, 0)),
                  row(), bd(), row(), bd(), row(), row()],
        out_specs=(pl.BlockSpec((1, tt, BRANCH_W), lambda i, t: (i, t, 0)),
                   pl.BlockSpec((1, 8, BRANCH_W), lambda i, t: (i, 0, 0)),
                   pl.BlockSpec((1, 1, BRANCH_W), lambda i, t: (i, 0, 0))),
        scratch_shapes=[pltpu.VMEM((tt + 8, BRANCH_W), f32), pltpu.VMEM((tt, BRANCH_W), f32),
                        pltpu.VMEM((tt, BRANCH_W), f32), pltpu.VMEM((1, BRANCH_W), f32)],
        compiler_params=_params(("parallel", "arbitrary")),
        name="lru_prompt",
    )(z3, z3, cw, cb, wa, ba, wx, bx, lam)


def _pool_kernel(x_ref, g_ref, wp_ref, sc_ref, zb_ref, tail_ref, xs_ref):
    tt = x_ref.shape[1]
    t = pl.program_id(1)

    @pl.when(t == 0)
    def _():
        xs_ref[0:16, :] = jnp.zeros((16, BRANCH_W), f32)

    x = x_ref[0]
    xs_ref[16:, :] = x
    pos1 = t * tt + 1 + lax.broadcasted_iota(jnp.int32, (tt, POOL_GD), 0)
    outs = []
    for gi, w in enumerate(POOL_WINDOWS):
        sl = slice(gi * POOL_GD, (gi + 1) * POOL_GD)
        s = xs_ref[:, sl]
        sh = 1
        while sh < w:
            s = s + pltpu.roll(s, sh, 0)
            sh *= 2
        cnt = jnp.minimum(w, pos1).astype(f32)
        pooled = s[16:, :] / cnt - x[:, sl]
        outs.append(_dot(pooled.astype(bf16), wp_ref[gi]))
    y = jnp.concatenate(outs, axis=-1) * sc_ref[...]
    zb_ref[0] = (y * _silu(g_ref[0])).astype(bf16)
    xs_ref[0:16, :] = x[tt - 16:, :]
    tail_ref[0] = x[tt - 16:, :]


def _pool_prompt(z3, wp, sc):
    b, s, _ = z3.shape
    tt = min(s, 512)
    return pl.pallas_call(
        _pool_kernel,
        out_shape=(jax.ShapeDtypeStruct((b, s, BRANCH_W), bf16),
                   jax.ShapeDtypeStruct((b, 16, BRANCH_W), f32)),
        grid=(b, s // tt),
        in_specs=[pl.BlockSpec((1, tt, BRANCH_W), lambda i, t: (i, t, C_POOL_X // BRANCH_W)),
                  pl.BlockSpec((1, tt, BRANCH_W), lambda i, t: (i, t, C_POOL_G // BRANCH_W)),
                  pl.BlockSpec((len(POOL_WINDOWS), POOL_GD, POOL_GD), lambda i, t: (0, 0, 0)),
                  pl.BlockSpec((1, BRANCH_W), lambda i, t: (0, 0))],
        out_specs=(pl.BlockSpec((1, tt, BRANCH_W), lambda i, t: (i, t, 0)),
                   pl.BlockSpec((1, 16, BRANCH_W), lambda i, t: (i, 0, 0))),
        scratch_shapes=[pltpu.VMEM((tt + 16, BRANCH_W), f32)],
        compiler_params=_params(("parallel", "arbitrary")),
        name="pool_prompt",
    )(z3, z3, wp, sc)


def _mix_sample_kernel(past_len, lx_ref, lg_ref, px_ref, pg_ref, conv_ref, h0_ref, pbuf_ref,
                       cw_ref, cb_ref, wa_ref, ba_ref, wx_ref, bx_ref, lam_ref, wp_ref, sc_ref,
                       zl_ref, zp_ref, h_ref):
    x = lx_ref[...]
    xc = cb_ref[...] + x * cw_ref[CONV_W - 1:CONV_W, :]
    for k in range(CONV_W - 1):
        xc = xc + conv_ref[k] * cw_ref[k:k + 1, :]
    a, b = _lru_gates(xc, wa_ref, ba_ref, wx_ref, bx_ref, lam_ref)
    h = a * h0_ref[...] + b
    h_ref[...] = h
    zl_ref[...] = (h * _silu(lg_ref[...])).astype(bf16)

    px = px_ref[...]
    outs = []
    for gi, w in enumerate(POOL_WINDOWS):
        sl = slice(gi * POOL_GD, (gi + 1) * POOL_GD)
        s = px[:, sl]
        for k in range(1, w):
            s = s + pbuf_ref[POOL_BUF - k][:, sl]
        cnt = float(min(w, past_len + 1))
        pooled = s / cnt - px[:, sl]
        outs.append(_dot(pooled.astype(bf16), wp_ref[gi]))
    y = jnp.concatenate(outs, axis=-1) * sc_ref[...]
    zp_ref[...] = (y * _silu(pg_ref[...])).astype(bf16)


def _mix_sample(z2, conv_t, h0, pbuf_t, cw, cb, wa, ba, wx, bx, lam, wp, sc, past_len):
    n = z2.shape[0]
    col = lambda c: pl.BlockSpec((n, BRANCH_W), lambda i: (0, c // BRANCH_W))
    full = lambda a: pl.BlockSpec(a.shape, lambda i: (0,) * a.ndim)
    args = (conv_t, h0, pbuf_t, cw, cb, wa, ba, wx, bx, lam, wp, sc)
    return pl.pallas_call(
        functools.partial(_mix_sample_kernel, past_len),
        out_shape=(jax.ShapeDtypeStruct((n, BRANCH_W), bf16), jax.ShapeDtypeStruct((n, BRANCH_W), bf16),
                   jax.ShapeDtypeStruct((n, BRANCH_W), f32)),
        grid=(1,),
        in_specs=[col(C_LRU_X), col(C_LRU_G), col(C_POOL_X), col(C_POOL_G)] + [full(a) for a in args],
        out_specs=(pl.BlockSpec((n, BRANCH_W), lambda i: (0, 0)),) * 3,
        compiler_params=_params(("arbitrary",)),
        name="mix_sample",
    )(z2, z2, z2, z2, *args)


def _compress_kernel(tab_ref, k_ref, v_ref, wk_ref, wv_ref, fk_ref, sk_ref, fv_ref, sv_ref):
    del tab_ref
    p = pl.program_id(1)
    chunks = PAGE_SIZE // CMP_STRIDE

    @pl.when(p == 0)
    def _():
        sk_ref[...] = jnp.zeros_like(sk_ref)
        sv_ref[...] = jnp.zeros_like(sv_ref)

    for src, w_ref, f_ref, s_ref in ((k_ref, wk_ref, fk_ref, sk_ref), (v_ref, wv_ref, fv_ref, sv_ref)):
        tile = src[0]
        first = jnp.sum((tile * w_ref[0]).reshape(chunks, CMP_STRIDE, KV_W), axis=1)
        second = jnp.sum((tile * w_ref[1]).reshape(chunks, CMP_STRIDE, KV_W), axis=1)
        for m in range(chunks):
            f_ref[0, m % 4, pl.ds(2 * p + m // 4, 1), :] = first[m:m + 1, :]
            if m >= 1:
                s_ref[0, (m - 1) % 4, pl.ds(2 * p + (m - 1) // 4, 1), :] = second[m:m + 1, :]
            else:
                @pl.when(p > 0)
                def _():
                    s_ref[0, 3, pl.ds(2 * p - 1, 1), :] = second[0:1, :]


def _compress(table, src_k, src_v, col_k, col_v, wk, wv, nb):
    npg = table.shape[0] // nb
    out = jax.ShapeDtypeStruct((nb, 4, N_CMP // 4, KV_W), f32)
    ospec = lambda: pl.BlockSpec((1, 4, N_CMP // 4, KV_W), lambda b, p, tab: (b, 0, 0, 0))
    return pl.pallas_call(
        _compress_kernel,
        out_shape=(out,) * 4,
        grid_spec=pltpu.PrefetchScalarGridSpec(
            num_scalar_prefetch=1,
            grid=(nb, npg),
            in_specs=[pl.BlockSpec((1, PAGE_SIZE, KV_W), lambda b, p, tab: (tab[b * npg + p], 0, col_k)),
                      pl.BlockSpec((1, PAGE_SIZE, KV_W), lambda b, p, tab: (tab[b * npg + p], 0, col_v)),
                      pl.BlockSpec((2, PAGE_SIZE, KV_W), lambda b, p, tab: (0, 0, 0)),
                      pl.BlockSpec((2, PAGE_SIZE, KV_W), lambda b, p, tab: (0, 0, 0))],
            out_specs=(ospec(), ospec(), ospec(), ospec())),
        compiler_params=_params(("parallel", "arbitrary")),
        name="compress",
    )(table, src_k, src_v, wk, wv)


def _chunk_maps():
    n_chunk = PP * PAGE_SIZE // CMP_STRIDE
    rows = n_chunk // 4
    chunk_of = np.arange(PP * PAGE_SIZE) // CMP_STRIDE
    first = np.zeros((n_chunk, PP * PAGE_SIZE), np.float32)
    second = np.zeros((n_chunk + 8, PP * PAGE_SIZE), np.float32)
    for i in range(4):
        for jj in range(rows):
            first[i * rows + jj] = chunk_of == 4 * jj + i
            second[i * rows + jj] = chunk_of == 4 * jj + i + 1
    second[n_chunk] = chunk_of == 0
    return jnp.asarray(first, bf16), jnp.asarray(second, bf16)


def _compress_t_kernel(tab_ref, *refs):
    del tab_ref
    k_refs, v_refs = refs[:PP], refs[PP:2 * PP]
    wk_ref, wv_ref, ea_ref, eb_ref, fk_ref, sk_ref, fv_ref, sv_ref = refs[2 * PP:]
    ps = pl.program_id(1)
    rows = PP * PAGE_SIZE // CMP_STRIDE // 4
    r0 = pl.multiple_of(ps * rows, rows)
    for pages, w_ref, f_ref, s_ref in ((k_refs, wk_ref, fk_ref, sk_ref), (v_refs, wv_ref, fv_ref, sv_ref)):
        a1 = jnp.concatenate([(r[0, 0] * w_ref[0]).astype(bf16) for r in pages], axis=1)
        a2 = jnp.concatenate([(r[0, 0] * w_ref[1]).astype(bf16) for r in pages], axis=1)
        first = _nt(ea_ref[...], a1)
        second = _nt(eb_ref[...], a2)
        for i in range(4):
            f_ref[0, i, pl.ds(r0, rows), :] = first[i * rows:(i + 1) * rows]
            s_ref[0, i, pl.ds(r0, rows), :] = second[i * rows:(i + 1) * rows]

        @pl.when(ps > 0)
        def _():
            s_ref[0, 3, pl.ds(r0 - 1, 1), :] = second[4 * rows:4 * rows + 1]


def _compress_t(table, layer, cache_k, cache_v, wk, wv, nb):
    npg = table.shape[0] // nb
    ea, eb = _chunk_maps()
    out = jax.ShapeDtypeStruct((nb, 4, N_CMP // 4, KV_W), f32)
    ospec = lambda: pl.BlockSpec((1, 4, N_CMP // 4, KV_W), lambda b, p, tab: (b, 0, 0, 0))
    page = lambda i: pl.BlockSpec((1, 1, KV_W, PAGE_SIZE),
                                  lambda b, p, tab: (layer, tab[b * npg + p * PP + i], 0, 0))
    full = lambda a: pl.BlockSpec(a.shape, lambda b, p, tab: (0,) * a.ndim)
    return pl.pallas_call(
        _compress_t_kernel,
        out_shape=(out,) * 4,
        grid_spec=pltpu.PrefetchScalarGridSpec(
            num_scalar_prefetch=1,
            grid=(nb, npg // PP),
            in_specs=[page(i) for i in range(PP)] * 2 + [full(wk), full(wv), full(ea), full(eb)],
            out_specs=(ospec(), ospec(), ospec(), ospec())),
        compiler_params=_params(("parallel", "arbitrary")),
        name="compress_t",
    )(table, *([cache_k] * PP), *([cache_v] * PP), wk, wv, ea, eb)


def _finish_compress(f_ref, s_ref, phi_ref):
    blk = (f_ref[0] + s_ref[0]).reshape(N_CMP, KV_W)
    return _dot(blk.astype(bf16), phi_ref[...]).astype(bf16)


def _cmp_end(shape, axis):
    col = lax.broadcasted_iota(jnp.int32, shape, axis)
    n = ((col & (N_BLK - 1)) << 2) + (col >> 7)
    return n * CMP_STRIDE + (CMP_BLK - 1)


def _pick_blocks(score, n_pick):
    lane = lax.broadcasted_iota(jnp.int32, score.shape, 1).astype(f32)
    bias = jnp.full(score.shape, NEG, f32)
    for _ in range(n_pick):
        m = jnp.max(score, axis=-1, keepdims=True)
        first = jnp.min(jnp.where(score == m, lane, float(N_BLK)), axis=-1, keepdims=True)
        hit = lane == first
        bias = jnp.where(hit, 0.0, bias)
        score = jnp.where(hit, -jnp.inf, score)
    return bias


def _block_expand(first_block, n_keys):
    j = lax.broadcasted_iota(jnp.int32, (N_BLK, n_keys), 0)
    c = lax.broadcasted_iota(jnp.int32, (N_BLK, n_keys), 1)
    return jnp.where(j == first_block + (c >> 6), 1.0, 0.0).astype(bf16)


CQ = Q_PER_KV * TQ
WIN_T = WINDOW // TQ + 1
N_PARTS = 9
TILE_COL = HEAD_DIM + 6


def _pick_blocks_t(score, forced, n_pick):
    jrow = lax.broadcasted_iota(jnp.int32, score.shape, 0).astype(f32)
    bias = jnp.where(forced, 0.0, NEG)
    score = jnp.where(forced, -jnp.inf, score)
    for _ in range(n_pick):
        m = jnp.max(score, axis=0, keepdims=True)
        first = jnp.min(jnp.where(score == m, jrow, float(N_BLK)), axis=0, keepdims=True)
        hit = jrow == first
        bias = jnp.where(hit, 0.0, bias)
        score = jnp.where(hit, -jnp.inf, score)
    return bias


def _nsa_prompt_kernel(q_ref, srow_ref, slope_ref, bg_ref, ng_ref, fk_ref, sk_ref, fv_ref, sv_ref,
                       phik_ref, phivt_ref, cpos_ref, ksa_ref, vst_ref, kwa_ref, vwt_ref, o_ref,
                       kca_scr, vct_scr, qa_scr, oc_scr, m_scr, l_scr, acc_scr, negc_scr, negd_scr, negw_scr):
    qi = pl.program_id(1)
    s0 = qi * TQ

    @pl.when(qi == 0)
    def _():
        kc = _finish_compress(fk_ref, sk_ref, phik_ref)
        for g in range(KV_HEADS):
            kca_scr[g] = jnp.concatenate([kc[:, g * HEAD_DIM:(g + 1) * HEAD_DIM], cpos_ref[...]], axis=1)
        blk_v = (fv_ref[0] + sv_ref[0]).reshape(N_CMP, KV_W).astype(bf16)
        vct_scr[...] = _nt(phivt_ref[...], blk_v).astype(bf16)

    t_of = lambda shape: s0 + (lax.broadcasted_iota(jnp.int32, shape, 1) & (TQ - 1))
    q_t = (q_ref[0] * Q_SCALE).T.astype(bf16)

    negc_scr[...] = jnp.where(_cmp_end((N_CMP, CQ), 0) <= t_of((N_CMP, CQ)), 0.0, NEG)
    sees_block = t_of((1, CQ)) >= CMP_BLK - 1
    t_q = t_of((N_BLK, TQ))
    jr = lax.broadcasted_iota(jnp.int32, (N_BLK, TQ), 0)
    jb = t_q >> 6
    ok_b = (jr << 6) <= t_q
    forced = (jr == 0) | (jr == jb) | (jr == jb - 1)
    tile_any = [jnp.full((N_BLK // 8, 1), NEG, f32) for _ in LOOP_GROUPS]
    for g in range(KV_HEADS):
        q_g = jnp.concatenate([q_t[(g * Q_PER_KV + r) * HEAD_DIM:(g * Q_PER_KV + r + 1) * HEAD_DIM]
                               for r in range(Q_PER_KV)], axis=1)
        qa = jnp.concatenate([q_g, srow_ref[g]], axis=0)
        st = _dot(kca_scr[g], qa) + negc_scr[...]
        e = jnp.exp2(st - jnp.max(st, axis=0, keepdims=True))
        pt = e * jnp.where(sees_block, 1.0 / jnp.sum(e, axis=0, keepdims=True), 0.0)
        oc_scr[g] = _dot(vct_scr[g * HEAD_DIM:(g + 1) * HEAD_DIM], pt.astype(bf16))
        ps = None
        for i in range(4):
            for r in range(Q_PER_KV):
                slab = pt[i * N_BLK:(i + 1) * N_BLK, r * TQ:(r + 1) * TQ]
                ps = slab if ps is None else ps + slab
        bias = _pick_blocks_t(jnp.where(ok_b, ps, NEG), forced, N_SEL - 3)
        qa_scr[g] = jnp.concatenate([qa, jnp.concatenate([bias.astype(bf16)] * Q_PER_KV, axis=1)], axis=0)
        any_t = jnp.max(bias.reshape(N_BLK // 8, 8, TQ), axis=1)
        li = [g in grp for grp in LOOP_GROUPS].index(True)
        tile_any[li] = jnp.maximum(tile_any[li], jnp.max(any_t, axis=1, keepdims=True))
    kt_row = lax.broadcasted_iota(jnp.int32, tile_any[0].shape, 0)
    tile_bits = [jnp.sum(jnp.where(ta == 0.0, 1 << kt_row, 0)) for ta in tile_any]

    m_scr[...] = jnp.full(m_scr.shape, NEG, f32)
    l_scr[...] = jnp.zeros_like(l_scr)
    acc_scr[...] = jnp.zeros_like(acc_scr)

    def sel_tile(kt, diagonal, groups):
        k0 = pl.multiple_of(kt * TK, TK)
        tile_off = (k0 - s0).astype(f32)
        key_r = lax.broadcasted_iota(jnp.int32, (TK, N_BLK), 0)
        blk_c = lax.broadcasted_iota(jnp.int32, (TK, N_BLK), 1)
        expand = jnp.where(blk_c == kt * (TK // SEL_BLK) + (key_r >> 6), 1.0, 0.0).astype(bf16)
        k_aug = ksa_ref[0, pl.ds(k0, TK), :]
        v_t = vst_ref[0, kt]
        for g in groups:
            lhs = jnp.concatenate([k_aug[:, g * QA_W:(g + 1) * QA_W], expand], axis=1)
            st = _dot(lhs, qa_scr[g])
            if diagonal:
                st = st + negd_scr[qi % (TK // TQ)]
            c = slope_ref[g] * tile_off
            m_old = m_scr[g]
            m_new = jnp.maximum(m_old, jnp.max(st, axis=0, keepdims=True) + c)
            alpha = jnp.exp2(m_old - m_new)
            pt = jnp.exp2(st - (m_new - c))
            l_scr[g] = alpha * l_scr[g] + jnp.sum(pt, axis=0, keepdims=True)
            acc_scr[g] = alpha * acc_scr[g] + _dot(v_t[g * HEAD_DIM:(g + 1) * HEAD_DIM], pt.astype(bf16))
            m_scr[g] = m_new

    last = s0 // TK

    @pl.when(qi < TK // TQ)
    def _():
        key_pos = lax.broadcasted_iota(jnp.int32, (TK, CQ), 0)
        negd_scr[qi] = jnp.where(key_pos <= t_of((TK, CQ)), 0.0, NEG)

    for groups, bits in zip(LOOP_GROUPS, tile_bits):
        def sel_body(kt, carry, groups=groups, bits=bits):
            @pl.when(((bits >> kt) & 1) == 1)
            def _():
                sel_tile(kt, False, groups)
            return carry

        lax.fori_loop(0, last, sel_body, 0)
    sel_tile(last, True, range(KV_HEADS))

    w_tile = jnp.maximum(qi - WINDOW // TQ, 0)
    n_win = WIN_T * TQ

    @pl.when(qi <= WINDOW // TQ)
    def _():
        dist_w = t_of((n_win, CQ)) - (w_tile * TQ + lax.broadcasted_iota(jnp.int32, (n_win, CQ), 0))
        negw_scr[...] = jnp.where((dist_w >= 0) & (dist_w < WINDOW), 0.0, NEG)

    lane_w = lax.broadcasted_iota(jnp.int32, (n_win, QA_W), 1)
    slab_off = (lax.broadcasted_iota(jnp.int32, (n_win, QA_W), 0) // TQ * TQ).astype(f32).astype(bf16)
    in_tile_col = (lane_w >= TILE_COL) & (lane_w < TILE_COL + 3)
    kw_all = kwa_ref[0, pl.ds(pl.multiple_of(w_tile * TQ, TQ), n_win), :]
    vw_t = jnp.concatenate([vwt_ref[0, w_tile + i] for i in range(WIN_T)], axis=1)
    gates = jax.nn.sigmoid(bg_ref[0].T)
    blocks = []
    for g in range(KV_HEADS):
        lhs = jnp.where(in_tile_col, slab_off, kw_all[:, g * QA_W:(g + 1) * QA_W])
        st = _dot(lhs, qa_scr[g, 0:QA_W]) + negw_scr[...]
        e = jnp.exp2(st - jnp.max(st, axis=0, keepdims=True))
        pt = (e * (1.0 / jnp.sum(e, axis=0, keepdims=True))).astype(bf16)
        o_w = _dot(vw_t[g * HEAD_DIM:(g + 1) * HEAD_DIM], pt)
        o_s = acc_scr[g] * (1.0 / l_scr[g])
        o_c = oc_scr[g]
        heads = []
        for r in range(Q_PER_KV):
            h = g * Q_PER_KV + r
            cols = slice(r * TQ, (r + 1) * TQ)
            heads.append(gates[3 * h:3 * h + 1] * o_c[:, cols] + gates[3 * h + 1:3 * h + 2] * o_s[:, cols]
                         + gates[3 * h + 2:3 * h + 3] * o_w[:, cols])
        for r in range(0, Q_PER_KV, 2):
            blocks.append(jnp.concatenate(heads[r:r + 2], axis=0).T)
    y = jnp.concatenate(blocks, axis=-1)
    o_ref[0] = (y * _silu(ng_ref[0])).astype(bf16)


def _nsa_prompt(z3, fk, sk, fv, sv, phik, phivt):
    b, s, _ = z3.shape
    nq = s // TQ
    parts = _slope_parts().reshape(KV_HEADS, Q_PER_KV, POS_ROWS)
    srow = jnp.asarray(np.repeat(parts.transpose(0, 2, 1), TQ, axis=2), bf16)
    slope = jnp.asarray(np.repeat(np.asarray(SLOPES_LOG2, np.float32).reshape(KV_HEADS, 1, Q_PER_KV), TQ, axis=2))
    ksa = _key_aug(_kv_seg(z3, 2), TK)
    kwa = _key_aug(_kv_seg(z3, 4), TQ)
    vst = _kv_seg(z3, 3).astype(bf16).reshape(b, s // TK, TK, KV_W).swapaxes(2, 3)
    vwt = _kv_seg(z3, 5).astype(bf16).reshape(b, nq, TQ, KV_W).swapaxes(2, 3)
    cpos = _cmp_pos_cols()

    once = pl.Buffered(1)
    cmp = lambda: pl.BlockSpec((1, 4, N_CMP // 4, KV_W), lambda i, t: (i, 0, 0, 0))
    full = lambda a: pl.BlockSpec(a.shape, lambda i, t: (0,) * a.ndim)
    return pl.pallas_call(
        _nsa_prompt_kernel,
        out_shape=jax.ShapeDtypeStruct((b, s, BRANCH_W), bf16),
        grid=(b, nq),
        in_specs=[pl.BlockSpec((1, TQ, BRANCH_W), lambda i, t: (i, t, C_Q // BRANCH_W)),
                  full(srow), full(slope),
                  pl.BlockSpec((1, TQ, BG_W), lambda i, t: (i, t, C_BG // BG_W)),
                  pl.BlockSpec((1, TQ, BRANCH_W), lambda i, t: (i, t, C_NSA_G // BRANCH_W)),
                  cmp(), cmp(), cmp(), cmp(), full(phik), full(phivt), full(cpos),
                  pl.BlockSpec((1, s, KV_HEADS * QA_W), lambda i, t: (i, 0, 0), pipeline_mode=once),
                  pl.BlockSpec((1, s // TK, KV_W, TK), lambda i, t: (i, 0, 0, 0), pipeline_mode=once),
                  pl.BlockSpec((1, s, KV_HEADS * QA_W), lambda i, t: (i, 0, 0), pipeline_mode=once),
                  pl.BlockSpec((1, nq, KV_W, TQ), lambda i, t: (i, 0, 0, 0), pipeline_mode=once)],
        out_specs=pl.BlockSpec((1, TQ, BRANCH_W), lambda i, t: (i, t, 0)),
        scratch_shapes=[pltpu.VMEM((KV_HEADS, N_CMP, QA_W), bf16), pltpu.VMEM((KV_W, N_CMP), bf16),
                        pltpu.VMEM((KV_HEADS, QA_W + N_BLK, CQ), bf16),
                        pltpu.VMEM((KV_HEADS, HEAD_DIM, CQ), f32),
                        pltpu.VMEM((KV_HEADS, 1, CQ), f32), pltpu.VMEM((KV_HEADS, 1, CQ), f32),
                        pltpu.VMEM((KV_HEADS, HEAD_DIM, CQ), f32),
                        pltpu.VMEM((N_CMP, CQ), f32), pltpu.VMEM((TK // TQ, TK, CQ), f32),
                        pltpu.VMEM((WIN_T * TQ, CQ), f32)],
        compiler_params=_params(("parallel", "arbitrary")),
        name="nsa_prompt",
    )(z3, srow, slope, z3, z3, fk, sk, fv, sv, phik, phivt, cpos, ksa, vst, kwa, vwt)


def _by_group(fn):
    hg = lax.broadcasted_iota(jnp.int32, (N_HEADS, 1), 0) >> 2
    out = fn(0)
    for g in range(1, KV_HEADS):
        out = jnp.where(hg == g, fn(g), out)
    return out


def _nsa_sample_kernel(past_len, tab_ref, q_ref, bg_ref, ng_ref, slope_ref,
                       fk_ref, sk_ref, fv_ref, sv_ref, phik_ref, phiv_ref,
                       ksn_ref, vsn_ref, kwn_ref, vwn_ref, bk_ref, bv_ref, *refs):
    del tab_ref
    kp_refs, vp_refs = refs[:PP_SEL], refs[PP_SEL:2 * PP_SEL]
    o_ref, q_scr, sel_scr, oc_scr, ow_scr, m_scr, l_scr, acc_scr = refs[2 * PP_SEL:]
    p = pl.program_id(1)
    slope = slope_ref[:, 0:1]
    gsl = lambda g: slice(g * HEAD_DIM, (g + 1) * HEAD_DIM)

    def per_head(row_ref):
        row = row_ref[0]
        return _by_group(lambda g: jnp.broadcast_to(row[:, gsl(g)], (N_HEADS, HEAD_DIM)))

    @pl.when(p == 0)
    def _():
        qrow = q_ref[0]
        q16 = jnp.concatenate([qrow[:, h * HEAD_DIM:(h + 1) * HEAD_DIM] for h in range(N_HEADS)], axis=0)
        q_scr[...] = q16
        q16f = q16.astype(f32)
        kc = _finish_compress(fk_ref, sk_ref, phik_ref)
        vc = _finish_compress(fv_ref, sv_ref, phiv_ref)

        dist_c = past_len - _cmp_end((1, N_CMP), 1)
        ok_c = dist_c >= 0
        s = _by_group(lambda g: _nt(q16, kc[:, gsl(g)])) - slope * dist_c.astype(f32)
        s = jnp.where(ok_c, s, NEG)
        e = jnp.exp(s - jnp.max(s, axis=-1, keepdims=True))
        pc = jnp.where(ok_c, e * (1.0 / jnp.sum(e, axis=-1, keepdims=True)), 0.0)
        pcb = pc.astype(bf16)
        oc_scr[...] = _by_group(lambda g: _dot(pcb, vc[:, gsl(g)]))
        ps16 = (pc[:, 0:N_BLK] + pc[:, N_BLK:2 * N_BLK]) + (pc[:, 2 * N_BLK:3 * N_BLK] + pc[:, 3 * N_BLK:])
        ps = jnp.concatenate([jnp.sum(ps16[g * Q_PER_KV:(g + 1) * Q_PER_KV], axis=0, keepdims=True)
                              for g in range(KV_HEADS)] + [jnp.zeros((8 - KV_HEADS, N_BLK), f32)], axis=0)
        jl = lax.broadcasted_iota(jnp.int32, (8, N_BLK), 1)
        jb = past_len // SEL_BLK
        forced = jnp.where((jl == 0) | (jl == jb) | (jl == jb - 1), FORCE, 0.0)
        bias8 = _pick_blocks(ps + forced, N_SEL - 1)
        sel_scr[...] = jnp.concatenate(
            [jnp.broadcast_to(bias8[g:g + 1], (Q_PER_KV, N_BLK)) for g in range(KV_HEADS)], axis=0).astype(bf16)

        wb = bk_ref.shape[-1]
        dist_w = wb - lax.broadcasted_iota(jnp.int32, (1, wb), 1)
        ok_w = (dist_w >= 0) & (dist_w < WINDOW)
        bk = bk_ref[0, 0].astype(bf16)
        bv = bv_ref[0, 0].astype(bf16)
        s_buf = _by_group(lambda g: _dot(q16, bk[gsl(g)])) - slope * dist_w.astype(f32)
        s_buf = jnp.where(ok_w, s_buf, NEG)
        s_new = jnp.sum(q16f * per_head(kwn_ref), axis=-1, keepdims=True)
        m_w = jnp.maximum(jnp.max(s_buf, axis=-1, keepdims=True), s_new)
        e_buf = jnp.exp(s_buf - m_w)
        e_new = jnp.exp(s_new - m_w)
        ebb = e_buf.astype(bf16)
        num = _by_group(lambda g: _nt(ebb, bv[gsl(g)])) + e_new * per_head(vwn_ref)
        ow_scr[...] = num * (1.0 / (jnp.sum(e_buf, axis=-1, keepdims=True) + e_new))

        m_scr[...] = jnp.sum(q16f * per_head(ksn_ref), axis=-1, keepdims=True)
        l_scr[...] = jnp.ones_like(l_scr)
        acc_scr[...] = per_head(vsn_ref)

    q16 = q_scr[...]
    n_keys = PP_SEL * PAGE_SIZE
    kp = jnp.concatenate([r[0, 0].astype(bf16) for r in kp_refs], axis=1)
    vp = jnp.concatenate([r[0, 0].astype(bf16) for r in vp_refs], axis=1)
    dist = past_len - (p * n_keys + lax.broadcasted_iota(jnp.int32, (1, n_keys), 1))
    expand = _block_expand(p * (n_keys // SEL_BLK), n_keys)
    s = _by_group(lambda g: _dot(q16, kp[gsl(g)])) - slope * dist.astype(f32) + _dot(sel_scr[...], expand)
    m_old = m_scr[...]
    m_new = jnp.maximum(m_old, jnp.max(s, axis=-1, keepdims=True))
    alpha = jnp.exp(m_old - m_new)
    pr = jnp.exp(s - m_new)
    prb = pr.astype(bf16)
    l_scr[...] = alpha * l_scr[...] + jnp.sum(pr, axis=-1, keepdims=True)
    acc_scr[...] = alpha * acc_scr[...] + _by_group(lambda g: _nt(prb, vp[gsl(g)]))
    m_scr[...] = m_new

    @pl.when(p == pl.num_programs(1) - 1)
    def _():
        o_s = acc_scr[...] * (1.0 / l_scr[...])
        gates = jax.nn.sigmoid(bg_ref[0])
        lane = lax.broadcasted_iota(jnp.int32, (N_HEADS, BG_W), 1)
        h3 = 3 * lax.broadcasted_iota(jnp.int32, (N_HEADS, BG_W), 0)
        gate = lambda n: jnp.sum(jnp.where(lane == h3 + n, gates, 0.0), axis=-1, keepdims=True)
        y16 = gate(0) * oc_scr[...] + gate(1) * o_s + gate(2) * ow_scr[...]
        y = jnp.concatenate([y16[h:h + 1, :] for h in range(N_HEADS)], axis=-1)
        o_ref[0] = (y * _silu(ng_ref[0])).astype(bf16)


def _nsa_sample(table, layer, qb, z2, slopes, fk, sk, fv, sv, phik, phiv, pool_k, pool_v, buf_k, buf_v, past_len):
    nb = qb.shape[0]
    npg = table.shape[0] // nb
    z3 = z2.reshape(nb, 1, IN_WP)
    tok = lambda w, c: pl.BlockSpec((1, 1, w), lambda b, p, tab: (b, 0, c))
    cmp = lambda: pl.BlockSpec((1, 4, N_CMP // 4, KV_W), lambda b, p, tab: (b, 0, 0, 0))
    phi = lambda: pl.BlockSpec((KV_W, KV_W), lambda b, p, tab: (0, 0))
    wb = buf_k.shape[-1]
    win = lambda: pl.BlockSpec((1, 1, KV_W, wb), lambda b, p, tab: (layer, b, 0, 0))
    page = lambda i: pl.BlockSpec((1, 1, KV_W, PAGE_SIZE),
                                  lambda b, p, tab: (layer, tab[b * npg + p * PP_SEL + i], 0, 0))
    kvc = C_KV // KV_W
    return pl.pallas_call(
        functools.partial(_nsa_sample_kernel, past_len),
        out_shape=jax.ShapeDtypeStruct((nb, 1, BRANCH_W), bf16),
        grid_spec=pltpu.PrefetchScalarGridSpec(
            num_scalar_prefetch=1,
            grid=(nb, npg // PP_SEL),
            in_specs=[tok(BRANCH_W, 0), tok(BG_W, C_BG // BG_W), tok(BRANCH_W, C_NSA_G // BRANCH_W),
                      pl.BlockSpec((N_HEADS, 128), lambda b, p, tab: (0, 0)),
                      cmp(), cmp(), cmp(), cmp(), phi(), phi(),
                      tok(KV_W, kvc + 2), tok(KV_W, kvc + 3), tok(KV_W, kvc + 4), tok(KV_W, kvc + 5),
                      win(), win()] + [page(i) for i in range(PP_SEL)] * 2,
            out_specs=pl.BlockSpec((1, 1, BRANCH_W), lambda b, p, tab: (b, 0, 0)),
            scratch_shapes=[pltpu.VMEM((N_HEADS, HEAD_DIM), bf16), pltpu.VMEM((N_HEADS, N_BLK), bf16),
                            pltpu.VMEM((N_HEADS, HEAD_DIM), f32), pltpu.VMEM((N_HEADS, HEAD_DIM), f32),
                            pltpu.VMEM((N_HEADS, 1), f32), pltpu.VMEM((N_HEADS, 1), f32),
                            pltpu.VMEM((N_HEADS, HEAD_DIM), f32)]),
        compiler_params=_params(("parallel", "arbitrary")),
        name="nsa_sample",
    )(table, qb.reshape(nb, 1, BRANCH_W), z3, z3, slopes, fk, sk, fv, sv, phik, phiv,
      z3, z3, z3, z3, buf_k, buf_v, *([pool_k] * PP_SEL), *([pool_v] * PP_SEL))


def _merge_kernel(zl_ref, zp_ref, zn_ref, m0_ref, m1_ref, m2_ref, wb_ref, wo_ref, g_ref, x_ref, y_ref):
    acc = None
    for n, (zz, mg) in enumerate(((zl_ref, m0_ref), (zp_ref, m1_ref), (zn_ref, m2_ref))):
        term = jax.nn.sigmoid(mg[...]) * _dot(zz[...], wb_ref[n])
        acc = term if acc is None else acc + term
    out = _dot(acc.astype(bf16), wo_ref[...])
    ms = jnp.mean(out * out, axis=-1, keepdims=True)
    y_ref[...] = x_ref[...] + out * lax.rsqrt(ms + EPS) * g_ref[...]


def _merge(zl, zp, zn, z2, wb, wo, g_row, x2d):
    n = x2d.shape[0]
    tm = min(n, 256)
    rowblk = lambda c: pl.BlockSpec((tm, D_MODEL), lambda i: (i, c))
    return pl.pallas_call(
        _merge_kernel,
        out_shape=jax.ShapeDtypeStruct((n, D_MODEL), f32),
        grid=(n // tm,),
        in_specs=[rowblk(0), rowblk(0), rowblk(0),
                  rowblk(C_MG // D_MODEL), rowblk(C_MG // D_MODEL + 1), rowblk(C_MG // D_MODEL + 2),
                  pl.BlockSpec((N_BRANCH, BRANCH_W, D_MODEL), lambda i: (0, 0, 0)),
                  pl.BlockSpec((D_MODEL, D_MODEL), lambda i: (0, 0)),
                  pl.BlockSpec((1, D_MODEL), lambda i: (0, 0)),
                  rowblk(0)],
        out_specs=rowblk(0),
        compiler_params=_params(("parallel",)),
        name="merge",
    )(zl, zp, zn, z2, z2, z2, wb, wo, g_row, x2d)


def _block_diag(w, per):
    n, d, _ = w.shape
    eye = jnp.eye(per, dtype=w.dtype)
    t = jnp.einsum('cpde,pq->cpdqe', w.reshape(n // per, per, d, d), eye)
    return t.reshape(n // per, per * d, per * d)


def _pack_w_in(w):
    old_kv = 6 * BRANCH_W
    old_bg = old_kv + 6 * KV_W
    old_mg = old_bg + N_BRANCH * N_HEADS
    parts = [w[:, :old_kv], w[:, old_mg:], w[:, old_kv:old_bg], w[:, old_bg:old_mg]]
    packed = jnp.concatenate(parts, axis=1)
    return jnp.pad(packed, ((0, 0), (0, IN_WP - packed.shape[1]))).astype(bf16)


def _tile_wpos(w_pos):
    halves = w_pos.reshape(2, CMP_STRIDE, HEAD_DIM)
    return jnp.tile(halves, (1, PAGE_SIZE // CMP_STRIDE, KV_HEADS))


def _tile_wpos_t(w_pos):
    halves = w_pos.reshape(2, CMP_STRIDE, HEAD_DIM).swapaxes(1, 2)
    return jnp.tile(halves, (1, KV_HEADS, PAGE_SIZE // CMP_STRIDE))


def _lanes_last(cache):
    d, n, rows = cache.shape[:3]
    return jnp.transpose(cache, (0, 1, 3, 4, 2)).reshape(d, n, KV_W, rows)


def _slope_parts():
    cols = np.zeros((N_HEADS, POS_ROWS), np.float32)
    rnd = lambda v: np.float32(np.float32(v).astype(bf16))
    for h, s in enumerate(SLOPES_LOG2):
        s1 = rnd(s)
        s2 = rnd(np.float32(s) - s1)
        s3 = rnd(np.float32(s) - s1 - s2)
        cols[h, 0:N_PARTS] = [s1, s2, s3] * 3
    return cols


def _split_pos(pos, shift):
    cols = np.zeros((pos.shape[0], POS_ROWS), np.float32)
    cols[:, 0:3] = ((pos >> shift) << shift)[:, None]
    cols[:, 3:6] = (pos & ((1 << shift) - 1))[:, None]
    return cols


def _key_aug(k_rows, tile):
    b, s, _ = k_rows.shape
    pos = jnp.asarray(_split_pos(np.arange(s) % tile, 4), bf16)
    kg = k_rows.astype(bf16).reshape(b, s, KV_HEADS, HEAD_DIM)
    posb = jnp.broadcast_to(pos[None, :, None, :], (b, s, KV_HEADS, POS_ROWS))
    return jnp.concatenate([kg, posb], axis=-1).reshape(b, s, KV_HEADS * QA_W)


def _cmp_pos_cols():
    slot = np.arange(N_CMP)
    n = ((slot & (N_BLK - 1)) << 2) + (slot >> 7)
    return jnp.asarray(_split_pos(n * CMP_STRIDE + CMP_BLK - 1, 8), bf16)


def _kv_seg(z, i):
    return z[..., C_KV + i * KV_W:C_KV + (i + 1) * KV_W]


def kernel(x_prompt, x_sample, cache_cmp_k, cache_cmp_v, cache_sel_k, cache_sel_v, cache_win_k, cache_win_v, state_conv, state_lru, state_pool, page_table, g_pre, g_post, w_in, conv_w, conv_b, w_rg_a, b_rg_a, w_rg_x, b_rg_x, lru_lambda, w_pool, pool_scale, cmp_pos_k, cmp_phi_k, cmp_pos_v, cmp_phi_v, w_branch, w_out):
    depth = w_in.shape[0]
    bp, seq, _ = x_prompt.shape
    bs = x_sample.shape[0]
    n_pages = page_table.shape[1]
    past_len = n_pages * PAGE_SIZE
    n_phys = cache_cmp_k.shape[1]
    assert seq == N_BLK * SEL_BLK and past_len == N_BLK * SEL_BLK and x_sample.shape[1] == 1
    wb = cache_win_k.shape[2]

    table_s = page_table.reshape(-1).astype(jnp.int32)
    table_p = jnp.arange(bp * (seq // PAGE_SIZE), dtype=jnp.int32)
    slopes = jnp.broadcast_to(jnp.asarray(SLOPES, f32)[:, None], (N_HEADS, 128))
    row = lambda v: v.reshape(1, -1)
    cmp_kt, cmp_vt = _lanes_last(cache_cmp_k), _lanes_last(cache_cmp_v)
    sel_kt, sel_vt = _lanes_last(cache_sel_k), _lanes_last(cache_sel_v)
    win_kt, win_vt = _lanes_last(cache_win_k), _lanes_last(cache_win_v)

    xp = x_prompt.reshape(bp * seq, D_MODEL)
    xs = x_sample.reshape(bs, D_MODEL)
    pr = [[] for _ in range(9)]
    sm = [[] for _ in range(9)]
    for l in range(depth):
        w_packed = _pack_w_in(w_in[l])
        wa = _block_diag(w_rg_a[l], MXU_W // LRU_BD).astype(bf16)
        wx = _block_diag(w_rg_x[l], MXU_W // LRU_BD).astype(bf16)
        wp = w_pool[l].astype(bf16)
        phik = _block_diag(jnp.broadcast_to(cmp_phi_k[l], (KV_HEADS, HEAD_DIM, HEAD_DIM)), KV_HEADS)[0].astype(bf16)
        phiv = _block_diag(jnp.broadcast_to(cmp_phi_v[l], (KV_HEADS, HEAD_DIM, HEAD_DIM)), KV_HEADS)[0].astype(bf16)
        wpos_k = _tile_wpos(cmp_pos_k[l])
        wpos_v = _tile_wpos(cmp_pos_v[l])
        wbr = w_branch[l].astype(bf16)
        wo = w_out[l].astype(bf16)
        lru_w = (conv_w[l], row(conv_b[l]), wa, row(b_rg_a[l]), wx, row(b_rg_x[l]), row(lru_lambda[l]))

        z = _inproj(xp, row(g_pre[l]), w_packed)
        z3 = z.reshape(bp, seq, IN_WP)
        zl, conv_tail, h_p = _lru_prompt(z3, *lru_w)
        zpool, pool_tail = _pool_prompt(z3, wp, row(pool_scale[l]))
        zr = z.reshape(bp * seq // PAGE_SIZE, PAGE_SIZE, IN_WP)
        fk, sk, fv, sv = _compress(table_p, zr, zr, C_KV // KV_W, C_KV // KV_W + 1, wpos_k, wpos_v, bp)
        zn = _nsa_prompt(z3, fk, sk, fv, sv, phik, phiv.T)
        xp = _merge(zl.reshape(bp * seq, BRANCH_W), zpool.reshape(bp * seq, BRANCH_W),
                    zn.reshape(bp * seq, BRANCH_W), z, wbr, wo, row(g_post[l]), xp)
        kv_rows = [_kv_seg(z3, i).reshape(bp, seq, KV_HEADS, HEAD_DIM) for i in range(6)]
        wlen = min(WINDOW, seq)
        st_p = kv_rows[:4] + [kv_rows[4][:, -wlen:], kv_rows[5][:, -wlen:],
                              conv_tail[:, -(CONV_W - 1):], h_p[:, 0], pool_tail[:, -POOL_BUF:]]

        zs = _inproj(xs, row(g_pre[l]), w_packed)
        zls, zps, h_s = _mix_sample(zs, state_conv[l].swapaxes(0, 1), state_lru[l], state_pool[l].swapaxes(0, 1),
                                    *lru_w, wp, row(pool_scale[l]), past_len)
        fk, sk, fv, sv = _compress_t(table_s, l, cmp_kt, cmp_vt, _tile_wpos_t(cmp_pos_k[l]),
                                     _tile_wpos_t(cmp_pos_v[l]), bs)
        qs = (zs[:, C_Q:C_Q + BRANCH_W] * (HEAD_DIM ** -0.5)).astype(bf16)
        zns = _nsa_sample(table_s, l, qs, zs, slopes, fk, sk, fv, sv, phik, phiv,
                          sel_kt, sel_vt, win_kt, win_vt, past_len)
        xs = _merge(zls, zps, zns.reshape(bs, BRANCH_W), zs, wbr, wo, row(g_post[l]), xs)
        kv_new = [_kv_seg(zs, i).reshape(bs, 1, KV_HEADS, HEAD_DIM) for i in range(6)]
        st_s = kv_new[:4] + [jnp.concatenate([cache_win_k[l], kv_new[4]], axis=1)[:, -wb:],
                             jnp.concatenate([cache_win_v[l], kv_new[5]], axis=1)[:, -wb:],
                             jnp.concatenate([state_conv[l], zs[:, None, C_LRU_X:C_LRU_X + BRANCH_W]], axis=1)[:, -(CONV_W - 1):],
                             h_s,
                             jnp.concatenate([state_pool[l], zs[:, None, C_POOL_X:C_POOL_X + BRANCH_W]], axis=1)[:, -POOL_BUF:]]
        for i in range(9):
            pr[i].append(st_p[i])
            sm[i].append(st_s[i])

    out = [xp.reshape(bp, seq, D_MODEL), xs.reshape(bs, 1, D_MODEL)]
    for i in range(9):
        out += [jnp.stack(pr[i]), jnp.stack(sm[i])]
    return tuple(out)
```

```python
import functools

import numpy as np
import jax
import jax.numpy as jnp
from jax import lax
from jax.experimental import pallas as pl
from jax.experimental.pallas import tpu as pltpu

f32 = jnp.float32
bf16 = jnp.bfloat16

D_MODEL = 1024
BRANCH_W = 1024
N_BRANCH = 3
LRU_BLOCKS = 16
LRU_BD = BRANCH_W // LRU_BLOCKS
CONV_W = 4
LRU_C = 8.0
POOL_WINDOWS = (2, 4, 8, 16)
POOL_GD = BRANCH_W // len(POOL_WINDOWS)
POOL_BUF = max(POOL_WINDOWS) - 1
N_HEADS = 16
HEAD_DIM = 64
KV_HEADS = 4
Q_PER_KV = N_HEADS // KV_HEADS
KV_W = KV_HEADS * HEAD_DIM
CMP_STRIDE = 16
CMP_BLK = 2 * CMP_STRIDE
SEL_BLK = 64
N_SEL = 16
WINDOW = 512
PAGE_SIZE = 128
FORCE = 1e4
NEG = -1e30
EPS = 1e-6

C_LRU_X, C_LRU_G, C_POOL_X, C_POOL_G, C_Q, C_NSA_G, C_MG = 0, 1024, 2048, 3072, 4096, 5120, 6144
C_KV = C_MG + N_BRANCH * D_MODEL
C_BG = C_KV + 6 * KV_W
BG_W = 128
IN_WP = 11264
IN_TN = 1024

MXU_W = 256
VMEM_LIMIT = 56 * 1024 * 1024

N_CMP = 512
N_BLK = 128
TQ = 256
TK = 512
PP = 8
PP_SEL = 16
POS_ROWS = 64
QA_W = HEAD_DIM + POS_ROWS

SLOPES = [float(np.float32(2.0 ** (-8.0 * (h + 1) / N_HEADS))) for h in range(N_HEADS)]
LOG2E = float(np.log2(np.e))
SLOPES_LOG2 = [float(np.float32(s * LOG2E)) for s in SLOPES]
Q_SCALE = float(np.float32(HEAD_DIM ** -0.5 * LOG2E))
LOOP_GROUPS = ((0,), (1,), (2,), (3,))


def _nt(a, b):
    return lax.dot_general(a, b, (((1,), (1,)), ((), ())), preferred_element_type=f32)


def _dot(a, b):
    return jnp.dot(a, b, preferred_element_type=f32)


def _silu(x):
    return x * jax.nn.sigmoid(x)


def _params(sem):
    return pltpu.CompilerParams(dimension_semantics=sem, vmem_limit_bytes=VMEM_LIMIT)


def _inproj_kernel(x_ref, g_ref, w_ref, o_ref, u_ref):
    @pl.when(pl.program_id(1) == 0)
    def _():
        x = x_ref[...]
        ms = jnp.mean(x * x, axis=-1, keepdims=True)
        u_ref[...] = (x * lax.rsqrt(ms + EPS) * g_ref[...]).astype(bf16)

    o_ref[...] = _dot(u_ref[...], w_ref[...])


def _inproj(x2d, g_row, w_packed):
    n = x2d.shape[0]
    tm = min(n, 1024)
    return pl.pallas_call(
        _inproj_kernel,
        out_shape=jax.ShapeDtypeStruct((n, IN_WP), f32),
        grid=(n // tm, IN_WP // IN_TN),
        in_specs=[pl.BlockSpec((tm, D_MODEL), lambda i, j: (i, 0)),
                  pl.BlockSpec((1, D_MODEL), lambda i, j: (0, 0)),
                  pl.BlockSpec((D_MODEL, IN_TN), lambda i, j: (0, j))],
        out_specs=pl.BlockSpec((tm, IN_TN), lambda i, j: (i, j)),
        scratch_shapes=[pltpu.VMEM((tm, D_MODEL), bf16)],
        compiler_params=_params(("parallel", "arbitrary")),
        name="inproj",
    )(x2d, g_row, w_packed)


def _lru_gates(xc, wa_ref, ba_ref, wx_ref, bx_ref, lam_ref):
    xb = xc.astype(bf16)
    ra, ri = [], []
    for c in range(BRANCH_W // MXU_W):
        sl = slice(c * MXU_W, (c + 1) * MXU_W)
        ra.append(_dot(xb[:, sl], wa_ref[c]))
        ri.append(_dot(xb[:, sl], wx_ref[c]))
    r = jax.nn.sigmoid(jnp.concatenate(ra, axis=-1) + ba_ref[...])
    i = jax.nn.sigmoid(jnp.concatenate(ri, axis=-1) + bx_ref[...])
    nl = -lam_ref[...]
    softplus = jnp.maximum(nl, 0.0) + jnp.log1p(jnp.exp(-jnp.abs(nl)))
    log_a = -LRU_C * r * softplus
    a = jnp.exp(log_a)
    b = jnp.sqrt(1.0 - a * a) * (i * xc)
    return a, b


def _lru_kernel(x_ref, g_ref, cw_ref, cb_ref, wa_ref, ba_ref, wx_ref, bx_ref, lam_ref,
                zb_ref, tail_ref, h_ref, xs_ref, a_ref, b_ref, hc_ref):
    tt = x_ref.shape[1]

    @pl.when(pl.program_id(1) == 0)
    def _():
        xs_ref[0:8, :] = jnp.zeros((8, BRANCH_W), f32)
        hc_ref[...] = jnp.zeros_like(hc_ref)

    x = x_ref[0]
    xs_ref[8:, :] = x
    xc = cb_ref[...] + x * cw_ref[CONV_W - 1:CONV_W, :]
    for k in range(CONV_W - 1):
        xc = xc + xs_ref[pl.ds(8 - (CONV_W - 1 - k), tt), :] * cw_ref[k:k + 1, :]
    xs_ref[0:8, :] = x[tt - 8:, :]
    tail_ref[0] = x[tt - 8:, :]

    a, b = _lru_gates(xc, wa_ref, ba_ref, wx_ref, bx_ref, lam_ref)
    a_ref[...] = a
    b_ref[...] = b
    row = lax.broadcasted_iota(jnp.int32, (8, BRANCH_W), 0)

    def body(i, h):
        r0 = pl.multiple_of(i * 8, 8)
        av = a_ref[pl.ds(r0, 8), :]
        bv = b_ref[pl.ds(r0, 8), :]
        for d in (1, 2, 4):
            a_s = jnp.where(row >= d, pltpu.roll(av, d, 0), 1.0)
            b_s = jnp.where(row >= d, pltpu.roll(bv, d, 0), 0.0)
            bv = av * b_s + bv
            av = av * a_s
        hs = bv + av * h
        b_ref[pl.ds(r0, 8), :] = hs
        return hs[7:8, :]

    h = lax.fori_loop(0, tt // 8, body, hc_ref[...])
    hc_ref[...] = h
    h_ref[0] = h
    zb_ref[0] = (b_ref[...] * _silu(g_ref[0])).astype(bf16)


def _lru_prompt(z3, cw, cb, wa, ba, wx, bx, lam):
    b, s, _ = z3.shape
    tt = min(s, 512)
    row = lambda: pl.BlockSpec((1, BRANCH_W), lambda i, t: (0, 0))
    bd = lambda: pl.BlockSpec((BRANCH_W // MXU_W, MXU_W, MXU_W), lambda i, t: (0, 0, 0))
    return pl.pallas_call(
        _lru_kernel,
        out_shape=(jax.ShapeDtypeStruct((b, s, BRANCH_W), bf16),
                   jax.ShapeDtypeStruct((b, 8, BRANCH_W), f32),
                   jax.ShapeDtypeStruct((b, 1, BRANCH_W), f32)),
        grid=(b, s // tt),
        in_specs=[pl.BlockSpec((1, tt, BRANCH_W), lambda i, t: (i, t, C_LRU_X // BRANCH_W)),
                  pl.BlockSpec((1, tt, BRANCH_W), lambda i, t: (i, t, C_LRU_G // BRANCH_W)),
                  pl.BlockSpec((CONV_W, BRANCH_W), lambda i, t: (0, 0)),
                  row(), bd(), row(), bd(), row(), row()],
        out_specs=(pl.BlockSpec((1, tt, BRANCH_W), lambda i, t: (i, t, 0)),
                   pl.BlockSpec((1, 8, BRANCH_W), lambda i, t: (i, 0, 0)),
                   pl.BlockSpec((1, 1, BRANCH_W), lambda i, t: (i, 0, 0))),
        scratch_shapes=[pltpu.VMEM((tt + 8, BRANCH_W), f32), pltpu.VMEM((tt, BRANCH_W), f32),
                        pltpu.VMEM((tt, BRANCH_W), f32), pltpu.VMEM((1, BRANCH_W), f32)],
        compiler_params=_params(("parallel", "arbitrary")),
        name="lru_prompt",
    )(z3, z3, cw, cb, wa, ba, wx, bx, lam)


def _pool_kernel(x_ref, g_ref, wp_ref, sc_ref, zb_ref, tail_ref, xs_ref):
    tt = x_ref.shape[1]
    t = pl.program_id(1)

    @pl.when(t == 0)
    def _():
        xs_ref[0:16, :] = jnp.zeros((16, BRANCH_W), f32)

    x = x_ref[0]
    xs_ref[16:, :] = x
    pos1 = t * tt + 1 + lax.broadcasted_iota(jnp.int32, (tt, POOL_GD), 0)
    outs = []
    for gi, w in enumerate(POOL_WINDOWS):
        sl = slice(gi * POOL_GD, (gi + 1) * POOL_GD)
        s = xs_ref[:, sl]
        sh = 1
        while sh < w:
            s = s + pltpu.roll(s, sh, 0)
            sh *= 2
        cnt = jnp.minimum(w, pos1).astype(f32)
        pooled = s[16:, :] / cnt - x[:, sl]
        outs.append(_dot(pooled.astype(bf16), wp_ref[gi]))
    y = jnp.concatenate(outs, axis=-1) * sc_ref[...]
    zb_ref[0] = (y * _silu(g_ref[0])).astype(bf16)
    xs_ref[0:16, :] = x[tt - 16:, :]
    tail_ref[0] = x[tt - 16:, :]


def _pool_prompt(z3, wp, sc):
    b, s, _ = z3.shape
    tt = min(s, 512)
    return pl.pallas_call(
        _pool_kernel,
        out_shape=(jax.ShapeDtypeStruct((b, s, BRANCH_W), bf16),
                   jax.ShapeDtypeStruct((b, 16, BRANCH_W), f32)),
        grid=(b, s // tt),
        in_specs=[pl.BlockSpec((1, tt, BRANCH_W), lambda i, t: (i, t, C_POOL_X // BRANCH_W)),
                  pl.BlockSpec((1, tt, BRANCH_W), lambda i, t: (i, t, C_POOL_G // BRANCH_W)),
                  pl.BlockSpec((len(POOL_WINDOWS), POOL_GD, POOL_GD), lambda i, t: (0, 0, 0)),
                  pl.BlockSpec((1, BRANCH_W), lambda i, t: (0, 0))],
        out_specs=(pl.BlockSpec((1, tt, BRANCH_W), lambda i, t: (i, t, 0)),
                   pl.BlockSpec((1, 16, BRANCH_W), lambda i, t: (i, 0, 0))),
        scratch_shapes=[pltpu.VMEM((tt + 16, BRANCH_W), f32)],
        compiler_params=_params(("parallel", "arbitrary")),
        name="pool_prompt",
    )(z3, z3, wp, sc)


def _mix_sample_kernel(past_len, lx_ref, lg_ref, px_ref, pg_ref, conv_ref, h0_ref, pbuf_ref,
                       cw_ref, cb_ref, wa_ref, ba_ref, wx_ref, bx_ref, lam_ref, wp_ref, sc_ref,
                       zl_ref, zp_ref, h_ref):
    x = lx_ref[...]
    xc = cb_ref[...] + x * cw_ref[CONV_W - 1:CONV_W, :]
    for k in range(CONV_W - 1):
        xc = xc + conv_ref[k] * cw_ref[k:k + 1, :]
    a, b = _lru_gates(xc, wa_ref, ba_ref, wx_ref, bx_ref, lam_ref)
    h = a * h0_ref[...] + b
    h_ref[...] = h
    zl_ref[...] = (h * _silu(lg_ref[...])).astype(bf16)

    px = px_ref[...]
    outs = []
    for gi, w in enumerate(POOL_WINDOWS):
        sl = slice(gi * POOL_GD, (gi + 1) * POOL_GD)
        s = px[:, sl]
        for k in range(1, w):
            s = s + pbuf_ref[POOL_BUF - k][:, sl]
        cnt = float(min(w, past_len + 1))
        pooled = s / cnt - px[:, sl]
        outs.append(_dot(pooled.astype(bf16), wp_ref[gi]))
    y = jnp.concatenate(outs, axis=-1) * sc_ref[...]
    zp_ref[...] = (y * _silu(pg_ref[...])).astype(bf16)


def _mix_sample(z2, conv_t, h0, pbuf_t, cw, cb, wa, ba, wx, bx, lam, wp, sc, past_len):
    n = z2.shape[0]
    col = lambda c: pl.BlockSpec((n, BRANCH_W), lambda i: (0, c // BRANCH_W))
    full = lambda a: pl.BlockSpec(a.shape, lambda i: (0,) * a.ndim)
    args = (conv_t, h0, pbuf_t, cw, cb, wa, ba, wx, bx, lam, wp, sc)
    return pl.pallas_call(
        functools.partial(_mix_sample_kernel, past_len),
        out_shape=(jax.ShapeDtypeStruct((n, BRANCH_W), bf16), jax.ShapeDtypeStruct((n, BRANCH_W), bf16),
                   jax.ShapeDtypeStruct((n, BRANCH_W), f32)),
        grid=(1,),
        in_specs=[col(C_LRU_X), col(C_LRU_G), col(C_POOL_X), col(C_POOL_G)] + [full(a) for a in args],
        out_specs=(pl.BlockSpec((n, BRANCH_W), lambda i: (0, 0)),) * 3,
        compiler_params=_params(("arbitrary",)),
        name="mix_sample",
    )(z2, z2, z2, z2, *args)


def _compress_kernel(tab_ref, k_ref, v_ref, wk_ref, wv_ref, fk_ref, sk_ref, fv_ref, sv_ref):
    del tab_ref
    p = pl.program_id(1)
    chunks = PAGE_SIZE // CMP_STRIDE

    @pl.when(p == 0)
    def _():
        sk_ref[...] = jnp.zeros_like(sk_ref)
        sv_ref[...] = jnp.zeros_like(sv_ref)

    for src, w_ref, f_ref, s_ref in ((k_ref, wk_ref, fk_ref, sk_ref), (v_ref, wv_ref, fv_ref, sv_ref)):
        tile = src[0]
        first = jnp.sum((tile * w_ref[0]).reshape(chunks, CMP_STRIDE, KV_W), axis=1)
        second = jnp.sum((tile * w_ref[1]).reshape(chunks, CMP_STRIDE, KV_W), axis=1)
        for m in range(chunks):
            f_ref[0, m % 4, pl.ds(2 * p + m // 4, 1), :] = first[m:m + 1, :]
            if m >= 1:
                s_ref[0, (m - 1) % 4, pl.ds(2 * p + (m - 1) // 4, 1), :] = second[m:m + 1, :]
            else:
                @pl.when(p > 0)
                def _():
                    s_ref[0, 3, pl.ds(2 * p - 1, 1), :] = second[0:1, :]


def _compress(table, src_k, src_v, col_k, col_v, wk, wv, nb):
    npg = table.shape[0] // nb
    out = jax.ShapeDtypeStruct((nb, 4, N_CMP // 4, KV_W), f32)
    ospec = lambda: pl.BlockSpec((1, 4, N_CMP // 4, KV_W), lambda b, p, tab: (b, 0, 0, 0))
    return pl.pallas_call(
        _compress_kernel,
        out_shape=(out,) * 4,
        grid_spec=pltpu.PrefetchScalarGridSpec(
            num_scalar_prefetch=1,
            grid=(nb, npg),
            in_specs=[pl.BlockSpec((1, PAGE_SIZE, KV_W), lambda b, p, tab: (tab[b * npg + p], 0, col_k)),
                      pl.BlockSpec((1, PAGE_SIZE, KV_W), lambda b, p, tab: (tab[b * npg + p], 0, col_v)),
                      pl.BlockSpec((2, PAGE_SIZE, KV_W), lambda b, p, tab: (0, 0, 0)),
                      pl.BlockSpec((2, PAGE_SIZE, KV_W), lambda b, p, tab: (0, 0, 0))],
            out_specs=(ospec(), ospec(), ospec(), ospec())),
        compiler_params=_params(("parallel", "arbitrary")),
        name="compress",
    )(table, src_k, src_v, wk, wv)


def _chunk_maps():
    n_chunk = PP * PAGE_SIZE // CMP_STRIDE
    rows = n_chunk // 4
    chunk_of = np.arange(PP * PAGE_SIZE) // CMP_STRIDE
    first = np.zeros((n_chunk, PP * PAGE_SIZE), np.float32)
    second = np.zeros((n_chunk + 8, PP * PAGE_SIZE), np.float32)
    for i in range(4):
        for jj in range(rows):
            first[i * rows + jj] = chunk_of == 4 * jj + i
            second[i * rows + jj] = chunk_of == 4 * jj + i + 1
    second[n_chunk] = chunk_of == 0
    return jnp.asarray(first, bf16), jnp.asarray(second, bf16)


def _compress_t_kernel(tab_ref, *refs):
    del tab_ref
    k_refs, v_refs = refs[:PP], refs[PP:2 * PP]
    wk_ref, wv_ref, ea_ref, eb_ref, fk_ref, sk_ref, fv_ref, sv_ref = refs[2 * PP:]
    ps = pl.program_id(1)
    rows = PP * PAGE_SIZE // CMP_STRIDE // 4
    r0 = pl.multiple_of(ps * rows, rows)
    for pages, w_ref, f_ref, s_ref in ((k_refs, wk_ref, fk_ref, sk_ref), (v_refs, wv_ref, fv_ref, sv_ref)):
        a1 = jnp.concatenate([(r[0, 0] * w_ref[0]).astype(bf16) for r in pages], axis=1)
        a2 = jnp.concatenate([(r[0, 0] * w_ref[1]).astype(bf16) for r in pages], axis=1)
        first = _nt(ea_ref[...], a1)
        second = _nt(eb_ref[...], a2)
        for i in range(4):
            f_ref[0, i, pl.ds(r0, rows), :] = first[i * rows:(i + 1) * rows]
            s_ref[0, i, pl.ds(r0, rows), :] = second[i * rows:(i + 1) * rows]

        @pl.when(ps > 0)
        def _():
            s_ref[0, 3, pl.ds(r0 - 1, 1), :] = second[4 * rows:4 * rows + 1]


def _compress_t(table, layer, cache_k, cache_v, wk, wv, nb):
    npg = table.shape[0] // nb
    ea, eb = _chunk_maps()
    out = jax.ShapeDtypeStruct((nb, 4, N_CMP // 4, KV_W), f32)
    ospec = lambda: pl.BlockSpec((1, 4, N_CMP // 4, KV_W), lambda b, p, tab: (b, 0, 0, 0))
    page = lambda i: pl.BlockSpec((1, 1, KV_W, PAGE_SIZE),
                                  lambda b, p, tab: (layer, tab[b * npg + p * PP + i], 0, 0))
    full = lambda a: pl.BlockSpec(a.shape, lambda b, p, tab: (0,) * a.ndim)
    return pl.pallas_call(
        _compress_t_kernel,
        out_shape=(out,) * 4,
        grid_spec=pltpu.PrefetchScalarGridSpec(
            num_scalar_prefetch=1,
            grid=(nb, npg // PP),
            in_specs=[page(i) for i in range(PP)] * 2 + [full(wk), full(wv), full(ea), full(eb)],
            out_specs=(ospec(), ospec(), ospec(), ospec())),
        compiler_params=_params(("parallel", "arbitrary")),
        name="compress_t",
    )(table, *([cache_k] * PP), *([cache_v] * PP), wk, wv, ea, eb)


def _finish_compress(f_ref, s_ref, phi_ref):
    blk = (f_ref[0] + s_ref[0]).reshape(N_CMP, KV_W)
    return _dot(blk.astype(bf16), phi_ref[...]).astype(bf16)


def _cmp_end(shape, axis):
    col = lax.broadcasted_iota(jnp.int32, shape, axis)
    n = ((col & (N_BLK - 1)) << 2) + (col >> 7)
    return n * CMP_STRIDE + (CMP_BLK - 1)


def _pick_blocks(score, n_pick):
    lane = lax.broadcasted_iota(jnp.int32, score.shape, 1).astype(f32)
    bias = jnp.full(score.shape, NEG, f32)
    for _ in range(n_pick):
        m = jnp.max(score, axis=-1, keepdims=True)
        first = jnp.min(jnp.where(score == m, lane, float(N_BLK)), axis=-1, keepdims=True)
        hit = lane == first
        bias = jnp.where(hit, 0.0, bias)
        score = jnp.where(hit, -jnp.inf, score)
    return bias


def _block_expand(first_block, n_keys):
    j = lax.broadcasted_iota(jnp.int32, (N_BLK, n_keys), 0)
    c = lax.broadcasted_iota(jnp.int32, (N_BLK, n_keys), 1)
    return jnp.where(j == first_block + (c >> 6), 1.0, 0.0).astype(bf16)


CQ = Q_PER_KV * TQ
WIN_T = WINDOW // TQ + 1
N_PARTS = 9
TILE_COL = HEAD_DIM + 6
V_ROWS = HEAD_DIM + 16


def _pick_blocks_t(score, forced, n_pick):
    jrow = lax.broadcasted_iota(jnp.int32, score.shape, 0).astype(f32)
    bias = jnp.where(forced, 0.0, NEG)
    score = jnp.where(forced, -jnp.inf, score)
    for _ in range(n_pick):
        m = jnp.max(score, axis=0, keepdims=True)
        first = jnp.min(jnp.where(score == m, jrow, float(N_BLK)), axis=0, keepdims=True)
        hit = jrow == first
        bias = jnp.where(hit, 0.0, bias)
        score = jnp.where(hit, -jnp.inf, score)
    return bias


def _nsa_prompt_kernel(q_ref, srow_ref, slope_ref, bg_ref, ng_ref, fk_ref, sk_ref, fv_ref, sv_ref,
                       phik_ref, phivt_ref, cpos_ref, ksa_ref, vst_ref, kwa_ref, vwt_ref, o_ref,
                       kca_scr, vct_scr, qa_scr, oc_scr, m_scr, acc_scr, negc_scr, negd_scr, negw_scr):
    qi = pl.program_id(1)
    s0 = qi * TQ

    @pl.when(qi == 0)
    def _():
        kc = _finish_compress(fk_ref, sk_ref, phik_ref)
        for g in range(KV_HEADS):
            kca_scr[g] = jnp.concatenate([kc[:, g * HEAD_DIM:(g + 1) * HEAD_DIM], cpos_ref[...]], axis=1)
        blk_v = (fv_ref[0] + sv_ref[0]).reshape(N_CMP, KV_W).astype(bf16)
        vct_scr[...] = _nt(phivt_ref[...], blk_v).astype(bf16)

    t_of = lambda shape: s0 + (lax.broadcasted_iota(jnp.int32, shape, 1) & (TQ - 1))
    q_t = (q_ref[0] * Q_SCALE).T.astype(bf16)

    negc_scr[...] = jnp.where(_cmp_end((N_CMP, CQ), 0) <= t_of((N_CMP, CQ)), 0.0, NEG)
    sees_block = t_of((1, CQ)) >= CMP_BLK - 1
    t_q = t_of((N_BLK, TQ))
    jr = lax.broadcasted_iota(jnp.int32, (N_BLK, TQ), 0)
    jb = t_q >> 6
    ok_b = (jr << 6) <= t_q
    forced = (jr == 0) | (jr == jb) | (jr == jb - 1)
    tile_any = [jnp.full((N_BLK // 8, 1), NEG, f32) for _ in LOOP_GROUPS]
    for g in range(KV_HEADS):
        q_g = jnp.concatenate([q_t[(g * Q_PER_KV + r) * HEAD_DIM:(g * Q_PER_KV + r + 1) * HEAD_DIM]
                               for r in range(Q_PER_KV)], axis=1)
        qa = jnp.concatenate([q_g, srow_ref[g]], axis=0)
        st = _dot(kca_scr[g], qa) + negc_scr[...]
        e = jnp.exp2(st - jnp.max(st, axis=0, keepdims=True))
        pt = e * jnp.where(sees_block, 1.0 / jnp.sum(e, axis=0, keepdims=True), 0.0)
        oc_scr[g] = _dot(vct_scr[g * HEAD_DIM:(g + 1) * HEAD_DIM], pt.astype(bf16))
        ps = None
        for i in range(4):
            for r in range(Q_PER_KV):
                slab = pt[i * N_BLK:(i + 1) * N_BLK, r * TQ:(r + 1) * TQ]
                ps = slab if ps is None else ps + slab
        bias = _pick_blocks_t(jnp.where(ok_b, ps, NEG), forced, N_SEL - 3)
        qa_scr[g] = jnp.concatenate([qa, jnp.concatenate([bias.astype(bf16)] * Q_PER_KV, axis=1)], axis=0)
        any_t = jnp.max(bias.reshape(N_BLK // 8, 8, TQ), axis=1)
        li = [g in grp for grp in LOOP_GROUPS].index(True)
        tile_any[li] = jnp.maximum(tile_any[li], jnp.max(any_t, axis=1, keepdims=True))
    kt_row = lax.broadcasted_iota(jnp.int32, tile_any[0].shape, 0)
    tile_bits = [jnp.sum(jnp.where(ta == 0.0, 1 << kt_row, 0)) for ta in tile_any]

    m_scr[...] = jnp.full(m_scr.shape, NEG, f32)
    acc_scr[...] = jnp.zeros_like(acc_scr)

    def sel_tile(kt, diagonal, groups, n_keys=TK):
        k0 = pl.multiple_of(kt * TK, TK)
        tile_off = (k0 - s0).astype(f32)
        key_r = lax.broadcasted_iota(jnp.int32, (n_keys, N_BLK), 0)
        blk_c = lax.broadcasted_iota(jnp.int32, (n_keys, N_BLK), 1)
        expand = jnp.where(blk_c == kt * (TK // SEL_BLK) + (key_r >> 6), 1.0, 0.0).astype(bf16)
        k_aug = ksa_ref[0, pl.ds(k0, n_keys), :]
        v_t = vst_ref[0, kt][:, 0:n_keys]
        for g in groups:
            lhs = jnp.concatenate([k_aug[:, g * QA_W:(g + 1) * QA_W], expand], axis=1)
            st = _dot(lhs, qa_scr[g])
            if diagonal:
                st = st + negd_scr[qi % (TK // TQ), 0:n_keys]
            c = slope_ref[g] * tile_off
            m_old = m_scr[g]
            m_new = jnp.maximum(m_old, jnp.max(st, axis=0, keepdims=True) + c)
            alpha = jnp.exp2(m_old - m_new)
            pt = jnp.exp2((st - (m_new - c)).astype(bf16))
            acc_scr[g] = alpha * acc_scr[g] + _dot(v_t[g * V_ROWS:(g + 1) * V_ROWS], pt)
            m_scr[g] = m_new

    last = s0 // TK

    @pl.when(qi < TK // TQ)
    def _():
        key_pos = lax.broadcasted_iota(jnp.int32, (TK, CQ), 0)
        negd_scr[qi] = jnp.where(key_pos <= t_of((TK, CQ)), 0.0, NEG)

    for groups, bits in zip(LOOP_GROUPS, tile_bits):
        def sel_body(kt, carry, groups=groups, bits=bits):
            @pl.when(((bits >> kt) & 1) == 1)
            def _():
                sel_tile(kt, False, groups)
            return carry

        lax.fori_loop(0, last, sel_body, 0)
    for sub in range(TK // TQ):
        @pl.when(qi % (TK // TQ) == sub)
        def _(sub=sub):
            sel_tile(last, True, range(KV_HEADS), (sub + 1) * TQ)

    w_tile = jnp.maximum(qi - WINDOW // TQ, 0)
    n_win = WIN_T * TQ

    @pl.when(qi <= WINDOW // TQ)
    def _():
        dist_w = t_of((n_win, CQ)) - (w_tile * TQ + lax.broadcasted_iota(jnp.int32, (n_win, CQ), 0))
        negw_scr[...] = jnp.where((dist_w >= 0) & (dist_w < WINDOW), 0.0, NEG)

    lane_w = lax.broadcasted_iota(jnp.int32, (n_win, QA_W), 1)
    slab_off = (lax.broadcasted_iota(jnp.int32, (n_win, QA_W), 0) // TQ * TQ).astype(f32).astype(bf16)
    in_tile_col = (lane_w >= TILE_COL) & (lane_w < TILE_COL + 3)
    kw_all = kwa_ref[0, pl.ds(pl.multiple_of(w_tile * TQ, TQ), n_win), :]
    vw_t = jnp.concatenate([vwt_ref[0, w_tile + i] for i in range(WIN_T)], axis=1)
    gates = jax.nn.sigmoid(bg_ref[0].T)
    blocks = []
    for g in range(KV_HEADS):
        lhs = jnp.where(in_tile_col, slab_off, kw_all[:, g * QA_W:(g + 1) * QA_W])
        st = _dot(lhs, qa_scr[g, 0:QA_W]) + negw_scr[...]
        pt = jnp.exp2((st - jnp.max(st, axis=0, keepdims=True)).astype(bf16))
        win = _dot(vw_t[g * V_ROWS:(g + 1) * V_ROWS], pt)
        o_w = win[0:HEAD_DIM] * (1.0 / win[HEAD_DIM:HEAD_DIM + 1])
        o_s = acc_scr[g, 0:HEAD_DIM] * (1.0 / acc_scr[g, HEAD_DIM:HEAD_DIM + 1])
        o_c = oc_scr[g]
        heads = []
        for r in range(Q_PER_KV):
            h = g * Q_PER_KV + r
            cols = slice(r * TQ, (r + 1) * TQ)
            heads.append(gates[3 * h:3 * h + 1] * o_c[:, cols] + gates[3 * h + 1:3 * h + 2] * o_s[:, cols]
                         + gates[3 * h + 2:3 * h + 3] * o_w[:, cols])
        for r in range(0, Q_PER_KV, 2):
            blocks.append(jnp.concatenate(heads[r:r + 2], axis=0).T)
    y = jnp.concatenate(blocks, axis=-1)
    o_ref[0] = (y * _silu(ng_ref[0])).astype(bf16)


def _nsa_prompt(z3, fk, sk, fv, sv, phik, phivt):
    b, s, _ = z3.shape
    nq = s // TQ
    parts = _slope_parts().reshape(KV_HEADS, Q_PER_KV, POS_ROWS)
    srow = jnp.asarray(np.repeat(parts.transpose(0, 2, 1), TQ, axis=2), bf16)
    slope = jnp.asarray(np.repeat(np.asarray(SLOPES_LOG2, np.float32).reshape(KV_HEADS, 1, Q_PER_KV), TQ, axis=2))
    ksa = _key_aug(_kv_seg(z3, 2), TK)
    kwa = _key_aug(_kv_seg(z3, 4), TQ)
    vst = _value_tiles(_kv_seg(z3, 3), TK)
    vwt = _value_tiles(_kv_seg(z3, 5), TQ)
    cpos = _cmp_pos_cols()

    once = pl.Buffered(1)
    cmp = lambda: pl.BlockSpec((1, 4, N_CMP // 4, KV_W), lambda i, t: (i, 0, 0, 0))
    full = lambda a: pl.BlockSpec(a.shape, lambda i, t: (0,) * a.ndim)
    return pl.pallas_call(
        _nsa_prompt_kernel,
        out_shape=jax.ShapeDtypeStruct((b, s, BRANCH_W), bf16),
        grid=(b, nq),
        in_specs=[pl.BlockSpec((1, TQ, BRANCH_W), lambda i, t: (i, t, C_Q // BRANCH_W)),
                  full(srow), full(slope),
                  pl.BlockSpec((1, TQ, BG_W), lambda i, t: (i, t, C_BG // BG_W)),
                  pl.BlockSpec((1, TQ, BRANCH_W), lambda i, t: (i, t, C_NSA_G // BRANCH_W)),
                  cmp(), cmp(), cmp(), cmp(), full(phik), full(phivt), full(cpos),
                  pl.BlockSpec((1, s, KV_HEADS * QA_W), lambda i, t: (i, 0, 0), pipeline_mode=once),
                  pl.BlockSpec((1, s // TK, KV_HEADS * V_ROWS, TK), lambda i, t: (i, 0, 0, 0), pipeline_mode=once),
                  pl.BlockSpec((1, s, KV_HEADS * QA_W), lambda i, t: (i, 0, 0), pipeline_mode=once),
                  pl.BlockSpec((1, nq, KV_HEADS * V_ROWS, TQ), lambda i, t: (i, 0, 0, 0), pipeline_mode=once)],
        out_specs=pl.BlockSpec((1, TQ, BRANCH_W), lambda i, t: (i, t, 0)),
        scratch_shapes=[pltpu.VMEM((KV_HEADS, N_CMP, QA_W), bf16), pltpu.VMEM((KV_W, N_CMP), bf16),
                        pltpu.VMEM((KV_HEADS, QA_W + N_BLK, CQ), bf16),
                        pltpu.VMEM((KV_HEADS, HEAD_DIM, CQ), f32),
                        pltpu.VMEM((KV_HEADS, 1, CQ), f32),
                        pltpu.VMEM((KV_HEADS, V_ROWS, CQ), f32),
                        pltpu.VMEM((N_CMP, CQ), f32), pltpu.VMEM((TK // TQ, TK, CQ), f32),
                        pltpu.VMEM((WIN_T * TQ, CQ), f32)],
        compiler_params=_params(("parallel", "arbitrary")),
        name="nsa_prompt",
    )(z3, srow, slope, z3, z3, fk, sk, fv, sv, phik, phivt, cpos, ksa, vst, kwa, vwt)


def _by_group(fn):
    hg = lax.broadcasted_iota(jnp.int32, (N_HEADS, 1), 0) >> 2
    out = fn(0)
    for g in range(1, KV_HEADS):
        out = jnp.where(hg == g, fn(g), out)
    return out


def _nsa_sample_kernel(past_len, tab_ref, q_ref, bg_ref, ng_ref, slope_ref,
                       fk_ref, sk_ref, fv_ref, sv_ref, phik_ref, phiv_ref,
                       ksn_ref, vsn_ref, kwn_ref, vwn_ref, bk_ref, bv_ref, *refs):
    del tab_ref
    kp_refs, vp_refs = refs[:PP_SEL], refs[PP_SEL:2 * PP_SEL]
    o_ref, q_scr, sel_scr, oc_scr, ow_scr, m_scr, l_scr, acc_scr = refs[2 * PP_SEL:]
    p = pl.program_id(1)
    slope = slope_ref[:, 0:1]
    gsl = lambda g: slice(g * HEAD_DIM, (g + 1) * HEAD_DIM)

    def per_head(row_ref):
        row = row_ref[0]
        return _by_group(lambda g: jnp.broadcast_to(row[:, gsl(g)], (N_HEADS, HEAD_DIM)))

    @pl.when(p == 0)
    def _():
        qrow = q_ref[0]
        q16 = jnp.concatenate([qrow[:, h * HEAD_DIM:(h + 1) * HEAD_DIM] for h in range(N_HEADS)], axis=0)
        q_scr[...] = q16
        q16f = q16.astype(f32)
        kc = _finish_compress(fk_ref, sk_ref, phik_ref)
        vc = _finish_compress(fv_ref, sv_ref, phiv_ref)

        dist_c = past_len - _cmp_end((1, N_CMP), 1)
        ok_c = dist_c >= 0
        s = _by_group(lambda g: _nt(q16, kc[:, gsl(g)])) - slope * dist_c.astype(f32)
        s = jnp.where(ok_c, s, NEG)
        e = jnp.exp(s - jnp.max(s, axis=-1, keepdims=True))
        pc = jnp.where(ok_c, e * (1.0 / jnp.sum(e, axis=-1, keepdims=True)), 0.0)
        pcb = pc.astype(bf16)
        oc_scr[...] = _by_group(lambda g: _dot(pcb, vc[:, gsl(g)]))
        ps16 = (pc[:, 0:N_BLK] + pc[:, N_BLK:2 * N_BLK]) + (pc[:, 2 * N_BLK:3 * N_BLK] + pc[:, 3 * N_BLK:])
        ps = jnp.concatenate([jnp.sum(ps16[g * Q_PER_KV:(g + 1) * Q_PER_KV], axis=0, keepdims=True)
                              for g in range(KV_HEADS)] + [jnp.zeros((8 - KV_HEADS, N_BLK), f32)], axis=0)
        jl = lax.broadcasted_iota(jnp.int32, (8, N_BLK), 1)
        jb = past_len // SEL_BLK
        forced = jnp.where((jl == 0) | (jl == jb) | (jl == jb - 1), FORCE, 0.0)
        bias8 = _pick_blocks(ps + forced, N_SEL - 1)
        sel_scr[...] = jnp.concatenate(
            [jnp.broadcast_to(bias8[g:g + 1], (Q_PER_KV, N_BLK)) for g in range(KV_HEADS)], axis=0).astype(bf16)

        wb = bk_ref.shape[-1]
        dist_w = wb - lax.broadcasted_iota(jnp.int32, (1, wb), 1)
        ok_w = (dist_w >= 0) & (dist_w < WINDOW)
        bk = bk_ref[0, 0].astype(bf16)
        bv = bv_ref[0, 0].astype(bf16)
        s_buf = _by_group(lambda g: _dot(q16, bk[gsl(g)])) - slope * dist_w.astype(f32)
        s_buf = jnp.where(ok_w, s_buf, NEG)
        s_new = jnp.sum(q16f * per_head(kwn_ref), axis=-1, keepdims=True)
        m_w = jnp.maximum(jnp.max(s_buf, axis=-1, keepdims=True), s_new)
        e_buf = jnp.exp(s_buf - m_w)
        e_new = jnp.exp(s_new - m_w)
        ebb = e_buf.astype(bf16)
        num = _by_group(lambda g: _nt(ebb, bv[gsl(g)])) + e_new * per_head(vwn_ref)
        ow_scr[...] = num * (1.0 / (jnp.sum(e_buf, axis=-1, keepdims=True) + e_new))

        m_scr[...] = jnp.sum(q16f * per_head(ksn_ref), axis=-1, keepdims=True)
        l_scr[...] = jnp.ones_like(l_scr)
        acc_scr[...] = per_head(vsn_ref)

    q16 = q_scr[...]
    n_keys = PP_SEL * PAGE_SIZE
    kp = jnp.concatenate([r[0, 0].astype(bf16) for r in kp_refs], axis=1)
    vp = jnp.concatenate([r[0, 0].astype(bf16) for r in vp_refs], axis=1)
    dist = past_len - (p * n_keys + lax.broadcasted_iota(jnp.int32, (1, n_keys), 1))
    expand = _block_expand(p * (n_keys // SEL_BLK), n_keys)
    s = _by_group(lambda g: _dot(q16, kp[gsl(g)])) - slope * dist.astype(f32) + _dot(sel_scr[...], expand)
    m_old = m_scr[...]
    m_new = jnp.maximum(m_old, jnp.max(s, axis=-1, keepdims=True))
    alpha = jnp.exp(m_old - m_new)
    pr = jnp.exp(s - m_new)
    prb = pr.astype(bf16)
    l_scr[...] = alpha * l_scr[...] + jnp.sum(pr, axis=-1, keepdims=True)
    acc_scr[...] = alpha * acc_scr[...] + _by_group(lambda g: _nt(prb, vp[gsl(g)]))
    m_scr[...] = m_new

    @pl.when(p == pl.num_programs(1) - 1)
    def _():
        o_s = acc_scr[...] * (1.0 / l_scr[...])
        gates = jax.nn.sigmoid(bg_ref[0])
        lane = lax.broadcasted_iota(jnp.int32, (N_HEADS, BG_W), 1)
        h3 = 3 * lax.broadcasted_iota(jnp.int32, (N_HEADS, BG_W), 0)
        gate = lambda n: jnp.sum(jnp.where(lane == h3 + n, gates, 0.0), axis=-1, keepdims=True)
        y16 = gate(0) * oc_scr[...] + gate(1) * o_s + gate(2) * ow_scr[...]
        y = jnp.concatenate([y16[h:h + 1, :] for h in range(N_HEADS)], axis=-1)
        o_ref[0] = (y * _silu(ng_ref[0])).astype(bf16)


def _nsa_sample(table, layer, qb, z2, slopes, fk, sk, fv, sv, phik, phiv, pool_k, pool_v, buf_k, buf_v, past_len):
    nb = qb.shape[0]
    npg = table.shape[0] // nb
    z3 = z2.reshape(nb, 1, IN_WP)
    tok = lambda w, c: pl.BlockSpec((1, 1, w), lambda b, p, tab: (b, 0, c))
    cmp = lambda: pl.BlockSpec((1, 4, N_CMP // 4, KV_W), lambda b, p, tab: (b, 0, 0, 0))
    phi = lambda: pl.BlockSpec((KV_W, KV_W), lambda b, p, tab: (0, 0))
    wb = buf_k.shape[-1]
    win = lambda: pl.BlockSpec((1, 1, KV_W, wb), lambda b, p, tab: (layer, b, 0, 0))
    page = lambda i: pl.BlockSpec((1, 1, KV_W, PAGE_SIZE),
                                  lambda b, p, tab: (layer, tab[b * npg + p * PP_SEL + i], 0, 0))
    kvc = C_KV // KV_W
    return pl.pallas_call(
        functools.partial(_nsa_sample_kernel, past_len),
        out_shape=jax.ShapeDtypeStruct((nb, 1, BRANCH_W), bf16),
        grid_spec=pltpu.PrefetchScalarGridSpec(
            num_scalar_prefetch=1,
            grid=(nb, npg // PP_SEL),
            in_specs=[tok(BRANCH_W, 0), tok(BG_W, C_BG // BG_W), tok(BRANCH_W, C_NSA_G // BRANCH_W),
                      pl.BlockSpec((N_HEADS, 128), lambda b, p, tab: (0, 0)),
                      cmp(), cmp(), cmp(), cmp(), phi(), phi(),
                      tok(KV_W, kvc + 2), tok(KV_W, kvc + 3), tok(KV_W, kvc + 4), tok(KV_W, kvc + 5),
                      win(), win()] + [page(i) for i in range(PP_SEL)] * 2,
            out_specs=pl.BlockSpec((1, 1, BRANCH_W), lambda b, p, tab: (b, 0, 0)),
            scratch_shapes=[pltpu.VMEM((N_HEADS, HEAD_DIM), bf16), pltpu.VMEM((N_HEADS, N_BLK), bf16),
                            pltpu.VMEM((N_HEADS, HEAD_DIM), f32), pltpu.VMEM((N_HEADS, HEAD_DIM), f32),
                            pltpu.VMEM((N_HEADS, 1), f32), pltpu.VMEM((N_HEADS, 1), f32),
                            pltpu.VMEM((N_HEADS, HEAD_DIM), f32)]),
        compiler_params=_params(("parallel", "arbitrary")),
        name="nsa_sample",
    )(table, qb.reshape(nb, 1, BRANCH_W), z3, z3, slopes, fk, sk, fv, sv, phik, phiv,
      z3, z3, z3, z3, buf_k, buf_v, *([pool_k] * PP_SEL), *([pool_v] * PP_SEL))


def _merge_kernel(zl_ref, zp_ref, zn_ref, m0_ref, m1_ref, m2_ref, wb_ref, wo_ref, g_ref, x_ref, y_ref):
    acc = None
    for n, (zz, mg) in enumerate(((zl_ref, m0_ref), (zp_ref, m1_ref), (zn_ref, m2_ref))):
        term = jax.nn.sigmoid(mg[...]) * _dot(zz[...], wb_ref[n])
        acc = term if acc is None else acc + term
    out = _dot(acc.astype(bf16), wo_ref[...])
    ms = jnp.mean(out * out, axis=-1, keepdims=True)
    y_ref[...] = x_ref[...] + out * lax.rsqrt(ms + EPS) * g_ref[...]


def _merge(zl, zp, zn, z2, wb, wo, g_row, x2d):
    n = x2d.shape[0]
    tm = min(n, 256)
    rowblk = lambda c: pl.BlockSpec((tm, D_MODEL), lambda i: (i, c))
    return pl.pallas_call(
        _merge_kernel,
        out_shape=jax.ShapeDtypeStruct((n, D_MODEL), f32),
        grid=(n // tm,),
        in_specs=[rowblk(0), rowblk(0), rowblk(0),
                  rowblk(C_MG // D_MODEL), rowblk(C_MG // D_MODEL + 1), rowblk(C_MG // D_MODEL + 2),
                  pl.BlockSpec((N_BRANCH, BRANCH_W, D_MODEL), lambda i: (0, 0, 0)),
                  pl.BlockSpec((D_MODEL, D_MODEL), lambda i: (0, 0)),
                  pl.BlockSpec((1, D_MODEL), lambda i: (0, 0)),
                  rowblk(0)],
        out_specs=rowblk(0),
        compiler_params=_params(("parallel",)),
        name="merge",
    )(zl, zp, zn, z2, z2, z2, wb, wo, g_row, x2d)


def _block_diag(w, per):
    n, d, _ = w.shape
    eye = jnp.eye(per, dtype=w.dtype)
    t = jnp.einsum('cpde,pq->cpdqe', w.reshape(n // per, per, d, d), eye)
    return t.reshape(n // per, per * d, per * d)


def _pack_w_in(w):
    old_kv = 6 * BRANCH_W
    old_bg = old_kv + 6 * KV_W
    old_mg = old_bg + N_BRANCH * N_HEADS
    parts = [w[:, :old_kv], w[:, old_mg:], w[:, old_kv:old_bg], w[:, old_bg:old_mg]]
    packed = jnp.concatenate(parts, axis=1)
    return jnp.pad(packed, ((0, 0), (0, IN_WP - packed.shape[1]))).astype(bf16)


def _tile_wpos(w_pos):
    halves = w_pos.reshape(2, CMP_STRIDE, HEAD_DIM)
    return jnp.tile(halves, (1, PAGE_SIZE // CMP_STRIDE, KV_HEADS))


def _tile_wpos_t(w_pos):
    halves = w_pos.reshape(2, CMP_STRIDE, HEAD_DIM).swapaxes(1, 2)
    return jnp.tile(halves, (1, KV_HEADS, PAGE_SIZE // CMP_STRIDE))


def _lanes_last(cache):
    d, n, rows = cache.shape[:3]
    return jnp.transpose(cache, (0, 1, 3, 4, 2)).reshape(d, n, KV_W, rows)


def _slope_parts():
    cols = np.zeros((N_HEADS, POS_ROWS), np.float32)
    rnd = lambda v: np.float32(np.float32(v).astype(bf16))
    for h, s in enumerate(SLOPES_LOG2):
        s1 = rnd(s)
        s2 = rnd(np.float32(s) - s1)
        s3 = rnd(np.float32(s) - s1 - s2)
        cols[h, 0:N_PARTS] = [s1, s2, s3] * 3
    return cols


def _split_pos(pos, shift):
    cols = np.zeros((pos.shape[0], POS_ROWS), np.float32)
    cols[:, 0:3] = ((pos >> shift) << shift)[:, None]
    cols[:, 3:6] = (pos & ((1 << shift) - 1))[:, None]
    return cols


def _key_aug(k_rows, tile):
    b, s, _ = k_rows.shape
    pos = jnp.asarray(_split_pos(np.arange(s) % tile, 4), bf16)
    kg = k_rows.astype(bf16).reshape(b, s, KV_HEADS, HEAD_DIM)
    posb = jnp.broadcast_to(pos[None, :, None, :], (b, s, KV_HEADS, POS_ROWS))
    return jnp.concatenate([kg, posb], axis=-1).reshape(b, s, KV_HEADS * QA_W)


def _value_tiles(v_rows, tile):
    b, s, _ = v_rows.shape
    vt = v_rows.astype(bf16).reshape(b, s // tile, tile, KV_HEADS, HEAD_DIM).transpose(0, 1, 3, 4, 2)
    ones = jnp.ones((b, s // tile, KV_HEADS, V_ROWS - HEAD_DIM, tile), bf16)
    return jnp.concatenate([vt, ones], axis=3).reshape(b, s // tile, KV_HEADS * V_ROWS, tile)


def _cmp_pos_cols():
    slot = np.arange(N_CMP)
    n = ((slot & (N_BLK - 1)) << 2) + (slot >> 7)
    return jnp.asarray(_split_pos(n * CMP_STRIDE + CMP_BLK - 1, 8), bf16)


def _kv_seg(z, i):
    return z[..., C_KV + i * KV_W:C_KV + (i + 1) * KV_W]


def kernel(x_prompt, x_sample, cache_cmp_k, cache_cmp_v, cache_sel_k, cache_sel_v, cache_win_k, cache_win_v, state_conv, state_lru, state_pool, page_table, g_pre, g_post, w_in, conv_w, conv_b, w_rg_a, b_rg_a, w_rg_x, b_rg_x, lru_lambda, w_pool, pool_scale, cmp_pos_k, cmp_phi_k, cmp_pos_v, cmp_phi_v, w_branch, w_out):
    depth = w_in.shape[0]
    bp, seq, _ = x_prompt.shape
    bs = x_sample.shape[0]
    n_pages = page_table.shape[1]
    past_len = n_pages * PAGE_SIZE
    n_phys = cache_cmp_k.shape[1]
    assert seq == N_BLK * SEL_BLK and past_len == N_BLK * SEL_BLK and x_sample.shape[1] == 1
    wb = cache_win_k.shape[2]

    table_s = page_table.reshape(-1).astype(jnp.int32)
    table_p = jnp.arange(bp * (seq // PAGE_SIZE), dtype=jnp.int32)
    slopes = jnp.broadcast_to(jnp.asarray(SLOPES, f32)[:, None], (N_HEADS, 128))
    row = lambda v: v.reshape(1, -1)
    cmp_kt, cmp_vt = _lanes_last(cache_cmp_k), _lanes_last(cache_cmp_v)
    sel_kt, sel_vt = _lanes_last(cache_sel_k), _lanes_last(cache_sel_v)
    win_kt, win_vt = _lanes_last(cache_win_k), _lanes_last(cache_win_v)

    xp = x_prompt.reshape(bp * seq, D_MODEL)
    xs = x_sample.reshape(bs, D_MODEL)
    pr = [[] for _ in range(9)]
    sm = [[] for _ in range(9)]
    for l in range(depth):
        w_packed = _pack_w_in(w_in[l])
        wa = _block_diag(w_rg_a[l], MXU_W // LRU_BD).astype(bf16)
        wx = _block_diag(w_rg_x[l], MXU_W // LRU_BD).astype(bf16)
        wp = w_pool[l].astype(bf16)
        phik = _block_diag(jnp.broadcast_to(cmp_phi_k[l], (KV_HEADS, HEAD_DIM, HEAD_DIM)), KV_HEADS)[0].astype(bf16)
        phiv = _block_diag(jnp.broadcast_to(cmp_phi_v[l], (KV_HEADS, HEAD_DIM, HEAD_DIM)), KV_HEADS)[0].astype(bf16)
        wpos_k = _tile_wpos(cmp_pos_k[l])
        wpos_v = _tile_wpos(cmp_pos_v[l])
        wbr = w_branch[l].astype(bf16)
        wo = w_out[l].astype(bf16)
        lru_w = (conv_w[l], row(conv_b[l]), wa, row(b_rg_a[l]), wx, row(b_rg_x[l]), row(lru_lambda[l]))

        z = _inproj(xp, row(g_pre[l]), w_packed)
        z3 = z.reshape(bp, seq, IN_WP)
        zl, conv_tail, h_p = _lru_prompt(z3, *lru_w)
        zpool, pool_tail = _pool_prompt(z3, wp, row(pool_scale[l]))
        zr = z.reshape(bp * seq // PAGE_SIZE, PAGE_SIZE, IN_WP)
        fk, sk, fv, sv = _compress(table_p, zr, zr, C_KV // KV_W, C_KV // KV_W + 1, wpos_k, wpos_v, bp)
        zn = _nsa_prompt(z3, fk, sk, fv, sv, phik, phiv.T)
        xp = _merge(zl.reshape(bp * seq, BRANCH_W), zpool.reshape(bp * seq, BRANCH_W),
                    zn.reshape(bp * seq, BRANCH_W), z, wbr, wo, row(g_post[l]), xp)
        kv_rows = [_kv_seg(z3, i).reshape(bp, seq, KV_HEADS, HEAD_DIM) for i in range(6)]
        wlen = min(WINDOW, seq)
        st_p = kv_rows[:4] + [kv_rows[4][:, -wlen:], kv_rows[5][:, -wlen:],
                              conv_tail[:, -(CONV_W - 1):], h_p[:, 0], pool_tail[:, -POOL_BUF:]]

        zs = _inproj(xs, row(g_pre[l]), w_packed)
        zls, zps, h_s = _mix_sample(zs, state_conv[l].swapaxes(0, 1), state_lru[l], state_pool[l].swapaxes(0, 1),
                                    *lru_w, wp, row(pool_scale[l]), past_len)
        fk, sk, fv, sv = _compress_t(table_s, l, cmp_kt, cmp_vt, _tile_wpos_t(cmp_pos_k[l]),
                                     _tile_wpos_t(cmp_pos_v[l]), bs)
        qs = (zs[:, C_Q:C_Q + BRANCH_W] * (HEAD_DIM ** -0.5)).astype(bf16)
        zns = _nsa_sample(table_s, l, qs, zs, slopes, fk, sk, fv, sv, phik, phiv,
                          sel_kt, sel_vt, win_kt, win_vt, past_len)
        xs = _merge(zls, zps, zns.reshape(bs, BRANCH_W), zs, wbr, wo, row(g_post[l]), xs)
        kv_new = [_kv_seg(zs, i).reshape(bs, 1, KV_HEADS, HEAD_DIM) for i in range(6)]
        st_s = kv_new[:4] + [jnp.concatenate([cache_win_k[l], kv_new[4]], axis=1)[:, -wb:],
                             jnp.concatenate([cache_win_v[l], kv_new[5]], axis=1)[:, -wb:],
                             jnp.concatenate([state_conv[l], zs[:, None, C_LRU_X:C_LRU_X + BRANCH_W]], axis=1)[:, -(CONV_W - 1):],
                             h_s,
                             jnp.concatenate([state_pool[l], zs[:, None, C_POOL_X:C_POOL_X + BRANCH_W]], axis=1)[:, -POOL_BUF:]]
        for i in range(9):
            pr[i].append(st_p[i])
            sm[i].append(st_s[i])

    out = [xp.reshape(bp, seq, D_MODEL), xs.reshape(bs, 1, D_MODEL)]
    for i in range(9):
        out += [jnp.stack(pr[i]), jnp.stack(sm[i])]
    return tuple(out)
```

```python
import functools

import numpy as np
import jax
import jax.numpy as jnp
from jax import lax
from jax.experimental import pallas as pl
from jax.experimental.pallas import tpu as pltpu

f32 = jnp.float32
bf16 = jnp.bfloat16

D_MODEL = 1024
BRANCH_W = 1024
N_BRANCH = 3
LRU_BLOCKS = 16
LRU_BD = BRANCH_W // LRU_BLOCKS
CONV_W = 4
LRU_C = 8.0
POOL_WINDOWS = (2, 4, 8, 16)
POOL_GD = BRANCH_W // len(POOL_WINDOWS)
POOL_BUF = max(POOL_WINDOWS) - 1
N_HEADS = 16
HEAD_DIM = 64
KV_HEADS = 4
Q_PER_KV = N_HEADS // KV_HEADS
KV_W = KV_HEADS * HEAD_DIM
CMP_STRIDE = 16
CMP_BLK = 2 * CMP_STRIDE
SEL_BLK = 64
N_SEL = 16
WINDOW = 512
PAGE_SIZE = 128
FORCE = 1e4
NEG = -1e30
EPS = 1e-6

C_LRU_X, C_POOL_X, C_KV = 0, 1024, 2048
C_BG = C_KV + 6 * KV_W
BG_W = 128
Z32_W = 4096
C_LRU_G, C_POOL_G, C_Q, C_NSA_G, C_MG = 0, 1024, 2048, 3072, 4096
Z16_W = C_MG + N_BRANCH * D_MODEL
IN_TN = 1024
N32_TILES = Z32_W // IN_TN

MXU_W = 256
VMEM_LIMIT = 56 * 1024 * 1024

N_CMP = 512
N_BLK = 128
TQ = 256
TK = 512
PP = 8
PP_SEL = 16
POS_ROWS = 64
QA_W = HEAD_DIM + POS_ROWS

SLOPES = [float(np.float32(2.0 ** (-8.0 * (h + 1) / N_HEADS))) for h in range(N_HEADS)]
LOG2E = float(np.log2(np.e))
SLOPES_LOG2 = [float(np.float32(s * LOG2E)) for s in SLOPES]
Q_SCALE = float(np.float32(HEAD_DIM ** -0.5 * LOG2E))
LOOP_GROUPS = ((0,), (1,), (2,), (3,))


def _nt(a, b):
    return lax.dot_general(a, b, (((1,), (1,)), ((), ())), preferred_element_type=f32)


def _dot(a, b):
    return jnp.dot(a, b, preferred_element_type=f32)


def _silu(x):
    x = x.astype(f32)
    return x * jax.nn.sigmoid(x)


def _params(sem):
    return pltpu.CompilerParams(dimension_semantics=sem, vmem_limit_bytes=VMEM_LIMIT)


def _inproj_kernel(x_ref, g_ref, w_ref, o32_ref, o16_ref, u_ref):
    j = pl.program_id(1)

    @pl.when(j == 0)
    def _():
        x = x_ref[...]
        ms = jnp.mean(x * x, axis=-1, keepdims=True)
        u_ref[...] = (x * lax.rsqrt(ms + EPS) * g_ref[...]).astype(bf16)

    @pl.when(j < N32_TILES)
    def _():
        o32_ref[...] = _dot(u_ref[...], w_ref[...])

    @pl.when(j >= N32_TILES)
    def _():
        o16_ref[...] = _dot(u_ref[...], w_ref[...]).astype(bf16)


def _inproj(x2d, g_row, w_packed):
    n = x2d.shape[0]
    tm = min(n, 1024)
    return pl.pallas_call(
        _inproj_kernel,
        out_shape=(jax.ShapeDtypeStruct((n, Z32_W), f32), jax.ShapeDtypeStruct((n, Z16_W), bf16)),
        grid=(n // tm, (Z32_W + Z16_W) // IN_TN),
        in_specs=[pl.BlockSpec((tm, D_MODEL), lambda i, j: (i, 0)),
                  pl.BlockSpec((1, D_MODEL), lambda i, j: (0, 0)),
                  pl.BlockSpec((D_MODEL, IN_TN), lambda i, j: (0, j))],
        out_specs=(pl.BlockSpec((tm, IN_TN), lambda i, j: (i, jnp.minimum(j, N32_TILES - 1))),
                   pl.BlockSpec((tm, IN_TN), lambda i, j: (i, jnp.maximum(j - N32_TILES, 0)))),
        scratch_shapes=[pltpu.VMEM((tm, D_MODEL), bf16)],
        compiler_params=_params(("parallel", "arbitrary")),
        name="inproj",
    )(x2d, g_row, w_packed)


def _lru_gates(xc, wa_ref, ba_ref, wx_ref, bx_ref, lam_ref):
    xb = xc.astype(bf16)
    ra, ri = [], []
    for c in range(BRANCH_W // MXU_W):
        sl = slice(c * MXU_W, (c + 1) * MXU_W)
        ra.append(_dot(xb[:, sl], wa_ref[c]))
        ri.append(_dot(xb[:, sl], wx_ref[c]))
    r = jax.nn.sigmoid(jnp.concatenate(ra, axis=-1) + ba_ref[...])
    i = jax.nn.sigmoid(jnp.concatenate(ri, axis=-1) + bx_ref[...])
    nl = -lam_ref[...]
    softplus = jnp.maximum(nl, 0.0) + jnp.log1p(jnp.exp(-jnp.abs(nl)))
    log_a = -LRU_C * r * softplus
    a = jnp.exp(log_a)
    b = jnp.sqrt(1.0 - a * a) * (i * xc)
    return a, b


def _lru_kernel(x_ref, g_ref, cw_ref, cb_ref, wa_ref, ba_ref, wx_ref, bx_ref, lam_ref,
                zb_ref, tail_ref, h_ref, xs_ref, a_ref, b_ref, hc_ref):
    tt = x_ref.shape[1]

    @pl.when(pl.program_id(1) == 0)
    def _():
        xs_ref[0:8, :] = jnp.zeros((8, BRANCH_W), f32)
        hc_ref[...] = jnp.zeros_like(hc_ref)

    x = x_ref[0]
    xs_ref[8:, :] = x
    xc = cb_ref[...] + x * cw_ref[CONV_W - 1:CONV_W, :]
    for k in range(CONV_W - 1):
        xc = xc + xs_ref[pl.ds(8 - (CONV_W - 1 - k), tt), :] * cw_ref[k:k + 1, :]
    xs_ref[0:8, :] = x[tt - 8:, :]
    tail_ref[0] = x[tt - 8:, :]

    a, b = _lru_gates(xc, wa_ref, ba_ref, wx_ref, bx_ref, lam_ref)
    a_ref[...] = a
    b_ref[...] = b
    row = lax.broadcasted_iota(jnp.int32, (8, BRANCH_W), 0)

    def body(i, h):
        r0 = pl.multiple_of(i * 8, 8)
        av = a_ref[pl.ds(r0, 8), :]
        bv = b_ref[pl.ds(r0, 8), :]
        for d in (1, 2, 4):
            a_s = jnp.where(row >= d, pltpu.roll(av, d, 0), 1.0)
            b_s = jnp.where(row >= d, pltpu.roll(bv, d, 0), 0.0)
            bv = av * b_s + bv
            av = av * a_s
        hs = bv + av * h
        b_ref[pl.ds(r0, 8), :] = hs
        return hs[7:8, :]

    h = lax.fori_loop(0, tt // 8, body, hc_ref[...])
    hc_ref[...] = h
    h_ref[0] = h
    zb_ref[0] = (b_ref[...] * _silu(g_ref[0])).astype(bf16)


def _lru_prompt(z32, z16, cw, cb, wa, ba, wx, bx, lam):
    b, s, _ = z32.shape
    tt = min(s, 512)
    row = lambda: pl.BlockSpec((1, BRANCH_W), lambda i, t: (0, 0))
    bd = lambda: pl.BlockSpec((BRANCH_W // MXU_W, MXU_W, MXU_W), lambda i, t: (0, 0, 0))
    return pl.pallas_call(
        _lru_kernel,
        out_shape=(jax.ShapeDtypeStruct((b, s, BRANCH_W), bf16),
                   jax.ShapeDtypeStruct((b, 8, BRANCH_W), f32),
                   jax.ShapeDtypeStruct((b, 1, BRANCH_W), f32)),
        grid=(b, s // tt),
        in_specs=[pl.BlockSpec((1, tt, BRANCH_W), lambda i, t: (i, t, C_LRU_X // BRANCH_W)),
                  pl.BlockSpec((1, tt, BRANCH_W), lambda i, t: (i, t, C_LRU_G // BRANCH_W)),
                  pl.BlockSpec((CONV_W, BRANCH_W), lambda i, t: (0, 0)),
                  row(), bd(), row(), bd(), row(), row()],
        out_specs=(pl.BlockSpec((1, tt, BRANCH_W), lambda i, t: (i, t, 0)),
                   pl.BlockSpec((1, 8, BRANCH_W), lambda i, t: (i, 0, 0)),
                   pl.BlockSpec((1, 1, BRANCH_W), lambda i, t: (i, 0, 0))),
        scratch_shapes=[pltpu.VMEM((tt + 8, BRANCH_W), f32), pltpu.VMEM((tt, BRANCH_W), f32),
                        pltpu.VMEM((tt, BRANCH_W), f32), pltpu.VMEM((1, BRANCH_W), f32)],
        compiler_params=_params(("parallel", "arbitrary")),
        name="lru_prompt",
    )(z32, z16, cw, cb, wa, ba, wx, bx, lam)


def _pool_kernel(x_ref, g_ref, wp_ref, sc_ref, zb_ref, tail_ref, xs_ref):
    tt = x_ref.shape[1]
    t = pl.program_id(1)

    @pl.when(t == 0)
    def _():
        xs_ref[0:16, :] = jnp.zeros((16, BRANCH_W), f32)

    x = x_ref[0]
    xs_ref[16:, :] = x
    pos1 = t * tt + 1 + lax.broadcasted_iota(jnp.int32, (tt, POOL_GD), 0)
    outs = []
    for gi, w in enumerate(POOL_WINDOWS):
        sl = slice(gi * POOL_GD, (gi + 1) * POOL_GD)
        s = xs_ref[:, sl]
        sh = 1
        while sh < w:
            s = s + pltpu.roll(s, sh, 0)
            sh *= 2
        cnt = jnp.minimum(w, pos1).astype(f32)
        pooled = s[16:, :] / cnt - x[:, sl]
        outs.append(_dot(pooled.astype(bf16), wp_ref[gi]))
    y = jnp.concatenate(outs, axis=-1) * sc_ref[...]
    zb_ref[0] = (y * _silu(g_ref[0])).astype(bf16)
    xs_ref[0:16, :] = x[tt - 16:, :]
    tail_ref[0] = x[tt - 16:, :]


def _pool_prompt(z32, z16, wp, sc):
    b, s, _ = z32.shape
    tt = min(s, 512)
    return pl.pallas_call(
        _pool_kernel,
        out_shape=(jax.ShapeDtypeStruct((b, s, BRANCH_W), bf16),
                   jax.ShapeDtypeStruct((b, 16, BRANCH_W), f32)),
        grid=(b, s // tt),
        in_specs=[pl.BlockSpec((1, tt, BRANCH_W), lambda i, t: (i, t, C_POOL_X // BRANCH_W)),
                  pl.BlockSpec((1, tt, BRANCH_W), lambda i, t: (i, t, C_POOL_G // BRANCH_W)),
                  pl.BlockSpec((len(POOL_WINDOWS), POOL_GD, POOL_GD), lambda i, t: (0, 0, 0)),
                  pl.BlockSpec((1, BRANCH_W), lambda i, t: (0, 0))],
        out_specs=(pl.BlockSpec((1, tt, BRANCH_W), lambda i, t: (i, t, 0)),
                   pl.BlockSpec((1, 16, BRANCH_W), lambda i, t: (i, 0, 0))),
        scratch_shapes=[pltpu.VMEM((tt + 16, BRANCH_W), f32)],
        compiler_params=_params(("parallel", "arbitrary")),
        name="pool_prompt",
    )(z32, z16, wp, sc)


def _mix_sample_kernel(past_len, lx_ref, lg_ref, px_ref, pg_ref, conv_ref, h0_ref, pbuf_ref,
                       cw_ref, cb_ref, wa_ref, ba_ref, wx_ref, bx_ref, lam_ref, wp_ref, sc_ref,
                       zl_ref, zp_ref, h_ref):
    x = lx_ref[...]
    xc = cb_ref[...] + x * cw_ref[CONV_W - 1:CONV_W, :]
    for k in range(CONV_W - 1):
        xc = xc + conv_ref[k] * cw_ref[k:k + 1, :]
    a, b = _lru_gates(xc, wa_ref, ba_ref, wx_ref, bx_ref, lam_ref)
    h = a * h0_ref[...] + b
    h_ref[...] = h
    zl_ref[...] = (h * _silu(lg_ref[...])).astype(bf16)

    px = px_ref[...]
    outs = []
    for gi, w in enumerate(POOL_WINDOWS):
        sl = slice(gi * POOL_GD, (gi + 1) * POOL_GD)
        s = px[:, sl]
        for k in range(1, w):
            s = s + pbuf_ref[POOL_BUF - k][:, sl]
        cnt = float(min(w, past_len + 1))
        pooled = s / cnt - px[:, sl]
        outs.append(_dot(pooled.astype(bf16), wp_ref[gi]))
    y = jnp.concatenate(outs, axis=-1) * sc_ref[...]
    zp_ref[...] = (y * _silu(pg_ref[...])).astype(bf16)


def _mix_sample(z32, z16, conv_t, h0, pbuf_t, cw, cb, wa, ba, wx, bx, lam, wp, sc, past_len):
    n = z32.shape[0]
    col = lambda c: pl.BlockSpec((n, BRANCH_W), lambda i: (0, c // BRANCH_W))
    full = lambda a: pl.BlockSpec(a.shape, lambda i: (0,) * a.ndim)
    args = (conv_t, h0, pbuf_t, cw, cb, wa, ba, wx, bx, lam, wp, sc)
    return pl.pallas_call(
        functools.partial(_mix_sample_kernel, past_len),
        out_shape=(jax.ShapeDtypeStruct((n, BRANCH_W), bf16), jax.ShapeDtypeStruct((n, BRANCH_W), bf16),
                   jax.ShapeDtypeStruct((n, BRANCH_W), f32)),
        grid=(1,),
        in_specs=[col(C_LRU_X), col(C_LRU_G), col(C_POOL_X), col(C_POOL_G)] + [full(a) for a in args],
        out_specs=(pl.BlockSpec((n, BRANCH_W), lambda i: (0, 0)),) * 3,
        compiler_params=_params(("arbitrary",)),
        name="mix_sample",
    )(z32, z16, z32, z16, *args)


def _compress_kernel(tab_ref, k_ref, v_ref, wk_ref, wv_ref, fk_ref, sk_ref, fv_ref, sv_ref):
    del tab_ref
    p = pl.program_id(1)
    chunks = PAGE_SIZE // CMP_STRIDE

    @pl.when(p == 0)
    def _():
        sk_ref[...] = jnp.zeros_like(sk_ref)
        sv_ref[...] = jnp.zeros_like(sv_ref)

    for src, w_ref, f_ref, s_ref in ((k_ref, wk_ref, fk_ref, sk_ref), (v_ref, wv_ref, fv_ref, sv_ref)):
        tile = src[0]
        first = jnp.sum((tile * w_ref[0]).reshape(chunks, CMP_STRIDE, KV_W), axis=1)
        second = jnp.sum((tile * w_ref[1]).reshape(chunks, CMP_STRIDE, KV_W), axis=1)
        for m in range(chunks):
            f_ref[0, m % 4, pl.ds(2 * p + m // 4, 1), :] = first[m:m + 1, :]
            if m >= 1:
                s_ref[0, (m - 1) % 4, pl.ds(2 * p + (m - 1) // 4, 1), :] = second[m:m + 1, :]
            else:
                @pl.when(p > 0)
                def _():
                    s_ref[0, 3, pl.ds(2 * p - 1, 1), :] = second[0:1, :]


def _compress(table, src_k, src_v, col_k, col_v, wk, wv, nb):
    npg = table.shape[0] // nb
    out = jax.ShapeDtypeStruct((nb, 4, N_CMP // 4, KV_W), f32)
    ospec = lambda: pl.BlockSpec((1, 4, N_CMP // 4, KV_W), lambda b, p, tab: (b, 0, 0, 0))
    return pl.pallas_call(
        _compress_kernel,
        out_shape=(out,) * 4,
        grid_spec=pltpu.PrefetchScalarGridSpec(
            num_scalar_prefetch=1,
            grid=(nb, npg),
            in_specs=[pl.BlockSpec((1, PAGE_SIZE, KV_W), lambda b, p, tab: (tab[b * npg + p], 0, col_k)),
                      pl.BlockSpec((1, PAGE_SIZE, KV_W), lambda b, p, tab: (tab[b * npg + p], 0, col_v)),
                      pl.BlockSpec((2, PAGE_SIZE, KV_W), lambda b, p, tab: (0, 0, 0)),
                      pl.BlockSpec((2, PAGE_SIZE, KV_W), lambda b, p, tab: (0, 0, 0))],
            out_specs=(ospec(), ospec(), ospec(), ospec())),
        compiler_params=_params(("parallel", "arbitrary")),
        name="compress",
    )(table, src_k, src_v, wk, wv)


def _chunk_maps():
    n_chunk = PP * PAGE_SIZE // CMP_STRIDE
    rows = n_chunk // 4
    chunk_of = np.arange(PP * PAGE_SIZE) // CMP_STRIDE
    first = np.zeros((n_chunk, PP * PAGE_SIZE), np.float32)
    second = np.zeros((n_chunk + 8, PP * PAGE_SIZE), np.float32)
    for i in range(4):
        for jj in range(rows):
            first[i * rows + jj] = chunk_of == 4 * jj + i
            second[i * rows + jj] = chunk_of == 4 * jj + i + 1
    second[n_chunk] = chunk_of == 0
    return jnp.asarray(first, bf16), jnp.asarray(second, bf16)


def _compress_t_kernel(tab_ref, *refs):
    del tab_ref
    k_refs, v_refs = refs[:PP], refs[PP:2 * PP]
    wk_ref, wv_ref, ea_ref, eb_ref, fk_ref, sk_ref, fv_ref, sv_ref = refs[2 * PP:]
    ps = pl.program_id(1)
    rows = PP * PAGE_SIZE // CMP_STRIDE // 4
    r0 = pl.multiple_of(ps * rows, rows)
    for pages, w_ref, f_ref, s_ref in ((k_refs, wk_ref, fk_ref, sk_ref), (v_refs, wv_ref, fv_ref, sv_ref)):
        a1 = jnp.concatenate([(r[0, 0] * w_ref[0]).astype(bf16) for r in pages], axis=1)
        a2 = jnp.concatenate([(r[0, 0] * w_ref[1]).astype(bf16) for r in pages], axis=1)
        first = _nt(ea_ref[...], a1)
        second = _nt(eb_ref[...], a2)
        for i in range(4):
            f_ref[0, i, pl.ds(r0, rows), :] = first[i * rows:(i + 1) * rows]
            s_ref[0, i, pl.ds(r0, rows), :] = second[i * rows:(i + 1) * rows]

        @pl.when(ps > 0)
        def _():
            s_ref[0, 3, pl.ds(r0 - 1, 1), :] = second[4 * rows:4 * rows + 1]


def _compress_t(table, layer, cache_k, cache_v, wk, wv, nb):
    npg = table.shape[0] // nb
    ea, eb = _chunk_maps()
    out = jax.ShapeDtypeStruct((nb, 4, N_CMP // 4, KV_W), f32)
    ospec = lambda: pl.BlockSpec((1, 4, N_CMP // 4, KV_W), lambda b, p, tab: (b, 0, 0, 0))
    page = lambda i: pl.BlockSpec((1, 1, KV_W, PAGE_SIZE),
                                  lambda b, p, tab: (layer, tab[b * npg + p * PP + i], 0, 0))
    full = lambda a: pl.BlockSpec(a.shape, lambda b, p, tab: (0,) * a.ndim)
    return pl.pallas_call(
        _compress_t_kernel,
        out_shape=(out,) * 4,
        grid_spec=pltpu.PrefetchScalarGridSpec(
            num_scalar_prefetch=1,
            grid=(nb, npg // PP),
            in_specs=[page(i) for i in range(PP)] * 2 + [full(wk), full(wv), full(ea), full(eb)],
            out_specs=(ospec(), ospec(), ospec(), ospec())),
        compiler_params=_params(("parallel", "arbitrary")),
        name="compress_t",
    )(table, *([cache_k] * PP), *([cache_v] * PP), wk, wv, ea, eb)


def _finish_compress(f_ref, s_ref, phi_ref):
    blk = (f_ref[0] + s_ref[0]).reshape(N_CMP, KV_W)
    return _dot(blk.astype(bf16), phi_ref[...]).astype(bf16)


def _cmp_end(shape, axis):
    col = lax.broadcasted_iota(jnp.int32, shape, axis)
    n = ((col & (N_BLK - 1)) << 2) + (col >> 7)
    return n * CMP_STRIDE + (CMP_BLK - 1)


def _pick_blocks(score, n_pick):
    lane = lax.broadcasted_iota(jnp.int32, score.shape, 1).astype(f32)
    bias = jnp.full(score.shape, NEG, f32)
    for _ in range(n_pick):
        m = jnp.max(score, axis=-1, keepdims=True)
        first = jnp.min(jnp.where(score == m, lane, float(N_BLK)), axis=-1, keepdims=True)
        hit = lane == first
        bias = jnp.where(hit, 0.0, bias)
        score = jnp.where(hit, -jnp.inf, score)
    return bias


def _block_expand(first_block, n_keys):
    j = lax.broadcasted_iota(jnp.int32, (N_BLK, n_keys), 0)
    c = lax.broadcasted_iota(jnp.int32, (N_BLK, n_keys), 1)
    return jnp.where(j == first_block + (c >> 6), 1.0, 0.0).astype(bf16)


CQ = Q_PER_KV * TQ
WIN_T = WINDOW // TQ + 1
N_PARTS = 9
TILE_COL = HEAD_DIM + 6
V_ROWS = HEAD_DIM + 16


def _pick_blocks_t(score, forced, n_pick):
    jrow = lax.broadcasted_iota(jnp.int32, score.shape, 0).astype(f32)
    bias = jnp.where(forced, 0.0, NEG)
    score = jnp.where(forced, -jnp.inf, score)
    for _ in range(n_pick):
        m = jnp.max(score, axis=0, keepdims=True)
        first = jnp.min(jnp.where(score == m, jrow, float(N_BLK)), axis=0, keepdims=True)
        hit = jrow == first
        bias = jnp.where(hit, 0.0, bias)
        score = jnp.where(hit, -jnp.inf, score)
    return bias


def _nsa_prompt_kernel(q_ref, srow_ref, slope_ref, bg_ref, ng_ref, fk_ref, sk_ref, fv_ref, sv_ref,
                       phik_ref, phivt_ref, cpos_ref, ksa_ref, vst_ref, kwa_ref, vwt_ref, o_ref,
                       kca_scr, vct_scr, qa_scr, oc_scr, m_scr, acc_scr, negc_scr, negd_scr, negw_scr):
    qi = pl.program_id(1)
    s0 = qi * TQ

    @pl.when(qi == 0)
    def _():
        kc = _finish_compress(fk_ref, sk_ref, phik_ref)
        for g in range(KV_HEADS):
            kca_scr[g] = jnp.concatenate([kc[:, g * HEAD_DIM:(g + 1) * HEAD_DIM], cpos_ref[...]], axis=1)
        blk_v = (fv_ref[0] + sv_ref[0]).reshape(N_CMP, KV_W).astype(bf16)
        vct_scr[...] = _nt(phivt_ref[...], blk_v).astype(bf16)

    t_of = lambda shape: s0 + (lax.broadcasted_iota(jnp.int32, shape, 1) & (TQ - 1))
    q_t = (q_ref[0].astype(f32) * Q_SCALE).T.astype(bf16)

    negc_scr[...] = jnp.where(_cmp_end((N_CMP, CQ), 0) <= t_of((N_CMP, CQ)), 0.0, NEG)
    sees_block = t_of((1, CQ)) >= CMP_BLK - 1
    t_q = t_of((N_BLK, TQ))
    jr = lax.broadcasted_iota(jnp.int32, (N_BLK, TQ), 0)
    jb = t_q >> 6
    ok_b = (jr << 6) <= t_q
    forced = (jr == 0) | (jr == jb) | (jr == jb - 1)
    tile_any = [jnp.full((N_BLK // 8, 1), NEG, f32) for _ in LOOP_GROUPS]
    for g in range(KV_HEADS):
        q_g = jnp.concatenate([q_t[(g * Q_PER_KV + r) * HEAD_DIM:(g * Q_PER_KV + r + 1) * HEAD_DIM]
                               for r in range(Q_PER_KV)], axis=1)
        qa = jnp.concatenate([q_g, srow_ref[g]], axis=0)
        st = _dot(kca_scr[g], qa) + negc_scr[...]
        e = jnp.exp2(st - jnp.max(st, axis=0, keepdims=True))
        pt = e * jnp.where(sees_block, 1.0 / jnp.sum(e, axis=0, keepdims=True), 0.0)
        oc_scr[g] = _dot(vct_scr[g * HEAD_DIM:(g + 1) * HEAD_DIM], pt.astype(bf16))
        ps = None
        for i in range(4):
            for r in range(Q_PER_KV):
                slab = pt[i * N_BLK:(i + 1) * N_BLK, r * TQ:(r + 1) * TQ]
                ps = slab if ps is None else ps + slab
        bias = _pick_blocks_t(jnp.where(ok_b, ps, NEG), forced, N_SEL - 3)
        qa_scr[g] = jnp.concatenate([qa, jnp.concatenate([bias.astype(bf16)] * Q_PER_KV, axis=1)], axis=0)
        any_t = jnp.max(bias.reshape(N_BLK // 8, 8, TQ), axis=1)
        li = [g in grp for grp in LOOP_GROUPS].index(True)
        tile_any[li] = jnp.maximum(tile_any[li], jnp.max(any_t, axis=1, keepdims=True))
    kt_row = lax.broadcasted_iota(jnp.int32, tile_any[0].shape, 0)
    tile_bits = [jnp.sum(jnp.where(ta == 0.0, 1 << kt_row, 0)) for ta in tile_any]

    m_scr[...] = jnp.full(m_scr.shape, NEG, f32)
    acc_scr[...] = jnp.zeros_like(acc_scr)

    def sel_tile(kt, diagonal, groups, n_keys=TK):
        k0 = pl.multiple_of(kt * TK, TK)
        tile_off = (k0 - s0).astype(f32)
        key_r = lax.broadcasted_iota(jnp.int32, (n_keys, N_BLK), 0)
        blk_c = lax.broadcasted_iota(jnp.int32, (n_keys, N_BLK), 1)
        expand = jnp.where(blk_c == kt * (TK // SEL_BLK) + (key_r >> 6), 1.0, 0.0).astype(bf16)
        k_aug = ksa_ref[0, pl.ds(k0, n_keys), :]
        v_t = vst_ref[0, kt][:, 0:n_keys]
        for g in groups:
            lhs = jnp.concatenate([k_aug[:, g * QA_W:(g + 1) * QA_W], expand], axis=1)
            st = _dot(lhs, qa_scr[g])
            if diagonal:
                st = st + negd_scr[qi % (TK // TQ), 0:n_keys]
            c = slope_ref[g] * tile_off
            m_old = m_scr[g]
            m_new = jnp.maximum(m_old, jnp.max(st, axis=0, keepdims=True) + c)
            alpha = jnp.exp2(m_old - m_new)
            pt = jnp.exp2((st - (m_new - c)).astype(bf16))
            acc_scr[g] = alpha * acc_scr[g] + _dot(v_t[g * V_ROWS:(g + 1) * V_ROWS], pt)
            m_scr[g] = m_new

    last = s0 // TK

    @pl.when(qi < TK // TQ)
    def _():
        key_pos = lax.broadcasted_iota(jnp.int32, (TK, CQ), 0)
        negd_scr[qi] = jnp.where(key_pos <= t_of((TK, CQ)), 0.0, NEG)

    for groups, bits in zip(LOOP_GROUPS, tile_bits):
        def sel_body(kt, carry, groups=groups, bits=bits):
            @pl.when(((bits >> kt) & 1) == 1)
            def _():
                sel_tile(kt, False, groups)
            return carry

        lax.fori_loop(0, last, sel_body, 0)
    for sub in range(TK // TQ):
        @pl.when(qi % (TK // TQ) == sub)
        def _(sub=sub):
            sel_tile(last, True, range(KV_HEADS), (sub + 1) * TQ)

    w_tile = jnp.maximum(qi - WINDOW // TQ, 0)
    n_win = WIN_T * TQ

    @pl.when(qi <= WINDOW // TQ)
    def _():
        dist_w = t_of((n_win, CQ)) - (w_tile * TQ + lax.broadcasted_iota(jnp.int32, (n_win, CQ), 0))
        negw_scr[...] = jnp.where((dist_w >= 0) & (dist_w < WINDOW), 0.0, NEG)

    lane_w = lax.broadcasted_iota(jnp.int32, (n_win, QA_W), 1)
    slab_off = (lax.broadcasted_iota(jnp.int32, (n_win, QA_W), 0) // TQ * TQ).astype(f32).astype(bf16)
    in_tile_col = (lane_w >= TILE_COL) & (lane_w < TILE_COL + 3)
    kw_all = kwa_ref[0, pl.ds(pl.multiple_of(w_tile * TQ, TQ), n_win), :]
    vw_t = jnp.concatenate([vwt_ref[0, w_tile + i] for i in range(WIN_T)], axis=1)
    gates = jax.nn.sigmoid(bg_ref[0].T)
    blocks = []
    for g in range(KV_HEADS):
        lhs = jnp.where(in_tile_col, slab_off, kw_all[:, g * QA_W:(g + 1) * QA_W])
        st = _dot(lhs, qa_scr[g, 0:QA_W]) + negw_scr[...]
        pt = jnp.exp2((st - jnp.max(st, axis=0, keepdims=True)).astype(bf16))
        win = _dot(vw_t[g * V_ROWS:(g + 1) * V_ROWS], pt)
        o_w = win[0:HEAD_DIM] * (1.0 / win[HEAD_DIM:HEAD_DIM + 1])
        o_s = acc_scr[g, 0:HEAD_DIM] * (1.0 / acc_scr[g, HEAD_DIM:HEAD_DIM + 1])
        o_c = oc_scr[g]
        heads = []
        for r in range(Q_PER_KV):
            h = g * Q_PER_KV + r
            cols = slice(r * TQ, (r + 1) * TQ)
            heads.append(gates[3 * h:3 * h + 1] * o_c[:, cols] + gates[3 * h + 1:3 * h + 2] * o_s[:, cols]
                         + gates[3 * h + 2:3 * h + 3] * o_w[:, cols])
        for r in range(0, Q_PER_KV, 2):
            blocks.append(jnp.concatenate(heads[r:r + 2], axis=0).T)
    y = jnp.concatenate(blocks, axis=-1)
    o_ref[0] = (y * _silu(ng_ref[0])).astype(bf16)


def _nsa_prompt(z3, z16, fk, sk, fv, sv, phik, phivt):
    b, s, _ = z3.shape
    nq = s // TQ
    parts = _slope_parts().reshape(KV_HEADS, Q_PER_KV, POS_ROWS)
    srow = jnp.asarray(np.repeat(parts.transpose(0, 2, 1), TQ, axis=2), bf16)
    slope = jnp.asarray(np.repeat(np.asarray(SLOPES_LOG2, np.float32).reshape(KV_HEADS, 1, Q_PER_KV), TQ, axis=2))
    ksa = _key_aug(_kv_seg(z3, 2), TK)
    kwa = _key_aug(_kv_seg(z3, 4), TQ)
    vst = _value_tiles(_kv_seg(z3, 3), TK)
    vwt = _value_tiles(_kv_seg(z3, 5), TQ)
    cpos = _cmp_pos_cols()

    once = pl.Buffered(1)
    cmp = lambda: pl.BlockSpec((1, 4, N_CMP // 4, KV_W), lambda i, t: (i, 0, 0, 0))
    full = lambda a: pl.BlockSpec(a.shape, lambda i, t: (0,) * a.ndim)
    return pl.pallas_call(
        _nsa_prompt_kernel,
        out_shape=jax.ShapeDtypeStruct((b, s, BRANCH_W), bf16),
        grid=(b, nq),
        in_specs=[pl.BlockSpec((1, TQ, BRANCH_W), lambda i, t: (i, t, C_Q // BRANCH_W)),
                  full(srow), full(slope),
                  pl.BlockSpec((1, TQ, BG_W), lambda i, t: (i, t, C_BG // BG_W)),
                  pl.BlockSpec((1, TQ, BRANCH_W), lambda i, t: (i, t, C_NSA_G // BRANCH_W)),
                  cmp(), cmp(), cmp(), cmp(), full(phik), full(phivt), full(cpos),
                  pl.BlockSpec((1, s, KV_HEADS * QA_W), lambda i, t: (i, 0, 0), pipeline_mode=once),
                  pl.BlockSpec((1, s // TK, KV_HEADS * V_ROWS, TK), lambda i, t: (i, 0, 0, 0), pipeline_mode=once),
                  pl.BlockSpec((1, s, KV_HEADS * QA_W), lambda i, t: (i, 0, 0), pipeline_mode=once),
                  pl.BlockSpec((1, nq, KV_HEADS * V_ROWS, TQ), lambda i, t: (i, 0, 0, 0), pipeline_mode=once)],
        out_specs=pl.BlockSpec((1, TQ, BRANCH_W), lambda i, t: (i, t, 0)),
        scratch_shapes=[pltpu.VMEM((KV_HEADS, N_CMP, QA_W), bf16), pltpu.VMEM((KV_W, N_CMP), bf16),
                        pltpu.VMEM((KV_HEADS, QA_W + N_BLK, CQ), bf16),
                        pltpu.VMEM((KV_HEADS, HEAD_DIM, CQ), f32),
                        pltpu.VMEM((KV_HEADS, 1, CQ), f32),
                        pltpu.VMEM((KV_HEADS, V_ROWS, CQ), f32),
                        pltpu.VMEM((N_CMP, CQ), f32), pltpu.VMEM((TK // TQ, TK, CQ), f32),
                        pltpu.VMEM((WIN_T * TQ, CQ), f32)],
        compiler_params=_params(("parallel", "arbitrary")),
        name="nsa_prompt",
    )(z16, srow, slope, z3, z16, fk, sk, fv, sv, phik, phivt, cpos, ksa, vst, kwa, vwt)


def _by_group(fn):
    hg = lax.broadcasted_iota(jnp.int32, (N_HEADS, 1), 0) >> 2
    out = fn(0)
    for g in range(1, KV_HEADS):
        out = jnp.where(hg == g, fn(g), out)
    return out


def _nsa_sample_kernel(past_len, tab_ref, q_ref, bg_ref, ng_ref, slope_ref,
                       fk_ref, sk_ref, fv_ref, sv_ref, phik_ref, phiv_ref,
                       ksn_ref, vsn_ref, kwn_ref, vwn_ref, bk_ref, bv_ref, *refs):
    del tab_ref
    kp_refs, vp_refs = refs[:PP_SEL], refs[PP_SEL:2 * PP_SEL]
    o_ref, q_scr, sel_scr, oc_scr, ow_scr, m_scr, l_scr, acc_scr = refs[2 * PP_SEL:]
    p = pl.program_id(1)
    slope = slope_ref[:, 0:1]
    gsl = lambda g: slice(g * HEAD_DIM, (g + 1) * HEAD_DIM)

    def per_head(row_ref):
        row = row_ref[0]
        return _by_group(lambda g: jnp.broadcast_to(row[:, gsl(g)], (N_HEADS, HEAD_DIM)))

    @pl.when(p == 0)
    def _():
        qrow = q_ref[0]
        q16 = jnp.concatenate([qrow[:, h * HEAD_DIM:(h + 1) * HEAD_DIM] for h in range(N_HEADS)], axis=0)
        q_scr[...] = q16
        q16f = q16.astype(f32)
        kc = _finish_compress(fk_ref, sk_ref, phik_ref)
        vc = _finish_compress(fv_ref, sv_ref, phiv_ref)

        dist_c = past_len - _cmp_end((1, N_CMP), 1)
        ok_c = dist_c >= 0
        s = _by_group(lambda g: _nt(q16, kc[:, gsl(g)])) - slope * dist_c.astype(f32)
        s = jnp.where(ok_c, s, NEG)
        e = jnp.exp(s - jnp.max(s, axis=-1, keepdims=True))
        pc = jnp.where(ok_c, e * (1.0 / jnp.sum(e, axis=-1, keepdims=True)), 0.0)
        pcb = pc.astype(bf16)
        oc_scr[...] = _by_group(lambda g: _dot(pcb, vc[:, gsl(g)]))
        ps16 = (pc[:, 0:N_BLK] + pc[:, N_BLK:2 * N_BLK]) + (pc[:, 2 * N_BLK:3 * N_BLK] + pc[:, 3 * N_BLK:])
        ps = jnp.concatenate([jnp.sum(ps16[g * Q_PER_KV:(g + 1) * Q_PER_KV], axis=0, keepdims=True)
                              for g in range(KV_HEADS)] + [jnp.zeros((8 - KV_HEADS, N_BLK), f32)], axis=0)
        jl = lax.broadcasted_iota(jnp.int32, (8, N_BLK), 1)
        jb = past_len // SEL_BLK
        forced = jnp.where((jl == 0) | (jl == jb) | (jl == jb - 1), FORCE, 0.0)
        bias8 = _pick_blocks(ps + forced, N_SEL - 1)
        sel_scr[...] = jnp.concatenate(
            [jnp.broadcast_to(bias8[g:g + 1], (Q_PER_KV, N_BLK)) for g in range(KV_HEADS)], axis=0).astype(bf16)

        wb = bk_ref.shape[-1]
        dist_w = wb - lax.broadcasted_iota(jnp.int32, (1, wb), 1)
        ok_w = (dist_w >= 0) & (dist_w < WINDOW)
        bk = bk_ref[0, 0].astype(bf16)
        bv = bv_ref[0, 0].astype(bf16)
        s_buf = _by_group(lambda g: _dot(q16, bk[gsl(g)])) - slope * dist_w.astype(f32)
        s_buf = jnp.where(ok_w, s_buf, NEG)
        s_new = jnp.sum(q16f * per_head(kwn_ref), axis=-1, keepdims=True)
        m_w = jnp.maximum(jnp.max(s_buf, axis=-1, keepdims=True), s_new)
        e_buf = jnp.exp(s_buf - m_w)
        e_new = jnp.exp(s_new - m_w)
        ebb = e_buf.astype(bf16)
        num = _by_group(lambda g: _nt(ebb, bv[gsl(g)])) + e_new * per_head(vwn_ref)
        ow_scr[...] = num * (1.0 / (jnp.sum(e_buf, axis=-1, keepdims=True) + e_new))

        m_scr[...] = jnp.sum(q16f * per_head(ksn_ref), axis=-1, keepdims=True)
        l_scr[...] = jnp.ones_like(l_scr)
        acc_scr[...] = per_head(vsn_ref)

    q16 = q_scr[...]
    n_keys = PP_SEL * PAGE_SIZE
    kp = jnp.concatenate([r[0, 0].astype(bf16) for r in kp_refs], axis=1)
    vp = jnp.concatenate([r[0, 0].astype(bf16) for r in vp_refs], axis=1)
    dist = past_len - (p * n_keys + lax.broadcasted_iota(jnp.int32, (1, n_keys), 1))
    expand = _block_expand(p * (n_keys // SEL_BLK), n_keys)
    s = _by_group(lambda g: _dot(q16, kp[gsl(g)])) - slope * dist.astype(f32) + _dot(sel_scr[...], expand)
    m_old = m_scr[...]
    m_new = jnp.maximum(m_old, jnp.max(s, axis=-1, keepdims=True))
    alpha = jnp.exp(m_old - m_new)
    pr = jnp.exp(s - m_new)
    prb = pr.astype(bf16)
    l_scr[...] = alpha * l_scr[...] + jnp.sum(pr, axis=-1, keepdims=True)
    acc_scr[...] = alpha * acc_scr[...] + _by_group(lambda g: _nt(prb, vp[gsl(g)]))
    m_scr[...] = m_new

    @pl.when(p == pl.num_programs(1) - 1)
    def _():
        o_s = acc_scr[...] * (1.0 / l_scr[...])
        gates = jax.nn.sigmoid(bg_ref[0])
        lane = lax.broadcasted_iota(jnp.int32, (N_HEADS, BG_W), 1)
        h3 = 3 * lax.broadcasted_iota(jnp.int32, (N_HEADS, BG_W), 0)
        gate = lambda n: jnp.sum(jnp.where(lane == h3 + n, gates, 0.0), axis=-1, keepdims=True)
        y16 = gate(0) * oc_scr[...] + gate(1) * o_s + gate(2) * ow_scr[...]
        y = jnp.concatenate([y16[h:h + 1, :] for h in range(N_HEADS)], axis=-1)
        o_ref[0] = (y * _silu(ng_ref[0])).astype(bf16)


def _nsa_sample(table, layer, qb, z2, z2h, slopes, fk, sk, fv, sv, phik, phiv, pool_k, pool_v, buf_k, buf_v,
                past_len):
    nb = qb.shape[0]
    npg = table.shape[0] // nb
    z3 = z2.reshape(nb, 1, Z32_W)
    z3h = z2h.reshape(nb, 1, Z16_W)
    tok = lambda w, c: pl.BlockSpec((1, 1, w), lambda b, p, tab: (b, 0, c))
    cmp = lambda: pl.BlockSpec((1, 4, N_CMP // 4, KV_W), lambda b, p, tab: (b, 0, 0, 0))
    phi = lambda: pl.BlockSpec((KV_W, KV_W), lambda b, p, tab: (0, 0))
    wb = buf_k.shape[-1]
    win = lambda: pl.BlockSpec((1, 1, KV_W, wb), lambda b, p, tab: (layer, b, 0, 0))
    page = lambda i: pl.BlockSpec((1, 1, KV_W, PAGE_SIZE),
                                  lambda b, p, tab: (layer, tab[b * npg + p * PP_SEL + i], 0, 0))
    kvc = C_KV // KV_W
    return pl.pallas_call(
        functools.partial(_nsa_sample_kernel, past_len),
        out_shape=jax.ShapeDtypeStruct((nb, 1, BRANCH_W), bf16),
        grid_spec=pltpu.PrefetchScalarGridSpec(
            num_scalar_prefetch=1,
            grid=(nb, npg // PP_SEL),
            in_specs=[tok(BRANCH_W, 0), tok(BG_W, C_BG // BG_W), tok(BRANCH_W, C_NSA_G // BRANCH_W),
                      pl.BlockSpec((N_HEADS, 128), lambda b, p, tab: (0, 0)),
                      cmp(), cmp(), cmp(), cmp(), phi(), phi(),
                      tok(KV_W, kvc + 2), tok(KV_W, kvc + 3), tok(KV_W, kvc + 4), tok(KV_W, kvc + 5),
                      win(), win()] + [page(i) for i in range(PP_SEL)] * 2,
            out_specs=pl.BlockSpec((1, 1, BRANCH_W), lambda b, p, tab: (b, 0, 0)),
            scratch_shapes=[pltpu.VMEM((N_HEADS, HEAD_DIM), bf16), pltpu.VMEM((N_HEADS, N_BLK), bf16),
                            pltpu.VMEM((N_HEADS, HEAD_DIM), f32), pltpu.VMEM((N_HEADS, HEAD_DIM), f32),
                            pltpu.VMEM((N_HEADS, 1), f32), pltpu.VMEM((N_HEADS, 1), f32),
                            pltpu.VMEM((N_HEADS, HEAD_DIM), f32)]),
        compiler_params=_params(("parallel", "arbitrary")),
        name="nsa_sample",
    )(table, qb.reshape(nb, 1, BRANCH_W), z3, z3h, slopes, fk, sk, fv, sv, phik, phiv,
      z3, z3, z3, z3, buf_k, buf_v, *([pool_k] * PP_SEL), *([pool_v] * PP_SEL))


def _merge_kernel(zl_ref, zp_ref, zn_ref, m0_ref, m1_ref, m2_ref, wb_ref, wo_ref, g_ref, x_ref, y_ref):
    acc = None
    for n, (zz, mg) in enumerate(((zl_ref, m0_ref), (zp_ref, m1_ref), (zn_ref, m2_ref))):
        term = jax.nn.sigmoid(mg[...].astype(f32)) * _dot(zz[...], wb_ref[n])
        acc = term if acc is None else acc + term
    out = _dot(acc.astype(bf16), wo_ref[...])
    ms = jnp.mean(out * out, axis=-1, keepdims=True)
    y_ref[...] = x_ref[...] + out * lax.rsqrt(ms + EPS) * g_ref[...]


def _merge(zl, zp, zn, z16, wb, wo, g_row, x2d):
    n = x2d.shape[0]
    tm = min(n, 256)
    rowblk = lambda c: pl.BlockSpec((tm, D_MODEL), lambda i: (i, c))
    return pl.pallas_call(
        _merge_kernel,
        out_shape=jax.ShapeDtypeStruct((n, D_MODEL), f32),
        grid=(n // tm,),
        in_specs=[rowblk(0), rowblk(0), rowblk(0),
                  rowblk(C_MG // D_MODEL), rowblk(C_MG // D_MODEL + 1), rowblk(C_MG // D_MODEL + 2),
                  pl.BlockSpec((N_BRANCH, BRANCH_W, D_MODEL), lambda i: (0, 0, 0)),
                  pl.BlockSpec((D_MODEL, D_MODEL), lambda i: (0, 0)),
                  pl.BlockSpec((1, D_MODEL), lambda i: (0, 0)),
                  rowblk(0)],
        out_specs=rowblk(0),
        compiler_params=_params(("parallel",)),
        name="merge",
    )(zl, zp, zn, z16, z16, z16, wb, wo, g_row, x2d)


def _block_diag(w, per):
    n, d, _ = w.shape
    eye = jnp.eye(per, dtype=w.dtype)
    t = jnp.einsum('cpde,pq->cpdqe', w.reshape(n // per, per, d, d), eye)
    return t.reshape(n // per, per * d, per * d)


def _pack_w_in(w):
    seg = lambda i: w[:, i * BRANCH_W:(i + 1) * BRANCH_W]
    old_kv = 6 * BRANCH_W
    old_bg = old_kv + 6 * KV_W
    old_mg = old_bg + N_BRANCH * N_HEADS
    f32_part = jnp.concatenate([seg(0), seg(2), w[:, old_kv:old_bg], w[:, old_bg:old_mg]], axis=1)
    f32_part = jnp.pad(f32_part, ((0, 0), (0, Z32_W - f32_part.shape[1])))
    bf16_part = jnp.concatenate([seg(1), seg(3), seg(4), seg(5), w[:, old_mg:]], axis=1)
    return jnp.concatenate([f32_part, bf16_part], axis=1).astype(bf16)


def _tile_wpos(w_pos):
    halves = w_pos.reshape(2, CMP_STRIDE, HEAD_DIM)
    return jnp.tile(halves, (1, PAGE_SIZE // CMP_STRIDE, KV_HEADS))


def _tile_wpos_t(w_pos):
    halves = w_pos.reshape(2, CMP_STRIDE, HEAD_DIM).swapaxes(1, 2)
    return jnp.tile(halves, (1, KV_HEADS, PAGE_SIZE // CMP_STRIDE))


def _lanes_last(cache):
    d, n, rows = cache.shape[:3]
    return jnp.transpose(cache, (0, 1, 3, 4, 2)).reshape(d, n, KV_W, rows)


def _slope_parts():
    cols = np.zeros((N_HEADS, POS_ROWS), np.float32)
    rnd = lambda v: np.float32(np.float32(v).astype(bf16))
    for h, s in enumerate(SLOPES_LOG2):
        s1 = rnd(s)
        s2 = rnd(np.float32(s) - s1)
        s3 = rnd(np.float32(s) - s1 - s2)
        cols[h, 0:N_PARTS] = [s1, s2, s3] * 3
    return cols


def _split_pos(pos, shift):
    cols = np.zeros((pos.shape[0], POS_ROWS), np.float32)
    cols[:, 0:3] = ((pos >> shift) << shift)[:, None]
    cols[:, 3:6] = (pos & ((1 << shift) - 1))[:, None]
    return cols


def _key_aug(k_rows, tile):
    b, s, _ = k_rows.shape
    pos = jnp.asarray(_split_pos(np.arange(s) % tile, 4), bf16)
    kg = k_rows.astype(bf16).reshape(b, s, KV_HEADS, HEAD_DIM)
    posb = jnp.broadcast_to(pos[None, :, None, :], (b, s, KV_HEADS, POS_ROWS))
    return jnp.concatenate([kg, posb], axis=-1).reshape(b, s, KV_HEADS * QA_W)


def _value_tiles(v_rows, tile):
    b, s, _ = v_rows.shape
    vt = v_rows.astype(bf16).reshape(b, s // tile, tile, KV_HEADS, HEAD_DIM).transpose(0, 1, 3, 4, 2)
    ones = jnp.ones((b, s // tile, KV_HEADS, V_ROWS - HEAD_DIM, tile), bf16)
    return jnp.concatenate([vt, ones], axis=3).reshape(b, s // tile, KV_HEADS * V_ROWS, tile)


def _cmp_pos_cols():
    slot = np.arange(N_CMP)
    n = ((slot & (N_BLK - 1)) << 2) + (slot >> 7)
    return jnp.asarray(_split_pos(n * CMP_STRIDE + CMP_BLK - 1, 8), bf16)


def _kv_seg(z, i):
    return z[..., C_KV + i * KV_W:C_KV + (i + 1) * KV_W]


def kernel(x_prompt, x_sample, cache_cmp_k, cache_cmp_v, cache_sel_k, cache_sel_v, cache_win_k, cache_win_v, state_conv, state_lru, state_pool, page_table, g_pre, g_post, w_in, conv_w, conv_b, w_rg_a, b_rg_a, w_rg_x, b_rg_x, lru_lambda, w_pool, pool_scale, cmp_pos_k, cmp_phi_k, cmp_pos_v, cmp_phi_v, w_branch, w_out):
    depth = w_in.shape[0]
    bp, seq, _ = x_prompt.shape
    bs = x_sample.shape[0]
    n_pages = page_table.shape[1]
    past_len = n_pages * PAGE_SIZE
    n_phys = cache_cmp_k.shape[1]
    assert seq == N_BLK * SEL_BLK and past_len == N_BLK * SEL_BLK and x_sample.shape[1] == 1
    wb = cache_win_k.shape[2]

    table_s = page_table.reshape(-1).astype(jnp.int32)
    table_p = jnp.arange(bp * (seq // PAGE_SIZE), dtype=jnp.int32)
    slopes = jnp.broadcast_to(jnp.asarray(SLOPES, f32)[:, None], (N_HEADS, 128))
    row = lambda v: v.reshape(1, -1)
    cmp_kt, cmp_vt = _lanes_last(cache_cmp_k), _lanes_last(cache_cmp_v)
    sel_kt, sel_vt = _lanes_last(cache_sel_k), _lanes_last(cache_sel_v)
    win_kt, win_vt = _lanes_last(cache_win_k), _lanes_last(cache_win_v)

    xp = x_prompt.reshape(bp * seq, D_MODEL)
    xs = x_sample.reshape(bs, D_MODEL)
    pr = [[] for _ in range(9)]
    sm = [[] for _ in range(9)]
    for l in range(depth):
        w_packed = _pack_w_in(w_in[l])
        wa = _block_diag(w_rg_a[l], MXU_W // LRU_BD).astype(bf16)
        wx = _block_diag(w_rg_x[l], MXU_W // LRU_BD).astype(bf16)
        wp = w_pool[l].astype(bf16)
        phik = _block_diag(jnp.broadcast_to(cmp_phi_k[l], (KV_HEADS, HEAD_DIM, HEAD_DIM)), KV_HEADS)[0].astype(bf16)
        phiv = _block_diag(jnp.broadcast_to(cmp_phi_v[l], (KV_HEADS, HEAD_DIM, HEAD_DIM)), KV_HEADS)[0].astype(bf16)
        wpos_k = _tile_wpos(cmp_pos_k[l])
        wpos_v = _tile_wpos(cmp_pos_v[l])
        wbr = w_branch[l].astype(bf16)
        wo = w_out[l].astype(bf16)
        lru_w = (conv_w[l], row(conv_b[l]), wa, row(b_rg_a[l]), wx, row(b_rg_x[l]), row(lru_lambda[l]))

        z, z16 = _inproj(xp, row(g_pre[l]), w_packed)
        z3 = z.reshape(bp, seq, Z32_W)
        z16_3 = z16.reshape(bp, seq, Z16_W)
        zl, conv_tail, h_p = _lru_prompt(z3, z16_3, *lru_w)
        zpool, pool_tail = _pool_prompt(z3, z16_3, wp, row(pool_scale[l]))
        zr = z.reshape(bp * seq // PAGE_SIZE, PAGE_SIZE, Z32_W)
        fk, sk, fv, sv = _compress(table_p, zr, zr, C_KV // KV_W, C_KV // KV_W + 1, wpos_k, wpos_v, bp)
        zn = _nsa_prompt(z3, z16_3, fk, sk, fv, sv, phik, phiv.T)
        xp = _merge(zl.reshape(bp * seq, BRANCH_W), zpool.reshape(bp * seq, BRANCH_W),
                    zn.reshape(bp * seq, BRANCH_W), z16, wbr, wo, row(g_post[l]), xp)
        kv_rows = [_kv_seg(z3, i).reshape(bp, seq, KV_HEADS, HEAD_DIM) for i in range(6)]
        wlen = min(WINDOW, seq)
        st_p = kv_rows[:4] + [kv_rows[4][:, -wlen:], kv_rows[5][:, -wlen:],
                              conv_tail[:, -(CONV_W - 1):], h_p[:, 0], pool_tail[:, -POOL_BUF:]]

        zs, zs16 = _inproj(xs, row(g_pre[l]), w_packed)
        zls, zps, h_s = _mix_sample(zs, zs16, state_conv[l].swapaxes(0, 1), state_lru[l],
                                    state_pool[l].swapaxes(0, 1), *lru_w, wp, row(pool_scale[l]), past_len)
        fk, sk, fv, sv = _compress_t(table_s, l, cmp_kt, cmp_vt, _tile_wpos_t(cmp_pos_k[l]),
                                     _tile_wpos_t(cmp_pos_v[l]), bs)
        qs = (zs16[:, C_Q:C_Q + BRANCH_W].astype(f32) * (HEAD_DIM ** -0.5)).astype(bf16)
        zns = _nsa_sample(table_s, l, qs, zs, zs16, slopes, fk, sk, fv, sv, phik, phiv,
                          sel_kt, sel_vt, win_kt, win_vt, past_len)
        xs = _merge(zls, zps, zns.reshape(bs, BRANCH_W), zs16, wbr, wo, row(g_post[l]), xs)
        kv_new = [_kv_seg(zs, i).reshape(bs, 1, KV_HEADS, HEAD_DIM) for i in range(6)]
        st_s = kv_new[:4] + [jnp.concatenate([cache_win_k[l], kv_new[4]], axis=1)[:, -wb:],
                             jnp.concatenate([cache_win_v[l], kv_new[5]], axis=1)[:, -wb:],
                             jnp.concatenate([state_conv[l], zs[:, None, C_LRU_X:C_LRU_X + BRANCH_W]], axis=1)[:, -(CONV_W - 1):],
                             h_s,
                             jnp.concatenate([state_pool[l], zs[:, None, C_POOL_X:C_POOL_X + BRANCH_W]], axis=1)[:, -POOL_BUF:]]
        for i in range(9):
            pr[i].append(st_p[i])
            sm[i].append(st_s[i])

    out = [xp.reshape(bp, seq, D_MODEL), xs.reshape(bs, 1, D_MODEL)]
    for i in range(9):
        out += [jnp.stack(pr[i]), jnp.stack(sm[i])]
    return tuple(out)
```

```python
import functools

import numpy as np
import jax
import jax.numpy as jnp
from jax import lax
from jax.experimental import pallas as pl
from jax.experimental.pallas import tpu as pltpu

f32 = jnp.float32
bf16 = jnp.bfloat16

D_MODEL = 1024
BRANCH_W = 1024
N_BRANCH = 3
LRU_BLOCKS = 16
LRU_BD = BRANCH_W // LRU_BLOCKS
CONV_W = 4
LRU_C = 8.0
POOL_WINDOWS = (2, 4, 8, 16)
POOL_GD = BRANCH_W // len(POOL_WINDOWS)
POOL_BUF = max(POOL_WINDOWS) - 1
N_HEADS = 16
HEAD_DIM = 64
KV_HEADS = 4
Q_PER_KV = N_HEADS // KV_HEADS
KV_W = KV_HEADS * HEAD_DIM
CMP_STRIDE = 16
CMP_BLK = 2 * CMP_STRIDE
SEL_BLK = 64
N_SEL = 16
WINDOW = 512
PAGE_SIZE = 128
FORCE = 1e4
NEG = -1e30
EPS = 1e-6

C_LRU_X, C_POOL_X, C_KV = 0, 1024, 2048
C_BG = C_KV + 6 * KV_W
BG_W = 128
Z32_W = 4096
C_LRU_G, C_POOL_G, C_Q, C_NSA_G, C_MG = 0, 1024, 2048, 3072, 4096
Z16_W = C_MG + N_BRANCH * D_MODEL
IN_TN = 1024
N32_TILES = Z32_W // IN_TN

MXU_W = 256
VMEM_LIMIT = 56 * 1024 * 1024

N_CMP = 512
N_BLK = 128
TQ = 256
TK = 512
PP = 8
PP_SEL = 16
POS_ROWS = 64
QA_W = HEAD_DIM + POS_ROWS

SLOPES = [float(np.float32(2.0 ** (-8.0 * (h + 1) / N_HEADS))) for h in range(N_HEADS)]
LOG2E = float(np.log2(np.e))
SLOPES_LOG2 = [float(np.float32(s * LOG2E)) for s in SLOPES]
Q_SCALE = float(np.float32(HEAD_DIM ** -0.5 * LOG2E))
LOOP_GROUPS = ((0,), (1,), (2,), (3,))


def _nt(a, b):
    return lax.dot_general(a, b, (((1,), (1,)), ((), ())), preferred_element_type=f32)


def _dot(a, b):
    return jnp.dot(a, b, preferred_element_type=f32)


def _silu(x):
    x = x.astype(f32)
    return x * jax.nn.sigmoid(x)


def _params(sem):
    return pltpu.CompilerParams(dimension_semantics=sem, vmem_limit_bytes=VMEM_LIMIT)


def _inproj_kernel(x_ref, g_ref, w_ref, o32_ref, o16_ref, u_ref):
    j = pl.program_id(1)

    @pl.when(j == 0)
    def _():
        x = x_ref[...]
        ms = jnp.mean(x * x, axis=-1, keepdims=True)
        u_ref[...] = (x * lax.rsqrt(ms + EPS) * g_ref[...]).astype(bf16)

    @pl.when(j < N32_TILES)
    def _():
        o32_ref[...] = _dot(u_ref[...], w_ref[...])

    @pl.when(j >= N32_TILES)
    def _():
        o16_ref[...] = _dot(u_ref[...], w_ref[...]).astype(bf16)


def _inproj(x2d, g_row, w_packed):
    n = x2d.shape[0]
    tm = min(n, 1024)
    return pl.pallas_call(
        _inproj_kernel,
        out_shape=(jax.ShapeDtypeStruct((n, Z32_W), f32), jax.ShapeDtypeStruct((n, Z16_W), bf16)),
        grid=(n // tm, (Z32_W + Z16_W) // IN_TN),
        in_specs=[pl.BlockSpec((tm, D_MODEL), lambda i, j: (i, 0)),
                  pl.BlockSpec((1, D_MODEL), lambda i, j: (0, 0)),
                  pl.BlockSpec((D_MODEL, IN_TN), lambda i, j: (0, j))],
        out_specs=(pl.BlockSpec((tm, IN_TN), lambda i, j: (i, jnp.minimum(j, N32_TILES - 1))),
                   pl.BlockSpec((tm, IN_TN), lambda i, j: (i, jnp.maximum(j - N32_TILES, 0)))),
        scratch_shapes=[pltpu.VMEM((tm, D_MODEL), bf16)],
        compiler_params=_params(("parallel", "arbitrary")),
        name="inproj",
    )(x2d, g_row, w_packed)


def _lru_gates(xc, wa_ref, ba_ref, wx_ref, bx_ref, lam_ref):
    xb = xc.astype(bf16)
    ra, ri = [], []
    for c in range(BRANCH_W // MXU_W):
        sl = slice(c * MXU_W, (c + 1) * MXU_W)
        ra.append(_dot(xb[:, sl], wa_ref[c]))
        ri.append(_dot(xb[:, sl], wx_ref[c]))
    r = jax.nn.sigmoid(jnp.concatenate(ra, axis=-1) + ba_ref[...])
    i = jax.nn.sigmoid(jnp.concatenate(ri, axis=-1) + bx_ref[...])
    nl = -lam_ref[...]
    softplus = jnp.maximum(nl, 0.0) + jnp.log1p(jnp.exp(-jnp.abs(nl)))
    log_a = -LRU_C * r * softplus
    a = jnp.exp(log_a)
    b = jnp.sqrt(1.0 - a * a) * (i * xc)
    return a, b


def _lru_kernel(x_ref, g_ref, cw_ref, cb_ref, wa_ref, ba_ref, wx_ref, bx_ref, lam_ref,
                zb_ref, tail_ref, h_ref, xs_ref, a_ref, b_ref, hc_ref):
    tt = x_ref.shape[1]

    @pl.when(pl.program_id(1) == 0)
    def _():
        xs_ref[0:8, :] = jnp.zeros((8, BRANCH_W), f32)
        hc_ref[...] = jnp.zeros_like(hc_ref)

    x = x_ref[0]
    xs_ref[8:, :] = x
    xc = cb_ref[...] + x * cw_ref[CONV_W - 1:CONV_W, :]
    for k in range(CONV_W - 1):
        xc = xc + xs_ref[pl.ds(8 - (CONV_W - 1 - k), tt), :] * cw_ref[k:k + 1, :]
    xs_ref[0:8, :] = x[tt - 8:, :]
    tail_ref[0] = x[tt - 8:, :]

    a, b = _lru_gates(xc, wa_ref, ba_ref, wx_ref, bx_ref, lam_ref)
    a_ref[...] = a
    b_ref[...] = b
    row = lax.broadcasted_iota(jnp.int32, (8, BRANCH_W), 0)

    def body(i, h):
        r0 = pl.multiple_of(i * 8, 8)
        av = a_ref[pl.ds(r0, 8), :]
        bv = b_ref[pl.ds(r0, 8), :]
        for d in (1, 2, 4):
            a_s = jnp.where(row >= d, pltpu.roll(av, d, 0), 1.0)
            b_s = jnp.where(row >= d, pltpu.roll(bv, d, 0), 0.0)
            bv = av * b_s + bv
            av = av * a_s
        hs = bv + av * h
        b_ref[pl.ds(r0, 8), :] = hs
        return hs[7:8, :]

    h = lax.fori_loop(0, tt // 8, body, hc_ref[...])
    hc_ref[...] = h
    h_ref[0] = h
    zb_ref[0] = (b_ref[...] * _silu(g_ref[0])).astype(bf16)


def _lru_prompt(z32, z16, cw, cb, wa, ba, wx, bx, lam):
    b, s, _ = z32.shape
    tt = min(s, 512)
    row = lambda: pl.BlockSpec((1, BRANCH_W), lambda i, t: (0, 0))
    bd = lambda: pl.BlockSpec((BRANCH_W // MXU_W, MXU_W, MXU_W), lambda i, t: (0, 0, 0))
    return pl.pallas_call(
        _lru_kernel,
        out_shape=(jax.ShapeDtypeStruct((b, s, BRANCH_W), bf16),
                   jax.ShapeDtypeStruct((b, 8, BRANCH_W), f32),
                   jax.ShapeDtypeStruct((b, 1, BRANCH_W), f32)),
        grid=(b, s // tt),
        in_specs=[pl.BlockSpec((1, tt, BRANCH_W), lambda i, t: (i, t, C_LRU_X // BRANCH_W)),
                  pl.BlockSpec((1, tt, BRANCH_W), lambda i, t: (i, t, C_LRU_G // BRANCH_W)),
                  pl.BlockSpec((CONV_W, BRANCH_W), lambda i, t: (0, 0)),
                  row(), bd(), row(), bd(), row(), row()],
        out_specs=(pl.BlockSpec((1, tt, BRANCH_W), lambda i, t: (i, t, 0)),
                   pl.BlockSpec((1, 8, BRANCH_W), lambda i, t: (i, 0, 0)),
                   pl.BlockSpec((1, 1, BRANCH_W), lambda i, t: (i, 0, 0))),
        scratch_shapes=[pltpu.VMEM((tt + 8, BRANCH_W), f32), pltpu.VMEM((tt, BRANCH_W), f32),
                        pltpu.VMEM((tt, BRANCH_W), f32), pltpu.VMEM((1, BRANCH_W), f32)],
        compiler_params=_params(("parallel", "arbitrary")),
        name="lru_prompt",
    )(z32, z16, cw, cb, wa, ba, wx, bx, lam)


def _pool_kernel(x_ref, g_ref, wp_ref, sc_ref, zb_ref, tail_ref, xs_ref):
    tt = x_ref.shape[1]
    t = pl.program_id(1)

    @pl.when(t == 0)
    def _():
        xs_ref[0:16, :] = jnp.zeros((16, BRANCH_W), f32)

    x = x_ref[0]
    xs_ref[16:, :] = x
    pos1 = t * tt + 1 + lax.broadcasted_iota(jnp.int32, (tt, POOL_GD), 0)
    outs = []
    for gi, w in enumerate(POOL_WINDOWS):
        sl = slice(gi * POOL_GD, (gi + 1) * POOL_GD)
        s = xs_ref[:, sl]
        sh = 1
        while sh < w:
            s = s + pltpu.roll(s, sh, 0)
            sh *= 2
        cnt = jnp.minimum(w, pos1).astype(f32)
        pooled = s[16:, :] / cnt - x[:, sl]
        outs.append(_dot(pooled.astype(bf16), wp_ref[gi]))
    y = jnp.concatenate(outs, axis=-1) * sc_ref[...]
    zb_ref[0] = (y * _silu(g_ref[0])).astype(bf16)
    xs_ref[0:16, :] = x[tt - 16:, :]
    tail_ref[0] = x[tt - 16:, :]


def _pool_prompt(z32, z16, wp, sc):
    b, s, _ = z32.shape
    tt = min(s, 512)
    return pl.pallas_call(
        _pool_kernel,
        out_shape=(jax.ShapeDtypeStruct((b, s, BRANCH_W), bf16),
                   jax.ShapeDtypeStruct((b, 16, BRANCH_W), f32)),
        grid=(b, s // tt),
        in_specs=[pl.BlockSpec((1, tt, BRANCH_W), lambda i, t: (i, t, C_POOL_X // BRANCH_W)),
                  pl.BlockSpec((1, tt, BRANCH_W), lambda i, t: (i, t, C_POOL_G // BRANCH_W)),
                  pl.BlockSpec((len(POOL_WINDOWS), POOL_GD, POOL_GD), lambda i, t: (0, 0, 0)),
                  pl.BlockSpec((1, BRANCH_W), lambda i, t: (0, 0))],
        out_specs=(pl.BlockSpec((1, tt, BRANCH_W), lambda i, t: (i, t, 0)),
                   pl.BlockSpec((1, 16, BRANCH_W), lambda i, t: (i, 0, 0))),
        scratch_shapes=[pltpu.VMEM((tt + 16, BRANCH_W), f32)],
        compiler_params=_params(("parallel", "arbitrary")),
        name="pool_prompt",
    )(z32, z16, wp, sc)


def _mix_sample_kernel(past_len, lx_ref, lg_ref, px_ref, pg_ref, conv_ref, h0_ref, pbuf_ref,
                       cw_ref, cb_ref, wa_ref, ba_ref, wx_ref, bx_ref, lam_ref, wp_ref, sc_ref,
                       zl_ref, zp_ref, h_ref):
    x = lx_ref[...]
    xc = cb_ref[...] + x * cw_ref[CONV_W - 1:CONV_W, :]
    for k in range(CONV_W - 1):
        xc = xc + conv_ref[k] * cw_ref[k:k + 1, :]
    a, b = _lru_gates(xc, wa_ref, ba_ref, wx_ref, bx_ref, lam_ref)
    h = a * h0_ref[...] + b
    h_ref[...] = h
    zl_ref[...] = (h * _silu(lg_ref[...])).astype(bf16)

    px = px_ref[...]
    outs = []
    for gi, w in enumerate(POOL_WINDOWS):
        sl = slice(gi * POOL_GD, (gi + 1) * POOL_GD)
        s = px[:, sl]
        for k in range(1, w):
            s = s + pbuf_ref[POOL_BUF - k][:, sl]
        cnt = float(min(w, past_len + 1))
        pooled = s / cnt - px[:, sl]
        outs.append(_dot(pooled.astype(bf16), wp_ref[gi]))
    y = jnp.concatenate(outs, axis=-1) * sc_ref[...]
    zp_ref[...] = (y * _silu(pg_ref[...])).astype(bf16)


def _mix_sample(z32, z16, conv_t, h0, pbuf_t, cw, cb, wa, ba, wx, bx, lam, wp, sc, past_len):
    n = z32.shape[0]
    col = lambda c: pl.BlockSpec((n, BRANCH_W), lambda i: (0, c // BRANCH_W))
    full = lambda a: pl.BlockSpec(a.shape, lambda i: (0,) * a.ndim)
    args = (conv_t, h0, pbuf_t, cw, cb, wa, ba, wx, bx, lam, wp, sc)
    return pl.pallas_call(
        functools.partial(_mix_sample_kernel, past_len),
        out_shape=(jax.ShapeDtypeStruct((n, BRANCH_W), bf16), jax.ShapeDtypeStruct((n, BRANCH_W), bf16),
                   jax.ShapeDtypeStruct((n, BRANCH_W), f32)),
        grid=(1,),
        in_specs=[col(C_LRU_X), col(C_LRU_G), col(C_POOL_X), col(C_POOL_G)] + [full(a) for a in args],
        out_specs=(pl.BlockSpec((n, BRANCH_W), lambda i: (0, 0)),) * 3,
        compiler_params=_params(("arbitrary",)),
        name="mix_sample",
    )(z32, z16, z32, z16, *args)


def _compress_kernel(tab_ref, k_ref, v_ref, wk_ref, wv_ref, fk_ref, sk_ref, fv_ref, sv_ref):
    del tab_ref
    p = pl.program_id(1)
    chunks = PAGE_SIZE // CMP_STRIDE

    @pl.when(p == 0)
    def _():
        sk_ref[...] = jnp.zeros_like(sk_ref)
        sv_ref[...] = jnp.zeros_like(sv_ref)

    for src, w_ref, f_ref, s_ref in ((k_ref, wk_ref, fk_ref, sk_ref), (v_ref, wv_ref, fv_ref, sv_ref)):
        tile = src[0]
        first = jnp.sum((tile * w_ref[0]).reshape(chunks, CMP_STRIDE, KV_W), axis=1)
        second = jnp.sum((tile * w_ref[1]).reshape(chunks, CMP_STRIDE, KV_W), axis=1)
        for m in range(chunks):
            f_ref[0, m % 4, pl.ds(2 * p + m // 4, 1), :] = first[m:m + 1, :]
            if m >= 1:
                s_ref[0, (m - 1) % 4, pl.ds(2 * p + (m - 1) // 4, 1), :] = second[m:m + 1, :]
            else:
                @pl.when(p > 0)
                def _():
                    s_ref[0, 3, pl.ds(2 * p - 1, 1), :] = second[0:1, :]


def _compress(table, src_k, src_v, col_k, col_v, wk, wv, nb):
    npg = table.shape[0] // nb
    out = jax.ShapeDtypeStruct((nb, 4, N_CMP // 4, KV_W), f32)
    ospec = lambda: pl.BlockSpec((1, 4, N_CMP // 4, KV_W), lambda b, p, tab: (b, 0, 0, 0))
    return pl.pallas_call(
        _compress_kernel,
        out_shape=(out,) * 4,
        grid_spec=pltpu.PrefetchScalarGridSpec(
            num_scalar_prefetch=1,
            grid=(nb, npg),
            in_specs=[pl.BlockSpec((1, PAGE_SIZE, KV_W), lambda b, p, tab: (tab[b * npg + p], 0, col_k)),
                      pl.BlockSpec((1, PAGE_SIZE, KV_W), lambda b, p, tab: (tab[b * npg + p], 0, col_v)),
                      pl.BlockSpec((2, PAGE_SIZE, KV_W), lambda b, p, tab: (0, 0, 0)),
                      pl.BlockSpec((2, PAGE_SIZE, KV_W), lambda b, p, tab: (0, 0, 0))],
            out_specs=(ospec(), ospec(), ospec(), ospec())),
        compiler_params=_params(("parallel", "arbitrary")),
        name="compress",
    )(table, src_k, src_v, wk, wv)


def _chunk_maps():
    n_chunk = PP * PAGE_SIZE // CMP_STRIDE
    rows = n_chunk // 4
    chunk_of = np.arange(PP * PAGE_SIZE) // CMP_STRIDE
    first = np.zeros((n_chunk, PP * PAGE_SIZE), np.float32)
    second = np.zeros((n_chunk + 8, PP * PAGE_SIZE), np.float32)
    for i in range(4):
        for jj in range(rows):
            first[i * rows + jj] = chunk_of == 4 * jj + i
            second[i * rows + jj] = chunk_of == 4 * jj + i + 1
    second[n_chunk] = chunk_of == 0
    return jnp.asarray(first, bf16), jnp.asarray(second, bf16)


def _page_copies(tab_ref, layer, step, n_pages, pools, bufs, sems, wait):
    slot = step % 2
    for i in range(n_pages):
        page = 0 if wait else tab_ref[step * n_pages + i]
        for pool, buf, sem in zip(pools, bufs, sems):
            copy = pltpu.make_async_copy(pool.at[layer, page],
                                         buf.at[slot, :, pl.ds(i * PAGE_SIZE, PAGE_SIZE)], sem.at[slot])
            if wait:
                copy.wait()
            else:
                copy.start()


def _paged_step(tab_ref, layer, n_pages, pools, bufs, sems):
    step = pl.program_id(0) * pl.num_programs(1) + pl.program_id(1)
    n_steps = pl.num_programs(0) * pl.num_programs(1)

    @pl.when(step == 0)
    def _():
        _page_copies(tab_ref, layer, step, n_pages, pools, bufs, sems, wait=False)

    @pl.when(step + 1 < n_steps)
    def _():
        _page_copies(tab_ref, layer, step + 1, n_pages, pools, bufs, sems, wait=False)

    _page_copies(tab_ref, layer, step, n_pages, pools, bufs, sems, wait=True)
    return step % 2


def _compress_t_kernel(layer, tab_ref, k_hbm, v_hbm, wk_ref, wv_ref, ea_ref, eb_ref,
                       fk_ref, sk_ref, fv_ref, sv_ref, kbuf, vbuf, ksem, vsem):
    slot = _paged_step(tab_ref, layer, PP, (k_hbm, v_hbm), (kbuf, vbuf), (ksem, vsem))
    ps = pl.program_id(1)
    rows = PP * PAGE_SIZE // CMP_STRIDE // 4
    r0 = pl.multiple_of(ps * rows, rows)
    for buf, w_ref, f_ref, s_ref in ((kbuf, wk_ref, fk_ref, sk_ref), (vbuf, wv_ref, fv_ref, sv_ref)):
        pages = buf[slot]
        a1 = (pages * w_ref[0]).astype(bf16)
        a2 = (pages * w_ref[1]).astype(bf16)
        first = _nt(ea_ref[...], a1)
        second = _nt(eb_ref[...], a2)
        for i in range(4):
            f_ref[0, i, pl.ds(r0, rows), :] = first[i * rows:(i + 1) * rows]
            s_ref[0, i, pl.ds(r0, rows), :] = second[i * rows:(i + 1) * rows]

        @pl.when(ps > 0)
        def _():
            s_ref[0, 3, pl.ds(r0 - 1, 1), :] = second[4 * rows:4 * rows + 1]


def _compress_t(table, layer, cache_k, cache_v, wk, wv, nb):
    npg = table.shape[0] // nb
    ea, eb = _chunk_maps()
    out = jax.ShapeDtypeStruct((nb, 4, N_CMP // 4, KV_W), f32)
    ospec = lambda: pl.BlockSpec((1, 4, N_CMP // 4, KV_W), lambda b, p, tab: (b, 0, 0, 0))
    full = lambda a: pl.BlockSpec(a.shape, lambda b, p, tab: (0,) * a.ndim)
    hbm = lambda: pl.BlockSpec(memory_space=pl.ANY)
    wk, wv = jnp.tile(wk, (1, 1, PP)), jnp.tile(wv, (1, 1, PP))
    return pl.pallas_call(
        functools.partial(_compress_t_kernel, layer),
        out_shape=(out,) * 4,
        grid_spec=pltpu.PrefetchScalarGridSpec(
            num_scalar_prefetch=1,
            grid=(nb, npg // PP),
            in_specs=[hbm(), hbm(), full(wk), full(wv), full(ea), full(eb)],
            out_specs=(ospec(), ospec(), ospec(), ospec()),
            scratch_shapes=[pltpu.VMEM((2, KV_W, PP * PAGE_SIZE), f32), pltpu.VMEM((2, KV_W, PP * PAGE_SIZE), f32),
                            pltpu.SemaphoreType.DMA((2,)), pltpu.SemaphoreType.DMA((2,))]),
        compiler_params=_params(("arbitrary", "arbitrary")),
        name="compress_t",
    )(table, cache_k, cache_v, wk, wv, ea, eb)


def _finish_compress(f_ref, s_ref, phi_ref):
    blk = (f_ref[0] + s_ref[0]).reshape(N_CMP, KV_W)
    return _dot(blk.astype(bf16), phi_ref[...]).astype(bf16)


def _cmp_end(shape, axis):
    col = lax.broadcasted_iota(jnp.int32, shape, axis)
    n = ((col & (N_BLK - 1)) << 2) + (col >> 7)
    return n * CMP_STRIDE + (CMP_BLK - 1)


def _pick_blocks(score, n_pick):
    lane = lax.broadcasted_iota(jnp.int32, score.shape, 1).astype(f32)
    bias = jnp.full(score.shape, NEG, f32)
    for _ in range(n_pick):
        m = jnp.max(score, axis=-1, keepdims=True)
        first = jnp.min(jnp.where(score == m, lane, float(N_BLK)), axis=-1, keepdims=True)
        hit = lane == first
        bias = jnp.where(hit, 0.0, bias)
        score = jnp.where(hit, -jnp.inf, score)
    return bias


def _block_expand(first_block, n_keys):
    j = lax.broadcasted_iota(jnp.int32, (N_BLK, n_keys), 0)
    c = lax.broadcasted_iota(jnp.int32, (N_BLK, n_keys), 1)
    return jnp.where(j == first_block + (c >> 6), 1.0, 0.0).astype(bf16)


CQ = Q_PER_KV * TQ
WIN_T = WINDOW // TQ + 1
N_PARTS = 9
TILE_COL = HEAD_DIM + 6
V_ROWS = HEAD_DIM + 16


def _pick_blocks_t(score, forced, n_pick):
    jrow = lax.broadcasted_iota(jnp.int32, score.shape, 0).astype(f32)
    bias = jnp.where(forced, 0.0, NEG)
    score = jnp.where(forced, -jnp.inf, score)
    for _ in range(n_pick):
        m = jnp.max(score, axis=0, keepdims=True)
        first = jnp.min(jnp.where(score == m, jrow, float(N_BLK)), axis=0, keepdims=True)
        hit = jrow == first
        bias = jnp.where(hit, 0.0, bias)
        score = jnp.where(hit, -jnp.inf, score)
    return bias


def _nsa_prompt_kernel(q_ref, srow_ref, slope_ref, bg_ref, ng_ref, fk_ref, sk_ref, fv_ref, sv_ref,
                       phik_ref, phivt_ref, cpos_ref, ksa_ref, vst_ref, kwa_ref, vwt_ref, o_ref,
                       kca_scr, vct_scr, qa_scr, oc_scr, m_scr, acc_scr, negc_scr, negd_scr, negw_scr):
    qi = pl.program_id(1)
    s0 = qi * TQ

    @pl.when(qi == 0)
    def _():
        kc = _finish_compress(fk_ref, sk_ref, phik_ref)
        for g in range(KV_HEADS):
            kca_scr[g] = jnp.concatenate([kc[:, g * HEAD_DIM:(g + 1) * HEAD_DIM], cpos_ref[...]], axis=1)
        blk_v = (fv_ref[0] + sv_ref[0]).reshape(N_CMP, KV_W).astype(bf16)
        vct_scr[...] = _nt(phivt_ref[...], blk_v).astype(bf16)

    t_of = lambda shape: s0 + (lax.broadcasted_iota(jnp.int32, shape, 1) & (TQ - 1))
    q_t = (q_ref[0].astype(f32) * Q_SCALE).T.astype(bf16)

    negc_scr[...] = jnp.where(_cmp_end((N_CMP, CQ), 0) <= t_of((N_CMP, CQ)), 0.0, NEG)
    sees_block = t_of((1, CQ)) >= CMP_BLK - 1
    t_q = t_of((N_BLK, TQ))
    jr = lax.broadcasted_iota(jnp.int32, (N_BLK, TQ), 0)
    jb = t_q >> 6
    ok_b = (jr << 6) <= t_q
    forced = (jr == 0) | (jr == jb) | (jr == jb - 1)
    tile_any = [jnp.full((N_BLK // 8, 1), NEG, f32) for _ in LOOP_GROUPS]
    for g in range(KV_HEADS):
        q_g = jnp.concatenate([q_t[(g * Q_PER_KV + r) * HEAD_DIM:(g * Q_PER_KV + r + 1) * HEAD_DIM]
                               for r in range(Q_PER_KV)], axis=1)
        qa = jnp.concatenate([q_g, srow_ref[g]], axis=0)
        st = _dot(kca_scr[g], qa) + negc_scr[...]
        e = jnp.exp2(st - jnp.max(st, axis=0, keepdims=True))
        pt = e * jnp.where(sees_block, 1.0 / jnp.sum(e, axis=0, keepdims=True), 0.0)
        oc_scr[g] = _dot(vct_scr[g * HEAD_DIM:(g + 1) * HEAD_DIM], pt.astype(bf16))
        ps = None
        for i in range(4):
            for r in range(Q_PER_KV):
                slab = pt[i * N_BLK:(i + 1) * N_BLK, r * TQ:(r + 1) * TQ]
                ps = slab if ps is None else ps + slab
        bias = _pick_blocks_t(jnp.where(ok_b, ps, NEG), forced, N_SEL - 3)
        qa_scr[g] = jnp.concatenate([qa, jnp.concatenate([bias.astype(bf16)] * Q_PER_KV, axis=1)], axis=0)
        any_t = jnp.max(bias.reshape(N_BLK // 8, 8, TQ), axis=1)
        li = [g in grp for grp in LOOP_GROUPS].index(True)
        tile_any[li] = jnp.maximum(tile_any[li], jnp.max(any_t, axis=1, keepdims=True))
    kt_row = lax.broadcasted_iota(jnp.int32, tile_any[0].shape, 0)
    tile_bits = [jnp.sum(jnp.where(ta == 0.0, 1 << kt_row, 0)) for ta in tile_any]

    m_scr[...] = jnp.full(m_scr.shape, NEG, f32)
    acc_scr[...] = jnp.zeros_like(acc_scr)

    def sel_tile(kt, diagonal, groups, n_keys=TK):
        k0 = pl.multiple_of(kt * TK, TK)
        tile_off = (k0 - s0).astype(f32)
        key_r = lax.broadcasted_iota(jnp.int32, (n_keys, N_BLK), 0)
        blk_c = lax.broadcasted_iota(jnp.int32, (n_keys, N_BLK), 1)
        expand = jnp.where(blk_c == kt * (TK // SEL_BLK) + (key_r >> 6), 1.0, 0.0).astype(bf16)
        k_aug = ksa_ref[0, pl.ds(k0, n_keys), :]
        v_t = vst_ref[0, kt][:, 0:n_keys]
        for g in groups:
            lhs = jnp.concatenate([k_aug[:, g * QA_W:(g + 1) * QA_W], expand], axis=1)
            st = _dot(lhs, qa_scr[g])
            if diagonal:
                st = st + negd_scr[qi % (TK // TQ), 0:n_keys]
            c = slope_ref[g] * tile_off
            m_old = m_scr[g]
            m_new = jnp.maximum(m_old, jnp.max(st, axis=0, keepdims=True) + c)
            alpha = jnp.exp2(m_old - m_new)
            pt = jnp.exp2((st - (m_new - c)).astype(bf16))
            acc_scr[g] = alpha * acc_scr[g] + _dot(v_t[g * V_ROWS:(g + 1) * V_ROWS], pt)
            m_scr[g] = m_new

    last = s0 // TK

    @pl.when(qi < TK // TQ)
    def _():
        key_pos = lax.broadcasted_iota(jnp.int32, (TK, CQ), 0)
        negd_scr[qi] = jnp.where(key_pos <= t_of((TK, CQ)), 0.0, NEG)

    for groups, bits in zip(LOOP_GROUPS, tile_bits):
        def sel_body(kt, carry, groups=groups, bits=bits):
            @pl.when(((bits >> kt) & 1) == 1)
            def _():
                sel_tile(kt, False, groups)
            return carry

        lax.fori_loop(0, last, sel_body, 0)
    for sub in range(TK // TQ):
        @pl.when(qi % (TK // TQ) == sub)
        def _(sub=sub):
            sel_tile(last, True, range(KV_HEADS), (sub + 1) * TQ)

    w_tile = jnp.maximum(qi - WINDOW // TQ, 0)
    n_win = WIN_T * TQ

    @pl.when(qi <= WINDOW // TQ)
    def _():
        dist_w = t_of((n_win, CQ)) - (w_tile * TQ + lax.broadcasted_iota(jnp.int32, (n_win, CQ), 0))
        negw_scr[...] = jnp.where((dist_w >= 0) & (dist_w < WINDOW), 0.0, NEG)

    lane_w = lax.broadcasted_iota(jnp.int32, (n_win, QA_W), 1)
    slab_off = (lax.broadcasted_iota(jnp.int32, (n_win, QA_W), 0) // TQ * TQ).astype(f32).astype(bf16)
    in_tile_col = (lane_w >= TILE_COL) & (lane_w < TILE_COL + 3)
    kw_all = kwa_ref[0, pl.ds(pl.multiple_of(w_tile * TQ, TQ), n_win), :]
    vw_t = jnp.concatenate([vwt_ref[0, w_tile + i] for i in range(WIN_T)], axis=1)
    gates = jax.nn.sigmoid(bg_ref[0].T)
    blocks = []
    for g in range(KV_HEADS):
        lhs = jnp.where(in_tile_col, slab_off, kw_all[:, g * QA_W:(g + 1) * QA_W])
        st = _dot(lhs, qa_scr[g, 0:QA_W]) + negw_scr[...]
        pt = jnp.exp2((st - jnp.max(st, axis=0, keepdims=True)).astype(bf16))
        win = _dot(vw_t[g * V_ROWS:(g + 1) * V_ROWS], pt)
        o_w = win[0:HEAD_DIM] * (1.0 / win[HEAD_DIM:HEAD_DIM + 1])
        o_s = acc_scr[g, 0:HEAD_DIM] * (1.0 / acc_scr[g, HEAD_DIM:HEAD_DIM + 1])
        o_c = oc_scr[g]
        heads = []
        for r in range(Q_PER_KV):
            h = g * Q_PER_KV + r
            cols = slice(r * TQ, (r + 1) * TQ)
            heads.append(gates[3 * h:3 * h + 1] * o_c[:, cols] + gates[3 * h + 1:3 * h + 2] * o_s[:, cols]
                         + gates[3 * h + 2:3 * h + 3] * o_w[:, cols])
        for r in range(0, Q_PER_KV, 2):
            blocks.append(jnp.concatenate(heads[r:r + 2], axis=0).T)
    y = jnp.concatenate(blocks, axis=-1)
    o_ref[0] = (y * _silu(ng_ref[0])).astype(bf16)


def _nsa_prompt(z3, z16, fk, sk, fv, sv, phik, phivt):
    b, s, _ = z3.shape
    nq = s // TQ
    parts = _slope_parts().reshape(KV_HEADS, Q_PER_KV, POS_ROWS)
    srow = jnp.asarray(np.repeat(parts.transpose(0, 2, 1), TQ, axis=2), bf16)
    slope = jnp.asarray(np.repeat(np.asarray(SLOPES_LOG2, np.float32).reshape(KV_HEADS, 1, Q_PER_KV), TQ, axis=2))
    ksa = _key_aug(_kv_seg(z3, 2), TK)
    kwa = _key_aug(_kv_seg(z3, 4), TQ)
    vst = _value_tiles(_kv_seg(z3, 3), TK)
    vwt = _value_tiles(_kv_seg(z3, 5), TQ)
    cpos = _cmp_pos_cols()

    once = pl.Buffered(1)
    cmp = lambda: pl.BlockSpec((1, 4, N_CMP // 4, KV_W), lambda i, t: (i, 0, 0, 0))
    full = lambda a: pl.BlockSpec(a.shape, lambda i, t: (0,) * a.ndim)
    return pl.pallas_call(
        _nsa_prompt_kernel,
        out_shape=jax.ShapeDtypeStruct((b, s, BRANCH_W), bf16),
        grid=(b, nq),
        in_specs=[pl.BlockSpec((1, TQ, BRANCH_W), lambda i, t: (i, t, C_Q // BRANCH_W)),
                  full(srow), full(slope),
                  pl.BlockSpec((1, TQ, BG_W), lambda i, t: (i, t, C_BG // BG_W)),
                  pl.BlockSpec((1, TQ, BRANCH_W), lambda i, t: (i, t, C_NSA_G // BRANCH_W)),
                  cmp(), cmp(), cmp(), cmp(), full(phik), full(phivt), full(cpos),
                  pl.BlockSpec((1, s, KV_HEADS * QA_W), lambda i, t: (i, 0, 0), pipeline_mode=once),
                  pl.BlockSpec((1, s // TK, KV_HEADS * V_ROWS, TK), lambda i, t: (i, 0, 0, 0), pipeline_mode=once),
                  pl.BlockSpec((1, s, KV_HEADS * QA_W), lambda i, t: (i, 0, 0), pipeline_mode=once),
                  pl.BlockSpec((1, nq, KV_HEADS * V_ROWS, TQ), lambda i, t: (i, 0, 0, 0), pipeline_mode=once)],
        out_specs=pl.BlockSpec((1, TQ, BRANCH_W), lambda i, t: (i, t, 0)),
        scratch_shapes=[pltpu.VMEM((KV_HEADS, N_CMP, QA_W), bf16), pltpu.VMEM((KV_W, N_CMP), bf16),
                        pltpu.VMEM((KV_HEADS, QA_W + N_BLK, CQ), bf16),
                        pltpu.VMEM((KV_HEADS, HEAD_DIM, CQ), f32),
                        pltpu.VMEM((KV_HEADS, 1, CQ), f32),
                        pltpu.VMEM((KV_HEADS, V_ROWS, CQ), f32),
                        pltpu.VMEM((N_CMP, CQ), f32), pltpu.VMEM((TK // TQ, TK, CQ), f32),
                        pltpu.VMEM((WIN_T * TQ, CQ), f32)],
        compiler_params=_params(("parallel", "arbitrary")),
        name="nsa_prompt",
    )(z16, srow, slope, z3, z16, fk, sk, fv, sv, phik, phivt, cpos, ksa, vst, kwa, vwt)


def _by_group(fn):
    hg = lax.broadcasted_iota(jnp.int32, (N_HEADS, 1), 0) >> 2
    out = fn(0)
    for g in range(1, KV_HEADS):
        out = jnp.where(hg == g, fn(g), out)
    return out


def _nsa_sample_kernel(past_len, layer, tab_ref, q_ref, bg_ref, ng_ref, slope_ref,
                       fk_ref, sk_ref, fv_ref, sv_ref, phik_ref, phiv_ref,
                       ksn_ref, vsn_ref, kwn_ref, vwn_ref, bk_ref, bv_ref, kp_hbm, vp_hbm, o_ref,
                       q_scr, sel_scr, oc_scr, ow_scr, m_scr, l_scr, acc_scr, kbuf, vbuf, ksem, vsem):
    slot = _paged_step(tab_ref, layer, PP_SEL, (kp_hbm, vp_hbm), (kbuf, vbuf), (ksem, vsem))
    p = pl.program_id(1)
    slope = slope_ref[:, 0:1]
    gsl = lambda g: slice(g * HEAD_DIM, (g + 1) * HEAD_DIM)

    def per_head(row_ref):
        row = row_ref[0]
        return _by_group(lambda g: jnp.broadcast_to(row[:, gsl(g)], (N_HEADS, HEAD_DIM)))

    @pl.when(p == 0)
    def _():
        qrow = q_ref[0]
        q16 = jnp.concatenate([qrow[:, h * HEAD_DIM:(h + 1) * HEAD_DIM] for h in range(N_HEADS)], axis=0)
        q_scr[...] = q16
        q16f = q16.astype(f32)
        kc = _finish_compress(fk_ref, sk_ref, phik_ref)
        vc = _finish_compress(fv_ref, sv_ref, phiv_ref)

        dist_c = past_len - _cmp_end((1, N_CMP), 1)
        ok_c = dist_c >= 0
        s = _by_group(lambda g: _nt(q16, kc[:, gsl(g)])) - slope * dist_c.astype(f32)
        s = jnp.where(ok_c, s, NEG)
        e = jnp.exp(s - jnp.max(s, axis=-1, keepdims=True))
        pc = jnp.where(ok_c, e * (1.0 / jnp.sum(e, axis=-1, keepdims=True)), 0.0)
        pcb = pc.astype(bf16)
        oc_scr[...] = _by_group(lambda g: _dot(pcb, vc[:, gsl(g)]))
        ps16 = (pc[:, 0:N_BLK] + pc[:, N_BLK:2 * N_BLK]) + (pc[:, 2 * N_BLK:3 * N_BLK] + pc[:, 3 * N_BLK:])
        ps = jnp.concatenate([jnp.sum(ps16[g * Q_PER_KV:(g + 1) * Q_PER_KV], axis=0, keepdims=True)
                              for g in range(KV_HEADS)] + [jnp.zeros((8 - KV_HEADS, N_BLK), f32)], axis=0)
        jl = lax.broadcasted_iota(jnp.int32, (8, N_BLK), 1)
        jb = past_len // SEL_BLK
        forced = jnp.where((jl == 0) | (jl == jb) | (jl == jb - 1), FORCE, 0.0)
        bias8 = _pick_blocks(ps + forced, N_SEL - 1)
        sel_scr[...] = jnp.concatenate(
            [jnp.broadcast_to(bias8[g:g + 1], (Q_PER_KV, N_BLK)) for g in range(KV_HEADS)], axis=0).astype(bf16)

        wb = bk_ref.shape[-1]
        dist_w = wb - lax.broadcasted_iota(jnp.int32, (1, wb), 1)
        ok_w = (dist_w >= 0) & (dist_w < WINDOW)
        bk = bk_ref[0, 0].astype(bf16)
        bv = bv_ref[0, 0].astype(bf16)
        s_buf = _by_group(lambda g: _dot(q16, bk[gsl(g)])) - slope * dist_w.astype(f32)
        s_buf = jnp.where(ok_w, s_buf, NEG)
        s_new = jnp.sum(q16f * per_head(kwn_ref), axis=-1, keepdims=True)
        m_w = jnp.maximum(jnp.max(s_buf, axis=-1, keepdims=True), s_new)
        e_buf = jnp.exp(s_buf - m_w)
        e_new = jnp.exp(s_new - m_w)
        ebb = e_buf.astype(bf16)
        num = _by_group(lambda g: _nt(ebb, bv[gsl(g)])) + e_new * per_head(vwn_ref)
        ow_scr[...] = num * (1.0 / (jnp.sum(e_buf, axis=-1, keepdims=True) + e_new))

        m_scr[...] = jnp.sum(q16f * per_head(ksn_ref), axis=-1, keepdims=True)
        l_scr[...] = jnp.ones_like(l_scr)
        acc_scr[...] = per_head(vsn_ref)

    q16 = q_scr[...]
    n_keys = PP_SEL * PAGE_SIZE
    kp = kbuf[slot].astype(bf16)
    vp = vbuf[slot].astype(bf16)
    dist = past_len - (p * n_keys + lax.broadcasted_iota(jnp.int32, (1, n_keys), 1))
    expand = _block_expand(p * (n_keys // SEL_BLK), n_keys)
    s = _by_group(lambda g: _dot(q16, kp[gsl(g)])) - slope * dist.astype(f32) + _dot(sel_scr[...], expand)
    m_old = m_scr[...]
    m_new = jnp.maximum(m_old, jnp.max(s, axis=-1, keepdims=True))
    alpha = jnp.exp(m_old - m_new)
    pr = jnp.exp(s - m_new)
    prb = pr.astype(bf16)
    l_scr[...] = alpha * l_scr[...] + jnp.sum(pr, axis=-1, keepdims=True)
    acc_scr[...] = alpha * acc_scr[...] + _by_group(lambda g: _nt(prb, vp[gsl(g)]))
    m_scr[...] = m_new

    @pl.when(p == pl.num_programs(1) - 1)
    def _():
        o_s = acc_scr[...] * (1.0 / l_scr[...])
        gates = jax.nn.sigmoid(bg_ref[0])
        lane = lax.broadcasted_iota(jnp.int32, (N_HEADS, BG_W), 1)
        h3 = 3 * lax.broadcasted_iota(jnp.int32, (N_HEADS, BG_W), 0)
        gate = lambda n: jnp.sum(jnp.where(lane == h3 + n, gates, 0.0), axis=-1, keepdims=True)
        y16 = gate(0) * oc_scr[...] + gate(1) * o_s + gate(2) * ow_scr[...]
        y = jnp.concatenate([y16[h:h + 1, :] for h in range(N_HEADS)], axis=-1)
        o_ref[0] = (y * _silu(ng_ref[0])).astype(bf16)


def _nsa_sample(table, layer, qb, z2, z2h, slopes, fk, sk, fv, sv, phik, phiv, pool_k, pool_v, buf_k, buf_v,
                past_len):
    nb = qb.shape[0]
    npg = table.shape[0] // nb
    z3 = z2.reshape(nb, 1, Z32_W)
    z3h = z2h.reshape(nb, 1, Z16_W)
    tok = lambda w, c: pl.BlockSpec((1, 1, w), lambda b, p, tab: (b, 0, c))
    cmp = lambda: pl.BlockSpec((1, 4, N_CMP // 4, KV_W), lambda b, p, tab: (b, 0, 0, 0))
    phi = lambda: pl.BlockSpec((KV_W, KV_W), lambda b, p, tab: (0, 0))
    wb = buf_k.shape[-1]
    win = lambda: pl.BlockSpec((1, 1, KV_W, wb), lambda b, p, tab: (layer, b, 0, 0))
    hbm = lambda: pl.BlockSpec(memory_space=pl.ANY)
    page_buf = lambda: pltpu.VMEM((2, KV_W, PP_SEL * PAGE_SIZE), f32)
    kvc = C_KV // KV_W
    return pl.pallas_call(
        functools.partial(_nsa_sample_kernel, past_len, layer),
        out_shape=jax.ShapeDtypeStruct((nb, 1, BRANCH_W), bf16),
        grid_spec=pltpu.PrefetchScalarGridSpec(
            num_scalar_prefetch=1,
            grid=(nb, npg // PP_SEL),
            in_specs=[tok(BRANCH_W, 0), tok(BG_W, C_BG // BG_W), tok(BRANCH_W, C_NSA_G // BRANCH_W),
                      pl.BlockSpec((N_HEADS, 128), lambda b, p, tab: (0, 0)),
                      cmp(), cmp(), cmp(), cmp(), phi(), phi(),
                      tok(KV_W, kvc + 2), tok(KV_W, kvc + 3), tok(KV_W, kvc + 4), tok(KV_W, kvc + 5),
                      win(), win(), hbm(), hbm()],
            out_specs=pl.BlockSpec((1, 1, BRANCH_W), lambda b, p, tab: (b, 0, 0)),
            scratch_shapes=[pltpu.VMEM((N_HEADS, HEAD_DIM), bf16), pltpu.VMEM((N_HEADS, N_BLK), bf16),
                            pltpu.VMEM((N_HEADS, HEAD_DIM), f32), pltpu.VMEM((N_HEADS, HEAD_DIM), f32),
                            pltpu.VMEM((N_HEADS, 1), f32), pltpu.VMEM((N_HEADS, 1), f32),
                            pltpu.VMEM((N_HEADS, HEAD_DIM), f32), page_buf(), page_buf(),
                            pltpu.SemaphoreType.DMA((2,)), pltpu.SemaphoreType.DMA((2,))]),
        compiler_params=_params(("arbitrary", "arbitrary")),
        name="nsa_sample",
    )(table, qb.reshape(nb, 1, BRANCH_W), z3, z3h, slopes, fk, sk, fv, sv, phik, phiv,
      z3, z3, z3, z3, buf_k, buf_v, pool_k, pool_v)


def _merge_kernel(zl_ref, zp_ref, zn_ref, m0_ref, m1_ref, m2_ref, wb_ref, wo_ref, g_ref, x_ref, y_ref):
    acc = None
    for n, (zz, mg) in enumerate(((zl_ref, m0_ref), (zp_ref, m1_ref), (zn_ref, m2_ref))):
        term = jax.nn.sigmoid(mg[...].astype(f32)) * _dot(zz[...], wb_ref[n])
        acc = term if acc is None else acc + term
    out = _dot(acc.astype(bf16), wo_ref[...])
    ms = jnp.mean(out * out, axis=-1, keepdims=True)
    y_ref[...] = x_ref[...] + out * lax.rsqrt(ms + EPS) * g_ref[...]


def _merge(zl, zp, zn, z16, wb, wo, g_row, x2d):
    n = x2d.shape[0]
    tm = min(n, 256)
    rowblk = lambda c: pl.BlockSpec((tm, D_MODEL), lambda i: (i, c))
    return pl.pallas_call(
        _merge_kernel,
        out_shape=jax.ShapeDtypeStruct((n, D_MODEL), f32),
        grid=(n // tm,),
        in_specs=[rowblk(0), rowblk(0), rowblk(0),
                  rowblk(C_MG // D_MODEL), rowblk(C_MG // D_MODEL + 1), rowblk(C_MG // D_MODEL + 2),
                  pl.BlockSpec((N_BRANCH, BRANCH_W, D_MODEL), lambda i: (0, 0, 0)),
                  pl.BlockSpec((D_MODEL, D_MODEL), lambda i: (0, 0)),
                  pl.BlockSpec((1, D_MODEL), lambda i: (0, 0)),
                  rowblk(0)],
        out_specs=rowblk(0),
        compiler_params=_params(("parallel",)),
        name="merge",
    )(zl, zp, zn, z16, z16, z16, wb, wo, g_row, x2d)


def _block_diag(w, per):
    n, d, _ = w.shape
    eye = jnp.eye(per, dtype=w.dtype)
    t = jnp.einsum('cpde,pq->cpdqe', w.reshape(n // per, per, d, d), eye)
    return t.reshape(n // per, per * d, per * d)


def _pack_w_in(w):
    seg = lambda i: w[:, i * BRANCH_W:(i + 1) * BRANCH_W]
    old_kv = 6 * BRANCH_W
    old_bg = old_kv + 6 * KV_W
    old_mg = old_bg + N_BRANCH * N_HEADS
    f32_part = jnp.concatenate([seg(0), seg(2), w[:, old_kv:old_bg], w[:, old_bg:old_mg]], axis=1)
    f32_part = jnp.pad(f32_part, ((0, 0), (0, Z32_W - f32_part.shape[1])))
    bf16_part = jnp.concatenate([seg(1), seg(3), seg(4), seg(5), w[:, old_mg:]], axis=1)
    return jnp.concatenate([f32_part, bf16_part], axis=1).astype(bf16)


def _tile_wpos(w_pos):
    halves = w_pos.reshape(2, CMP_STRIDE, HEAD_DIM)
    return jnp.tile(halves, (1, PAGE_SIZE // CMP_STRIDE, KV_HEADS))


def _tile_wpos_t(w_pos):
    halves = w_pos.reshape(2, CMP_STRIDE, HEAD_DIM).swapaxes(1, 2)
    return jnp.tile(halves, (1, KV_HEADS, PAGE_SIZE // CMP_STRIDE))


def _lanes_last(cache):
    d, n, rows = cache.shape[:3]
    return jnp.transpose(cache, (0, 1, 3, 4, 2)).reshape(d, n, KV_W, rows)


def _slope_parts():
    cols = np.zeros((N_HEADS, POS_ROWS), np.float32)
    rnd = lambda v: np.float32(np.float32(v).astype(bf16))
    for h, s in enumerate(SLOPES_LOG2):
        s1 = rnd(s)
        s2 = rnd(np.float32(s) - s1)
        s3 = rnd(np.float32(s) - s1 - s2)
        cols[h, 0:N_PARTS] = [s1, s2, s3] * 3
    return cols


def _split_pos(pos, shift):
    cols = np.zeros((pos.shape[0], POS_ROWS), np.float32)
    cols[:, 0:3] = ((pos >> shift) << shift)[:, None]
    cols[:, 3:6] = (pos & ((1 << shift) - 1))[:, None]
    return cols


def _key_aug(k_rows, tile):
    b, s, _ = k_rows.shape
    pos = jnp.asarray(_split_pos(np.arange(s) % tile, 4), bf16)
    kg = k_rows.astype(bf16).reshape(b, s, KV_HEADS, HEAD_DIM)
    posb = jnp.broadcast_to(pos[None, :, None, :], (b, s, KV_HEADS, POS_ROWS))
    return jnp.concatenate([kg, posb], axis=-1).reshape(b, s, KV_HEADS * QA_W)


def _value_tiles(v_rows, tile):
    b, s, _ = v_rows.shape
    vt = v_rows.astype(bf16).reshape(b, s // tile, tile, KV_HEADS, HEAD_DIM).transpose(0, 1, 3, 4, 2)
    ones = jnp.ones((b, s // tile, KV_HEADS, V_ROWS - HEAD_DIM, tile), bf16)
    return jnp.concatenate([vt, ones], axis=3).reshape(b, s // tile, KV_HEADS * V_ROWS, tile)


def _cmp_pos_cols():
    slot = np.arange(N_CMP)
    n = ((slot & (N_BLK - 1)) << 2) + (slot >> 7)
    return jnp.asarray(_split_pos(n * CMP_STRIDE + CMP_BLK - 1, 8), bf16)


def _kv_seg(z, i):
    return z[..., C_KV + i * KV_W:C_KV + (i + 1) * KV_W]


def kernel(x_prompt, x_sample, cache_cmp_k, cache_cmp_v, cache_sel_k, cache_sel_v, cache_win_k, cache_win_v, state_conv, state_lru, state_pool, page_table, g_pre, g_post, w_in, conv_w, conv_b, w_rg_a, b_rg_a, w_rg_x, b_rg_x, lru_lambda, w_pool, pool_scale, cmp_pos_k, cmp_phi_k, cmp_pos_v, cmp_phi_v, w_branch, w_out):
    depth = w_in.shape[0]
    bp, seq, _ = x_prompt.shape
    bs = x_sample.shape[0]
    n_pages = page_table.shape[1]
    past_len = n_pages * PAGE_SIZE
    n_phys = cache_cmp_k.shape[1]
    assert seq == N_BLK * SEL_BLK and past_len == N_BLK * SEL_BLK and x_sample.shape[1] == 1
    wb = cache_win_k.shape[2]

    table_s = page_table.reshape(-1).astype(jnp.int32)
    table_p = jnp.arange(bp * (seq // PAGE_SIZE), dtype=jnp.int32)
    slopes = jnp.broadcast_to(jnp.asarray(SLOPES, f32)[:, None], (N_HEADS, 128))
    row = lambda v: v.reshape(1, -1)
    cmp_kt, cmp_vt = _lanes_last(cache_cmp_k), _lanes_last(cache_cmp_v)
    sel_kt, sel_vt = _lanes_last(cache_sel_k), _lanes_last(cache_sel_v)
    win_kt, win_vt = _lanes_last(cache_win_k), _lanes_last(cache_win_v)

    xp = x_prompt.reshape(bp * seq, D_MODEL)
    xs = x_sample.reshape(bs, D_MODEL)
    pr = [[] for _ in range(9)]
    sm = [[] for _ in range(9)]
    for l in range(depth):
        w_packed = _pack_w_in(w_in[l])
        wa = _block_diag(w_rg_a[l], MXU_W // LRU_BD).astype(bf16)
        wx = _block_diag(w_rg_x[l], MXU_W // LRU_BD).astype(bf16)
        wp = w_pool[l].astype(bf16)
        phik = _block_diag(jnp.broadcast_to(cmp_phi_k[l], (KV_HEADS, HEAD_DIM, HEAD_DIM)), KV_HEADS)[0].astype(bf16)
        phiv = _block_diag(jnp.broadcast_to(cmp_phi_v[l], (KV_HEADS, HEAD_DIM, HEAD_DIM)), KV_HEADS)[0].astype(bf16)
        wpos_k = _tile_wpos(cmp_pos_k[l])
        wpos_v = _tile_wpos(cmp_pos_v[l])
        wbr = w_branch[l].astype(bf16)
        wo = w_out[l].astype(bf16)
        lru_w = (conv_w[l], row(conv_b[l]), wa, row(b_rg_a[l]), wx, row(b_rg_x[l]), row(lru_lambda[l]))

        z, z16 = _inproj(xp, row(g_pre[l]), w_packed)
        z3 = z.reshape(bp, seq, Z32_W)
        z16_3 = z16.reshape(bp, seq, Z16_W)
        zl, conv_tail, h_p = _lru_prompt(z3, z16_3, *lru_w)
        zpool, pool_tail = _pool_prompt(z3, z16_3, wp, row(pool_scale[l]))
        zr = z.reshape(bp * seq // PAGE_SIZE, PAGE_SIZE, Z32_W)
        fk, sk, fv, sv = _compress(table_p, zr, zr, C_KV // KV_W, C_KV // KV_W + 1, wpos_k, wpos_v, bp)
        zn = _nsa_prompt(z3, z16_3, fk, sk, fv, sv, phik, phiv.T)
        xp = _merge(zl.reshape(bp * seq, BRANCH_W), zpool.reshape(bp * seq, BRANCH_W),
                    zn.reshape(bp * seq, BRANCH_W), z16, wbr, wo, row(g_post[l]), xp)
        kv_rows = [_kv_seg(z3, i).reshape(bp, seq, KV_HEADS, HEAD_DIM) for i in range(6)]
        wlen = min(WINDOW, seq)
        st_p = kv_rows[:4] + [kv_rows[4][:, -wlen:], kv_rows[5][:, -wlen:],
                              conv_tail[:, -(CONV_W - 1):], h_p[:, 0], pool_tail[:, -POOL_BUF:]]

        zs, zs16 = _inproj(xs, row(g_pre[l]), w_packed)
        zls, zps, h_s = _mix_sample(zs, zs16, state_conv[l].swapaxes(0, 1), state_lru[l],
                                    state_pool[l].swapaxes(0, 1), *lru_w, wp, row(pool_scale[l]), past_len)
        fk, sk, fv, sv = _compress_t(table_s, l, cmp_kt, cmp_vt, _tile_wpos_t(cmp_pos_k[l]),
                                     _tile_wpos_t(cmp_pos_v[l]), bs)
        qs = (zs16[:, C_Q:C_Q + BRANCH_W].astype(f32) * (HEAD_DIM ** -0.5)).astype(bf16)
        zns = _nsa_sample(table_s, l, qs, zs, zs16, slopes, fk, sk, fv, sv, phik, phiv,
                          sel_kt, sel_vt, win_kt, win_vt, past_len)
        xs = _merge(zls, zps, zns.reshape(bs, BRANCH_W), zs16, wbr, wo, row(g_post[l]), xs)
        kv_new = [_kv_seg(zs, i).reshape(bs, 1, KV_HEADS, HEAD_DIM) for i in range(6)]
        st_s = kv_new[:4] + [jnp.concatenate([cache_win_k[l], kv_new[4]], axis=1)[:, -wb:],
                             jnp.concatenate([cache_win_v[l], kv_new[5]], axis=1)[:, -wb:],
                             jnp.concatenate([state_conv[l], zs[:, None, C_LRU_X:C_LRU_X + BRANCH_W]], axis=1)[:, -(CONV_W - 1):],
                             h_s,
                             jnp.concatenate([state_pool[l], zs[:, None, C_POOL_X:C_POOL_X + BRANCH_W]], axis=1)[:, -POOL_BUF:]]
        for i in range(9):
            pr[i].append(st_p[i])
            sm[i].append(st_s[i])

    out = [xp.reshape(bp, seq, D_MODEL), xs.reshape(bs, 1, D_MODEL)]
    for i in range(9):
        out += [jnp.stack(pr[i]), jnp.stack(sm[i])]
    return tuple(out)
```

```python
import functools

import numpy as np
import jax
import jax.numpy as jnp
from jax import lax
from jax.experimental import pallas as pl
from jax.experimental.pallas import tpu as pltpu

f32 = jnp.float32
bf16 = jnp.bfloat16

D_MODEL = 1024
BRANCH_W = 1024
N_BRANCH = 3
LRU_BLOCKS = 16
LRU_BD = BRANCH_W // LRU_BLOCKS
CONV_W = 4
LRU_C = 8.0
POOL_WINDOWS = (2, 4, 8, 16)
POOL_GD = BRANCH_W // len(POOL_WINDOWS)
POOL_BUF = max(POOL_WINDOWS) - 1
N_HEADS = 16
HEAD_DIM = 64
KV_HEADS = 4
Q_PER_KV = N_HEADS // KV_HEADS
KV_W = KV_HEADS * HEAD_DIM
CMP_STRIDE = 16
CMP_BLK = 2 * CMP_STRIDE
SEL_BLK = 64
N_SEL = 16
WINDOW = 512
PAGE_SIZE = 128
FORCE = 1e4
NEG = -1e30
EPS = 1e-6

C_LRU_X, C_POOL_X, C_KV = 0, 1024, 2048
C_BG = C_KV + 6 * KV_W
BG_W = 128
Z32_W = 4096
C_LRU_G, C_POOL_G, C_Q, C_NSA_G, C_MG = 0, 1024, 2048, 3072, 4096
Z16_W = C_MG + N_BRANCH * D_MODEL
IN_TN = 1024
N32_TILES = Z32_W // IN_TN

MXU_W = 256
VMEM_LIMIT = 56 * 1024 * 1024

N_CMP = 512
N_BLK = 128
TQ = 256
TK = 512
PP = 8
PP_SEL = 16
POS_ROWS = 64
QA_W = HEAD_DIM + POS_ROWS

SLOPES = [float(np.float32(2.0 ** (-8.0 * (h + 1) / N_HEADS))) for h in range(N_HEADS)]
LOG2E = float(np.log2(np.e))
SLOPES_LOG2 = [float(np.float32(s * LOG2E)) for s in SLOPES]
Q_SCALE = float(np.float32(HEAD_DIM ** -0.5 * LOG2E))
LOOP_GROUPS = ((0,), (1,), (2,), (3,))


def _nt(a, b):
    return lax.dot_general(a, b, (((1,), (1,)), ((), ())), preferred_element_type=f32)


def _dot(a, b):
    return jnp.dot(a, b, preferred_element_type=f32)


def _silu(x):
    x = x.astype(f32)
    return x * jax.nn.sigmoid(x)


def _params(sem):
    return pltpu.CompilerParams(dimension_semantics=sem, vmem_limit_bytes=VMEM_LIMIT)


def _inproj_kernel(x_ref, g_ref, w_ref, o32_ref, o16_ref, u_ref):
    j = pl.program_id(1)

    @pl.when(j == 0)
    def _():
        x = x_ref[...]
        ms = jnp.mean(x * x, axis=-1, keepdims=True)
        u_ref[...] = (x * lax.rsqrt(ms + EPS) * g_ref[...]).astype(bf16)

    @pl.when(j < N32_TILES)
    def _():
        o32_ref[...] = _dot(u_ref[...], w_ref[...])

    @pl.when(j >= N32_TILES)
    def _():
        o16_ref[...] = _dot(u_ref[...], w_ref[...]).astype(bf16)


def _inproj(x2d, g_row, w_packed):
    n = x2d.shape[0]
    tm = min(n, 1024)
    return pl.pallas_call(
        _inproj_kernel,
        out_shape=(jax.ShapeDtypeStruct((n, Z32_W), f32), jax.ShapeDtypeStruct((n, Z16_W), bf16)),
        grid=(n // tm, (Z32_W + Z16_W) // IN_TN),
        in_specs=[pl.BlockSpec((tm, D_MODEL), lambda i, j: (i, 0)),
                  pl.BlockSpec((1, D_MODEL), lambda i, j: (0, 0)),
                  pl.BlockSpec((D_MODEL, IN_TN), lambda i, j: (0, j))],
        out_specs=(pl.BlockSpec((tm, IN_TN), lambda i, j: (i, jnp.minimum(j, N32_TILES - 1))),
                   pl.BlockSpec((tm, IN_TN), lambda i, j: (i, jnp.maximum(j - N32_TILES, 0)))),
        scratch_shapes=[pltpu.VMEM((tm, D_MODEL), bf16)],
        compiler_params=_params(("parallel", "arbitrary")),
        name="inproj",
    )(x2d, g_row, w_packed)


def _lru_gates(xc, wa_ref, ba_ref, wx_ref, bx_ref, lam_ref):
    xb = xc.astype(bf16)
    ra, ri = [], []
    for c in range(BRANCH_W // MXU_W):
        sl = slice(c * MXU_W, (c + 1) * MXU_W)
        ra.append(_dot(xb[:, sl], wa_ref[c]))
        ri.append(_dot(xb[:, sl], wx_ref[c]))
    r = jax.nn.sigmoid(jnp.concatenate(ra, axis=-1) + ba_ref[...])
    i = jax.nn.sigmoid(jnp.concatenate(ri, axis=-1) + bx_ref[...])
    nl = -lam_ref[...]
    softplus = jnp.maximum(nl, 0.0) + jnp.log1p(jnp.exp(-jnp.abs(nl)))
    log_a = -LRU_C * r * softplus
    a = jnp.exp(log_a)
    b = jnp.sqrt(1.0 - a * a) * (i * xc)
    return a, b


def _lru_kernel(x_ref, g_ref, cw_ref, cb_ref, wa_ref, ba_ref, wx_ref, bx_ref, lam_ref,
                zb_ref, tail_ref, h_ref, xs_ref, a_ref, b_ref, hc_ref):
    tt = x_ref.shape[1]

    @pl.when(pl.program_id(1) == 0)
    def _():
        xs_ref[0:8, :] = jnp.zeros((8, BRANCH_W), f32)
        hc_ref[...] = jnp.zeros_like(hc_ref)

    x = x_ref[0]
    xs_ref[8:, :] = x
    xc = cb_ref[...] + x * cw_ref[CONV_W - 1:CONV_W, :]
    for k in range(CONV_W - 1):
        xc = xc + xs_ref[pl.ds(8 - (CONV_W - 1 - k), tt), :] * cw_ref[k:k + 1, :]
    xs_ref[0:8, :] = x[tt - 8:, :]
    tail_ref[0] = x[tt - 8:, :]

    a, b = _lru_gates(xc, wa_ref, ba_ref, wx_ref, bx_ref, lam_ref)
    a_ref[...] = a
    b_ref[...] = b
    row = lax.broadcasted_iota(jnp.int32, (8, BRANCH_W), 0)

    def body(i, h):
        r0 = pl.multiple_of(i * 8, 8)
        av = a_ref[pl.ds(r0, 8), :]
        bv = b_ref[pl.ds(r0, 8), :]
        for d in (1, 2, 4):
            a_s = jnp.where(row >= d, pltpu.roll(av, d, 0), 1.0)
            b_s = jnp.where(row >= d, pltpu.roll(bv, d, 0), 0.0)
            bv = av * b_s + bv
            av = av * a_s
        hs = bv + av * h
        b_ref[pl.ds(r0, 8), :] = hs
        return hs[7:8, :]

    h = lax.fori_loop(0, tt // 8, body, hc_ref[...])
    hc_ref[...] = h
    h_ref[0] = h
    zb_ref[0] = (b_ref[...] * _silu(g_ref[0])).astype(bf16)


def _lru_prompt(z32, z16, cw, cb, wa, ba, wx, bx, lam):
    b, s, _ = z32.shape
    tt = min(s, 512)
    row = lambda: pl.BlockSpec((1, BRANCH_W), lambda i, t: (0, 0))
    bd = lambda: pl.BlockSpec((BRANCH_W // MXU_W, MXU_W, MXU_W), lambda i, t: (0, 0, 0))
    return pl.pallas_call(
        _lru_kernel,
        out_shape=(jax.ShapeDtypeStruct((b, s, BRANCH_W), bf16),
                   jax.ShapeDtypeStruct((b, 8, BRANCH_W), f32),
                   jax.ShapeDtypeStruct((b, 1, BRANCH_W), f32)),
        grid=(b, s // tt),
        in_specs=[pl.BlockSpec((1, tt, BRANCH_W), lambda i, t: (i, t, C_LRU_X // BRANCH_W)),
                  pl.BlockSpec((1, tt, BRANCH_W), lambda i, t: (i, t, C_LRU_G // BRANCH_W)),
                  pl.BlockSpec((CONV_W, BRANCH_W), lambda i, t: (0, 0)),
                  row(), bd(), row(), bd(), row(), row()],
        out_specs=(pl.BlockSpec((1, tt, BRANCH_W), lambda i, t: (i, t, 0)),
                   pl.BlockSpec((1, 8, BRANCH_W), lambda i, t: (i, 0, 0)),
                   pl.BlockSpec((1, 1, BRANCH_W), lambda i, t: (i, 0, 0))),
        scratch_shapes=[pltpu.VMEM((tt + 8, BRANCH_W), f32), pltpu.VMEM((tt, BRANCH_W), f32),
                        pltpu.VMEM((tt, BRANCH_W), f32), pltpu.VMEM((1, BRANCH_W), f32)],
        compiler_params=_params(("parallel", "arbitrary")),
        name="lru_prompt",
    )(z32, z16, cw, cb, wa, ba, wx, bx, lam)


def _pool_kernel(x_ref, g_ref, wp_ref, sc_ref, zb_ref, tail_ref, xs_ref):
    tt = x_ref.shape[1]
    t = pl.program_id(1)

    @pl.when(t == 0)
    def _():
        xs_ref[0:16, :] = jnp.zeros((16, BRANCH_W), f32)

    x = x_ref[0]
    xs_ref[16:, :] = x
    pos1 = t * tt + 1 + lax.broadcasted_iota(jnp.int32, (tt, POOL_GD), 0)
    outs = []
    for gi, w in enumerate(POOL_WINDOWS):
        sl = slice(gi * POOL_GD, (gi + 1) * POOL_GD)
        s = xs_ref[:, sl]
        sh = 1
        while sh < w:
            s = s + pltpu.roll(s, sh, 0)
            sh *= 2
        cnt = jnp.minimum(w, pos1).astype(f32)
        pooled = s[16:, :] / cnt - x[:, sl]
        outs.append(_dot(pooled.astype(bf16), wp_ref[gi]))
    y = jnp.concatenate(outs, axis=-1) * sc_ref[...]
    zb_ref[0] = (y * _silu(g_ref[0])).astype(bf16)
    xs_ref[0:16, :] = x[tt - 16:, :]
    tail_ref[0] = x[tt - 16:, :]


def _pool_prompt(z32, z16, wp, sc):
    b, s, _ = z32.shape
    tt = min(s, 512)
    return pl.pallas_call(
        _pool_kernel,
        out_shape=(jax.ShapeDtypeStruct((b, s, BRANCH_W), bf16),
                   jax.ShapeDtypeStruct((b, 16, BRANCH_W), f32)),
        grid=(b, s // tt),
        in_specs=[pl.BlockSpec((1, tt, BRANCH_W), lambda i, t: (i, t, C_POOL_X // BRANCH_W)),
                  pl.BlockSpec((1, tt, BRANCH_W), lambda i, t: (i, t, C_POOL_G // BRANCH_W)),
                  pl.BlockSpec((len(POOL_WINDOWS), POOL_GD, POOL_GD), lambda i, t: (0, 0, 0)),
                  pl.BlockSpec((1, BRANCH_W), lambda i, t: (0, 0))],
        out_specs=(pl.BlockSpec((1, tt, BRANCH_W), lambda i, t: (i, t, 0)),
                   pl.BlockSpec((1, 16, BRANCH_W), lambda i, t: (i, 0, 0))),
        scratch_shapes=[pltpu.VMEM((tt + 16, BRANCH_W), f32)],
        compiler_params=_params(("parallel", "arbitrary")),
        name="pool_prompt",
    )(z32, z16, wp, sc)


def _mix_sample_kernel(past_len, lx_ref, lg_ref, px_ref, pg_ref, conv_ref, h0_ref, pbuf_ref,
                       cw_ref, cb_ref, wa_ref, ba_ref, wx_ref, bx_ref, lam_ref, wp_ref, sc_ref,
                       zl_ref, zp_ref, h_ref):
    x = lx_ref[...]
    xc = cb_ref[...] + x * cw_ref[CONV_W - 1:CONV_W, :]
    for k in range(CONV_W - 1):
        xc = xc + conv_ref[k] * cw_ref[k:k + 1, :]
    a, b = _lru_gates(xc, wa_ref, ba_ref, wx_ref, bx_ref, lam_ref)
    h = a * h0_ref[...] + b
    h_ref[...] = h
    zl_ref[...] = (h * _silu(lg_ref[...])).astype(bf16)

    px = px_ref[...]
    outs = []
    for gi, w in enumerate(POOL_WINDOWS):
        sl = slice(gi * POOL_GD, (gi + 1) * POOL_GD)
        s = px[:, sl]
        for k in range(1, w):
            s = s + pbuf_ref[POOL_BUF - k][:, sl]
        cnt = float(min(w, past_len + 1))
        pooled = s / cnt - px[:, sl]
        outs.append(_dot(pooled.astype(bf16), wp_ref[gi]))
    y = jnp.concatenate(outs, axis=-1) * sc_ref[...]
    zp_ref[...] = (y * _silu(pg_ref[...])).astype(bf16)


def _mix_sample(z32, z16, conv_t, h0, pbuf_t, cw, cb, wa, ba, wx, bx, lam, wp, sc, past_len):
    n = z32.shape[0]
    col = lambda c: pl.BlockSpec((n, BRANCH_W), lambda i: (0, c // BRANCH_W))
    full = lambda a: pl.BlockSpec(a.shape, lambda i: (0,) * a.ndim)
    args = (conv_t, h0, pbuf_t, cw, cb, wa, ba, wx, bx, lam, wp, sc)
    return pl.pallas_call(
        functools.partial(_mix_sample_kernel, past_len),
        out_shape=(jax.ShapeDtypeStruct((n, BRANCH_W), bf16), jax.ShapeDtypeStruct((n, BRANCH_W), bf16),
                   jax.ShapeDtypeStruct((n, BRANCH_W), f32)),
        grid=(1,),
        in_specs=[col(C_LRU_X), col(C_LRU_G), col(C_POOL_X), col(C_POOL_G)] + [full(a) for a in args],
        out_specs=(pl.BlockSpec((n, BRANCH_W), lambda i: (0, 0)),) * 3,
        compiler_params=_params(("arbitrary",)),
        name="mix_sample",
    )(z32, z16, z32, z16, *args)


def _compress_kernel(tab_ref, k_ref, v_ref, wk_ref, wv_ref, fk_ref, sk_ref, fv_ref, sv_ref):
    del tab_ref
    p = pl.program_id(1)
    chunks = PAGE_SIZE // CMP_STRIDE

    @pl.when(p == 0)
    def _():
        sk_ref[...] = jnp.zeros_like(sk_ref)
        sv_ref[...] = jnp.zeros_like(sv_ref)

    for src, w_ref, f_ref, s_ref in ((k_ref, wk_ref, fk_ref, sk_ref), (v_ref, wv_ref, fv_ref, sv_ref)):
        tile = src[0]
        first = jnp.sum((tile * w_ref[0]).reshape(chunks, CMP_STRIDE, KV_W), axis=1)
        second = jnp.sum((tile * w_ref[1]).reshape(chunks, CMP_STRIDE, KV_W), axis=1)
        for m in range(chunks):
            f_ref[0, m % 4, pl.ds(2 * p + m // 4, 1), :] = first[m:m + 1, :]
            if m >= 1:
                s_ref[0, (m - 1) % 4, pl.ds(2 * p + (m - 1) // 4, 1), :] = second[m:m + 1, :]
            else:
                @pl.when(p > 0)
                def _():
                    s_ref[0, 3, pl.ds(2 * p - 1, 1), :] = second[0:1, :]


def _compress(table, src_k, src_v, col_k, col_v, wk, wv, nb):
    npg = table.shape[0] // nb
    out = jax.ShapeDtypeStruct((nb, 4, N_CMP // 4, KV_W), f32)
    ospec = lambda: pl.BlockSpec((1, 4, N_CMP // 4, KV_W), lambda b, p, tab: (b, 0, 0, 0))
    return pl.pallas_call(
        _compress_kernel,
        out_shape=(out,) * 4,
        grid_spec=pltpu.PrefetchScalarGridSpec(
            num_scalar_prefetch=1,
            grid=(nb, npg),
            in_specs=[pl.BlockSpec((1, PAGE_SIZE, KV_W), lambda b, p, tab: (tab[b * npg + p], 0, col_k)),
                      pl.BlockSpec((1, PAGE_SIZE, KV_W), lambda b, p, tab: (tab[b * npg + p], 0, col_v)),
                      pl.BlockSpec((2, PAGE_SIZE, KV_W), lambda b, p, tab: (0, 0, 0)),
                      pl.BlockSpec((2, PAGE_SIZE, KV_W), lambda b, p, tab: (0, 0, 0))],
            out_specs=(ospec(), ospec(), ospec(), ospec())),
        compiler_params=_params(("parallel", "arbitrary")),
        name="compress",
    )(table, src_k, src_v, wk, wv)


def _chunk_maps():
    n_chunk = PP * PAGE_SIZE // CMP_STRIDE
    rows = n_chunk // 4
    chunk_of = np.arange(PP * PAGE_SIZE) // CMP_STRIDE
    first = np.zeros((n_chunk, PP * PAGE_SIZE), np.float32)
    second = np.zeros((n_chunk + 8, PP * PAGE_SIZE), np.float32)
    for i in range(4):
        for jj in range(rows):
            first[i * rows + jj] = chunk_of == 4 * jj + i
            second[i * rows + jj] = chunk_of == 4 * jj + i + 1
    second[n_chunk] = chunk_of == 0
    return jnp.asarray(first, bf16), jnp.asarray(second, bf16)


def _page_copies(tab_ref, layer, step, n_pages, pools, bufs, sems, wait):
    slot = step % 2
    for i in range(n_pages):
        page = 0 if wait else tab_ref[step * n_pages + i]
        for pool, buf, sem in zip(pools, bufs, sems):
            copy = pltpu.make_async_copy(pool.at[layer, page],
                                         buf.at[slot, i], sem.at[slot])
            if wait:
                copy.wait()
            else:
                copy.start()


def _side_by_side(buf, slot):
    return jnp.concatenate([buf[slot, i] for i in range(buf.shape[1])], axis=1)


def _paged_step(tab_ref, layer, n_pages, pools, bufs, sems):
    step = pl.program_id(0) * pl.num_programs(1) + pl.program_id(1)
    n_steps = pl.num_programs(0) * pl.num_programs(1)

    @pl.when(step == 0)
    def _():
        _page_copies(tab_ref, layer, step, n_pages, pools, bufs, sems, wait=False)

    @pl.when(step + 1 < n_steps)
    def _():
        _page_copies(tab_ref, layer, step + 1, n_pages, pools, bufs, sems, wait=False)

    _page_copies(tab_ref, layer, step, n_pages, pools, bufs, sems, wait=True)
    return step % 2


def _compress_t_kernel(layer, tab_ref, k_hbm, v_hbm, wk_ref, wv_ref, ea_ref, eb_ref,
                       fk_ref, sk_ref, fv_ref, sv_ref, kbuf, vbuf, ksem, vsem):
    slot = _paged_step(tab_ref, layer, PP, (k_hbm, v_hbm), (kbuf, vbuf), (ksem, vsem))
    ps = pl.program_id(1)
    rows = PP * PAGE_SIZE // CMP_STRIDE // 4
    r0 = pl.multiple_of(ps * rows, rows)
    for buf, w_ref, f_ref, s_ref in ((kbuf, wk_ref, fk_ref, sk_ref), (vbuf, wv_ref, fv_ref, sv_ref)):
        pages = _side_by_side(buf, slot)
        a1 = (pages * w_ref[0]).astype(bf16)
        a2 = (pages * w_ref[1]).astype(bf16)
        first = _nt(ea_ref[...], a1)
        second = _nt(eb_ref[...], a2)
        for i in range(4):
            f_ref[0, i, pl.ds(r0, rows), :] = first[i * rows:(i + 1) * rows]
            s_ref[0, i, pl.ds(r0, rows), :] = second[i * rows:(i + 1) * rows]

        @pl.when(ps > 0)
        def _():
            s_ref[0, 3, pl.ds(r0 - 1, 1), :] = second[4 * rows:4 * rows + 1]


def _compress_t(table, layer, cache_k, cache_v, wk, wv, nb):
    npg = table.shape[0] // nb
    ea, eb = _chunk_maps()
    out = jax.ShapeDtypeStruct((nb, 4, N_CMP // 4, KV_W), f32)
    ospec = lambda: pl.BlockSpec((1, 4, N_CMP // 4, KV_W), lambda b, p, tab: (b, 0, 0, 0))
    full = lambda a: pl.BlockSpec(a.shape, lambda b, p, tab: (0,) * a.ndim)
    hbm = lambda: pl.BlockSpec(memory_space=pl.ANY)
    wk, wv = jnp.tile(wk, (1, 1, PP)), jnp.tile(wv, (1, 1, PP))
    return pl.pallas_call(
        functools.partial(_compress_t_kernel, layer),
        out_shape=(out,) * 4,
        grid_spec=pltpu.PrefetchScalarGridSpec(
            num_scalar_prefetch=1,
            grid=(nb, npg // PP),
            in_specs=[hbm(), hbm(), full(wk), full(wv), full(ea), full(eb)],
            out_specs=(ospec(), ospec(), ospec(), ospec()),
            scratch_shapes=[pltpu.VMEM((2, PP, KV_W, PAGE_SIZE), f32), pltpu.VMEM((2, PP, KV_W, PAGE_SIZE), f32),
                            pltpu.SemaphoreType.DMA((2,)), pltpu.SemaphoreType.DMA((2,))]),
        compiler_params=_params(("arbitrary", "arbitrary")),
        name="compress_t",
    )(table, cache_k, cache_v, wk, wv, ea, eb)


def _finish_compress(f_ref, s_ref, phi_ref):
    blk = (f_ref[0] + s_ref[0]).reshape(N_CMP, KV_W)
    return _dot(blk.astype(bf16), phi_ref[...]).astype(bf16)


def _cmp_end(shape, axis):
    col = lax.broadcasted_iota(jnp.int32, shape, axis)
    n = ((col & (N_BLK - 1)) << 2) + (col >> 7)
    return n * CMP_STRIDE + (CMP_BLK - 1)


def _pick_blocks(score, n_pick):
    lane = lax.broadcasted_iota(jnp.int32, score.shape, 1).astype(f32)
    bias = jnp.full(score.shape, NEG, f32)
    for _ in range(n_pick):
        m = jnp.max(score, axis=-1, keepdims=True)
        first = jnp.min(jnp.where(score == m, lane, float(N_BLK)), axis=-1, keepdims=True)
        hit = lane == first
        bias = jnp.where(hit, 0.0, bias)
        score = jnp.where(hit, -jnp.inf, score)
    return bias


def _block_expand(first_block, n_keys):
    j = lax.broadcasted_iota(jnp.int32, (N_BLK, n_keys), 0)
    c = lax.broadcasted_iota(jnp.int32, (N_BLK, n_keys), 1)
    return jnp.where(j == first_block + (c >> 6), 1.0, 0.0).astype(bf16)


CQ = Q_PER_KV * TQ
WIN_T = WINDOW // TQ + 1
N_PARTS = 9
TILE_COL = HEAD_DIM + 6
V_ROWS = HEAD_DIM + 16


def _pick_blocks_t(score, forced, n_pick):
    jrow = lax.broadcasted_iota(jnp.int32, score.shape, 0).astype(f32)
    bias = jnp.where(forced, 0.0, NEG)
    score = jnp.where(forced, -jnp.inf, score)
    for _ in range(n_pick):
        m = jnp.max(score, axis=0, keepdims=True)
        first = jnp.min(jnp.where(score == m, jrow, float(N_BLK)), axis=0, keepdims=True)
        hit = jrow == first
        bias = jnp.where(hit, 0.0, bias)
        score = jnp.where(hit, -jnp.inf, score)
    return bias


def _nsa_prompt_kernel(q_ref, srow_ref, slope_ref, bg_ref, ng_ref, fk_ref, sk_ref, fv_ref, sv_ref,
                       phik_ref, phivt_ref, cpos_ref, ksa_ref, vst_ref, kwa_ref, vwt_ref, o_ref,
                       kca_scr, vct_scr, qa_scr, oc_scr, m_scr, acc_scr, negc_scr, negd_scr, negw_scr):
    qi = pl.program_id(1)
    s0 = qi * TQ

    @pl.when(qi == 0)
    def _():
        kc = _finish_compress(fk_ref, sk_ref, phik_ref)
        for g in range(KV_HEADS):
            kca_scr[g] = jnp.concatenate([kc[:, g * HEAD_DIM:(g + 1) * HEAD_DIM], cpos_ref[...]], axis=1)
        blk_v = (fv_ref[0] + sv_ref[0]).reshape(N_CMP, KV_W).astype(bf16)
        vct_scr[...] = _nt(phivt_ref[...], blk_v).astype(bf16)

    t_of = lambda shape: s0 + (lax.broadcasted_iota(jnp.int32, shape, 1) & (TQ - 1))
    q_t = (q_ref[0].astype(f32) * Q_SCALE).T.astype(bf16)

    negc_scr[...] = jnp.where(_cmp_end((N_CMP, CQ), 0) <= t_of((N_CMP, CQ)), 0.0, NEG)
    sees_block = t_of((1, CQ)) >= CMP_BLK - 1
    t_q = t_of((N_BLK, TQ))
    jr = lax.broadcasted_iota(jnp.int32, (N_BLK, TQ), 0)
    jb = t_q >> 6
    ok_b = (jr << 6) <= t_q
    forced = (jr == 0) | (jr == jb) | (jr == jb - 1)
    tile_any = [jnp.full((N_BLK // 8, 1), NEG, f32) for _ in LOOP_GROUPS]
    for g in range(KV_HEADS):
        q_g = jnp.concatenate([q_t[(g * Q_PER_KV + r) * HEAD_DIM:(g * Q_PER_KV + r + 1) * HEAD_DIM]
                               for r in range(Q_PER_KV)], axis=1)
        qa = jnp.concatenate([q_g, srow_ref[g]], axis=0)
        st = _dot(kca_scr[g], qa) + negc_scr[...]
        e = jnp.exp2(st - jnp.max(st, axis=0, keepdims=True))
        pt = e * jnp.where(sees_block, 1.0 / jnp.sum(e, axis=0, keepdims=True), 0.0)
        oc_scr[g] = _dot(vct_scr[g * HEAD_DIM:(g + 1) * HEAD_DIM], pt.astype(bf16))
        ps = None
        for i in range(4):
            for r in range(Q_PER_KV):
                slab = pt[i * N_BLK:(i + 1) * N_BLK, r * TQ:(r + 1) * TQ]
                ps = slab if ps is None else ps + slab
        bias = _pick_blocks_t(jnp.where(ok_b, ps, NEG), forced, N_SEL - 3)
        qa_scr[g] = jnp.concatenate([qa, jnp.concatenate([bias.astype(bf16)] * Q_PER_KV, axis=1)], axis=0)
        any_t = jnp.max(bias.reshape(N_BLK // 8, 8, TQ), axis=1)
        li = [g in grp for grp in LOOP_GROUPS].index(True)
        tile_any[li] = jnp.maximum(tile_any[li], jnp.max(any_t, axis=1, keepdims=True))
    kt_row = lax.broadcasted_iota(jnp.int32, tile_any[0].shape, 0)
    tile_bits = [jnp.sum(jnp.where(ta == 0.0, 1 << kt_row, 0)) for ta in tile_any]

    m_scr[...] = jnp.full(m_scr.shape, NEG, f32)
    acc_scr[...] = jnp.zeros_like(acc_scr)

    def sel_tile(kt, diagonal, groups, n_keys=TK):
        k0 = pl.multiple_of(kt * TK, TK)
        tile_off = (k0 - s0).astype(f32)
        key_r = lax.broadcasted_iota(jnp.int32, (n_keys, N_BLK), 0)
        blk_c = lax.broadcasted_iota(jnp.int32, (n_keys, N_BLK), 1)
        expand = jnp.where(blk_c == kt * (TK // SEL_BLK) + (key_r >> 6), 1.0, 0.0).astype(bf16)
        k_aug = ksa_ref[0, pl.ds(k0, n_keys), :]
        v_t = vst_ref[0, kt][:, 0:n_keys]
        for g in groups:
            lhs = jnp.concatenate([k_aug[:, g * QA_W:(g + 1) * QA_W], expand], axis=1)
            st = _dot(lhs, qa_scr[g])
            if diagonal:
                st = st + negd_scr[qi % (TK // TQ), 0:n_keys]
            c = slope_ref[g] * tile_off
            m_old = m_scr[g]
            m_new = jnp.maximum(m_old, jnp.max(st, axis=0, keepdims=True) + c)
            alpha = jnp.exp2(m_old - m_new)
            pt = jnp.exp2((st - (m_new - c)).astype(bf16))
            acc_scr[g] = alpha * acc_scr[g] + _dot(v_t[g * V_ROWS:(g + 1) * V_ROWS], pt)
            m_scr[g] = m_new

    last = s0 // TK

    @pl.when(qi < TK // TQ)
    def _():
        key_pos = lax.broadcasted_iota(jnp.int32, (TK, CQ), 0)
        negd_scr[qi] = jnp.where(key_pos <= t_of((TK, CQ)), 0.0, NEG)

    for groups, bits in zip(LOOP_GROUPS, tile_bits):
        def sel_body(kt, carry, groups=groups, bits=bits):
            @pl.when(((bits >> kt) & 1) == 1)
            def _():
                sel_tile(kt, False, groups)
            return carry

        lax.fori_loop(0, last, sel_body, 0)
    for sub in range(TK // TQ):
        @pl.when(qi % (TK // TQ) == sub)
        def _(sub=sub):
            sel_tile(last, True, range(KV_HEADS), (sub + 1) * TQ)

    w_tile = jnp.maximum(qi - WINDOW // TQ, 0)
    n_win = WIN_T * TQ

    @pl.when(qi <= WINDOW // TQ)
    def _():
        dist_w = t_of((n_win, CQ)) - (w_tile * TQ + lax.broadcasted_iota(jnp.int32, (n_win, CQ), 0))
        negw_scr[...] = jnp.where((dist_w >= 0) & (dist_w < WINDOW), 0.0, NEG)

    lane_w = lax.broadcasted_iota(jnp.int32, (n_win, QA_W), 1)
    slab_off = (lax.broadcasted_iota(jnp.int32, (n_win, QA_W), 0) // TQ * TQ).astype(f32).astype(bf16)
    in_tile_col = (lane_w >= TILE_COL) & (lane_w < TILE_COL + 3)
    kw_all = kwa_ref[0, pl.ds(pl.multiple_of(w_tile * TQ, TQ), n_win), :]
    vw_t = jnp.concatenate([vwt_ref[0, w_tile + i] for i in range(WIN_T)], axis=1)
    gates = jax.nn.sigmoid(bg_ref[0].T)
    blocks = []
    for g in range(KV_HEADS):
        lhs = jnp.where(in_tile_col, slab_off, kw_all[:, g * QA_W:(g + 1) * QA_W])
        st = _dot(lhs, qa_scr[g, 0:QA_W]) + negw_scr[...]
        pt = jnp.exp2((st - jnp.max(st, axis=0, keepdims=True)).astype(bf16))
        win = _dot(vw_t[g * V_ROWS:(g + 1) * V_ROWS], pt)
        o_w = win[0:HEAD_DIM] * (1.0 / win[HEAD_DIM:HEAD_DIM + 1])
        o_s = acc_scr[g, 0:HEAD_DIM] * (1.0 / acc_scr[g, HEAD_DIM:HEAD_DIM + 1])
        o_c = oc_scr[g]
        heads = []
        for r in range(Q_PER_KV):
            h = g * Q_PER_KV + r
            cols = slice(r * TQ, (r + 1) * TQ)
            heads.append(gates[3 * h:3 * h + 1] * o_c[:, cols] + gates[3 * h + 1:3 * h + 2] * o_s[:, cols]
                         + gates[3 * h + 2:3 * h + 3] * o_w[:, cols])
        for r in range(0, Q_PER_KV, 2):
            blocks.append(jnp.concatenate(heads[r:r + 2], axis=0).T)
    y = jnp.concatenate(blocks, axis=-1)
    o_ref[0] = (y * _silu(ng_ref[0])).astype(bf16)


def _nsa_prompt(z3, z16, fk, sk, fv, sv, phik, phivt):
    b, s, _ = z3.shape
    nq = s // TQ
    parts = _slope_parts().reshape(KV_HEADS, Q_PER_KV, POS_ROWS)
    srow = jnp.asarray(np.repeat(parts.transpose(0, 2, 1), TQ, axis=2), bf16)
    slope = jnp.asarray(np.repeat(np.asarray(SLOPES_LOG2, np.float32).reshape(KV_HEADS, 1, Q_PER_KV), TQ, axis=2))
    ksa = _key_aug(_kv_seg(z3, 2), TK)
    kwa = _key_aug(_kv_seg(z3, 4), TQ)
    vst = _value_tiles(_kv_seg(z3, 3), TK)
    vwt = _value_tiles(_kv_seg(z3, 5), TQ)
    cpos = _cmp_pos_cols()

    once = pl.Buffered(1)
    cmp = lambda: pl.BlockSpec((1, 4, N_CMP // 4, KV_W), lambda i, t: (i, 0, 0, 0))
    full = lambda a: pl.BlockSpec(a.shape, lambda i, t: (0,) * a.ndim)
    return pl.pallas_call(
        _nsa_prompt_kernel,
        out_shape=jax.ShapeDtypeStruct((b, s, BRANCH_W), bf16),
        grid=(b, nq),
        in_specs=[pl.BlockSpec((1, TQ, BRANCH_W), lambda i, t: (i, t, C_Q // BRANCH_W)),
                  full(srow), full(slope),
                  pl.BlockSpec((1, TQ, BG_W), lambda i, t: (i, t, C_BG // BG_W)),
                  pl.BlockSpec((1, TQ, BRANCH_W), lambda i, t: (i, t, C_NSA_G // BRANCH_W)),
                  cmp(), cmp(), cmp(), cmp(), full(phik), full(phivt), full(cpos),
                  pl.BlockSpec((1, s, KV_HEADS * QA_W), lambda i, t: (i, 0, 0), pipeline_mode=once),
                  pl.BlockSpec((1, s // TK, KV_HEADS * V_ROWS, TK), lambda i, t: (i, 0, 0, 0), pipeline_mode=once),
                  pl.BlockSpec((1, s, KV_HEADS * QA_W), lambda i, t: (i, 0, 0), pipeline_mode=once),
                  pl.BlockSpec((1, nq, KV_HEADS * V_ROWS, TQ), lambda i, t: (i, 0, 0, 0), pipeline_mode=once)],
        out_specs=pl.BlockSpec((1, TQ, BRANCH_W), lambda i, t: (i, t, 0)),
        scratch_shapes=[pltpu.VMEM((KV_HEADS, N_CMP, QA_W), bf16), pltpu.VMEM((KV_W, N_CMP), bf16),
                        pltpu.VMEM((KV_HEADS, QA_W + N_BLK, CQ), bf16),
                        pltpu.VMEM((KV_HEADS, HEAD_DIM, CQ), f32),
                        pltpu.VMEM((KV_HEADS, 1, CQ), f32),
                        pltpu.VMEM((KV_HEADS, V_ROWS, CQ), f32),
                        pltpu.VMEM((N_CMP, CQ), f32), pltpu.VMEM((TK // TQ, TK, CQ), f32),
                        pltpu.VMEM((WIN_T * TQ, CQ), f32)],
        compiler_params=_params(("parallel", "arbitrary")),
        name="nsa_prompt",
    )(z16, srow, slope, z3, z16, fk, sk, fv, sv, phik, phivt, cpos, ksa, vst, kwa, vwt)


def _by_group(fn):
    hg = lax.broadcasted_iota(jnp.int32, (N_HEADS, 1), 0) >> 2
    out = fn(0)
    for g in range(1, KV_HEADS):
        out = jnp.where(hg == g, fn(g), out)
    return out


def _nsa_sample_kernel(past_len, layer, tab_ref, q_ref, bg_ref, ng_ref, slope_ref,
                       fk_ref, sk_ref, fv_ref, sv_ref, phik_ref, phiv_ref,
                       ksn_ref, vsn_ref, kwn_ref, vwn_ref, bk_ref, bv_ref, kp_hbm, vp_hbm, o_ref,
                       q_scr, sel_scr, oc_scr, ow_scr, m_scr, l_scr, acc_scr, kbuf, vbuf, ksem, vsem):
    slot = _paged_step(tab_ref, layer, PP_SEL, (kp_hbm, vp_hbm), (kbuf, vbuf), (ksem, vsem))
    p = pl.program_id(1)
    slope = slope_ref[:, 0:1]
    gsl = lambda g: slice(g * HEAD_DIM, (g + 1) * HEAD_DIM)

    def per_head(row_ref):
        row = row_ref[0]
        return _by_group(lambda g: jnp.broadcast_to(row[:, gsl(g)], (N_HEADS, HEAD_DIM)))

    @pl.when(p == 0)
    def _():
        qrow = q_ref[0]
        q16 = jnp.concatenate([qrow[:, h * HEAD_DIM:(h + 1) * HEAD_DIM] for h in range(N_HEADS)], axis=0)
        q_scr[...] = q16
        q16f = q16.astype(f32)
        kc = _finish_compress(fk_ref, sk_ref, phik_ref)
        vc = _finish_compress(fv_ref, sv_ref, phiv_ref)

        dist_c = past_len - _cmp_end((1, N_CMP), 1)
        ok_c = dist_c >= 0
        s = _by_group(lambda g: _nt(q16, kc[:, gsl(g)])) - slope * dist_c.astype(f32)
        s = jnp.where(ok_c, s, NEG)
        e = jnp.exp(s - jnp.max(s, axis=-1, keepdims=True))
        pc = jnp.where(ok_c, e * (1.0 / jnp.sum(e, axis=-1, keepdims=True)), 0.0)
        pcb = pc.astype(bf16)
        oc_scr[...] = _by_group(lambda g: _dot(pcb, vc[:, gsl(g)]))
        ps16 = (pc[:, 0:N_BLK] + pc[:, N_BLK:2 * N_BLK]) + (pc[:, 2 * N_BLK:3 * N_BLK] + pc[:, 3 * N_BLK:])
        ps = jnp.concatenate([jnp.sum(ps16[g * Q_PER_KV:(g + 1) * Q_PER_KV], axis=0, keepdims=True)
                              for g in range(KV_HEADS)] + [jnp.zeros((8 - KV_HEADS, N_BLK), f32)], axis=0)
        jl = lax.broadcasted_iota(jnp.int32, (8, N_BLK), 1)
        jb = past_len // SEL_BLK
        forced = jnp.where((jl == 0) | (jl == jb) | (jl == jb - 1), FORCE, 0.0)
        bias8 = _pick_blocks(ps + forced, N_SEL - 1)
        sel_scr[...] = jnp.concatenate(
            [jnp.broadcast_to(bias8[g:g + 1], (Q_PER_KV, N_BLK)) for g in range(KV_HEADS)], axis=0).astype(bf16)

        wb = bk_ref.shape[-1]
        dist_w = wb - lax.broadcasted_iota(jnp.int32, (1, wb), 1)
        ok_w = (dist_w >= 0) & (dist_w < WINDOW)
        bk = bk_ref[0, 0].astype(bf16)
        bv = bv_ref[0, 0].astype(bf16)
        s_buf = _by_group(lambda g: _dot(q16, bk[gsl(g)])) - slope * dist_w.astype(f32)
        s_buf = jnp.where(ok_w, s_buf, NEG)
        s_new = jnp.sum(q16f * per_head(kwn_ref), axis=-1, keepdims=True)
        m_w = jnp.maximum(jnp.max(s_buf, axis=-1, keepdims=True), s_new)
        e_buf = jnp.exp(s_buf - m_w)
        e_new = jnp.exp(s_new - m_w)
        ebb = e_buf.astype(bf16)
        num = _by_group(lambda g: _nt(ebb, bv[gsl(g)])) + e_new * per_head(vwn_ref)
        ow_scr[...] = num * (1.0 / (jnp.sum(e_buf, axis=-1, keepdims=True) + e_new))

        m_scr[...] = jnp.sum(q16f * per_head(ksn_ref), axis=-1, keepdims=True)
        l_scr[...] = jnp.ones_like(l_scr)
        acc_scr[...] = per_head(vsn_ref)

    q16 = q_scr[...]
    n_keys = PP_SEL * PAGE_SIZE
    kp = _side_by_side(kbuf, slot).astype(bf16)
    vp = _side_by_side(vbuf, slot).astype(bf16)
    dist = past_len - (p * n_keys + lax.broadcasted_iota(jnp.int32, (1, n_keys), 1))
    expand = _block_expand(p * (n_keys // SEL_BLK), n_keys)
    s = _by_group(lambda g: _dot(q16, kp[gsl(g)])) - slope * dist.astype(f32) + _dot(sel_scr[...], expand)
    m_old = m_scr[...]
    m_new = jnp.maximum(m_old, jnp.max(s, axis=-1, keepdims=True))
    alpha = jnp.exp(m_old - m_new)
    pr = jnp.exp(s - m_new)
    prb = pr.astype(bf16)
    l_scr[...] = alpha * l_scr[...] + jnp.sum(pr, axis=-1, keepdims=True)
    acc_scr[...] = alpha * acc_scr[...] + _by_group(lambda g: _nt(prb, vp[gsl(g)]))
    m_scr[...] = m_new

    @pl.when(p == pl.num_programs(1) - 1)
    def _():
        o_s = acc_scr[...] * (1.0 / l_scr[...])
        gates = jax.nn.sigmoid(bg_ref[0])
        lane = lax.broadcasted_iota(jnp.int32, (N_HEADS, BG_W), 1)
        h3 = 3 * lax.broadcasted_iota(jnp.int32, (N_HEADS, BG_W), 0)
        gate = lambda n: jnp.sum(jnp.where(lane == h3 + n, gates, 0.0), axis=-1, keepdims=True)
        y16 = gate(0) * oc_scr[...] + gate(1) * o_s + gate(2) * ow_scr[...]
        y = jnp.concatenate([y16[h:h + 1, :] for h in range(N_HEADS)], axis=-1)
        o_ref[0] = (y * _silu(ng_ref[0])).astype(bf16)


def _nsa_sample(table, layer, qb, z2, z2h, slopes, fk, sk, fv, sv, phik, phiv, pool_k, pool_v, buf_k, buf_v,
                past_len):
    nb = qb.shape[0]
    npg = table.shape[0] // nb
    z3 = z2.reshape(nb, 1, Z32_W)
    z3h = z2h.reshape(nb, 1, Z16_W)
    tok = lambda w, c: pl.BlockSpec((1, 1, w), lambda b, p, tab: (b, 0, c))
    cmp = lambda: pl.BlockSpec((1, 4, N_CMP // 4, KV_W), lambda b, p, tab: (b, 0, 0, 0))
    phi = lambda: pl.BlockSpec((KV_W, KV_W), lambda b, p, tab: (0, 0))
    wb = buf_k.shape[-1]
    win = lambda: pl.BlockSpec((1, 1, KV_W, wb), lambda b, p, tab: (layer, b, 0, 0))
    hbm = lambda: pl.BlockSpec(memory_space=pl.ANY)
    page_buf = lambda: pltpu.VMEM((2, PP_SEL, KV_W, PAGE_SIZE), f32)
    kvc = C_KV // KV_W
    return pl.pallas_call(
        functools.partial(_nsa_sample_kernel, past_len, layer),
        out_shape=jax.ShapeDtypeStruct((nb, 1, BRANCH_W), bf16),
        grid_spec=pltpu.PrefetchScalarGridSpec(
            num_scalar_prefetch=1,
            grid=(nb, npg // PP_SEL),
            in_specs=[tok(BRANCH_W, 0), tok(BG_W, C_BG // BG_W), tok(BRANCH_W, C_NSA_G // BRANCH_W),
                      pl.BlockSpec((N_HEADS, 128), lambda b, p, tab: (0, 0)),
                      cmp(), cmp(), cmp(), cmp(), phi(), phi(),
                      tok(KV_W, kvc + 2), tok(KV_W, kvc + 3), tok(KV_W, kvc + 4), tok(KV_W, kvc + 5),
                      win(), win(), hbm(), hbm()],
            out_specs=pl.BlockSpec((1, 1, BRANCH_W), lambda b, p, tab: (b, 0, 0)),
            scratch_shapes=[pltpu.VMEM((N_HEADS, HEAD_DIM), bf16), pltpu.VMEM((N_HEADS, N_BLK), bf16),
                            pltpu.VMEM((N_HEADS, HEAD_DIM), f32), pltpu.VMEM((N_HEADS, HEAD_DIM), f32),
                            pltpu.VMEM((N_HEADS, 1), f32), pltpu.VMEM((N_HEADS, 1), f32),
                            pltpu.VMEM((N_HEADS, HEAD_DIM), f32), page_buf(), page_buf(),
                            pltpu.SemaphoreType.DMA((2,)), pltpu.SemaphoreType.DMA((2,))]),
        compiler_params=_params(("arbitrary", "arbitrary")),
        name="nsa_sample",
    )(table, qb.reshape(nb, 1, BRANCH_W), z3, z3h, slopes, fk, sk, fv, sv, phik, phiv,
      z3, z3, z3, z3, buf_k, buf_v, pool_k, pool_v)


def _merge_kernel(zl_ref, zp_ref, zn_ref, m0_ref, m1_ref, m2_ref, wb_ref, wo_ref, g_ref, x_ref, y_ref):
    acc = None
    for n, (zz, mg) in enumerate(((zl_ref, m0_ref), (zp_ref, m1_ref), (zn_ref, m2_ref))):
        term = jax.nn.sigmoid(mg[...].astype(f32)) * _dot(zz[...], wb_ref[n])
        acc = term if acc is None else acc + term
    out = _dot(acc.astype(bf16), wo_ref[...])
    ms = jnp.mean(out * out, axis=-1, keepdims=True)
    y_ref[...] = x_ref[...] + out * lax.rsqrt(ms + EPS) * g_ref[...]


def _merge(zl, zp, zn, z16, wb, wo, g_row, x2d):
    n = x2d.shape[0]
    tm = min(n, 256)
    rowblk = lambda c: pl.BlockSpec((tm, D_MODEL), lambda i: (i, c))
    return pl.pallas_call(
        _merge_kernel,
        out_shape=jax.ShapeDtypeStruct((n, D_MODEL), f32),
        grid=(n // tm,),
        in_specs=[rowblk(0), rowblk(0), rowblk(0),
                  rowblk(C_MG // D_MODEL), rowblk(C_MG // D_MODEL + 1), rowblk(C_MG // D_MODEL + 2),
                  pl.BlockSpec((N_BRANCH, BRANCH_W, D_MODEL), lambda i: (0, 0, 0)),
                  pl.BlockSpec((D_MODEL, D_MODEL), lambda i: (0, 0)),
                  pl.BlockSpec((1, D_MODEL), lambda i: (0, 0)),
                  rowblk(0)],
        out_specs=rowblk(0),
        compiler_params=_params(("parallel",)),
        name="merge",
    )(zl, zp, zn, z16, z16, z16, wb, wo, g_row, x2d)


def _block_diag(w, per):
    n, d, _ = w.shape
    eye = jnp.eye(per, dtype=w.dtype)
    t = jnp.einsum('cpde,pq->cpdqe', w.reshape(n // per, per, d, d), eye)
    return t.reshape(n // per, per * d, per * d)


def _pack_w_in(w):
    seg = lambda i: w[:, i * BRANCH_W:(i + 1) * BRANCH_W]
    old_kv = 6 * BRANCH_W
    old_bg = old_kv + 6 * KV_W
    old_mg = old_bg + N_BRANCH * N_HEADS
    f32_part = jnp.concatenate([seg(0), seg(2), w[:, old_kv:old_bg], w[:, old_bg:old_mg]], axis=1)
    f32_part = jnp.pad(f32_part, ((0, 0), (0, Z32_W - f32_part.shape[1])))
    bf16_part = jnp.concatenate([seg(1), seg(3), seg(4), seg(5), w[:, old_mg:]], axis=1)
    return jnp.concatenate([f32_part, bf16_part], axis=1).astype(bf16)


def _tile_wpos(w_pos):
    halves = w_pos.reshape(2, CMP_STRIDE, HEAD_DIM)
    return jnp.tile(halves, (1, PAGE_SIZE // CMP_STRIDE, KV_HEADS))


def _tile_wpos_t(w_pos):
    halves = w_pos.reshape(2, CMP_STRIDE, HEAD_DIM).swapaxes(1, 2)
    return jnp.tile(halves, (1, KV_HEADS, PAGE_SIZE // CMP_STRIDE))


def _lanes_last(cache):
    d, n, rows = cache.shape[:3]
    return jnp.transpose(cache, (0, 1, 3, 4, 2)).reshape(d, n, KV_W, rows)


def _slope_parts():
    cols = np.zeros((N_HEADS, POS_ROWS), np.float32)
    rnd = lambda v: np.float32(np.float32(v).astype(bf16))
    for h, s in enumerate(SLOPES_LOG2):
        s1 = rnd(s)
        s2 = rnd(np.float32(s) - s1)
        s3 = rnd(np.float32(s) - s1 - s2)
        cols[h, 0:N_PARTS] = [s1, s2, s3] * 3
    return cols


def _split_pos(pos, shift):
    cols = np.zeros((pos.shape[0], POS_ROWS), np.float32)
    cols[:, 0:3] = ((pos >> shift) << shift)[:, None]
    cols[:, 3:6] = (pos & ((1 << shift) - 1))[:, None]
    return cols


def _key_aug(k_rows, tile):
    b, s, _ = k_rows.shape
    pos = jnp.asarray(_split_pos(np.arange(s) % tile, 4), bf16)
    kg = k_rows.astype(bf16).reshape(b, s, KV_HEADS, HEAD_DIM)
    posb = jnp.broadcast_to(pos[None, :, None, :], (b, s, KV_HEADS, POS_ROWS))
    return jnp.concatenate([kg, posb], axis=-1).reshape(b, s, KV_HEADS * QA_W)


def _value_tiles(v_rows, tile):
    b, s, _ = v_rows.shape
    vt = v_rows.astype(bf16).reshape(b, s // tile, tile, KV_HEADS, HEAD_DIM).transpose(0, 1, 3, 4, 2)
    ones = jnp.ones((b, s // tile, KV_HEADS, V_ROWS - HEAD_DIM, tile), bf16)
    return jnp.concatenate([vt, ones], axis=3).reshape(b, s // tile, KV_HEADS * V_ROWS, tile)


def _cmp_pos_cols():
    slot = np.arange(N_CMP)
    n = ((slot & (N_BLK - 1)) << 2) + (slot >> 7)
    return jnp.asarray(_split_pos(n * CMP_STRIDE + CMP_BLK - 1, 8), bf16)


def _kv_seg(z, i):
    return z[..., C_KV + i * KV_W:C_KV + (i + 1) * KV_W]


def kernel(x_prompt, x_sample, cache_cmp_k, cache_cmp_v, cache_sel_k, cache_sel_v, cache_win_k, cache_win_v, state_conv, state_lru, state_pool, page_table, g_pre, g_post, w_in, conv_w, conv_b, w_rg_a, b_rg_a, w_rg_x, b_rg_x, lru_lambda, w_pool, pool_scale, cmp_pos_k, cmp_phi_k, cmp_pos_v, cmp_phi_v, w_branch, w_out):
    depth = w_in.shape[0]
    bp, seq, _ = x_prompt.shape
    bs = x_sample.shape[0]
    n_pages = page_table.shape[1]
    past_len = n_pages * PAGE_SIZE
    n_phys = cache_cmp_k.shape[1]
    assert seq == N_BLK * SEL_BLK and past_len == N_BLK * SEL_BLK and x_sample.shape[1] == 1
    wb = cache_win_k.shape[2]

    table_s = page_table.reshape(-1).astype(jnp.int32)
    table_p = jnp.arange(bp * (seq // PAGE_SIZE), dtype=jnp.int32)
    slopes = jnp.broadcast_to(jnp.asarray(SLOPES, f32)[:, None], (N_HEADS, 128))
    row = lambda v: v.reshape(1, -1)
    cmp_kt, cmp_vt = _lanes_last(cache_cmp_k), _lanes_last(cache_cmp_v)
    sel_kt, sel_vt = _lanes_last(cache_sel_k), _lanes_last(cache_sel_v)
    win_kt, win_vt = _lanes_last(cache_win_k), _lanes_last(cache_win_v)

    xp = x_prompt.reshape(bp * seq, D_MODEL)
    xs = x_sample.reshape(bs, D_MODEL)
    pr = [[] for _ in range(9)]
    sm = [[] for _ in range(9)]
    for l in range(depth):
        w_packed = _pack_w_in(w_in[l])
        wa = _block_diag(w_rg_a[l], MXU_W // LRU_BD).astype(bf16)
        wx = _block_diag(w_rg_x[l], MXU_W // LRU_BD).astype(bf16)
        wp = w_pool[l].astype(bf16)
        phik = _block_diag(jnp.broadcast_to(cmp_phi_k[l], (KV_HEADS, HEAD_DIM, HEAD_DIM)), KV_HEADS)[0].astype(bf16)
        phiv = _block_diag(jnp.broadcast_to(cmp_phi_v[l], (KV_HEADS, HEAD_DIM, HEAD_DIM)), KV_HEADS)[0].astype(bf16)
        wpos_k = _tile_wpos(cmp_pos_k[l])
        wpos_v = _tile_wpos(cmp_pos_v[l])
        wbr = w_branch[l].astype(bf16)
        wo = w_out[l].astype(bf16)
        lru_w = (conv_w[l], row(conv_b[l]), wa, row(b_rg_a[l]), wx, row(b_rg_x[l]), row(lru_lambda[l]))

        z, z16 = _inproj(xp, row(g_pre[l]), w_packed)
        z3 = z.reshape(bp, seq, Z32_W)
        z16_3 = z16.reshape(bp, seq, Z16_W)
        zl, conv_tail, h_p = _lru_prompt(z3, z16_3, *lru_w)
        zpool, pool_tail = _pool_prompt(z3, z16_3, wp, row(pool_scale[l]))
        zr = z.reshape(bp * seq // PAGE_SIZE, PAGE_SIZE, Z32_W)
        fk, sk, fv, sv = _compress(table_p, zr, zr, C_KV // KV_W, C_KV // KV_W + 1, wpos_k, wpos_v, bp)
        zn = _nsa_prompt(z3, z16_3, fk, sk, fv, sv, phik, phiv.T)
        xp = _merge(zl.reshape(bp * seq, BRANCH_W), zpool.reshape(bp * seq, BRANCH_W),
                    zn.reshape(bp * seq, BRANCH_W), z16, wbr, wo, row(g_post[l]), xp)
        kv_rows = [_kv_seg(z3, i).reshape(bp, seq, KV_HEADS, HEAD_DIM) for i in range(6)]
        wlen = min(WINDOW, seq)
        st_p = kv_rows[:4] + [kv_rows[4][:, -wlen:], kv_rows[5][:, -wlen:],
                              conv_tail[:, -(CONV_W - 1):], h_p[:, 0], pool_tail[:, -POOL_BUF:]]

        zs, zs16 = _inproj(xs, row(g_pre[l]), w_packed)
        zls, zps, h_s = _mix_sample(zs, zs16, state_conv[l].swapaxes(0, 1), state_lru[l],
                                    state_pool[l].swapaxes(0, 1), *lru_w, wp, row(pool_scale[l]), past_len)
        fk, sk, fv, sv = _compress_t(table_s, l, cmp_kt, cmp_vt, _tile_wpos_t(cmp_pos_k[l]),
                                     _tile_wpos_t(cmp_pos_v[l]), bs)
        qs = (zs16[:, C_Q:C_Q + BRANCH_W].astype(f32) * (HEAD_DIM ** -0.5)).astype(bf16)
        zns = _nsa_sample(table_s, l, qs, zs, zs16, slopes, fk, sk, fv, sv, phik, phiv,
                          sel_kt, sel_vt, win_kt, win_vt, past_len)
        xs = _merge(zls, zps, zns.reshape(bs, BRANCH_W), zs16, wbr, wo, row(g_post[l]), xs)
        kv_new = [_kv_seg(zs, i).reshape(bs, 1, KV_HEADS, HEAD_DIM) for i in range(6)]
        st_s = kv_new[:4] + [jnp.concatenate([cache_win_k[l], kv_new[4]], axis=1)[:, -wb:],
                             jnp.concatenate([cache_win_v[l], kv_new[5]], axis=1)[:, -wb:],
                             jnp.concatenate([state_conv[l], zs[:, None, C_LRU_X:C_LRU_X + BRANCH_W]], axis=1)[:, -(CONV_W - 1):],
                             h_s,
                             jnp.concatenate([state_pool[l], zs[:, None, C_POOL_X:C_POOL_X + BRANCH_W]], axis=1)[:, -POOL_BUF:]]
        for i in range(9):
            pr[i].append(st_p[i])
            sm[i].append(st_s[i])

    out = [xp.reshape(bp, seq, D_MODEL), xs.reshape(bs, 1, D_MODEL)]
    for i in range(9):
        out += [jnp.stack(pr[i]), jnp.stack(sm[i])]
    return tuple(out)
```

```python
import functools

import numpy as np
import jax
import jax.numpy as jnp
from jax import lax
from jax.experimental import pallas as pl
from jax.experimental.pallas import tpu as pltpu

f32 = jnp.float32
bf16 = jnp.bfloat16

D_MODEL = 1024
BRANCH_W = 1024
N_BRANCH = 3
LRU_BLOCKS = 16
LRU_BD = BRANCH_W // LRU_BLOCKS
CONV_W = 4
LRU_C = 8.0
POOL_WINDOWS = (2, 4, 8, 16)
POOL_GD = BRANCH_W // len(POOL_WINDOWS)
POOL_BUF = max(POOL_WINDOWS) - 1
N_HEADS = 16
HEAD_DIM = 64
KV_HEADS = 4
Q_PER_KV = N_HEADS // KV_HEADS
KV_W = KV_HEADS * HEAD_DIM
CMP_STRIDE = 16
CMP_BLK = 2 * CMP_STRIDE
SEL_BLK = 64
N_SEL = 16
WINDOW = 512
PAGE_SIZE = 128
FORCE = 1e4
NEG = -1e30
EPS = 1e-6

C_LRU_X, C_POOL_X, C_KV = 0, 1024, 2048
C_BG = C_KV + 6 * KV_W
BG_W = 128
Z32_W = 4096
C_LRU_G, C_POOL_G, C_Q, C_NSA_G, C_MG = 0, 1024, 2048, 3072, 4096
Z16_W = C_MG + N_BRANCH * D_MODEL
IN_TN = 1024
N32_TILES = Z32_W // IN_TN

MXU_W = 256
VMEM_LIMIT = 56 * 1024 * 1024

N_CMP = 512
N_BLK = 128
TQ = 256
TK = 512
PP = 8
PP_SEL = 16
PAGE_SLOTS = 3
POS_ROWS = 64
QA_W = HEAD_DIM + POS_ROWS

SLOPES = [float(np.float32(2.0 ** (-8.0 * (h + 1) / N_HEADS))) for h in range(N_HEADS)]
LOG2E = float(np.log2(np.e))
SLOPES_LOG2 = [float(np.float32(s * LOG2E)) for s in SLOPES]
Q_SCALE = float(np.float32(HEAD_DIM ** -0.5 * LOG2E))
LOOP_GROUPS = ((0,), (1,), (2,), (3,))


def _nt(a, b):
    return lax.dot_general(a, b, (((1,), (1,)), ((), ())), preferred_element_type=f32)


def _dot(a, b):
    return jnp.dot(a, b, preferred_element_type=f32)


def _silu(x):
    x = x.astype(f32)
    return x * jax.nn.sigmoid(x)


def _params(sem):
    return pltpu.CompilerParams(dimension_semantics=sem, vmem_limit_bytes=VMEM_LIMIT)


def _inproj_kernel(x_ref, g_ref, w_ref, o32_ref, o16_ref, u_ref):
    j = pl.program_id(1)

    @pl.when(j == 0)
    def _():
        x = x_ref[...]
        ms = jnp.mean(x * x, axis=-1, keepdims=True)
        u_ref[...] = (x * lax.rsqrt(ms + EPS) * g_ref[...]).astype(bf16)

    @pl.when(j < N32_TILES)
    def _():
        o32_ref[...] = _dot(u_ref[...], w_ref[...])

    @pl.when(j >= N32_TILES)
    def _():
        o16_ref[...] = _dot(u_ref[...], w_ref[...]).astype(bf16)


def _inproj(x2d, g_row, w_packed):
    n = x2d.shape[0]
    tm = min(n, 1024)
    return pl.pallas_call(
        _inproj_kernel,
        out_shape=(jax.ShapeDtypeStruct((n, Z32_W), f32), jax.ShapeDtypeStruct((n, Z16_W), bf16)),
        grid=(n // tm, (Z32_W + Z16_W) // IN_TN),
        in_specs=[pl.BlockSpec((tm, D_MODEL), lambda i, j: (i, 0)),
                  pl.BlockSpec((1, D_MODEL), lambda i, j: (0, 0)),
                  pl.BlockSpec((D_MODEL, IN_TN), lambda i, j: (0, j))],
        out_specs=(pl.BlockSpec((tm, IN_TN), lambda i, j: (i, jnp.minimum(j, N32_TILES - 1))),
                   pl.BlockSpec((tm, IN_TN), lambda i, j: (i, jnp.maximum(j - N32_TILES, 0)))),
        scratch_shapes=[pltpu.VMEM((tm, D_MODEL), bf16)],
        compiler_params=_params(("parallel", "arbitrary")),
        name="inproj",
    )(x2d, g_row, w_packed)


def _lru_gates(xc, wa_ref, ba_ref, wx_ref, bx_ref, lam_ref):
    xb = xc.astype(bf16)
    ra, ri = [], []
    for c in range(BRANCH_W // MXU_W):
        sl = slice(c * MXU_W, (c + 1) * MXU_W)
        ra.append(_dot(xb[:, sl], wa_ref[c]))
        ri.append(_dot(xb[:, sl], wx_ref[c]))
    r = jax.nn.sigmoid(jnp.concatenate(ra, axis=-1) + ba_ref[...])
    i = jax.nn.sigmoid(jnp.concatenate(ri, axis=-1) + bx_ref[...])
    nl = -lam_ref[...]
    softplus = jnp.maximum(nl, 0.0) + jnp.log1p(jnp.exp(-jnp.abs(nl)))
    log_a = -LRU_C * r * softplus
    a = jnp.exp(log_a)
    b = jnp.sqrt(1.0 - a * a) * (i * xc)
    return a, b


def _lru_kernel(x_ref, g_ref, cw_ref, cb_ref, wa_ref, ba_ref, wx_ref, bx_ref, lam_ref,
                zb_ref, tail_ref, h_ref, xs_ref, a_ref, b_ref, hc_ref):
    tt = x_ref.shape[1]

    @pl.when(pl.program_id(1) == 0)
    def _():
        xs_ref[0:8, :] = jnp.zeros((8, BRANCH_W), f32)
        hc_ref[...] = jnp.zeros_like(hc_ref)

    x = x_ref[0]
    xs_ref[8:, :] = x
    xc = cb_ref[...] + x * cw_ref[CONV_W - 1:CONV_W, :]
    for k in range(CONV_W - 1):
        xc = xc + xs_ref[pl.ds(8 - (CONV_W - 1 - k), tt), :] * cw_ref[k:k + 1, :]
    xs_ref[0:8, :] = x[tt - 8:, :]
    tail_ref[0] = x[tt - 8:, :]

    a, b = _lru_gates(xc, wa_ref, ba_ref, wx_ref, bx_ref, lam_ref)
    a_ref[...] = a
    b_ref[...] = b
    row = lax.broadcasted_iota(jnp.int32, (8, BRANCH_W), 0)

    def body(i, h):
        r0 = pl.multiple_of(i * 8, 8)
        av = a_ref[pl.ds(r0, 8), :]
        bv = b_ref[pl.ds(r0, 8), :]
        for d in (1, 2, 4):
            a_s = jnp.where(row >= d, pltpu.roll(av, d, 0), 1.0)
            b_s = jnp.where(row >= d, pltpu.roll(bv, d, 0), 0.0)
            bv = av * b_s + bv
            av = av * a_s
        hs = bv + av * h
        b_ref[pl.ds(r0, 8), :] = hs
        return hs[7:8, :]

    h = lax.fori_loop(0, tt // 8, body, hc_ref[...])
    hc_ref[...] = h
    h_ref[0] = h
    zb_ref[0] = (b_ref[...] * _silu(g_ref[0])).astype(bf16)


def _lru_prompt(z32, z16, cw, cb, wa, ba, wx, bx, lam):
    b, s, _ = z32.shape
    tt = min(s, 512)
    row = lambda: pl.BlockSpec((1, BRANCH_W), lambda i, t: (0, 0))
    bd = lambda: pl.BlockSpec((BRANCH_W // MXU_W, MXU_W, MXU_W), lambda i, t: (0, 0, 0))
    return pl.pallas_call(
        _lru_kernel,
        out_shape=(jax.ShapeDtypeStruct((b, s, BRANCH_W), bf16),
                   jax.ShapeDtypeStruct((b, 8, BRANCH_W), f32),
                   jax.ShapeDtypeStruct((b, 1, BRANCH_W), f32)),
        grid=(b, s // tt),
        in_specs=[pl.BlockSpec((1, tt, BRANCH_W), lambda i, t: (i, t, C_LRU_X // BRANCH_W)),
                  pl.BlockSpec((1, tt, BRANCH_W), lambda i, t: (i, t, C_LRU_G // BRANCH_W)),
                  pl.BlockSpec((CONV_W, BRANCH_W), lambda i, t: (0, 0)),
                  row(), bd(), row(), bd(), row(), row()],
        out_specs=(pl.BlockSpec((1, tt, BRANCH_W), lambda i, t: (i, t, 0)),
                   pl.BlockSpec((1, 8, BRANCH_W), lambda i, t: (i, 0, 0)),
                   pl.BlockSpec((1, 1, BRANCH_W), lambda i, t: (i, 0, 0))),
        scratch_shapes=[pltpu.VMEM((tt + 8, BRANCH_W), f32), pltpu.VMEM((tt, BRANCH_W), f32),
                        pltpu.VMEM((tt, BRANCH_W), f32), pltpu.VMEM((1, BRANCH_W), f32)],
        compiler_params=_params(("parallel", "arbitrary")),
        name="lru_prompt",
    )(z32, z16, cw, cb, wa, ba, wx, bx, lam)


def _pool_kernel(x_ref, g_ref, wp_ref, sc_ref, zb_ref, tail_ref, xs_ref):
    tt = x_ref.shape[1]
    t = pl.program_id(1)

    @pl.when(t == 0)
    def _():
        xs_ref[0:16, :] = jnp.zeros((16, BRANCH_W), f32)

    x = x_ref[0]
    xs_ref[16:, :] = x
    pos1 = t * tt + 1 + lax.broadcasted_iota(jnp.int32, (tt, POOL_GD), 0)
    outs = []
    for gi, w in enumerate(POOL_WINDOWS):
        sl = slice(gi * POOL_GD, (gi + 1) * POOL_GD)
        s = xs_ref[:, sl]
        sh = 1
        while sh < w:
            s = s + pltpu.roll(s, sh, 0)
            sh *= 2
        cnt = jnp.minimum(w, pos1).astype(f32)
        pooled = s[16:, :] / cnt - x[:, sl]
        outs.append(_dot(pooled.astype(bf16), wp_ref[gi]))
    y = jnp.concatenate(outs, axis=-1) * sc_ref[...]
    zb_ref[0] = (y * _silu(g_ref[0])).astype(bf16)
    xs_ref[0:16, :] = x[tt - 16:, :]
    tail_ref[0] = x[tt - 16:, :]


def _pool_prompt(z32, z16, wp, sc):
    b, s, _ = z32.shape
    tt = min(s, 512)
    return pl.pallas_call(
        _pool_kernel,
        out_shape=(jax.ShapeDtypeStruct((b, s, BRANCH_W), bf16),
                   jax.ShapeDtypeStruct((b, 16, BRANCH_W), f32)),
        grid=(b, s // tt),
        in_specs=[pl.BlockSpec((1, tt, BRANCH_W), lambda i, t: (i, t, C_POOL_X // BRANCH_W)),
                  pl.BlockSpec((1, tt, BRANCH_W), lambda i, t: (i, t, C_POOL_G // BRANCH_W)),
                  pl.BlockSpec((len(POOL_WINDOWS), POOL_GD, POOL_GD), lambda i, t: (0, 0, 0)),
                  pl.BlockSpec((1, BRANCH_W), lambda i, t: (0, 0))],
        out_specs=(pl.BlockSpec((1, tt, BRANCH_W), lambda i, t: (i, t, 0)),
                   pl.BlockSpec((1, 16, BRANCH_W), lambda i, t: (i, 0, 0))),
        scratch_shapes=[pltpu.VMEM((tt + 16, BRANCH_W), f32)],
        compiler_params=_params(("parallel", "arbitrary")),
        name="pool_prompt",
    )(z32, z16, wp, sc)


def _mix_sample_kernel(past_len, lx_ref, lg_ref, px_ref, pg_ref, conv_ref, h0_ref, pbuf_ref,
                       cw_ref, cb_ref, wa_ref, ba_ref, wx_ref, bx_ref, lam_ref, wp_ref, sc_ref,
                       zl_ref, zp_ref, h_ref):
    x = lx_ref[...]
    xc = cb_ref[...] + x * cw_ref[CONV_W - 1:CONV_W, :]
    for k in range(CONV_W - 1):
        xc = xc + conv_ref[k] * cw_ref[k:k + 1, :]
    a, b = _lru_gates(xc, wa_ref, ba_ref, wx_ref, bx_ref, lam_ref)
    h = a * h0_ref[...] + b
    h_ref[...] = h
    zl_ref[...] = (h * _silu(lg_ref[...])).astype(bf16)

    px = px_ref[...]
    outs = []
    for gi, w in enumerate(POOL_WINDOWS):
        sl = slice(gi * POOL_GD, (gi + 1) * POOL_GD)
        s = px[:, sl]
        for k in range(1, w):
            s = s + pbuf_ref[POOL_BUF - k][:, sl]
        cnt = float(min(w, past_len + 1))
        pooled = s / cnt - px[:, sl]
        outs.append(_dot(pooled.astype(bf16), wp_ref[gi]))
    y = jnp.concatenate(outs, axis=-1) * sc_ref[...]
    zp_ref[...] = (y * _silu(pg_ref[...])).astype(bf16)


def _mix_sample(z32, z16, conv_t, h0, pbuf_t, cw, cb, wa, ba, wx, bx, lam, wp, sc, past_len):
    n = z32.shape[0]
    col = lambda c: pl.BlockSpec((n, BRANCH_W), lambda i: (0, c // BRANCH_W))
    full = lambda a: pl.BlockSpec(a.shape, lambda i: (0,) * a.ndim)
    args = (conv_t, h0, pbuf_t, cw, cb, wa, ba, wx, bx, lam, wp, sc)
    return pl.pallas_call(
        functools.partial(_mix_sample_kernel, past_len),
        out_shape=(jax.ShapeDtypeStruct((n, BRANCH_W), bf16), jax.ShapeDtypeStruct((n, BRANCH_W), bf16),
                   jax.ShapeDtypeStruct((n, BRANCH_W), f32)),
        grid=(1,),
        in_specs=[col(C_LRU_X), col(C_LRU_G), col(C_POOL_X), col(C_POOL_G)] + [full(a) for a in args],
        out_specs=(pl.BlockSpec((n, BRANCH_W), lambda i: (0, 0)),) * 3,
        compiler_params=_params(("arbitrary",)),
        name="mix_sample",
    )(z32, z16, z32, z16, *args)


def _compress_kernel(tab_ref, k_ref, v_ref, wk_ref, wv_ref, fk_ref, sk_ref, fv_ref, sv_ref):
    del tab_ref
    p = pl.program_id(1)
    chunks = PAGE_SIZE // CMP_STRIDE

    @pl.when(p == 0)
    def _():
        sk_ref[...] = jnp.zeros_like(sk_ref)
        sv_ref[...] = jnp.zeros_like(sv_ref)

    for src, w_ref, f_ref, s_ref in ((k_ref, wk_ref, fk_ref, sk_ref), (v_ref, wv_ref, fv_ref, sv_ref)):
        tile = src[0]
        first = jnp.sum((tile * w_ref[0]).reshape(chunks, CMP_STRIDE, KV_W), axis=1)
        second = jnp.sum((tile * w_ref[1]).reshape(chunks, CMP_STRIDE, KV_W), axis=1)
        for m in range(chunks):
            f_ref[0, m % 4, pl.ds(2 * p + m // 4, 1), :] = first[m:m + 1, :]
            if m >= 1:
                s_ref[0, (m - 1) % 4, pl.ds(2 * p + (m - 1) // 4, 1), :] = second[m:m + 1, :]
            else:
                @pl.when(p > 0)
                def _():
                    s_ref[0, 3, pl.ds(2 * p - 1, 1), :] = second[0:1, :]


def _compress(table, src_k, src_v, col_k, col_v, wk, wv, nb):
    npg = table.shape[0] // nb
    out = jax.ShapeDtypeStruct((nb, 4, N_CMP // 4, KV_W), f32)
    ospec = lambda: pl.BlockSpec((1, 4, N_CMP // 4, KV_W), lambda b, p, tab: (b, 0, 0, 0))
    return pl.pallas_call(
        _compress_kernel,
        out_shape=(out,) * 4,
        grid_spec=pltpu.PrefetchScalarGridSpec(
            num_scalar_prefetch=1,
            grid=(nb, npg),
            in_specs=[pl.BlockSpec((1, PAGE_SIZE, KV_W), lambda b, p, tab: (tab[b * npg + p], 0, col_k)),
                      pl.BlockSpec((1, PAGE_SIZE, KV_W), lambda b, p, tab: (tab[b * npg + p], 0, col_v)),
                      pl.BlockSpec((2, PAGE_SIZE, KV_W), lambda b, p, tab: (0, 0, 0)),
                      pl.BlockSpec((2, PAGE_SIZE, KV_W), lambda b, p, tab: (0, 0, 0))],
            out_specs=(ospec(), ospec(), ospec(), ospec())),
        compiler_params=_params(("parallel", "arbitrary")),
        name="compress",
    )(table, src_k, src_v, wk, wv)


def _chunk_maps():
    n_chunk = PP * PAGE_SIZE // CMP_STRIDE
    rows = n_chunk // 4
    chunk_of = np.arange(PP * PAGE_SIZE) // CMP_STRIDE
    first = np.zeros((n_chunk, PP * PAGE_SIZE), np.float32)
    second = np.zeros((n_chunk + 8, PP * PAGE_SIZE), np.float32)
    for i in range(4):
        for jj in range(rows):
            first[i * rows + jj] = chunk_of == 4 * jj + i
            second[i * rows + jj] = chunk_of == 4 * jj + i + 1
    second[n_chunk] = chunk_of == 0
    return jnp.asarray(first, bf16), jnp.asarray(second, bf16)


def _page_copies(tab_ref, layer, step, n_pages, pools, bufs, sems, wait):
    slot = step % PAGE_SLOTS
    for i in range(n_pages):
        page = 0 if wait else tab_ref[step * n_pages + i]
        for pool, buf, sem in zip(pools, bufs, sems):
            copy = pltpu.make_async_copy(pool.at[layer, page],
                                         buf.at[slot, i], sem.at[slot])
            if wait:
                copy.wait()
            else:
                copy.start()


def _side_by_side(buf, slot):
    return jnp.concatenate([buf[slot, i] for i in range(buf.shape[1])], axis=1)


def _paged_step(tab_ref, layer, n_pages, pools, bufs, sems):
    ahead = PAGE_SLOTS - 1
    step = pl.program_id(0) * pl.num_programs(1) + pl.program_id(1)
    n_steps = pl.num_programs(0) * pl.num_programs(1)

    for first in range(ahead):
        @pl.when((step == 0) & (first < n_steps))
        def _(first=first):
            _page_copies(tab_ref, layer, first, n_pages, pools, bufs, sems, wait=False)

    @pl.when(step + ahead < n_steps)
    def _():
        _page_copies(tab_ref, layer, step + ahead, n_pages, pools, bufs, sems, wait=False)

    _page_copies(tab_ref, layer, step, n_pages, pools, bufs, sems, wait=True)
    return step % PAGE_SLOTS


def _compress_t_kernel(layer, tab_ref, k_hbm, v_hbm, wk_ref, wv_ref, ea_ref, eb_ref,
                       fk_ref, sk_ref, fv_ref, sv_ref, kbuf, vbuf, ksem, vsem):
    slot = _paged_step(tab_ref, layer, PP, (k_hbm, v_hbm), (kbuf, vbuf), (ksem, vsem))
    ps = pl.program_id(1)
    rows = PP * PAGE_SIZE // CMP_STRIDE // 4
    r0 = pl.multiple_of(ps * rows, rows)
    for buf, w_ref, f_ref, s_ref in ((kbuf, wk_ref, fk_ref, sk_ref), (vbuf, wv_ref, fv_ref, sv_ref)):
        pages = _side_by_side(buf, slot)
        a1 = (pages * w_ref[0]).astype(bf16)
        a2 = (pages * w_ref[1]).astype(bf16)
        first = _nt(ea_ref[...], a1)
        second = _nt(eb_ref[...], a2)
        for i in range(4):
            f_ref[0, i, pl.ds(r0, rows), :] = first[i * rows:(i + 1) * rows]
            s_ref[0, i, pl.ds(r0, rows), :] = second[i * rows:(i + 1) * rows]

        @pl.when(ps > 0)
        def _():
            s_ref[0, 3, pl.ds(r0 - 1, 1), :] = second[4 * rows:4 * rows + 1]


def _compress_t(table, layer, cache_k, cache_v, wk, wv, nb):
    npg = table.shape[0] // nb
    ea, eb = _chunk_maps()
    out = jax.ShapeDtypeStruct((nb, 4, N_CMP // 4, KV_W), f32)
    ospec = lambda: pl.BlockSpec((1, 4, N_CMP // 4, KV_W), lambda b, p, tab: (b, 0, 0, 0))
    full = lambda a: pl.BlockSpec(a.shape, lambda b, p, tab: (0,) * a.ndim)
    hbm = lambda: pl.BlockSpec(memory_space=pl.ANY)
    wk, wv = jnp.tile(wk, (1, 1, PP)), jnp.tile(wv, (1, 1, PP))
    return pl.pallas_call(
        functools.partial(_compress_t_kernel, layer),
        out_shape=(out,) * 4,
        grid_spec=pltpu.PrefetchScalarGridSpec(
            num_scalar_prefetch=1,
            grid=(nb, npg // PP),
            in_specs=[hbm(), hbm(), full(wk), full(wv), full(ea), full(eb)],
            out_specs=(ospec(), ospec(), ospec(), ospec()),
            scratch_shapes=[pltpu.VMEM((PAGE_SLOTS, PP, KV_W, PAGE_SIZE), f32),
                            pltpu.VMEM((PAGE_SLOTS, PP, KV_W, PAGE_SIZE), f32),
                            pltpu.SemaphoreType.DMA((PAGE_SLOTS,)), pltpu.SemaphoreType.DMA((PAGE_SLOTS,))]),
        compiler_params=_params(("arbitrary", "arbitrary")),
        name="compress_t",
    )(table, cache_k, cache_v, wk, wv, ea, eb)


def _finish_compress(f_ref, s_ref, phi_ref):
    blk = (f_ref[0] + s_ref[0]).reshape(N_CMP, KV_W)
    return _dot(blk.astype(bf16), phi_ref[...]).astype(bf16)


def _cmp_end(shape, axis):
    col = lax.broadcasted_iota(jnp.int32, shape, axis)
    n = ((col & (N_BLK - 1)) << 2) + (col >> 7)
    return n * CMP_STRIDE + (CMP_BLK - 1)


def _pick_blocks(score, n_pick):
    lane = lax.broadcasted_iota(jnp.int32, score.shape, 1).astype(f32)
    bias = jnp.full(score.shape, NEG, f32)
    for _ in range(n_pick):
        m = jnp.max(score, axis=-1, keepdims=True)
        first = jnp.min(jnp.where(score == m, lane, float(N_BLK)), axis=-1, keepdims=True)
        hit = lane == first
        bias = jnp.where(hit, 0.0, bias)
        score = jnp.where(hit, -jnp.inf, score)
    return bias


def _block_expand(first_block, n_keys):
    j = lax.broadcasted_iota(jnp.int32, (N_BLK, n_keys), 0)
    c = lax.broadcasted_iota(jnp.int32, (N_BLK, n_keys), 1)
    return jnp.where(j == first_block + (c >> 6), 1.0, 0.0).astype(bf16)


CQ = Q_PER_KV * TQ
WIN_T = WINDOW // TQ + 1
N_PARTS = 9
TILE_COL = HEAD_DIM + 6
V_ROWS = HEAD_DIM + 16


def _pick_blocks_t(score, forced, n_pick):
    jrow = lax.broadcasted_iota(jnp.int32, score.shape, 0).astype(f32)
    bias = jnp.where(forced, 0.0, NEG)
    score = jnp.where(forced, -jnp.inf, score)
    for _ in range(n_pick):
        m = jnp.max(score, axis=0, keepdims=True)
        first = jnp.min(jnp.where(score == m, jrow, float(N_BLK)), axis=0, keepdims=True)
        hit = jrow == first
        bias = jnp.where(hit, 0.0, bias)
        score = jnp.where(hit, -jnp.inf, score)
    return bias


def _nsa_prompt_kernel(q_ref, srow_ref, slope_ref, bg_ref, ng_ref, fk_ref, sk_ref, fv_ref, sv_ref,
                       phik_ref, phivt_ref, cpos_ref, ksa_ref, vst_ref, kwa_ref, vwt_ref, o_ref,
                       kca_scr, vct_scr, qa_scr, oc_scr, m_scr, acc_scr, negc_scr, negd_scr, negw_scr):
    qi = pl.program_id(1)
    s0 = qi * TQ

    @pl.when(qi == 0)
    def _():
        kc = _finish_compress(fk_ref, sk_ref, phik_ref)
        for g in range(KV_HEADS):
            kca_scr[g] = jnp.concatenate([kc[:, g * HEAD_DIM:(g + 1) * HEAD_DIM], cpos_ref[...]], axis=1)
        blk_v = (fv_ref[0] + sv_ref[0]).reshape(N_CMP, KV_W).astype(bf16)
        vct_scr[...] = _nt(phivt_ref[...], blk_v).astype(bf16)

    t_of = lambda shape: s0 + (lax.broadcasted_iota(jnp.int32, shape, 1) & (TQ - 1))
    q_t = (q_ref[0].astype(f32) * Q_SCALE).T.astype(bf16)

    negc_scr[...] = jnp.where(_cmp_end((N_CMP, CQ), 0) <= t_of((N_CMP, CQ)), 0.0, NEG)
    sees_block = t_of((1, CQ)) >= CMP_BLK - 1
    t_q = t_of((N_BLK, TQ))
    jr = lax.broadcasted_iota(jnp.int32, (N_BLK, TQ), 0)
    jb = t_q >> 6
    ok_b = (jr << 6) <= t_q
    forced = (jr == 0) | (jr == jb) | (jr == jb - 1)
    tile_any = [jnp.full((N_BLK // 8, 1), NEG, f32) for _ in LOOP_GROUPS]
    for g in range(KV_HEADS):
        q_g = jnp.concatenate([q_t[(g * Q_PER_KV + r) * HEAD_DIM:(g * Q_PER_KV + r + 1) * HEAD_DIM]
                               for r in range(Q_PER_KV)], axis=1)
        qa = jnp.concatenate([q_g, srow_ref[g]], axis=0)
        st = _dot(kca_scr[g], qa) + negc_scr[...]
        e = jnp.exp2(st - jnp.max(st, axis=0, keepdims=True))
        pt = e * jnp.where(sees_block, 1.0 / jnp.sum(e, axis=0, keepdims=True), 0.0)
        oc_scr[g] = _dot(vct_scr[g * HEAD_DIM:(g + 1) * HEAD_DIM], pt.astype(bf16))
        ps = None
        for i in range(4):
            for r in range(Q_PER_KV):
                slab = pt[i * N_BLK:(i + 1) * N_BLK, r * TQ:(r + 1) * TQ]
                ps = slab if ps is None else ps + slab
        bias = _pick_blocks_t(jnp.where(ok_b, ps, NEG), forced, N_SEL - 3)
        qa_scr[g] = jnp.concatenate([qa, jnp.concatenate([bias.astype(bf16)] * Q_PER_KV, axis=1)], axis=0)
        any_t = jnp.max(bias.reshape(N_BLK // 8, 8, TQ), axis=1)
        li = [g in grp for grp in LOOP_GROUPS].index(True)
        tile_any[li] = jnp.maximum(tile_any[li], jnp.max(any_t, axis=1, keepdims=True))
    kt_row = lax.broadcasted_iota(jnp.int32, tile_any[0].shape, 0)
    tile_bits = [jnp.sum(jnp.where(ta == 0.0, 1 << kt_row, 0)) for ta in tile_any]

    m_scr[...] = jnp.full(m_scr.shape, NEG, f32)
    acc_scr[...] = jnp.zeros_like(acc_scr)

    def sel_tile(kt, diagonal, groups, n_keys=TK):
        k0 = pl.multiple_of(kt * TK, TK)
        tile_off = (k0 - s0).astype(f32)
        key_r = lax.broadcasted_iota(jnp.int32, (n_keys, N_BLK), 0)
        blk_c = lax.broadcasted_iota(jnp.int32, (n_keys, N_BLK), 1)
        expand = jnp.where(blk_c == kt * (TK // SEL_BLK) + (key_r >> 6), 1.0, 0.0).astype(bf16)
        k_aug = ksa_ref[0, pl.ds(k0, n_keys), :]
        v_t = vst_ref[0, kt][:, 0:n_keys]
        for g in groups:
            lhs = jnp.concatenate([k_aug[:, g * QA_W:(g + 1) * QA_W], expand], axis=1)
            st = _dot(lhs, qa_scr[g])
            if diagonal:
                st = st + negd_scr[qi % (TK // TQ), 0:n_keys]
            c = slope_ref[g] * tile_off
            m_old = m_scr[g]
            m_new = jnp.maximum(m_old, jnp.max(st, axis=0, keepdims=True) + c)
            alpha = jnp.exp2(m_old - m_new)
            pt = jnp.exp2((st - (m_new - c)).astype(bf16))
            acc_scr[g] = alpha * acc_scr[g] + _dot(v_t[g * V_ROWS:(g + 1) * V_ROWS], pt)
            m_scr[g] = m_new

    last = s0 // TK

    @pl.when(qi < TK // TQ)
    def _():
        key_pos = lax.broadcasted_iota(jnp.int32, (TK, CQ), 0)
        negd_scr[qi] = jnp.where(key_pos <= t_of((TK, CQ)), 0.0, NEG)

    for groups, bits in zip(LOOP_GROUPS, tile_bits):
        def sel_body(kt, carry, groups=groups, bits=bits):
            @pl.when(((bits >> kt) & 1) == 1)
            def _():
                sel_tile(kt, False, groups)
            return carry

        lax.fori_loop(0, last, sel_body, 0)
    for sub in range(TK // TQ):
        @pl.when(qi % (TK // TQ) == sub)
        def _(sub=sub):
            sel_tile(last, True, range(KV_HEADS), (sub + 1) * TQ)

    w_tile = jnp.maximum(qi - WINDOW // TQ, 0)
    n_win = WIN_T * TQ

    @pl.when(qi <= WINDOW // TQ)
    def _():
        dist_w = t_of((n_win, CQ)) - (w_tile * TQ + lax.broadcasted_iota(jnp.int32, (n_win, CQ), 0))
        negw_scr[...] = jnp.where((dist_w >= 0) & (dist_w < WINDOW), 0.0, NEG)

    lane_w = lax.broadcasted_iota(jnp.int32, (n_win, QA_W), 1)
    slab_off = (lax.broadcasted_iota(jnp.int32, (n_win, QA_W), 0) // TQ * TQ).astype(f32).astype(bf16)
    in_tile_col = (lane_w >= TILE_COL) & (lane_w < TILE_COL + 3)
    kw_all = kwa_ref[0, pl.ds(pl.multiple_of(w_tile * TQ, TQ), n_win), :]
    vw_t = jnp.concatenate([vwt_ref[0, w_tile + i] for i in range(WIN_T)], axis=1)
    gates = jax.nn.sigmoid(bg_ref[0].T)
    blocks = []
    for g in range(KV_HEADS):
        lhs = jnp.where(in_tile_col, slab_off, kw_all[:, g * QA_W:(g + 1) * QA_W])
        st = _dot(lhs, qa_scr[g, 0:QA_W]) + negw_scr[...]
        pt = jnp.exp2((st - jnp.max(st, axis=0, keepdims=True)).astype(bf16))
        win = _dot(vw_t[g * V_ROWS:(g + 1) * V_ROWS], pt)
        o_w = win[0:HEAD_DIM] * (1.0 / win[HEAD_DIM:HEAD_DIM + 1])
        o_s = acc_scr[g, 0:HEAD_DIM] * (1.0 / acc_scr[g, HEAD_DIM:HEAD_DIM + 1])
        o_c = oc_scr[g]
        heads = []
        for r in range(Q_PER_KV):
            h = g * Q_PER_KV + r
            cols = slice(r * TQ, (r + 1) * TQ)
            heads.append(gates[3 * h:3 * h + 1] * o_c[:, cols] + gates[3 * h + 1:3 * h + 2] * o_s[:, cols]
                         + gates[3 * h + 2:3 * h + 3] * o_w[:, cols])
        for r in range(0, Q_PER_KV, 2):
            blocks.append(jnp.concatenate(heads[r:r + 2], axis=0).T)
    y = jnp.concatenate(blocks, axis=-1)
    o_ref[0] = (y * _silu(ng_ref[0])).astype(bf16)


def _nsa_prompt(z3, z16, fk, sk, fv, sv, phik, phivt):
    b, s, _ = z3.shape
    nq = s // TQ
    parts = _slope_parts().reshape(KV_HEADS, Q_PER_KV, POS_ROWS)
    srow = jnp.asarray(np.repeat(parts.transpose(0, 2, 1), TQ, axis=2), bf16)
    slope = jnp.asarray(np.repeat(np.asarray(SLOPES_LOG2, np.float32).reshape(KV_HEADS, 1, Q_PER_KV), TQ, axis=2))
    ksa = _key_aug(_kv_seg(z3, 2), TK)
    kwa = _key_aug(_kv_seg(z3, 4), TQ)
    vst = _value_tiles(_kv_seg(z3, 3), TK)
    vwt = _value_tiles(_kv_seg(z3, 5), TQ)
    cpos = _cmp_pos_cols()

    once = pl.Buffered(1)
    cmp = lambda: pl.BlockSpec((1, 4, N_CMP // 4, KV_W), lambda i, t: (i, 0, 0, 0))
    full = lambda a: pl.BlockSpec(a.shape, lambda i, t: (0,) * a.ndim)
    return pl.pallas_call(
        _nsa_prompt_kernel,
        out_shape=jax.ShapeDtypeStruct((b, s, BRANCH_W), bf16),
        grid=(b, nq),
        in_specs=[pl.BlockSpec((1, TQ, BRANCH_W), lambda i, t: (i, t, C_Q // BRANCH_W)),
                  full(srow), full(slope),
                  pl.BlockSpec((1, TQ, BG_W), lambda i, t: (i, t, C_BG // BG_W)),
                  pl.BlockSpec((1, TQ, BRANCH_W), lambda i, t: (i, t, C_NSA_G // BRANCH_W)),
                  cmp(), cmp(), cmp(), cmp(), full(phik), full(phivt), full(cpos),
                  pl.BlockSpec((1, s, KV_HEADS * QA_W), lambda i, t: (i, 0, 0), pipeline_mode=once),
                  pl.BlockSpec((1, s // TK, KV_HEADS * V_ROWS, TK), lambda i, t: (i, 0, 0, 0), pipeline_mode=once),
                  pl.BlockSpec((1, s, KV_HEADS * QA_W), lambda i, t: (i, 0, 0), pipeline_mode=once),
                  pl.BlockSpec((1, nq, KV_HEADS * V_ROWS, TQ), lambda i, t: (i, 0, 0, 0), pipeline_mode=once)],
        out_specs=pl.BlockSpec((1, TQ, BRANCH_W), lambda i, t: (i, t, 0)),
        scratch_shapes=[pltpu.VMEM((KV_HEADS, N_CMP, QA_W), bf16), pltpu.VMEM((KV_W, N_CMP), bf16),
                        pltpu.VMEM((KV_HEADS, QA_W + N_BLK, CQ), bf16),
                        pltpu.VMEM((KV_HEADS, HEAD_DIM, CQ), f32),
                        pltpu.VMEM((KV_HEADS, 1, CQ), f32),
                        pltpu.VMEM((KV_HEADS, V_ROWS, CQ), f32),
                        pltpu.VMEM((N_CMP, CQ), f32), pltpu.VMEM((TK // TQ, TK, CQ), f32),
                        pltpu.VMEM((WIN_T * TQ, CQ), f32)],
        compiler_params=_params(("parallel", "arbitrary")),
        name="nsa_prompt",
    )(z16, srow, slope, z3, z16, fk, sk, fv, sv, phik, phivt, cpos, ksa, vst, kwa, vwt)


def _by_group(fn):
    hg = lax.broadcasted_iota(jnp.int32, (N_HEADS, 1), 0) >> 2
    out = fn(0)
    for g in range(1, KV_HEADS):
        out = jnp.where(hg == g, fn(g), out)
    return out


def _nsa_sample_kernel(past_len, layer, tab_ref, q_ref, bg_ref, ng_ref, slope_ref,
                       fk_ref, sk_ref, fv_ref, sv_ref, phik_ref, phiv_ref,
                       ksn_ref, vsn_ref, kwn_ref, vwn_ref, bk_ref, bv_ref, kp_hbm, vp_hbm, o_ref,
                       q_scr, sel_scr, oc_scr, ow_scr, m_scr, l_scr, acc_scr, kbuf, vbuf, ksem, vsem):
    slot = _paged_step(tab_ref, layer, PP_SEL, (kp_hbm, vp_hbm), (kbuf, vbuf), (ksem, vsem))
    p = pl.program_id(1)
    slope = slope_ref[:, 0:1]
    gsl = lambda g: slice(g * HEAD_DIM, (g + 1) * HEAD_DIM)

    def per_head(row_ref):
        row = row_ref[0]
        return _by_group(lambda g: jnp.broadcast_to(row[:, gsl(g)], (N_HEADS, HEAD_DIM)))

    @pl.when(p == 0)
    def _():
        qrow = q_ref[0]
        q16 = jnp.concatenate([qrow[:, h * HEAD_DIM:(h + 1) * HEAD_DIM] for h in range(N_HEADS)], axis=0)
        q_scr[...] = q16
        q16f = q16.astype(f32)
        kc = _finish_compress(fk_ref, sk_ref, phik_ref)
        vc = _finish_compress(fv_ref, sv_ref, phiv_ref)

        dist_c = past_len - _cmp_end((1, N_CMP), 1)
        ok_c = dist_c >= 0
        s = _by_group(lambda g: _nt(q16, kc[:, gsl(g)])) - slope * dist_c.astype(f32)
        s = jnp.where(ok_c, s, NEG)
        e = jnp.exp(s - jnp.max(s, axis=-1, keepdims=True))
        pc = jnp.where(ok_c, e * (1.0 / jnp.sum(e, axis=-1, keepdims=True)), 0.0)
        pcb = pc.astype(bf16)
        oc_scr[...] = _by_group(lambda g: _dot(pcb, vc[:, gsl(g)]))
        ps16 = (pc[:, 0:N_BLK] + pc[:, N_BLK:2 * N_BLK]) + (pc[:, 2 * N_BLK:3 * N_BLK] + pc[:, 3 * N_BLK:])
        ps = jnp.concatenate([jnp.sum(ps16[g * Q_PER_KV:(g + 1) * Q_PER_KV], axis=0, keepdims=True)
                              for g in range(KV_HEADS)] + [jnp.zeros((8 - KV_HEADS, N_BLK), f32)], axis=0)
        jl = lax.broadcasted_iota(jnp.int32, (8, N_BLK), 1)
        jb = past_len // SEL_BLK
        forced = jnp.where((jl == 0) | (jl == jb) | (jl == jb - 1), FORCE, 0.0)
        bias8 = _pick_blocks(ps + forced, N_SEL - 1)
        sel_scr[...] = jnp.concatenate(
            [jnp.broadcast_to(bias8[g:g + 1], (Q_PER_KV, N_BLK)) for g in range(KV_HEADS)], axis=0).astype(bf16)

        wb = bk_ref.shape[-1]
        dist_w = wb - lax.broadcasted_iota(jnp.int32, (1, wb), 1)
        ok_w = (dist_w >= 0) & (dist_w < WINDOW)
        bk = bk_ref[0, 0].astype(bf16)
        bv = bv_ref[0, 0].astype(bf16)
        s_buf = _by_group(lambda g: _dot(q16, bk[gsl(g)])) - slope * dist_w.astype(f32)
        s_buf = jnp.where(ok_w, s_buf, NEG)
        s_new = jnp.sum(q16f * per_head(kwn_ref), axis=-1, keepdims=True)
        m_w = jnp.maximum(jnp.max(s_buf, axis=-1, keepdims=True), s_new)
        e_buf = jnp.exp(s_buf - m_w)
        e_new = jnp.exp(s_new - m_w)
        ebb = e_buf.astype(bf16)
        num = _by_group(lambda g: _nt(ebb, bv[gsl(g)])) + e_new * per_head(vwn_ref)
        ow_scr[...] = num * (1.0 / (jnp.sum(e_buf, axis=-1, keepdims=True) + e_new))

        m_scr[...] = jnp.sum(q16f * per_head(ksn_ref), axis=-1, keepdims=True)
        l_scr[...] = jnp.ones_like(l_scr)
        acc_scr[...] = per_head(vsn_ref)

    q16 = q_scr[...]
    n_keys = PP_SEL * PAGE_SIZE
    kp = _side_by_side(kbuf, slot).astype(bf16)
    vp = _side_by_side(vbuf, slot).astype(bf16)
    dist = past_len - (p * n_keys + lax.broadcasted_iota(jnp.int32, (1, n_keys), 1))
    expand = _block_expand(p * (n_keys // SEL_BLK), n_keys)
    s = _by_group(lambda g: _dot(q16, kp[gsl(g)])) - slope * dist.astype(f32) + _dot(sel_scr[...], expand)
    m_old = m_scr[...]
    m_new = jnp.maximum(m_old, jnp.max(s, axis=-1, keepdims=True))
    alpha = jnp.exp(m_old - m_new)
    pr = jnp.exp(s - m_new)
    prb = pr.astype(bf16)
    l_scr[...] = alpha * l_scr[...] + jnp.sum(pr, axis=-1, keepdims=True)
    acc_scr[...] = alpha * acc_scr[...] + _by_group(lambda g: _nt(prb, vp[gsl(g)]))
    m_scr[...] = m_new

    @pl.when(p == pl.num_programs(1) - 1)
    def _():
        o_s = acc_scr[...] * (1.0 / l_scr[...])
        gates = jax.nn.sigmoid(bg_ref[0])
        lane = lax.broadcasted_iota(jnp.int32, (N_HEADS, BG_W), 1)
        h3 = 3 * lax.broadcasted_iota(jnp.int32, (N_HEADS, BG_W), 0)
        gate = lambda n: jnp.sum(jnp.where(lane == h3 + n, gates, 0.0), axis=-1, keepdims=True)
        y16 = gate(0) * oc_scr[...] + gate(1) * o_s + gate(2) * ow_scr[...]
        y = jnp.concatenate([y16[h:h + 1, :] for h in range(N_HEADS)], axis=-1)
        o_ref[0] = (y * _silu(ng_ref[0])).astype(bf16)


def _nsa_sample(table, layer, qb, z2, z2h, slopes, fk, sk, fv, sv, phik, phiv, pool_k, pool_v, buf_k, buf_v,
                past_len):
    nb = qb.shape[0]
    npg = table.shape[0] // nb
    z3 = z2.reshape(nb, 1, Z32_W)
    z3h = z2h.reshape(nb, 1, Z16_W)
    tok = lambda w, c: pl.BlockSpec((1, 1, w), lambda b, p, tab: (b, 0, c))
    cmp = lambda: pl.BlockSpec((1, 4, N_CMP // 4, KV_W), lambda b, p, tab: (b, 0, 0, 0))
    phi = lambda: pl.BlockSpec((KV_W, KV_W), lambda b, p, tab: (0, 0))
    wb = buf_k.shape[-1]
    win = lambda: pl.BlockSpec((1, 1, KV_W, wb), lambda b, p, tab: (layer, b, 0, 0))
    hbm = lambda: pl.BlockSpec(memory_space=pl.ANY)
    page_buf = lambda: pltpu.VMEM((PAGE_SLOTS, PP_SEL, KV_W, PAGE_SIZE), f32)
    kvc = C_KV // KV_W
    return pl.pallas_call(
        functools.partial(_nsa_sample_kernel, past_len, layer),
        out_shape=jax.ShapeDtypeStruct((nb, 1, BRANCH_W), bf16),
        grid_spec=pltpu.PrefetchScalarGridSpec(
            num_scalar_prefetch=1,
            grid=(nb, npg // PP_SEL),
            in_specs=[tok(BRANCH_W, 0), tok(BG_W, C_BG // BG_W), tok(BRANCH_W, C_NSA_G // BRANCH_W),
                      pl.BlockSpec((N_HEADS, 128), lambda b, p, tab: (0, 0)),
                      cmp(), cmp(), cmp(), cmp(), phi(), phi(),
                      tok(KV_W, kvc + 2), tok(KV_W, kvc + 3), tok(KV_W, kvc + 4), tok(KV_W, kvc + 5),
                      win(), win(), hbm(), hbm()],
            out_specs=pl.BlockSpec((1, 1, BRANCH_W), lambda b, p, tab: (b, 0, 0)),
            scratch_shapes=[pltpu.VMEM((N_HEADS, HEAD_DIM), bf16), pltpu.VMEM((N_HEADS, N_BLK), bf16),
                            pltpu.VMEM((N_HEADS, HEAD_DIM), f32), pltpu.VMEM((N_HEADS, HEAD_DIM), f32),
                            pltpu.VMEM((N_HEADS, 1), f32), pltpu.VMEM((N_HEADS, 1), f32),
                            pltpu.VMEM((N_HEADS, HEAD_DIM), f32), page_buf(), page_buf(),
                            pltpu.SemaphoreType.DMA((PAGE_SLOTS,)), pltpu.SemaphoreType.DMA((PAGE_SLOTS,))]),
        compiler_params=_params(("arbitrary", "arbitrary")),
        name="nsa_sample",
    )(table, qb.reshape(nb, 1, BRANCH_W), z3, z3h, slopes, fk, sk, fv, sv, phik, phiv,
      z3, z3, z3, z3, buf_k, buf_v, pool_k, pool_v)


def _merge_kernel(zl_ref, zp_ref, zn_ref, m0_ref, m1_ref, m2_ref, wb_ref, wo_ref, g_ref, x_ref, y_ref):
    acc = None
    for n, (zz, mg) in enumerate(((zl_ref, m0_ref), (zp_ref, m1_ref), (zn_ref, m2_ref))):
        term = jax.nn.sigmoid(mg[...].astype(f32)) * _dot(zz[...], wb_ref[n])
        acc = term if acc is None else acc + term
    out = _dot(acc.astype(bf16), wo_ref[...])
    ms = jnp.mean(out * out, axis=-1, keepdims=True)
    y_ref[...] = x_ref[...] + out * lax.rsqrt(ms + EPS) * g_ref[...]


def _merge(zl, zp, zn, z16, wb, wo, g_row, x2d):
    n = x2d.shape[0]
    tm = min(n, 256)
    rowblk = lambda c: pl.BlockSpec((tm, D_MODEL), lambda i: (i, c))
    return pl.pallas_call(
        _merge_kernel,
        out_shape=jax.ShapeDtypeStruct((n, D_MODEL), f32),
        grid=(n // tm,),
        in_specs=[rowblk(0), rowblk(0), rowblk(0),
                  rowblk(C_MG // D_MODEL), rowblk(C_MG // D_MODEL + 1), rowblk(C_MG // D_MODEL + 2),
                  pl.BlockSpec((N_BRANCH, BRANCH_W, D_MODEL), lambda i: (0, 0, 0)),
                  pl.BlockSpec((D_MODEL, D_MODEL), lambda i: (0, 0)),
                  pl.BlockSpec((1, D_MODEL), lambda i: (0, 0)),
                  rowblk(0)],
        out_specs=rowblk(0),
        compiler_params=_params(("parallel",)),
        name="merge",
    )(zl, zp, zn, z16, z16, z16, wb, wo, g_row, x2d)


def _block_diag(w, per):
    n, d, _ = w.shape
    eye = jnp.eye(per, dtype=w.dtype)
    t = jnp.einsum('cpde,pq->cpdqe', w.reshape(n // per, per, d, d), eye)
    return t.reshape(n // per, per * d, per * d)


def _pack_w_in(w):
    seg = lambda i: w[:, i * BRANCH_W:(i + 1) * BRANCH_W]
    old_kv = 6 * BRANCH_W
    old_bg = old_kv + 6 * KV_W
    old_mg = old_bg + N_BRANCH * N_HEADS
    f32_part = jnp.concatenate([seg(0), seg(2), w[:, old_kv:old_bg], w[:, old_bg:old_mg]], axis=1)
    f32_part = jnp.pad(f32_part, ((0, 0), (0, Z32_W - f32_part.shape[1])))
    bf16_part = jnp.concatenate([seg(1), seg(3), seg(4), seg(5), w[:, old_mg:]], axis=1)
    return jnp.concatenate([f32_part, bf16_part], axis=1).astype(bf16)


def _tile_wpos(w_pos):
    halves = w_pos.reshape(2, CMP_STRIDE, HEAD_DIM)
    return jnp.tile(halves, (1, PAGE_SIZE // CMP_STRIDE, KV_HEADS))


def _tile_wpos_t(w_pos):
    halves = w_pos.reshape(2, CMP_STRIDE, HEAD_DIM).swapaxes(1, 2)
    return jnp.tile(halves, (1, KV_HEADS, PAGE_SIZE // CMP_STRIDE))


def _lanes_last(cache):
    d, n, rows = cache.shape[:3]
    return jnp.transpose(cache, (0, 1, 3, 4, 2)).reshape(d, n, KV_W, rows)


def _slope_parts():
    cols = np.zeros((N_HEADS, POS_ROWS), np.float32)
    rnd = lambda v: np.float32(np.float32(v).astype(bf16))
    for h, s in enumerate(SLOPES_LOG2):
        s1 = rnd(s)
        s2 = rnd(np.float32(s) - s1)
        s3 = rnd(np.float32(s) - s1 - s2)
        cols[h, 0:N_PARTS] = [s1, s2, s3] * 3
    return cols


def _split_pos(pos, shift):
    cols = np.zeros((pos.shape[0], POS_ROWS), np.float32)
    cols[:, 0:3] = ((pos >> shift) << shift)[:, None]
    cols[:, 3:6] = (pos & ((1 << shift) - 1))[:, None]
    return cols


def _key_aug(k_rows, tile):
    b, s, _ = k_rows.shape
    pos = jnp.asarray(_split_pos(np.arange(s) % tile, 4), bf16)
    kg = k_rows.astype(bf16).reshape(b, s, KV_HEADS, HEAD_DIM)
    posb = jnp.broadcast_to(pos[None, :, None, :], (b, s, KV_HEADS, POS_ROWS))
    return jnp.concatenate([kg, posb], axis=-1).reshape(b, s, KV_HEADS * QA_W)


def _value_tiles(v_rows, tile):
    b, s, _ = v_rows.shape
    vt = v_rows.astype(bf16).reshape(b, s // tile, tile, KV_HEADS, HEAD_DIM).transpose(0, 1, 3, 4, 2)
    ones = jnp.ones((b, s // tile, KV_HEADS, V_ROWS - HEAD_DIM, tile), bf16)
    return jnp.concatenate([vt, ones], axis=3).reshape(b, s // tile, KV_HEADS * V_ROWS, tile)


def _cmp_pos_cols():
    slot = np.arange(N_CMP)
    n = ((slot & (N_BLK - 1)) << 2) + (slot >> 7)
    return jnp.asarray(_split_pos(n * CMP_STRIDE + CMP_BLK - 1, 8), bf16)


def _kv_seg(z, i):
    return z[..., C_KV + i * KV_W:C_KV + (i + 1) * KV_W]


def kernel(x_prompt, x_sample, cache_cmp_k, cache_cmp_v, cache_sel_k, cache_sel_v, cache_win_k, cache_win_v, state_conv, state_lru, state_pool, page_table, g_pre, g_post, w_in, conv_w, conv_b, w_rg_a, b_rg_a, w_rg_x, b_rg_x, lru_lambda, w_pool, pool_scale, cmp_pos_k, cmp_phi_k, cmp_pos_v, cmp_phi_v, w_branch, w_out):
    depth = w_in.shape[0]
    bp, seq, _ = x_prompt.shape
    bs = x_sample.shape[0]
    n_pages = page_table.shape[1]
    past_len = n_pages * PAGE_SIZE
    n_phys = cache_cmp_k.shape[1]
    assert seq == N_BLK * SEL_BLK and past_len == N_BLK * SEL_BLK and x_sample.shape[1] == 1
    wb = cache_win_k.shape[2]

    table_s = page_table.reshape(-1).astype(jnp.int32)
    table_p = jnp.arange(bp * (seq // PAGE_SIZE), dtype=jnp.int32)
    slopes = jnp.broadcast_to(jnp.asarray(SLOPES, f32)[:, None], (N_HEADS, 128))
    row = lambda v: v.reshape(1, -1)
    cmp_kt, cmp_vt = _lanes_last(cache_cmp_k), _lanes_last(cache_cmp_v)
    sel_kt, sel_vt = _lanes_last(cache_sel_k), _lanes_last(cache_sel_v)
    win_kt, win_vt = _lanes_last(cache_win_k), _lanes_last(cache_win_v)

    xp = x_prompt.reshape(bp * seq, D_MODEL)
    xs = x_sample.reshape(bs, D_MODEL)
    pr = [[] for _ in range(9)]
    sm = [[] for _ in range(9)]
    for l in range(depth):
        w_packed = _pack_w_in(w_in[l])
        wa = _block_diag(w_rg_a[l], MXU_W // LRU_BD).astype(bf16)
        wx = _block_diag(w_rg_x[l], MXU_W // LRU_BD).astype(bf16)
        wp = w_pool[l].astype(bf16)
        phik = _block_diag(jnp.broadcast_to(cmp_phi_k[l], (KV_HEADS, HEAD_DIM, HEAD_DIM)), KV_HEADS)[0].astype(bf16)
        phiv = _block_diag(jnp.broadcast_to(cmp_phi_v[l], (KV_HEADS, HEAD_DIM, HEAD_DIM)), KV_HEADS)[0].astype(bf16)
        wpos_k = _tile_wpos(cmp_pos_k[l])
        wpos_v = _tile_wpos(cmp_pos_v[l])
        wbr = w_branch[l].astype(bf16)
        wo = w_out[l].astype(bf16)
        lru_w = (conv_w[l], row(conv_b[l]), wa, row(b_rg_a[l]), wx, row(b_rg_x[l]), row(lru_lambda[l]))

        z, z16 = _inproj(xp, row(g_pre[l]), w_packed)
        z3 = z.reshape(bp, seq, Z32_W)
        z16_3 = z16.reshape(bp, seq, Z16_W)
        zl, conv_tail, h_p = _lru_prompt(z3, z16_3, *lru_w)
        zpool, pool_tail = _pool_prompt(z3, z16_3, wp, row(pool_scale[l]))
        zr = z.reshape(bp * seq // PAGE_SIZE, PAGE_SIZE, Z32_W)
        fk, sk, fv, sv = _compress(table_p, zr, zr, C_KV // KV_W, C_KV // KV_W + 1, wpos_k, wpos_v, bp)
        zn = _nsa_prompt(z3, z16_3, fk, sk, fv, sv, phik, phiv.T)
        xp = _merge(zl.reshape(bp * seq, BRANCH_W), zpool.reshape(bp * seq, BRANCH_W),
                    zn.reshape(bp * seq, BRANCH_W), z16, wbr, wo, row(g_post[l]), xp)
        kv_rows = [_kv_seg(z3, i).reshape(bp, seq, KV_HEADS, HEAD_DIM) for i in range(6)]
        wlen = min(WINDOW, seq)
        st_p = kv_rows[:4] + [kv_rows[4][:, -wlen:], kv_rows[5][:, -wlen:],
                              conv_tail[:, -(CONV_W - 1):], h_p[:, 0], pool_tail[:, -POOL_BUF:]]

        zs, zs16 = _inproj(xs, row(g_pre[l]), w_packed)
        zls, zps, h_s = _mix_sample(zs, zs16, state_conv[l].swapaxes(0, 1), state_lru[l],
                                    state_pool[l].swapaxes(0, 1), *lru_w, wp, row(pool_scale[l]), past_len)
        fk, sk, fv, sv = _compress_t(table_s, l, cmp_kt, cmp_vt, _tile_wpos_t(cmp_pos_k[l]),
                                     _tile_wpos_t(cmp_pos_v[l]), bs)
        qs = (zs16[:, C_Q:C_Q + BRANCH_W].astype(f32) * (HEAD_DIM ** -0.5)).astype(bf16)
        zns = _nsa_sample(table_s, l, qs, zs, zs16, slopes, fk, sk, fv, sv, phik, phiv,
                          sel_kt, sel_vt, win_kt, win_vt, past_len)
        xs = _merge(zls, zps, zns.reshape(bs, BRANCH_W), zs16, wbr, wo, row(g_post[l]), xs)
        kv_new = [_kv_seg(zs, i).reshape(bs, 1, KV_HEADS, HEAD_DIM) for i in range(6)]
        st_s = kv_new[:4] + [jnp.concatenate([cache_win_k[l], kv_new[4]], axis=1)[:, -wb:],
                             jnp.concatenate([cache_win_v[l], kv_new[5]], axis=1)[:, -wb:],
                             jnp.concatenate([state_conv[l], zs[:, None, C_LRU_X:C_LRU_X + BRANCH_W]], axis=1)[:, -(CONV_W - 1):],
                             h_s,
                             jnp.concatenate([state_pool[l], zs[:, None, C_POOL_X:C_POOL_X + BRANCH_W]], axis=1)[:, -POOL_BUF:]]
        for i in range(9):
            pr[i].append(st_p[i])
            sm[i].append(st_s[i])

    out = [xp.reshape(bp, seq, D_MODEL), xs.reshape(bs, 1, D_MODEL)]
    for i in range(9):
        out += [jnp.stack(pr[i]), jnp.stack(sm[i])]
    return tuple(out)
```

```python
import functools

import numpy as np
import jax
import jax.numpy as jnp
from jax import lax
from jax.experimental import pallas as pl
from jax.experimental.pallas import tpu as pltpu

f32 = jnp.float32
bf16 = jnp.bfloat16

D_MODEL = 1024
BRANCH_W = 1024
N_BRANCH = 3
LRU_BLOCKS = 16
LRU_BD = BRANCH_W // LRU_BLOCKS
CONV_W = 4
LRU_C = 8.0
POOL_WINDOWS = (2, 4, 8, 16)
POOL_GD = BRANCH_W // len(POOL_WINDOWS)
POOL_BUF = max(POOL_WINDOWS) - 1
N_HEADS = 16
HEAD_DIM = 64
KV_HEADS = 4
Q_PER_KV = N_HEADS // KV_HEADS
KV_W = KV_HEADS * HEAD_DIM
CMP_STRIDE = 16
CMP_BLK = 2 * CMP_STRIDE
SEL_BLK = 64
N_SEL = 16
WINDOW = 512
PAGE_SIZE = 128
FORCE = 1e4
NEG = -1e30
EPS = 1e-6

C_LRU_X, C_POOL_X, C_KV = 0, 1024, 2048
C_BG = C_KV + 6 * KV_W
BG_W = 128
Z32_W = 4096
C_LRU_G, C_POOL_G, C_Q, C_NSA_G, C_MG = 0, 1024, 2048, 3072, 4096
Z16_W = C_MG + N_BRANCH * D_MODEL
IN_TN = 1024
N32_TILES = Z32_W // IN_TN

MXU_W = 256
VMEM_LIMIT = 56 * 1024 * 1024

N_CMP = 512
N_BLK = 128
TQ = 256
TK = 512
PP = 8
PP_SEL = 16
PAGE_SLOTS = 4
POS_ROWS = 64
QA_W = HEAD_DIM + POS_ROWS

SLOPES = [float(np.float32(2.0 ** (-8.0 * (h + 1) / N_HEADS))) for h in range(N_HEADS)]
LOG2E = float(np.log2(np.e))
SLOPES_LOG2 = [float(np.float32(s * LOG2E)) for s in SLOPES]
Q_SCALE = float(np.float32(HEAD_DIM ** -0.5 * LOG2E))
LOOP_GROUPS = ((0,), (1,), (2,), (3,))


def _nt(a, b):
    return lax.dot_general(a, b, (((1,), (1,)), ((), ())), preferred_element_type=f32)


def _dot(a, b):
    return jnp.dot(a, b, preferred_element_type=f32)


def _silu(x):
    x = x.astype(f32)
    return x * jax.nn.sigmoid(x)


def _params(sem):
    return pltpu.CompilerParams(dimension_semantics=sem, vmem_limit_bytes=VMEM_LIMIT)


def _inproj_kernel(x_ref, g_ref, w_ref, o32_ref, o16_ref, u_ref):
    j = pl.program_id(1)

    @pl.when(j == 0)
    def _():
        x = x_ref[...]
        ms = jnp.mean(x * x, axis=-1, keepdims=True)
        u_ref[...] = (x * lax.rsqrt(ms + EPS) * g_ref[...]).astype(bf16)

    @pl.when(j < N32_TILES)
    def _():
        o32_ref[...] = _dot(u_ref[...], w_ref[...])

    @pl.when(j >= N32_TILES)
    def _():
        o16_ref[...] = _dot(u_ref[...], w_ref[...]).astype(bf16)


def _inproj(x2d, g_row, w_packed):
    n = x2d.shape[0]
    tm = min(n, 1024)
    return pl.pallas_call(
        _inproj_kernel,
        out_shape=(jax.ShapeDtypeStruct((n, Z32_W), f32), jax.ShapeDtypeStruct((n, Z16_W), bf16)),
        grid=(n // tm, (Z32_W + Z16_W) // IN_TN),
        in_specs=[pl.BlockSpec((tm, D_MODEL), lambda i, j: (i, 0)),
                  pl.BlockSpec((1, D_MODEL), lambda i, j: (0, 0)),
                  pl.BlockSpec((D_MODEL, IN_TN), lambda i, j: (0, j))],
        out_specs=(pl.BlockSpec((tm, IN_TN), lambda i, j: (i, jnp.minimum(j, N32_TILES - 1))),
                   pl.BlockSpec((tm, IN_TN), lambda i, j: (i, jnp.maximum(j - N32_TILES, 0)))),
        scratch_shapes=[pltpu.VMEM((tm, D_MODEL), bf16)],
        compiler_params=_params(("parallel", "arbitrary")),
        name="inproj",
    )(x2d, g_row, w_packed)


def _lru_gates(xc, wa_ref, ba_ref, wx_ref, bx_ref, lam_ref):
    xb = xc.astype(bf16)
    ra, ri = [], []
    for c in range(BRANCH_W // MXU_W):
        sl = slice(c * MXU_W, (c + 1) * MXU_W)
        ra.append(_dot(xb[:, sl], wa_ref[c]))
        ri.append(_dot(xb[:, sl], wx_ref[c]))
    r = jax.nn.sigmoid(jnp.concatenate(ra, axis=-1) + ba_ref[...])
    i = jax.nn.sigmoid(jnp.concatenate(ri, axis=-1) + bx_ref[...])
    nl = -lam_ref[...]
    softplus = jnp.maximum(nl, 0.0) + jnp.log1p(jnp.exp(-jnp.abs(nl)))
    log_a = -LRU_C * r * softplus
    a = jnp.exp(log_a)
    b = jnp.sqrt(1.0 - a * a) * (i * xc)
    return a, b


def _lru_kernel(x_ref, g_ref, cw_ref, cb_ref, wa_ref, ba_ref, wx_ref, bx_ref, lam_ref,
                zb_ref, tail_ref, h_ref, xs_ref, a_ref, b_ref, hc_ref):
    tt = x_ref.shape[1]

    @pl.when(pl.program_id(1) == 0)
    def _():
        xs_ref[0:8, :] = jnp.zeros((8, BRANCH_W), f32)
        hc_ref[...] = jnp.zeros_like(hc_ref)

    x = x_ref[0]
    xs_ref[8:, :] = x
    xc = cb_ref[...] + x * cw_ref[CONV_W - 1:CONV_W, :]
    for k in range(CONV_W - 1):
        xc = xc + xs_ref[pl.ds(8 - (CONV_W - 1 - k), tt), :] * cw_ref[k:k + 1, :]
    xs_ref[0:8, :] = x[tt - 8:, :]
    tail_ref[0] = x[tt - 8:, :]

    a, b = _lru_gates(xc, wa_ref, ba_ref, wx_ref, bx_ref, lam_ref)
    a_ref[...] = a
    b_ref[...] = b
    row = lax.broadcasted_iota(jnp.int32, (8, BRANCH_W), 0)

    def body(i, h):
        r0 = pl.multiple_of(i * 8, 8)
        av = a_ref[pl.ds(r0, 8), :]
        bv = b_ref[pl.ds(r0, 8), :]
        for d in (1, 2, 4):
            a_s = jnp.where(row >= d, pltpu.roll(av, d, 0), 1.0)
            b_s = jnp.where(row >= d, pltpu.roll(bv, d, 0), 0.0)
            bv = av * b_s + bv
            av = av * a_s
        hs = bv + av * h
        b_ref[pl.ds(r0, 8), :] = hs
        return hs[7:8, :]

    h = lax.fori_loop(0, tt // 8, body, hc_ref[...])
    hc_ref[...] = h
    h_ref[0] = h
    zb_ref[0] = (b_ref[...] * _silu(g_ref[0])).astype(bf16)


def _lru_prompt(z32, z16, cw, cb, wa, ba, wx, bx, lam):
    b, s, _ = z32.shape
    tt = min(s, 512)
    row = lambda: pl.BlockSpec((1, BRANCH_W), lambda i, t: (0, 0))
    bd = lambda: pl.BlockSpec((BRANCH_W // MXU_W, MXU_W, MXU_W), lambda i, t: (0, 0, 0))
    return pl.pallas_call(
        _lru_kernel,
        out_shape=(jax.ShapeDtypeStruct((b, s, BRANCH_W), bf16),
                   jax.ShapeDtypeStruct((b, 8, BRANCH_W), f32),
                   jax.ShapeDtypeStruct((b, 1, BRANCH_W), f32)),
        grid=(b, s // tt),
        in_specs=[pl.BlockSpec((1, tt, BRANCH_W), lambda i, t: (i, t, C_LRU_X // BRANCH_W)),
                  pl.BlockSpec((1, tt, BRANCH_W), lambda i, t: (i, t, C_LRU_G // BRANCH_W)),
                  pl.BlockSpec((CONV_W, BRANCH_W), lambda i, t: (0, 0)),
                  row(), bd(), row(), bd(), row(), row()],
        out_specs=(pl.BlockSpec((1, tt, BRANCH_W), lambda i, t: (i, t, 0)),
                   pl.BlockSpec((1, 8, BRANCH_W), lambda i, t: (i, 0, 0)),
                   pl.BlockSpec((1, 1, BRANCH_W), lambda i, t: (i, 0, 0))),
        scratch_shapes=[pltpu.VMEM((tt + 8, BRANCH_W), f32), pltpu.VMEM((tt, BRANCH_W), f32),
                        pltpu.VMEM((tt, BRANCH_W), f32), pltpu.VMEM((1, BRANCH_W), f32)],
        compiler_params=_params(("parallel", "arbitrary")),
        name="lru_prompt",
    )(z32, z16, cw, cb, wa, ba, wx, bx, lam)


def _pool_kernel(x_ref, g_ref, wp_ref, sc_ref, zb_ref, tail_ref, xs_ref):
    tt = x_ref.shape[1]
    t = pl.program_id(1)

    @pl.when(t == 0)
    def _():
        xs_ref[0:16, :] = jnp.zeros((16, BRANCH_W), f32)

    x = x_ref[0]
    xs_ref[16:, :] = x
    pos1 = t * tt + 1 + lax.broadcasted_iota(jnp.int32, (tt, POOL_GD), 0)
    outs = []
    for gi, w in enumerate(POOL_WINDOWS):
        sl = slice(gi * POOL_GD, (gi + 1) * POOL_GD)
        s = xs_ref[:, sl]
        sh = 1
        while sh < w:
            s = s + pltpu.roll(s, sh, 0)
            sh *= 2
        cnt = jnp.minimum(w, pos1).astype(f32)
        pooled = s[16:, :] / cnt - x[:, sl]
        outs.append(_dot(pooled.astype(bf16), wp_ref[gi]))
    y = jnp.concatenate(outs, axis=-1) * sc_ref[...]
    zb_ref[0] = (y * _silu(g_ref[0])).astype(bf16)
    xs_ref[0:16, :] = x[tt - 16:, :]
    tail_ref[0] = x[tt - 16:, :]


def _pool_prompt(z32, z16, wp, sc):
    b, s, _ = z32.shape
    tt = min(s, 512)
    return pl.pallas_call(
        _pool_kernel,
        out_shape=(jax.ShapeDtypeStruct((b, s, BRANCH_W), bf16),
                   jax.ShapeDtypeStruct((b, 16, BRANCH_W), f32)),
        grid=(b, s // tt),
        in_specs=[pl.BlockSpec((1, tt, BRANCH_W), lambda i, t: (i, t, C_POOL_X // BRANCH_W)),
                  pl.BlockSpec((1, tt, BRANCH_W), lambda i, t: (i, t, C_POOL_G // BRANCH_W)),
                  pl.BlockSpec((len(POOL_WINDOWS), POOL_GD, POOL_GD), lambda i, t: (0, 0, 0)),
                  pl.BlockSpec((1, BRANCH_W), lambda i, t: (0, 0))],
        out_specs=(pl.BlockSpec((1, tt, BRANCH_W), lambda i, t: (i, t, 0)),
                   pl.BlockSpec((1, 16, BRANCH_W), lambda i, t: (i, 0, 0))),
        scratch_shapes=[pltpu.VMEM((tt + 16, BRANCH_W), f32)],
        compiler_params=_params(("parallel", "arbitrary")),
        name="pool_prompt",
    )(z32, z16, wp, sc)


def _mix_sample_kernel(past_len, lx_ref, lg_ref, px_ref, pg_ref, conv_ref, h0_ref, pbuf_ref,
                       cw_ref, cb_ref, wa_ref, ba_ref, wx_ref, bx_ref, lam_ref, wp_ref, sc_ref,
                       zl_ref, zp_ref, h_ref):
    x = lx_ref[...]
    xc = cb_ref[...] + x * cw_ref[CONV_W - 1:CONV_W, :]
    for k in range(CONV_W - 1):
        xc = xc + conv_ref[k] * cw_ref[k:k + 1, :]
    a, b = _lru_gates(xc, wa_ref, ba_ref, wx_ref, bx_ref, lam_ref)
    h = a * h0_ref[...] + b
    h_ref[...] = h
    zl_ref[...] = (h * _silu(lg_ref[...])).astype(bf16)

    px = px_ref[...]
    outs = []
    for gi, w in enumerate(POOL_WINDOWS):
        sl = slice(gi * POOL_GD, (gi + 1) * POOL_GD)
        s = px[:, sl]
        for k in range(1, w):
            s = s + pbuf_ref[POOL_BUF - k][:, sl]
        cnt = float(min(w, past_len + 1))
        pooled = s / cnt - px[:, sl]
        outs.append(_dot(pooled.astype(bf16), wp_ref[gi]))
    y = jnp.concatenate(outs, axis=-1) * sc_ref[...]
    zp_ref[...] = (y * _silu(pg_ref[...])).astype(bf16)


def _mix_sample(z32, z16, conv_t, h0, pbuf_t, cw, cb, wa, ba, wx, bx, lam, wp, sc, past_len):
    n = z32.shape[0]
    col = lambda c: pl.BlockSpec((n, BRANCH_W), lambda i: (0, c // BRANCH_W))
    full = lambda a: pl.BlockSpec(a.shape, lambda i: (0,) * a.ndim)
    args = (conv_t, h0, pbuf_t, cw, cb, wa, ba, wx, bx, lam, wp, sc)
    return pl.pallas_call(
        functools.partial(_mix_sample_kernel, past_len),
        out_shape=(jax.ShapeDtypeStruct((n, BRANCH_W), bf16), jax.ShapeDtypeStruct((n, BRANCH_W), bf16),
                   jax.ShapeDtypeStruct((n, BRANCH_W), f32)),
        grid=(1,),
        in_specs=[col(C_LRU_X), col(C_LRU_G), col(C_POOL_X), col(C_POOL_G)] + [full(a) for a in args],
        out_specs=(pl.BlockSpec((n, BRANCH_W), lambda i: (0, 0)),) * 3,
        compiler_params=_params(("arbitrary",)),
        name="mix_sample",
    )(z32, z16, z32, z16, *args)


def _compress_kernel(tab_ref, k_ref, v_ref, wk_ref, wv_ref, fk_ref, sk_ref, fv_ref, sv_ref):
    del tab_ref
    p = pl.program_id(1)
    chunks = PAGE_SIZE // CMP_STRIDE

    @pl.when(p == 0)
    def _():
        sk_ref[...] = jnp.zeros_like(sk_ref)
        sv_ref[...] = jnp.zeros_like(sv_ref)

    for src, w_ref, f_ref, s_ref in ((k_ref, wk_ref, fk_ref, sk_ref), (v_ref, wv_ref, fv_ref, sv_ref)):
        tile = src[0]
        first = jnp.sum((tile * w_ref[0]).reshape(chunks, CMP_STRIDE, KV_W), axis=1)
        second = jnp.sum((tile * w_ref[1]).reshape(chunks, CMP_STRIDE, KV_W), axis=1)
        for m in range(chunks):
            f_ref[0, m % 4, pl.ds(2 * p + m // 4, 1), :] = first[m:m + 1, :]
            if m >= 1:
                s_ref[0, (m - 1) % 4, pl.ds(2 * p + (m - 1) // 4, 1), :] = second[m:m + 1, :]
            else:
                @pl.when(p > 0)
                def _():
                    s_ref[0, 3, pl.ds(2 * p - 1, 1), :] = second[0:1, :]


def _compress(table, src_k, src_v, col_k, col_v, wk, wv, nb):
    npg = table.shape[0] // nb
    out = jax.ShapeDtypeStruct((nb, 4, N_CMP // 4, KV_W), f32)
    ospec = lambda: pl.BlockSpec((1, 4, N_CMP // 4, KV_W), lambda b, p, tab: (b, 0, 0, 0))
    return pl.pallas_call(
        _compress_kernel,
        out_shape=(out,) * 4,
        grid_spec=pltpu.PrefetchScalarGridSpec(
            num_scalar_prefetch=1,
            grid=(nb, npg),
            in_specs=[pl.BlockSpec((1, PAGE_SIZE, KV_W), lambda b, p, tab: (tab[b * npg + p], 0, col_k)),
                      pl.BlockSpec((1, PAGE_SIZE, KV_W), lambda b, p, tab: (tab[b * npg + p], 0, col_v)),
                      pl.BlockSpec((2, PAGE_SIZE, KV_W), lambda b, p, tab: (0, 0, 0)),
                      pl.BlockSpec((2, PAGE_SIZE, KV_W), lambda b, p, tab: (0, 0, 0))],
            out_specs=(ospec(), ospec(), ospec(), ospec())),
        compiler_params=_params(("parallel", "arbitrary")),
        name="compress",
    )(table, src_k, src_v, wk, wv)


def _chunk_maps():
    n_chunk = PP * PAGE_SIZE // CMP_STRIDE
    rows = n_chunk // 4
    chunk_of = np.arange(PP * PAGE_SIZE) // CMP_STRIDE
    first = np.zeros((n_chunk, PP * PAGE_SIZE), np.float32)
    second = np.zeros((n_chunk + 8, PP * PAGE_SIZE), np.float32)
    for i in range(4):
        for jj in range(rows):
            first[i * rows + jj] = chunk_of == 4 * jj + i
            second[i * rows + jj] = chunk_of == 4 * jj + i + 1
    second[n_chunk] = chunk_of == 0
    return jnp.asarray(first, bf16), jnp.asarray(second, bf16)


def _page_copies(tab_ref, layer, step, n_pages, pools, bufs, sems, wait):
    slot = step % PAGE_SLOTS
    for i in range(n_pages):
        page = 0 if wait else tab_ref[step * n_pages + i]
        for pool, buf, sem in zip(pools, bufs, sems):
            copy = pltpu.make_async_copy(pool.at[layer, page],
                                         buf.at[slot, i], sem.at[slot])
            if wait:
                copy.wait()
            else:
                copy.start()


def _side_by_side(buf, slot):
    return jnp.concatenate([buf[slot, i] for i in range(buf.shape[1])], axis=1)


def _paged_step(tab_ref, layer, n_pages, pools, bufs, sems):
    ahead = PAGE_SLOTS - 1
    step = pl.program_id(0) * pl.num_programs(1) + pl.program_id(1)
    n_steps = pl.num_programs(0) * pl.num_programs(1)

    for first in range(ahead):
        @pl.when((step == 0) & (first < n_steps))
        def _(first=first):
            _page_copies(tab_ref, layer, first, n_pages, pools, bufs, sems, wait=False)

    @pl.when(step + ahead < n_steps)
    def _():
        _page_copies(tab_ref, layer, step + ahead, n_pages, pools, bufs, sems, wait=False)

    _page_copies(tab_ref, layer, step, n_pages, pools, bufs, sems, wait=True)
    return step % PAGE_SLOTS


def _compress_t_kernel(layer, tab_ref, k_hbm, v_hbm, wk_ref, wv_ref, ea_ref, eb_ref,
                       fk_ref, sk_ref, fv_ref, sv_ref, kbuf, vbuf, ksem, vsem):
    slot = _paged_step(tab_ref, layer, PP, (k_hbm, v_hbm), (kbuf, vbuf), (ksem, vsem))
    ps = pl.program_id(1)
    rows = PP * PAGE_SIZE // CMP_STRIDE // 4
    r0 = pl.multiple_of(ps * rows, rows)
    for buf, w_ref, f_ref, s_ref in ((kbuf, wk_ref, fk_ref, sk_ref), (vbuf, wv_ref, fv_ref, sv_ref)):
        pages = _side_by_side(buf, slot)
        a1 = (pages * w_ref[0]).astype(bf16)
        a2 = (pages * w_ref[1]).astype(bf16)
        first = _nt(ea_ref[...], a1)
        second = _nt(eb_ref[...], a2)
        for i in range(4):
            f_ref[0, i, pl.ds(r0, rows), :] = first[i * rows:(i + 1) * rows]
            s_ref[0, i, pl.ds(r0, rows), :] = second[i * rows:(i + 1) * rows]

        @pl.when(ps > 0)
        def _():
            s_ref[0, 3, pl.ds(r0 - 1, 1), :] = second[4 * rows:4 * rows + 1]


def _compress_t(table, layer, cache_k, cache_v, wk, wv, nb):
    npg = table.shape[0] // nb
    ea, eb = _chunk_maps()
    out = jax.ShapeDtypeStruct((nb, 4, N_CMP // 4, KV_W), f32)
    ospec = lambda: pl.BlockSpec((1, 4, N_CMP // 4, KV_W), lambda b, p, tab: (b, 0, 0, 0))
    full = lambda a: pl.BlockSpec(a.shape, lambda b, p, tab: (0,) * a.ndim)
    hbm = lambda: pl.BlockSpec(memory_space=pl.ANY)
    wk, wv = jnp.tile(wk, (1, 1, PP)), jnp.tile(wv, (1, 1, PP))
    return pl.pallas_call(
        functools.partial(_compress_t_kernel, layer),
        out_shape=(out,) * 4,
        grid_spec=pltpu.PrefetchScalarGridSpec(
            num_scalar_prefetch=1,
            grid=(nb, npg // PP),
            in_specs=[hbm(), hbm(), full(wk), full(wv), full(ea), full(eb)],
            out_specs=(ospec(), ospec(), ospec(), ospec()),
            scratch_shapes=[pltpu.VMEM((PAGE_SLOTS, PP, KV_W, PAGE_SIZE), f32),
                            pltpu.VMEM((PAGE_SLOTS, PP, KV_W, PAGE_SIZE), f32),
                            pltpu.SemaphoreType.DMA((PAGE_SLOTS,)), pltpu.SemaphoreType.DMA((PAGE_SLOTS,))]),
        compiler_params=_params(("arbitrary", "arbitrary")),
        name="compress_t",
    )(table, cache_k, cache_v, wk, wv, ea, eb)


def _finish_compress(f_ref, s_ref, phi_ref):
    blk = (f_ref[0] + s_ref[0]).reshape(N_CMP, KV_W)
    return _dot(blk.astype(bf16), phi_ref[...]).astype(bf16)


def _cmp_end(shape, axis):
    col = lax.broadcasted_iota(jnp.int32, shape, axis)
    n = ((col & (N_BLK - 1)) << 2) + (col >> 7)
    return n * CMP_STRIDE + (CMP_BLK - 1)


def _pick_blocks(score, n_pick):
    lane = lax.broadcasted_iota(jnp.int32, score.shape, 1).astype(f32)
    bias = jnp.full(score.shape, NEG, f32)
    for _ in range(n_pick):
        m = jnp.max(score, axis=-1, keepdims=True)
        first = jnp.min(jnp.where(score == m, lane, float(N_BLK)), axis=-1, keepdims=True)
        hit = lane == first
        bias = jnp.where(hit, 0.0, bias)
        score = jnp.where(hit, -jnp.inf, score)
    return bias


def _block_expand(first_block, n_keys):
    j = lax.broadcasted_iota(jnp.int32, (N_BLK, n_keys), 0)
    c = lax.broadcasted_iota(jnp.int32, (N_BLK, n_keys), 1)
    return jnp.where(j == first_block + (c >> 6), 1.0, 0.0).astype(bf16)


CQ = Q_PER_KV * TQ
WIN_T = WINDOW // TQ + 1
N_PARTS = 9
TILE_COL = HEAD_DIM + 6
V_ROWS = HEAD_DIM + 16


def _pick_blocks_t(score, forced, n_pick):
    jrow = lax.broadcasted_iota(jnp.int32, score.shape, 0).astype(f32)
    bias = jnp.where(forced, 0.0, NEG)
    score = jnp.where(forced, -jnp.inf, score)
    for _ in range(n_pick):
        m = jnp.max(score, axis=0, keepdims=True)
        first = jnp.min(jnp.where(score == m, jrow, float(N_BLK)), axis=0, keepdims=True)
        hit = jrow == first
        bias = jnp.where(hit, 0.0, bias)
        score = jnp.where(hit, -jnp.inf, score)
    return bias


def _nsa_prompt_kernel(q_ref, srow_ref, slope_ref, bg_ref, ng_ref, fk_ref, sk_ref, fv_ref, sv_ref,
                       phik_ref, phivt_ref, cpos_ref, ksa_ref, vst_ref, kwa_ref, vwt_ref, o_ref,
                       kca_scr, vct_scr, qa_scr, oc_scr, m_scr, acc_scr, negc_scr, negd_scr, negw_scr):
    qi = pl.program_id(1)
    s0 = qi * TQ

    @pl.when(qi == 0)
    def _():
        kc = _finish_compress(fk_ref, sk_ref, phik_ref)
        for g in range(KV_HEADS):
            kca_scr[g] = jnp.concatenate([kc[:, g * HEAD_DIM:(g + 1) * HEAD_DIM], cpos_ref[...]], axis=1)
        blk_v = (fv_ref[0] + sv_ref[0]).reshape(N_CMP, KV_W).astype(bf16)
        vct_scr[...] = _nt(phivt_ref[...], blk_v).astype(bf16)

    t_of = lambda shape: s0 + (lax.broadcasted_iota(jnp.int32, shape, 1) & (TQ - 1))
    q_t = (q_ref[0].astype(f32) * Q_SCALE).T.astype(bf16)

    negc_scr[...] = jnp.where(_cmp_end((N_CMP, CQ), 0) <= t_of((N_CMP, CQ)), 0.0, NEG)
    sees_block = t_of((1, CQ)) >= CMP_BLK - 1
    t_q = t_of((N_BLK, TQ))
    jr = lax.broadcasted_iota(jnp.int32, (N_BLK, TQ), 0)
    jb = t_q >> 6
    ok_b = (jr << 6) <= t_q
    forced = (jr == 0) | (jr == jb) | (jr == jb - 1)
    tile_any = [jnp.full((N_BLK // 8, 1), NEG, f32) for _ in LOOP_GROUPS]
    for g in range(KV_HEADS):
        q_g = jnp.concatenate([q_t[(g * Q_PER_KV + r) * HEAD_DIM:(g * Q_PER_KV + r + 1) * HEAD_DIM]
                               for r in range(Q_PER_KV)], axis=1)
        qa = jnp.concatenate([q_g, srow_ref[g]], axis=0)
        st = _dot(kca_scr[g], qa) + negc_scr[...]
        e = jnp.exp2(st - jnp.max(st, axis=0, keepdims=True))
        pt = e * jnp.where(sees_block, 1.0 / jnp.sum(e, axis=0, keepdims=True), 0.0)
        oc_scr[g] = _dot(vct_scr[g * HEAD_DIM:(g + 1) * HEAD_DIM], pt.astype(bf16))
        ps = None
        for i in range(4):
            for r in range(Q_PER_KV):
                slab = pt[i * N_BLK:(i + 1) * N_BLK, r * TQ:(r + 1) * TQ]
                ps = slab if ps is None else ps + slab
        bias = _pick_blocks_t(jnp.where(ok_b, ps, NEG), forced, N_SEL - 3)
        qa_scr[g] = jnp.concatenate([qa, jnp.concatenate([bias.astype(bf16)] * Q_PER_KV, axis=1)], axis=0)
        any_t = jnp.max(bias.reshape(N_BLK // 8, 8, TQ), axis=1)
        li = [g in grp for grp in LOOP_GROUPS].index(True)
        tile_any[li] = jnp.maximum(tile_any[li], jnp.max(any_t, axis=1, keepdims=True))
    kt_row = lax.broadcasted_iota(jnp.int32, tile_any[0].shape, 0)
    tile_bits = [jnp.sum(jnp.where(ta == 0.0, 1 << kt_row, 0)) for ta in tile_any]

    m_scr[...] = jnp.full(m_scr.shape, NEG, f32)
    acc_scr[...] = jnp.zeros_like(acc_scr)

    def sel_tile(kt, diagonal, groups, n_keys=TK):
        k0 = pl.multiple_of(kt * TK, TK)
        tile_off = (k0 - s0).astype(f32)
        key_r = lax.broadcasted_iota(jnp.int32, (n_keys, N_BLK), 0)
        blk_c = lax.broadcasted_iota(jnp.int32, (n_keys, N_BLK), 1)
        expand = jnp.where(blk_c == kt * (TK // SEL_BLK) + (key_r >> 6), 1.0, 0.0).astype(bf16)
        k_aug = ksa_ref[0, pl.ds(k0, n_keys), :]
        v_t = vst_ref[0, kt][:, 0:n_keys]
        for g in groups:
            lhs = jnp.concatenate([k_aug[:, g * QA_W:(g + 1) * QA_W], expand], axis=1)
            st = _dot(lhs, qa_scr[g])
            if diagonal:
                st = st + negd_scr[qi % (TK // TQ), 0:n_keys]
            c = slope_ref[g] * tile_off
            m_old = m_scr[g]
            m_new = jnp.maximum(m_old, jnp.max(st, axis=0, keepdims=True) + c)
            alpha = jnp.exp2(m_old - m_new)
            pt = jnp.exp2((st - (m_new - c)).astype(bf16))
            acc_scr[g] = alpha * acc_scr[g] + _dot(v_t[g * V_ROWS:(g + 1) * V_ROWS], pt)
            m_scr[g] = m_new

    last = s0 // TK

    @pl.when(qi < TK // TQ)
    def _():
        key_pos = lax.broadcasted_iota(jnp.int32, (TK, CQ), 0)
        negd_scr[qi] = jnp.where(key_pos <= t_of((TK, CQ)), 0.0, NEG)

    for groups, bits in zip(LOOP_GROUPS, tile_bits):
        def sel_body(kt, carry, groups=groups, bits=bits):
            @pl.when(((bits >> kt) & 1) == 1)
            def _():
                sel_tile(kt, False, groups)
            return carry

        lax.fori_loop(0, last, sel_body, 0)
    for sub in range(TK // TQ):
        @pl.when(qi % (TK // TQ) == sub)
        def _(sub=sub):
            sel_tile(last, True, range(KV_HEADS), (sub + 1) * TQ)

    w_tile = jnp.maximum(qi - WINDOW // TQ, 0)
    n_win = WIN_T * TQ

    @pl.when(qi <= WINDOW // TQ)
    def _():
        dist_w = t_of((n_win, CQ)) - (w_tile * TQ + lax.broadcasted_iota(jnp.int32, (n_win, CQ), 0))
        negw_scr[...] = jnp.where((dist_w >= 0) & (dist_w < WINDOW), 0.0, NEG)

    lane_w = lax.broadcasted_iota(jnp.int32, (n_win, QA_W), 1)
    slab_off = (lax.broadcasted_iota(jnp.int32, (n_win, QA_W), 0) // TQ * TQ).astype(f32).astype(bf16)
    in_tile_col = (lane_w >= TILE_COL) & (lane_w < TILE_COL + 3)
    kw_all = kwa_ref[0, pl.ds(pl.multiple_of(w_tile * TQ, TQ), n_win), :]
    vw_t = jnp.concatenate([vwt_ref[0, w_tile + i] for i in range(WIN_T)], axis=1)
    gates = jax.nn.sigmoid(bg_ref[0].T)
    blocks = []
    for g in range(KV_HEADS):
        lhs = jnp.where(in_tile_col, slab_off, kw_all[:, g * QA_W:(g + 1) * QA_W])
        st = _dot(lhs, qa_scr[g, 0:QA_W]) + negw_scr[...]
        pt = jnp.exp2((st - jnp.max(st, axis=0, keepdims=True)).astype(bf16))
        win = _dot(vw_t[g * V_ROWS:(g + 1) * V_ROWS], pt)
        o_w = win[0:HEAD_DIM] * (1.0 / win[HEAD_DIM:HEAD_DIM + 1])
        o_s = acc_scr[g, 0:HEAD_DIM] * (1.0 / acc_scr[g, HEAD_DIM:HEAD_DIM + 1])
        o_c = oc_scr[g]
        heads = []
        for r in range(Q_PER_KV):
            h = g * Q_PER_KV + r
            cols = slice(r * TQ, (r + 1) * TQ)
            heads.append(gates[3 * h:3 * h + 1] * o_c[:, cols] + gates[3 * h + 1:3 * h + 2] * o_s[:, cols]
                         + gates[3 * h + 2:3 * h + 3] * o_w[:, cols])
        for r in range(0, Q_PER_KV, 2):
            blocks.append(jnp.concatenate(heads[r:r + 2], axis=0).T)
    y = jnp.concatenate(blocks, axis=-1)
    o_ref[0] = (y * _silu(ng_ref[0])).astype(bf16)


def _nsa_prompt(z3, z16, fk, sk, fv, sv, phik, phivt):
    b, s, _ = z3.shape
    nq = s // TQ
    parts = _slope_parts().reshape(KV_HEADS, Q_PER_KV, POS_ROWS)
    srow = jnp.asarray(np.repeat(parts.transpose(0, 2, 1), TQ, axis=2), bf16)
    slope = jnp.asarray(np.repeat(np.asarray(SLOPES_LOG2, np.float32).reshape(KV_HEADS, 1, Q_PER_KV), TQ, axis=2))
    ksa = _key_aug(_kv_seg(z3, 2), TK)
    kwa = _key_aug(_kv_seg(z3, 4), TQ)
    vst = _value_tiles(_kv_seg(z3, 3), TK)
    vwt = _value_tiles(_kv_seg(z3, 5), TQ)
    cpos = _cmp_pos_cols()

    once = pl.Buffered(1)
    cmp = lambda: pl.BlockSpec((1, 4, N_CMP // 4, KV_W), lambda i, t: (i, 0, 0, 0))
    full = lambda a: pl.BlockSpec(a.shape, lambda i, t: (0,) * a.ndim)
    return pl.pallas_call(
        _nsa_prompt_kernel,
        out_shape=jax.ShapeDtypeStruct((b, s, BRANCH_W), bf16),
        grid=(b, nq),
        in_specs=[pl.BlockSpec((1, TQ, BRANCH_W), lambda i, t: (i, t, C_Q // BRANCH_W)),
                  full(srow), full(slope),
                  pl.BlockSpec((1, TQ, BG_W), lambda i, t: (i, t, C_BG // BG_W)),
                  pl.BlockSpec((1, TQ, BRANCH_W), lambda i, t: (i, t, C_NSA_G // BRANCH_W)),
                  cmp(), cmp(), cmp(), cmp(), full(phik), full(phivt), full(cpos),
                  pl.BlockSpec((1, s, KV_HEADS * QA_W), lambda i, t: (i, 0, 0), pipeline_mode=once),
                  pl.BlockSpec((1, s // TK, KV_HEADS * V_ROWS, TK), lambda i, t: (i, 0, 0, 0), pipeline_mode=once),
                  pl.BlockSpec((1, s, KV_HEADS * QA_W), lambda i, t: (i, 0, 0), pipeline_mode=once),
                  pl.BlockSpec((1, nq, KV_HEADS * V_ROWS, TQ), lambda i, t: (i, 0, 0, 0), pipeline_mode=once)],
        out_specs=pl.BlockSpec((1, TQ, BRANCH_W), lambda i, t: (i, t, 0)),
        scratch_shapes=[pltpu.VMEM((KV_HEADS, N_CMP, QA_W), bf16), pltpu.VMEM((KV_W, N_CMP), bf16),
                        pltpu.VMEM((KV_HEADS, QA_W + N_BLK, CQ), bf16),
                        pltpu.VMEM((KV_HEADS, HEAD_DIM, CQ), f32),
                        pltpu.VMEM((KV_HEADS, 1, CQ), f32),
                        pltpu.VMEM((KV_HEADS, V_ROWS, CQ), f32),
                        pltpu.VMEM((N_CMP, CQ), f32), pltpu.VMEM((TK // TQ, TK, CQ), f32),
                        pltpu.VMEM((WIN_T * TQ, CQ), f32)],
        compiler_params=_params(("parallel", "arbitrary")),
        name="nsa_prompt",
    )(z16, srow, slope, z3, z16, fk, sk, fv, sv, phik, phivt, cpos, ksa, vst, kwa, vwt)


def _by_group(fn):
    hg = lax.broadcasted_iota(jnp.int32, (N_HEADS, 1), 0) >> 2
    out = fn(0)
    for g in range(1, KV_HEADS):
        out = jnp.where(hg == g, fn(g), out)
    return out


def _nsa_sample_kernel(past_len, layer, tab_ref, q_ref, bg_ref, ng_ref, slope_ref,
                       fk_ref, sk_ref, fv_ref, sv_ref, phik_ref, phiv_ref,
                       ksn_ref, vsn_ref, kwn_ref, vwn_ref, bk_ref, bv_ref, kp_hbm, vp_hbm, o_ref,
                       q_scr, sel_scr, oc_scr, ow_scr, m_scr, l_scr, acc_scr, kbuf, vbuf, ksem, vsem):
    slot = _paged_step(tab_ref, layer, PP_SEL, (kp_hbm, vp_hbm), (kbuf, vbuf), (ksem, vsem))
    p = pl.program_id(1)
    slope = slope_ref[:, 0:1]
    gsl = lambda g: slice(g * HEAD_DIM, (g + 1) * HEAD_DIM)

    def per_head(row_ref):
        row = row_ref[0]
        return _by_group(lambda g: jnp.broadcast_to(row[:, gsl(g)], (N_HEADS, HEAD_DIM)))

    @pl.when(p == 0)
    def _():
        qrow = q_ref[0]
        q16 = jnp.concatenate([qrow[:, h * HEAD_DIM:(h + 1) * HEAD_DIM] for h in range(N_HEADS)], axis=0)
        q_scr[...] = q16
        q16f = q16.astype(f32)
        kc = _finish_compress(fk_ref, sk_ref, phik_ref)
        vc = _finish_compress(fv_ref, sv_ref, phiv_ref)

        dist_c = past_len - _cmp_end((1, N_CMP), 1)
        ok_c = dist_c >= 0
        s = _by_group(lambda g: _nt(q16, kc[:, gsl(g)])) - slope * dist_c.astype(f32)
        s = jnp.where(ok_c, s, NEG)
        e = jnp.exp(s - jnp.max(s, axis=-1, keepdims=True))
        pc = jnp.where(ok_c, e * (1.0 / jnp.sum(e, axis=-1, keepdims=True)), 0.0)
        pcb = pc.astype(bf16)
        oc_scr[...] = _by_group(lambda g: _dot(pcb, vc[:, gsl(g)]))
        ps16 = (pc[:, 0:N_BLK] + pc[:, N_BLK:2 * N_BLK]) + (pc[:, 2 * N_BLK:3 * N_BLK] + pc[:, 3 * N_BLK:])
        ps = jnp.concatenate([jnp.sum(ps16[g * Q_PER_KV:(g + 1) * Q_PER_KV], axis=0, keepdims=True)
                              for g in range(KV_HEADS)] + [jnp.zeros((8 - KV_HEADS, N_BLK), f32)], axis=0)
        jl = lax.broadcasted_iota(jnp.int32, (8, N_BLK), 1)
        jb = past_len // SEL_BLK
        forced = jnp.where((jl == 0) | (jl == jb) | (jl == jb - 1), FORCE, 0.0)
        bias8 = _pick_blocks(ps + forced, N_SEL - 1)
        sel_scr[...] = jnp.concatenate(
            [jnp.broadcast_to(bias8[g:g + 1], (Q_PER_KV, N_BLK)) for g in range(KV_HEADS)], axis=0).astype(bf16)

        wb = bk_ref.shape[-1]
        dist_w = wb - lax.broadcasted_iota(jnp.int32, (1, wb), 1)
        ok_w = (dist_w >= 0) & (dist_w < WINDOW)
        bk = bk_ref[0, 0].astype(bf16)
        bv = bv_ref[0, 0].astype(bf16)
        s_buf = _by_group(lambda g: _dot(q16, bk[gsl(g)])) - slope * dist_w.astype(f32)
        s_buf = jnp.where(ok_w, s_buf, NEG)
        s_new = jnp.sum(q16f * per_head(kwn_ref), axis=-1, keepdims=True)
        m_w = jnp.maximum(jnp.max(s_buf, axis=-1, keepdims=True), s_new)
        e_buf = jnp.exp(s_buf - m_w)
        e_new = jnp.exp(s_new - m_w)
        ebb = e_buf.astype(bf16)
        num = _by_group(lambda g: _nt(ebb, bv[gsl(g)])) + e_new * per_head(vwn_ref)
        ow_scr[...] = num * (1.0 / (jnp.sum(e_buf, axis=-1, keepdims=True) + e_new))

        m_scr[...] = jnp.sum(q16f * per_head(ksn_ref), axis=-1, keepdims=True)
        l_scr[...] = jnp.ones_like(l_scr)
        acc_scr[...] = per_head(vsn_ref)

    q16 = q_scr[...]
    n_keys = PP_SEL * PAGE_SIZE
    kp = _side_by_side(kbuf, slot).astype(bf16)
    vp = _side_by_side(vbuf, slot).astype(bf16)
    dist = past_len - (p * n_keys + lax.broadcasted_iota(jnp.int32, (1, n_keys), 1))
    expand = _block_expand(p * (n_keys // SEL_BLK), n_keys)
    s = _by_group(lambda g: _dot(q16, kp[gsl(g)])) - slope * dist.astype(f32) + _dot(sel_scr[...], expand)
    m_old = m_scr[...]
    m_new = jnp.maximum(m_old, jnp.max(s, axis=-1, keepdims=True))
    alpha = jnp.exp(m_old - m_new)
    pr = jnp.exp(s - m_new)
    prb = pr.astype(bf16)
    l_scr[...] = alpha * l_scr[...] + jnp.sum(pr, axis=-1, keepdims=True)
    acc_scr[...] = alpha * acc_scr[...] + _by_group(lambda g: _nt(prb, vp[gsl(g)]))
    m_scr[...] = m_new

    @pl.when(p == pl.num_programs(1) - 1)
    def _():
        o_s = acc_scr[...] * (1.0 / l_scr[...])
        gates = jax.nn.sigmoid(bg_ref[0])
        lane = lax.broadcasted_iota(jnp.int32, (N_HEADS, BG_W), 1)
        h3 = 3 * lax.broadcasted_iota(jnp.int32, (N_HEADS, BG_W), 0)
        gate = lambda n: jnp.sum(jnp.where(lane == h3 + n, gates, 0.0), axis=-1, keepdims=True)
        y16 = gate(0) * oc_scr[...] + gate(1) * o_s + gate(2) * ow_scr[...]
        y = jnp.concatenate([y16[h:h + 1, :] for h in range(N_HEADS)], axis=-1)
        o_ref[0] = (y * _silu(ng_ref[0])).astype(bf16)


def _nsa_sample(table, layer, qb, z2, z2h, slopes, fk, sk, fv, sv, phik, phiv, pool_k, pool_v, buf_k, buf_v,
                past_len):
    nb = qb.shape[0]
    npg = table.shape[0] // nb
    z3 = z2.reshape(nb, 1, Z32_W)
    z3h = z2h.reshape(nb, 1, Z16_W)
    tok = lambda w, c: pl.BlockSpec((1, 1, w), lambda b, p, tab: (b, 0, c))
    cmp = lambda: pl.BlockSpec((1, 4, N_CMP // 4, KV_W), lambda b, p, tab: (b, 0, 0, 0))
    phi = lambda: pl.BlockSpec((KV_W, KV_W), lambda b, p, tab: (0, 0))
    wb = buf_k.shape[-1]
    win = lambda: pl.BlockSpec((1, 1, KV_W, wb), lambda b, p, tab: (layer, b, 0, 0))
    hbm = lambda: pl.BlockSpec(memory_space=pl.ANY)
    page_buf = lambda: pltpu.VMEM((PAGE_SLOTS, PP_SEL, KV_W, PAGE_SIZE), f32)
    kvc = C_KV // KV_W
    return pl.pallas_call(
        functools.partial(_nsa_sample_kernel, past_len, layer),
        out_shape=jax.ShapeDtypeStruct((nb, 1, BRANCH_W), bf16),
        grid_spec=pltpu.PrefetchScalarGridSpec(
            num_scalar_prefetch=1,
            grid=(nb, npg // PP_SEL),
            in_specs=[tok(BRANCH_W, 0), tok(BG_W, C_BG // BG_W), tok(BRANCH_W, C_NSA_G // BRANCH_W),
                      pl.BlockSpec((N_HEADS, 128), lambda b, p, tab: (0, 0)),
                      cmp(), cmp(), cmp(), cmp(), phi(), phi(),
                      tok(KV_W, kvc + 2), tok(KV_W, kvc + 3), tok(KV_W, kvc + 4), tok(KV_W, kvc + 5),
                      win(), win(), hbm(), hbm()],
            out_specs=pl.BlockSpec((1, 1, BRANCH_W), lambda b, p, tab: (b, 0, 0)),
            scratch_shapes=[pltpu.VMEM((N_HEADS, HEAD_DIM), bf16), pltpu.VMEM((N_HEADS, N_BLK), bf16),
                            pltpu.VMEM((N_HEADS, HEAD_DIM), f32), pltpu.VMEM((N_HEADS, HEAD_DIM), f32),
                            pltpu.VMEM((N_HEADS, 1), f32), pltpu.VMEM((N_HEADS, 1), f32),
                            pltpu.VMEM((N_HEADS, HEAD_DIM), f32), page_buf(), page_buf(),
                            pltpu.SemaphoreType.DMA((PAGE_SLOTS,)), pltpu.SemaphoreType.DMA((PAGE_SLOTS,))]),
        compiler_params=_params(("arbitrary", "arbitrary")),
        name="nsa_sample",
    )(table, qb.reshape(nb, 1, BRANCH_W), z3, z3h, slopes, fk, sk, fv, sv, phik, phiv,
      z3, z3, z3, z3, buf_k, buf_v, pool_k, pool_v)


def _merge_kernel(zl_ref, zp_ref, zn_ref, m0_ref, m1_ref, m2_ref, wb_ref, wo_ref, g_ref, x_ref, y_ref):
    acc = None
    for n, (zz, mg) in enumerate(((zl_ref, m0_ref), (zp_ref, m1_ref), (zn_ref, m2_ref))):
        term = jax.nn.sigmoid(mg[...].astype(f32)) * _dot(zz[...], wb_ref[n])
        acc = term if acc is None else acc + term
    out = _dot(acc.astype(bf16), wo_ref[...])
    ms = jnp.mean(out * out, axis=-1, keepdims=True)
    y_ref[...] = x_ref[...] + out * lax.rsqrt(ms + EPS) * g_ref[...]


def _merge(zl, zp, zn, z16, wb, wo, g_row, x2d):
    n = x2d.shape[0]
    tm = min(n, 256)
    rowblk = lambda c: pl.BlockSpec((tm, D_MODEL), lambda i: (i, c))
    return pl.pallas_call(
        _merge_kernel,
        out_shape=jax.ShapeDtypeStruct((n, D_MODEL), f32),
        grid=(n // tm,),
        in_specs=[rowblk(0), rowblk(0), rowblk(0),
                  rowblk(C_MG // D_MODEL), rowblk(C_MG // D_MODEL + 1), rowblk(C_MG // D_MODEL + 2),
                  pl.BlockSpec((N_BRANCH, BRANCH_W, D_MODEL), lambda i: (0, 0, 0)),
                  pl.BlockSpec((D_MODEL, D_MODEL), lambda i: (0, 0)),
                  pl.BlockSpec((1, D_MODEL), lambda i: (0, 0)),
                  rowblk(0)],
        out_specs=rowblk(0),
        compiler_params=_params(("parallel",)),
        name="merge",
    )(zl, zp, zn, z16, z16, z16, wb, wo, g_row, x2d)


def _block_diag(w, per):
    n, d, _ = w.shape
    eye = jnp.eye(per, dtype=w.dtype)
    t = jnp.einsum('cpde,pq->cpdqe', w.reshape(n // per, per, d, d), eye)
    return t.reshape(n // per, per * d, per * d)


def _pack_w_in(w):
    seg = lambda i: w[:, i * BRANCH_W:(i + 1) * BRANCH_W]
    old_kv = 6 * BRANCH_W
    old_bg = old_kv + 6 * KV_W
    old_mg = old_bg + N_BRANCH * N_HEADS
    f32_part = jnp.concatenate([seg(0), seg(2), w[:, old_kv:old_bg], w[:, old_bg:old_mg]], axis=1)
    f32_part = jnp.pad(f32_part, ((0, 0), (0, Z32_W - f32_part.shape[1])))
    bf16_part = jnp.concatenate([seg(1), seg(3), seg(4), seg(5), w[:, old_mg:]], axis=1)
    return jnp.concatenate([f32_part, bf16_part], axis=1).astype(bf16)


def _tile_wpos(w_pos):
    halves = w_pos.reshape(2, CMP_STRIDE, HEAD_DIM)
    return jnp.tile(halves, (1, PAGE_SIZE // CMP_STRIDE, KV_HEADS))


def _tile_wpos_t(w_pos):
    halves = w_pos.reshape(2, CMP_STRIDE, HEAD_DIM).swapaxes(1, 2)
    return jnp.tile(halves, (1, KV_HEADS, PAGE_SIZE // CMP_STRIDE))


def _lanes_last(cache):
    d, n, rows = cache.shape[:3]
    return jnp.transpose(cache, (0, 1, 3, 4, 2)).reshape(d, n, KV_W, rows)


def _slope_parts():
    cols = np.zeros((N_HEADS, POS_ROWS), np.float32)
    rnd = lambda v: np.float32(np.float32(v).astype(bf16))
    for h, s in enumerate(SLOPES_LOG2):
        s1 = rnd(s)
        s2 = rnd(np.float32(s) - s1)
        s3 = rnd(np.float32(s) - s1 - s2)
        cols[h, 0:N_PARTS] = [s1, s2, s3] * 3
    return cols


def _split_pos(pos, shift):
    cols = np.zeros((pos.shape[0], POS_ROWS), np.float32)
    cols[:, 0:3] = ((pos >> shift) << shift)[:, None]
    cols[:, 3:6] = (pos & ((1 << shift) - 1))[:, None]
    return cols


def _key_aug(k_rows, tile):
    b, s, _ = k_rows.shape
    pos = jnp.asarray(_split_pos(np.arange(s) % tile, 4), bf16)
    kg = k_rows.astype(bf16).reshape(b, s, KV_HEADS, HEAD_DIM)
    posb = jnp.broadcast_to(pos[None, :, None, :], (b, s, KV_HEADS, POS_ROWS))
    return jnp.concatenate([kg, posb], axis=-1).reshape(b, s, KV_HEADS * QA_W)


def _value_tiles(v_rows, tile):
    b, s, _ = v_rows.shape
    vt = v_rows.astype(bf16).reshape(b, s // tile, tile, KV_HEADS, HEAD_DIM).transpose(0, 1, 3, 4, 2)
    ones = jnp.ones((b, s // tile, KV_HEADS, V_ROWS - HEAD_DIM, tile), bf16)
    return jnp.concatenate([vt, ones], axis=3).reshape(b, s // tile, KV_HEADS * V_ROWS, tile)


def _cmp_pos_cols():
    slot = np.arange(N_CMP)
    n = ((slot & (N_BLK - 1)) << 2) + (slot >> 7)
    return jnp.asarray(_split_pos(n * CMP_STRIDE + CMP_BLK - 1, 8), bf16)


def _kv_seg(z, i):
    return z[..., C_KV + i * KV_W:C_KV + (i + 1) * KV_W]


def kernel(x_prompt, x_sample, cache_cmp_k, cache_cmp_v, cache_sel_k, cache_sel_v, cache_win_k, cache_win_v, state_conv, state_lru, state_pool, page_table, g_pre, g_post, w_in, conv_w, conv_b, w_rg_a, b_rg_a, w_rg_x, b_rg_x, lru_lambda, w_pool, pool_scale, cmp_pos_k, cmp_phi_k, cmp_pos_v, cmp_phi_v, w_branch, w_out):
    depth = w_in.shape[0]
    bp, seq, _ = x_prompt.shape
    bs = x_sample.shape[0]
    n_pages = page_table.shape[1]
    past_len = n_pages * PAGE_SIZE
    n_phys = cache_cmp_k.shape[1]
    assert seq == N_BLK * SEL_BLK and past_len == N_BLK * SEL_BLK and x_sample.shape[1] == 1
    wb = cache_win_k.shape[2]

    table_s = page_table.reshape(-1).astype(jnp.int32)
    table_p = jnp.arange(bp * (seq // PAGE_SIZE), dtype=jnp.int32)
    slopes = jnp.broadcast_to(jnp.asarray(SLOPES, f32)[:, None], (N_HEADS, 128))
    row = lambda v: v.reshape(1, -1)
    cmp_kt, cmp_vt = _lanes_last(cache_cmp_k), _lanes_last(cache_cmp_v)
    sel_kt, sel_vt = _lanes_last(cache_sel_k), _lanes_last(cache_sel_v)
    win_kt, win_vt = _lanes_last(cache_win_k), _lanes_last(cache_win_v)

    xp = x_prompt.reshape(bp * seq, D_MODEL)
    xs = x_sample.reshape(bs, D_MODEL)
    pr = [[] for _ in range(9)]
    sm = [[] for _ in range(9)]
    for l in range(depth):
        w_packed = _pack_w_in(w_in[l])
        wa = _block_diag(w_rg_a[l], MXU_W // LRU_BD).astype(bf16)
        wx = _block_diag(w_rg_x[l], MXU_W // LRU_BD).astype(bf16)
        wp = w_pool[l].astype(bf16)
        phik = _block_diag(jnp.broadcast_to(cmp_phi_k[l], (KV_HEADS, HEAD_DIM, HEAD_DIM)), KV_HEADS)[0].astype(bf16)
        phiv = _block_diag(jnp.broadcast_to(cmp_phi_v[l], (KV_HEADS, HEAD_DIM, HEAD_DIM)), KV_HEADS)[0].astype(bf16)
        wpos_k = _tile_wpos(cmp_pos_k[l])
        wpos_v = _tile_wpos(cmp_pos_v[l])
        wbr = w_branch[l].astype(bf16)
        wo = w_out[l].astype(bf16)
        lru_w = (conv_w[l], row(conv_b[l]), wa, row(b_rg_a[l]), wx, row(b_rg_x[l]), row(lru_lambda[l]))

        z, z16 = _inproj(xp, row(g_pre[l]), w_packed)
        z3 = z.reshape(bp, seq, Z32_W)
        z16_3 = z16.reshape(bp, seq, Z16_W)
        zl, conv_tail, h_p = _lru_prompt(z3, z16_3, *lru_w)
        zpool, pool_tail = _pool_prompt(z3, z16_3, wp, row(pool_scale[l]))
        zr = z.reshape(bp * seq // PAGE_SIZE, PAGE_SIZE, Z32_W)
        fk, sk, fv, sv = _compress(table_p, zr, zr, C_KV // KV_W, C_KV // KV_W + 1, wpos_k, wpos_v, bp)
        zn = _nsa_prompt(z3, z16_3, fk, sk, fv, sv, phik, phiv.T)
        xp = _merge(zl.reshape(bp * seq, BRANCH_W), zpool.reshape(bp * seq, BRANCH_W),
                    zn.reshape(bp * seq, BRANCH_W), z16, wbr, wo, row(g_post[l]), xp)
        kv_rows = [_kv_seg(z3, i).reshape(bp, seq, KV_HEADS, HEAD_DIM) for i in range(6)]
        wlen = min(WINDOW, seq)
        st_p = kv_rows[:4] + [kv_rows[4][:, -wlen:], kv_rows[5][:, -wlen:],
                              conv_tail[:, -(CONV_W - 1):], h_p[:, 0], pool_tail[:, -POOL_BUF:]]

        zs, zs16 = _inproj(xs, row(g_pre[l]), w_packed)
        zls, zps, h_s = _mix_sample(zs, zs16, state_conv[l].swapaxes(0, 1), state_lru[l],
                                    state_pool[l].swapaxes(0, 1), *lru_w, wp, row(pool_scale[l]), past_len)
        fk, sk, fv, sv = _compress_t(table_s, l, cmp_kt, cmp_vt, _tile_wpos_t(cmp_pos_k[l]),
                                     _tile_wpos_t(cmp_pos_v[l]), bs)
        qs = (zs16[:, C_Q:C_Q + BRANCH_W].astype(f32) * (HEAD_DIM ** -0.5)).astype(bf16)
        zns = _nsa_sample(table_s, l, qs, zs, zs16, slopes, fk, sk, fv, sv, phik, phiv,
                          sel_kt, sel_vt, win_kt, win_vt, past_len)
        xs = _merge(zls, zps, zns.reshape(bs, BRANCH_W), zs16, wbr, wo, row(g_post[l]), xs)
        kv_new = [_kv_seg(zs, i).reshape(bs, 1, KV_HEADS, HEAD_DIM) for i in range(6)]
        st_s = kv_new[:4] + [jnp.concatenate([cache_win_k[l], kv_new[4]], axis=1)[:, -wb:],
                             jnp.concatenate([cache_win_v[l], kv_new[5]], axis=1)[:, -wb:],
                             jnp.concatenate([state_conv[l], zs[:, None, C_LRU_X:C_LRU_X + BRANCH_W]], axis=1)[:, -(CONV_W - 1):],
                             h_s,
                             jnp.concatenate([state_pool[l], zs[:, None, C_POOL_X:C_POOL_X + BRANCH_W]], axis=1)[:, -POOL_BUF:]]
        for i in range(9):
            pr[i].append(st_p[i])
            sm[i].append(st_s[i])

    out = [xp.reshape(bp, seq, D_MODEL), xs.reshape(bs, 1, D_MODEL)]
    for i in range(9):
        out += [jnp.stack(pr[i]), jnp.stack(sm[i])]
    return tuple(out)
```

```python
import functools

import numpy as np
import jax
import jax.numpy as jnp
from jax import lax
from jax.experimental import pallas as pl
from jax.experimental.pallas import tpu as pltpu

f32 = jnp.float32
bf16 = jnp.bfloat16

D_MODEL = 1024
BRANCH_W = 1024
N_BRANCH = 3
LRU_BLOCKS = 16
LRU_BD = BRANCH_W // LRU_BLOCKS
CONV_W = 4
LRU_C = 8.0
POOL_WINDOWS = (2, 4, 8, 16)
POOL_GD = BRANCH_W // len(POOL_WINDOWS)
POOL_BUF = max(POOL_WINDOWS) - 1
N_HEADS = 16
HEAD_DIM = 64
KV_HEADS = 4
Q_PER_KV = N_HEADS // KV_HEADS
KV_W = KV_HEADS * HEAD_DIM
CMP_STRIDE = 16
CMP_BLK = 2 * CMP_STRIDE
SEL_BLK = 64
N_SEL = 16
WINDOW = 512
PAGE_SIZE = 128
FORCE = 1e4
NEG = -1e30
EPS = 1e-6

C_LRU_X, C_POOL_X, C_KV = 0, 1024, 2048
C_BG = C_KV + 6 * KV_W
BG_W = 128
Z32_W = 4096
C_LRU_G, C_POOL_G, C_Q, C_NSA_G, C_MG = 0, 1024, 2048, 3072, 4096
Z16_W = C_MG + N_BRANCH * D_MODEL
IN_TN = 1024
N32_TILES = Z32_W // IN_TN

MXU_W = 256
VMEM_LIMIT = 56 * 1024 * 1024

N_CMP = 512
N_BLK = 128
TQ = 256
TK = 512
PP = 8
PP_SEL = 16
PAGE_SLOTS = 4
POS_ROWS = 64
QA_W = HEAD_DIM + POS_ROWS

SLOPES = [float(np.float32(2.0 ** (-8.0 * (h + 1) / N_HEADS))) for h in range(N_HEADS)]
LOG2E = float(np.log2(np.e))
SLOPES_LOG2 = [float(np.float32(s * LOG2E)) for s in SLOPES]
Q_SCALE = float(np.float32(HEAD_DIM ** -0.5 * LOG2E))
LOOP_GROUPS = ((0,), (1,), (2,), (3,))


def _nt(a, b):
    return lax.dot_general(a, b, (((1,), (1,)), ((), ())), preferred_element_type=f32)


def _dot(a, b):
    return jnp.dot(a, b, preferred_element_type=f32)


def _silu(x):
    x = x.astype(f32)
    return x * jax.nn.sigmoid(x)


def _params(sem):
    return pltpu.CompilerParams(dimension_semantics=sem, vmem_limit_bytes=VMEM_LIMIT)


def _inproj_kernel(x_ref, g_ref, w_ref, o32_ref, o16_ref, u_ref):
    j = pl.program_id(1)

    @pl.when(j == 0)
    def _():
        x = x_ref[...]
        ms = jnp.mean(x * x, axis=-1, keepdims=True)
        u_ref[...] = (x * lax.rsqrt(ms + EPS) * g_ref[...]).astype(bf16)

    @pl.when(j < N32_TILES)
    def _():
        o32_ref[...] = _dot(u_ref[...], w_ref[...])

    @pl.when(j >= N32_TILES)
    def _():
        o16_ref[...] = _dot(u_ref[...], w_ref[...]).astype(bf16)


def _inproj(x2d, g_row, w_packed):
    n = x2d.shape[0]
    tm = min(n, 1024)
    return pl.pallas_call(
        _inproj_kernel,
        out_shape=(jax.ShapeDtypeStruct((n, Z32_W), f32), jax.ShapeDtypeStruct((n, Z16_W), bf16)),
        grid=(n // tm, (Z32_W + Z16_W) // IN_TN),
        in_specs=[pl.BlockSpec((tm, D_MODEL), lambda i, j: (i, 0)),
                  pl.BlockSpec((1, D_MODEL), lambda i, j: (0, 0)),
                  pl.BlockSpec((D_MODEL, IN_TN), lambda i, j: (0, j))],
        out_specs=(pl.BlockSpec((tm, IN_TN), lambda i, j: (i, jnp.minimum(j, N32_TILES - 1))),
                   pl.BlockSpec((tm, IN_TN), lambda i, j: (i, jnp.maximum(j - N32_TILES, 0)))),
        scratch_shapes=[pltpu.VMEM((tm, D_MODEL), bf16)],
        compiler_params=_params(("parallel", "arbitrary")),
        name="inproj",
    )(x2d, g_row, w_packed)


def _lru_gates(xc, wa_ref, ba_ref, wx_ref, bx_ref, lam_ref):
    xb = xc.astype(bf16)
    ra, ri = [], []
    for c in range(BRANCH_W // MXU_W):
        sl = slice(c * MXU_W, (c + 1) * MXU_W)
        ra.append(_dot(xb[:, sl], wa_ref[c]))
        ri.append(_dot(xb[:, sl], wx_ref[c]))
    r = jax.nn.sigmoid(jnp.concatenate(ra, axis=-1) + ba_ref[...])
    i = jax.nn.sigmoid(jnp.concatenate(ri, axis=-1) + bx_ref[...])
    nl = -lam_ref[...]
    softplus = jnp.maximum(nl, 0.0) + jnp.log1p(jnp.exp(-jnp.abs(nl)))
    log_a = -LRU_C * r * softplus
    a = jnp.exp(log_a)
    b = jnp.sqrt(1.0 - a * a) * (i * xc)
    return a, b


def _lru_kernel(x_ref, g_ref, cw_ref, cb_ref, wa_ref, ba_ref, wx_ref, bx_ref, lam_ref,
                zb_ref, tail_ref, h_ref, xs_ref, a_ref, b_ref, hc_ref):
    nb, tt = x_ref.shape[0], x_ref.shape[1]

    @pl.when(pl.program_id(0) == 0)
    def _():
        xs_ref[:, 0:8, :] = jnp.zeros((nb, 8, BRANCH_W), f32)
        hc_ref[...] = jnp.zeros_like(hc_ref)

    for s in range(nb):
        x = x_ref[s]
        xs_ref[s, 8:, :] = x
        xc = cb_ref[...] + x * cw_ref[CONV_W - 1:CONV_W, :]
        for k in range(CONV_W - 1):
            xc = xc + xs_ref[s, pl.ds(8 - (CONV_W - 1 - k), tt), :] * cw_ref[k:k + 1, :]
        xs_ref[s, 0:8, :] = x[tt - 8:, :]
        tail_ref[s] = x[tt - 8:, :]
        a, b = _lru_gates(xc, wa_ref, ba_ref, wx_ref, bx_ref, lam_ref)
        a_ref[s] = a
        b_ref[s] = b
    row = lax.broadcasted_iota(jnp.int32, (8, BRANCH_W), 0)

    def body(i, hs_in):
        r0 = pl.multiple_of(i * 8, 8)
        out = []
        for s in range(nb):
            av = a_ref[s, pl.ds(r0, 8), :]
            bv = b_ref[s, pl.ds(r0, 8), :]
            for d in (1, 2, 4):
                a_s = jnp.where(row >= d, pltpu.roll(av, d, 0), 1.0)
                b_s = jnp.where(row >= d, pltpu.roll(bv, d, 0), 0.0)
                bv = av * b_s + bv
                av = av * a_s
            hs = bv + av * hs_in[s]
            b_ref[s, pl.ds(r0, 8), :] = hs
            out.append(hs[7:8, :])
        return tuple(out)

    h = lax.fori_loop(0, tt // 8, body, tuple(hc_ref[s] for s in range(nb)))
    for s in range(nb):
        hc_ref[s] = h[s]
        h_ref[s] = h[s]
        zb_ref[s] = (b_ref[s] * _silu(g_ref[s])).astype(bf16)


def _lru_prompt(z32, z16, cw, cb, wa, ba, wx, bx, lam):
    b, s, _ = z32.shape
    tt = min(s, 512)
    row = lambda: pl.BlockSpec((1, BRANCH_W), lambda t: (0, 0))
    bd = lambda: pl.BlockSpec((BRANCH_W // MXU_W, MXU_W, MXU_W), lambda t: (0, 0, 0))
    return pl.pallas_call(
        _lru_kernel,
        out_shape=(jax.ShapeDtypeStruct((b, s, BRANCH_W), bf16),
                   jax.ShapeDtypeStruct((b, 8, BRANCH_W), f32),
                   jax.ShapeDtypeStruct((b, 1, BRANCH_W), f32)),
        grid=(s // tt,),
        in_specs=[pl.BlockSpec((b, tt, BRANCH_W), lambda t: (0, t, C_LRU_X // BRANCH_W)),
                  pl.BlockSpec((b, tt, BRANCH_W), lambda t: (0, t, C_LRU_G // BRANCH_W)),
                  pl.BlockSpec((CONV_W, BRANCH_W), lambda t: (0, 0)),
                  row(), bd(), row(), bd(), row(), row()],
        out_specs=(pl.BlockSpec((b, tt, BRANCH_W), lambda t: (0, t, 0)),
                   pl.BlockSpec((b, 8, BRANCH_W), lambda t: (0, 0, 0)),
                   pl.BlockSpec((b, 1, BRANCH_W), lambda t: (0, 0, 0))),
        scratch_shapes=[pltpu.VMEM((b, tt + 8, BRANCH_W), f32), pltpu.VMEM((b, tt, BRANCH_W), f32),
                        pltpu.VMEM((b, tt, BRANCH_W), f32), pltpu.VMEM((b, 1, BRANCH_W), f32)],
        compiler_params=_params(("arbitrary",)),
        name="lru_prompt",
    )(z32, z16, cw, cb, wa, ba, wx, bx, lam)


def _pool_kernel(x_ref, g_ref, wp_ref, sc_ref, zb_ref, tail_ref, xs_ref):
    tt = x_ref.shape[1]
    t = pl.program_id(1)

    @pl.when(t == 0)
    def _():
        xs_ref[0:16, :] = jnp.zeros((16, BRANCH_W), f32)

    x = x_ref[0]
    xs_ref[16:, :] = x
    pos1 = t * tt + 1 + lax.broadcasted_iota(jnp.int32, (tt, POOL_GD), 0)
    outs = []
    for gi, w in enumerate(POOL_WINDOWS):
        sl = slice(gi * POOL_GD, (gi + 1) * POOL_GD)
        s = xs_ref[:, sl]
        sh = 1
        while sh < w:
            s = s + pltpu.roll(s, sh, 0)
            sh *= 2
        cnt = jnp.minimum(w, pos1).astype(f32)
        pooled = s[16:, :] / cnt - x[:, sl]
        outs.append(_dot(pooled.astype(bf16), wp_ref[gi]))
    y = jnp.concatenate(outs, axis=-1) * sc_ref[...]
    zb_ref[0] = (y * _silu(g_ref[0])).astype(bf16)
    xs_ref[0:16, :] = x[tt - 16:, :]
    tail_ref[0] = x[tt - 16:, :]


def _pool_prompt(z32, z16, wp, sc):
    b, s, _ = z32.shape
    tt = min(s, 512)
    return pl.pallas_call(
        _pool_kernel,
        out_shape=(jax.ShapeDtypeStruct((b, s, BRANCH_W), bf16),
                   jax.ShapeDtypeStruct((b, 16, BRANCH_W), f32)),
        grid=(b, s // tt),
        in_specs=[pl.BlockSpec((1, tt, BRANCH_W), lambda i, t: (i, t, C_POOL_X // BRANCH_W)),
                  pl.BlockSpec((1, tt, BRANCH_W), lambda i, t: (i, t, C_POOL_G // BRANCH_W)),
                  pl.BlockSpec((len(POOL_WINDOWS), POOL_GD, POOL_GD), lambda i, t: (0, 0, 0)),
                  pl.BlockSpec((1, BRANCH_W), lambda i, t: (0, 0))],
        out_specs=(pl.BlockSpec((1, tt, BRANCH_W), lambda i, t: (i, t, 0)),
                   pl.BlockSpec((1, 16, BRANCH_W), lambda i, t: (i, 0, 0))),
        scratch_shapes=[pltpu.VMEM((tt + 16, BRANCH_W), f32)],
        compiler_params=_params(("parallel", "arbitrary")),
        name="pool_prompt",
    )(z32, z16, wp, sc)


def _mix_sample_kernel(past_len, lx_ref, lg_ref, px_ref, pg_ref, conv_ref, h0_ref, pbuf_ref,
                       cw_ref, cb_ref, wa_ref, ba_ref, wx_ref, bx_ref, lam_ref, wp_ref, sc_ref,
                       zl_ref, zp_ref, h_ref):
    x = lx_ref[...]
    xc = cb_ref[...] + x * cw_ref[CONV_W - 1:CONV_W, :]
    for k in range(CONV_W - 1):
        xc = xc + conv_ref[k] * cw_ref[k:k + 1, :]
    a, b = _lru_gates(xc, wa_ref, ba_ref, wx_ref, bx_ref, lam_ref)
    h = a * h0_ref[...] + b
    h_ref[...] = h
    zl_ref[...] = (h * _silu(lg_ref[...])).astype(bf16)

    px = px_ref[...]
    outs = []
    for gi, w in enumerate(POOL_WINDOWS):
        sl = slice(gi * POOL_GD, (gi + 1) * POOL_GD)
        s = px[:, sl]
        for k in range(1, w):
            s = s + pbuf_ref[POOL_BUF - k][:, sl]
        cnt = float(min(w, past_len + 1))
        pooled = s / cnt - px[:, sl]
        outs.append(_dot(pooled.astype(bf16), wp_ref[gi]))
    y = jnp.concatenate(outs, axis=-1) * sc_ref[...]
    zp_ref[...] = (y * _silu(pg_ref[...])).astype(bf16)


def _mix_sample(z32, z16, conv_t, h0, pbuf_t, cw, cb, wa, ba, wx, bx, lam, wp, sc, past_len):
    n = z32.shape[0]
    col = lambda c: pl.BlockSpec((n, BRANCH_W), lambda i: (0, c // BRANCH_W))
    full = lambda a: pl.BlockSpec(a.shape, lambda i: (0,) * a.ndim)
    args = (conv_t, h0, pbuf_t, cw, cb, wa, ba, wx, bx, lam, wp, sc)
    return pl.pallas_call(
        functools.partial(_mix_sample_kernel, past_len),
        out_shape=(jax.ShapeDtypeStruct((n, BRANCH_W), bf16), jax.ShapeDtypeStruct((n, BRANCH_W), bf16),
                   jax.ShapeDtypeStruct((n, BRANCH_W), f32)),
        grid=(1,),
        in_specs=[col(C_LRU_X), col(C_LRU_G), col(C_POOL_X), col(C_POOL_G)] + [full(a) for a in args],
        out_specs=(pl.BlockSpec((n, BRANCH_W), lambda i: (0, 0)),) * 3,
        compiler_params=_params(("arbitrary",)),
        name="mix_sample",
    )(z32, z16, z32, z16, *args)


def _compress_kernel(tab_ref, k_ref, v_ref, wk_ref, wv_ref, fk_ref, sk_ref, fv_ref, sv_ref):
    del tab_ref
    p = pl.program_id(1)
    chunks = PAGE_SIZE // CMP_STRIDE

    @pl.when(p == 0)
    def _():
        sk_ref[...] = jnp.zeros_like(sk_ref)
        sv_ref[...] = jnp.zeros_like(sv_ref)

    for src, w_ref, f_ref, s_ref in ((k_ref, wk_ref, fk_ref, sk_ref), (v_ref, wv_ref, fv_ref, sv_ref)):
        tile = src[0]
        first = jnp.sum((tile * w_ref[0]).reshape(chunks, CMP_STRIDE, KV_W), axis=1)
        second = jnp.sum((tile * w_ref[1]).reshape(chunks, CMP_STRIDE, KV_W), axis=1)
        for m in range(chunks):
            f_ref[0, m % 4, pl.ds(2 * p + m // 4, 1), :] = first[m:m + 1, :]
            if m >= 1:
                s_ref[0, (m - 1) % 4, pl.ds(2 * p + (m - 1) // 4, 1), :] = second[m:m + 1, :]
            else:
                @pl.when(p > 0)
                def _():
                    s_ref[0, 3, pl.ds(2 * p - 1, 1), :] = second[0:1, :]


def _compress(table, src_k, src_v, col_k, col_v, wk, wv, nb):
    npg = table.shape[0] // nb
    out = jax.ShapeDtypeStruct((nb, 4, N_CMP // 4, KV_W), f32)
    ospec = lambda: pl.BlockSpec((1, 4, N_CMP // 4, KV_W), lambda b, p, tab: (b, 0, 0, 0))
    return pl.pallas_call(
        _compress_kernel,
        out_shape=(out,) * 4,
        grid_spec=pltpu.PrefetchScalarGridSpec(
            num_scalar_prefetch=1,
            grid=(nb, npg),
            in_specs=[pl.BlockSpec((1, PAGE_SIZE, KV_W), lambda b, p, tab: (tab[b * npg + p], 0, col_k)),
                      pl.BlockSpec((1, PAGE_SIZE, KV_W), lambda b, p, tab: (tab[b * npg + p], 0, col_v)),
                      pl.BlockSpec((2, PAGE_SIZE, KV_W), lambda b, p, tab: (0, 0, 0)),
                      pl.BlockSpec((2, PAGE_SIZE, KV_W), lambda b, p, tab: (0, 0, 0))],
            out_specs=(ospec(), ospec(), ospec(), ospec())),
        compiler_params=_params(("parallel", "arbitrary")),
        name="compress",
    )(table, src_k, src_v, wk, wv)


def _chunk_maps():
    n_chunk = PP * PAGE_SIZE // CMP_STRIDE
    rows = n_chunk // 4
    chunk_of = np.arange(PP * PAGE_SIZE) // CMP_STRIDE
    first = np.zeros((n_chunk, PP * PAGE_SIZE), np.float32)
    second = np.zeros((n_chunk + 8, PP * PAGE_SIZE), np.float32)
    for i in range(4):
        for jj in range(rows):
            first[i * rows + jj] = chunk_of == 4 * jj + i
            second[i * rows + jj] = chunk_of == 4 * jj + i + 1
    second[n_chunk] = chunk_of == 0
    return jnp.asarray(first, bf16), jnp.asarray(second, bf16)


def _page_copies(tab_ref, layer, step, n_pages, pools, bufs, sems, wait):
    slot = step % PAGE_SLOTS
    for i in range(n_pages):
        page = 0 if wait else tab_ref[step * n_pages + i]
        for pool, buf, sem in zip(pools, bufs, sems):
            copy = pltpu.make_async_copy(pool.at[layer, page],
                                         buf.at[slot, i], sem.at[slot])
            if wait:
                copy.wait()
            else:
                copy.start()


def _side_by_side(buf, slot):
    return jnp.concatenate([buf[slot, i] for i in range(buf.shape[1])], axis=1)


def _paged_step(tab_ref, layer, n_pages, pools, bufs, sems):
    ahead = PAGE_SLOTS - 1
    step = pl.program_id(0) * pl.num_programs(1) + pl.program_id(1)
    n_steps = pl.num_programs(0) * pl.num_programs(1)

    for first in range(ahead):
        @pl.when((step == 0) & (first < n_steps))
        def _(first=first):
            _page_copies(tab_ref, layer, first, n_pages, pools, bufs, sems, wait=False)

    @pl.when(step + ahead < n_steps)
    def _():
        _page_copies(tab_ref, layer, step + ahead, n_pages, pools, bufs, sems, wait=False)

    _page_copies(tab_ref, layer, step, n_pages, pools, bufs, sems, wait=True)
    return step % PAGE_SLOTS


def _compress_t_kernel(layer, tab_ref, k_hbm, v_hbm, wk_ref, wv_ref, ea_ref, eb_ref,
                       fk_ref, sk_ref, fv_ref, sv_ref, kbuf, vbuf, ksem, vsem):
    slot = _paged_step(tab_ref, layer, PP, (k_hbm, v_hbm), (kbuf, vbuf), (ksem, vsem))
    ps = pl.program_id(1)
    rows = PP * PAGE_SIZE // CMP_STRIDE // 4
    r0 = pl.multiple_of(ps * rows, rows)
    for buf, w_ref, f_ref, s_ref in ((kbuf, wk_ref, fk_ref, sk_ref), (vbuf, wv_ref, fv_ref, sv_ref)):
        pages = _side_by_side(buf, slot)
        a1 = (pages * w_ref[0]).astype(bf16)
        a2 = (pages * w_ref[1]).astype(bf16)
        first = _nt(ea_ref[...], a1)
        second = _nt(eb_ref[...], a2)
        for i in range(4):
            f_ref[0, i, pl.ds(r0, rows), :] = first[i * rows:(i + 1) * rows]
            s_ref[0, i, pl.ds(r0, rows), :] = second[i * rows:(i + 1) * rows]

        @pl.when(ps > 0)
        def _():
            s_ref[0, 3, pl.ds(r0 - 1, 1), :] = second[4 * rows:4 * rows + 1]


def _compress_t(table, layer, cache_k, cache_v, wk, wv, nb):
    npg = table.shape[0] // nb
    ea, eb = _chunk_maps()
    out = jax.ShapeDtypeStruct((nb, 4, N_CMP // 4, KV_W), f32)
    ospec = lambda: pl.BlockSpec((1, 4, N_CMP // 4, KV_W), lambda b, p, tab: (b, 0, 0, 0))
    full = lambda a: pl.BlockSpec(a.shape, lambda b, p, tab: (0,) * a.ndim)
    hbm = lambda: pl.BlockSpec(memory_space=pl.ANY)
    wk, wv = jnp.tile(wk, (1, 1, PP)), jnp.tile(wv, (1, 1, PP))
    return pl.pallas_call(
        functools.partial(_compress_t_kernel, layer),
        out_shape=(out,) * 4,
        grid_spec=pltpu.PrefetchScalarGridSpec(
            num_scalar_prefetch=1,
            grid=(nb, npg // PP),
            in_specs=[hbm(), hbm(), full(wk), full(wv), full(ea), full(eb)],
            out_specs=(ospec(), ospec(), ospec(), ospec()),
            scratch_shapes=[pltpu.VMEM((PAGE_SLOTS, PP, KV_W, PAGE_SIZE), f32),
                            pltpu.VMEM((PAGE_SLOTS, PP, KV_W, PAGE_SIZE), f32),
                            pltpu.SemaphoreType.DMA((PAGE_SLOTS,)), pltpu.SemaphoreType.DMA((PAGE_SLOTS,))]),
        compiler_params=_params(("arbitrary", "arbitrary")),
        name="compress_t",
    )(table, cache_k, cache_v, wk, wv, ea, eb)


def _finish_compress(f_ref, s_ref, phi_ref):
    blk = (f_ref[0] + s_ref[0]).reshape(N_CMP, KV_W)
    return _dot(blk.astype(bf16), phi_ref[...]).astype(bf16)


def _cmp_end(shape, axis):
    col = lax.broadcasted_iota(jnp.int32, shape, axis)
    n = ((col & (N_BLK - 1)) << 2) + (col >> 7)
    return n * CMP_STRIDE + (CMP_BLK - 1)


def _pick_blocks(score, n_pick):
    lane = lax.broadcasted_iota(jnp.int32, score.shape, 1).astype(f32)
    bias = jnp.full(score.shape, NEG, f32)
    for _ in range(n_pick):
        m = jnp.max(score, axis=-1, keepdims=True)
        first = jnp.min(jnp.where(score == m, lane, float(N_BLK)), axis=-1, keepdims=True)
        hit = lane == first
        bias = jnp.where(hit, 0.0, bias)
        score = jnp.where(hit, -jnp.inf, score)
    return bias


def _block_expand(first_block, n_keys):
    j = lax.broadcasted_iota(jnp.int32, (N_BLK, n_keys), 0)
    c = lax.broadcasted_iota(jnp.int32, (N_BLK, n_keys), 1)
    return jnp.where(j == first_block + (c >> 6), 1.0, 0.0).astype(bf16)


CQ = Q_PER_KV * TQ
WIN_T = WINDOW // TQ + 1
N_PARTS = 9
TILE_COL = HEAD_DIM + 6
V_ROWS = HEAD_DIM + 16


def _pick_blocks_t(score, forced, n_pick):
    jrow = lax.broadcasted_iota(jnp.int32, score.shape, 0).astype(f32)
    bias = jnp.where(forced, 0.0, NEG)
    score = jnp.where(forced, -jnp.inf, score)
    for _ in range(n_pick):
        m = jnp.max(score, axis=0, keepdims=True)
        first = jnp.min(jnp.where(score == m, jrow, float(N_BLK)), axis=0, keepdims=True)
        hit = jrow == first
        bias = jnp.where(hit, 0.0, bias)
        score = jnp.where(hit, -jnp.inf, score)
    return bias


def _nsa_prompt_kernel(q_ref, srow_ref, slope_ref, bg_ref, ng_ref, fk_ref, sk_ref, fv_ref, sv_ref,
                       phik_ref, phivt_ref, cpos_ref, ksa_ref, vst_ref, kwa_ref, vwt_ref, o_ref,
                       kca_scr, vct_scr, qa_scr, oc_scr, m_scr, acc_scr, negc_scr, negd_scr, negw_scr):
    qi = pl.program_id(1)
    s0 = qi * TQ

    @pl.when(qi == 0)
    def _():
        kc = _finish_compress(fk_ref, sk_ref, phik_ref)
        for g in range(KV_HEADS):
            kca_scr[g] = jnp.concatenate([kc[:, g * HEAD_DIM:(g + 1) * HEAD_DIM], cpos_ref[...]], axis=1)
        blk_v = (fv_ref[0] + sv_ref[0]).reshape(N_CMP, KV_W).astype(bf16)
        vct_scr[...] = _nt(phivt_ref[...], blk_v).astype(bf16)

    t_of = lambda shape: s0 + (lax.broadcasted_iota(jnp.int32, shape, 1) & (TQ - 1))
    q_t = (q_ref[0].astype(f32) * Q_SCALE).T.astype(bf16)

    negc_scr[...] = jnp.where(_cmp_end((N_CMP, CQ), 0) <= t_of((N_CMP, CQ)), 0.0, NEG)
    sees_block = t_of((1, CQ)) >= CMP_BLK - 1
    t_q = t_of((N_BLK, TQ))
    jr = lax.broadcasted_iota(jnp.int32, (N_BLK, TQ), 0)
    jb = t_q >> 6
    ok_b = (jr << 6) <= t_q
    forced = (jr == 0) | (jr == jb) | (jr == jb - 1)
    tile_any = [jnp.full((N_BLK // 8, 1), NEG, f32) for _ in LOOP_GROUPS]
    for g in range(KV_HEADS):
        q_g = jnp.concatenate([q_t[(g * Q_PER_KV + r) * HEAD_DIM:(g * Q_PER_KV + r + 1) * HEAD_DIM]
                               for r in range(Q_PER_KV)], axis=1)
        qa = jnp.concatenate([q_g, srow_ref[g]], axis=0)
        st = _dot(kca_scr[g], qa) + negc_scr[...]
        e = jnp.exp2(st - jnp.max(st, axis=0, keepdims=True))
        pt = e * jnp.where(sees_block, 1.0 / jnp.sum(e, axis=0, keepdims=True), 0.0)
        oc_scr[g] = _dot(vct_scr[g * HEAD_DIM:(g + 1) * HEAD_DIM], pt.astype(bf16))
        ps = None
        for i in range(4):
            for r in range(Q_PER_KV):
                slab = pt[i * N_BLK:(i + 1) * N_BLK, r * TQ:(r + 1) * TQ]
                ps = slab if ps is None else ps + slab
        bias = _pick_blocks_t(jnp.where(ok_b, ps, NEG), forced, N_SEL - 3)
        qa_scr[g] = jnp.concatenate([qa, jnp.concatenate([bias.astype(bf16)] * Q_PER_KV, axis=1)], axis=0)
        any_t = jnp.max(bias.reshape(N_BLK // 8, 8, TQ), axis=1)
        li = [g in grp for grp in LOOP_GROUPS].index(True)
        tile_any[li] = jnp.maximum(tile_any[li], jnp.max(any_t, axis=1, keepdims=True))
    kt_row = lax.broadcasted_iota(jnp.int32, tile_any[0].shape, 0)
    tile_bits = [jnp.sum(jnp.where(ta == 0.0, 1 << kt_row, 0)) for ta in tile_any]

    m_scr[...] = jnp.full(m_scr.shape, NEG, f32)
    acc_scr[...] = jnp.zeros_like(acc_scr)

    def sel_tile(kt, diagonal, groups, n_keys=TK):
        k0 = pl.multiple_of(kt * TK, TK)
        tile_off = (k0 - s0).astype(f32)
        key_r = lax.broadcasted_iota(jnp.int32, (n_keys, N_BLK), 0)
        blk_c = lax.broadcasted_iota(jnp.int32, (n_keys, N_BLK), 1)
        expand = jnp.where(blk_c == kt * (TK // SEL_BLK) + (key_r >> 6), 1.0, 0.0).astype(bf16)
        k_aug = ksa_ref[0, pl.ds(k0, n_keys), :]
        v_t = vst_ref[0, kt][:, 0:n_keys]
        for g in groups:
            lhs = jnp.concatenate([k_aug[:, g * QA_W:(g + 1) * QA_W], expand], axis=1)
            st = _dot(lhs, qa_scr[g])
            if diagonal:
                st = st + negd_scr[qi % (TK // TQ), 0:n_keys]
            c = slope_ref[g] * tile_off
            m_old = m_scr[g]
            m_new = jnp.maximum(m_old, jnp.max(st, axis=0, keepdims=True) + c)
            alpha = jnp.exp2(m_old - m_new)
            pt = jnp.exp2((st - (m_new - c)).astype(bf16))
            acc_scr[g] = alpha * acc_scr[g] + _dot(v_t[g * V_ROWS:(g + 1) * V_ROWS], pt)
            m_scr[g] = m_new

    last = s0 // TK

    @pl.when(qi < TK // TQ)
    def _():
        key_pos = lax.broadcasted_iota(jnp.int32, (TK, CQ), 0)
        negd_scr[qi] = jnp.where(key_pos <= t_of((TK, CQ)), 0.0, NEG)

    for groups, bits in zip(LOOP_GROUPS, tile_bits):
        def sel_body(kt, carry, groups=groups, bits=bits):
            @pl.when(((bits >> kt) & 1) == 1)
            def _():
                sel_tile(kt, False, groups)
            return carry

        lax.fori_loop(0, last, sel_body, 0)
    for sub in range(TK // TQ):
        @pl.when(qi % (TK // TQ) == sub)
        def _(sub=sub):
            sel_tile(last, True, range(KV_HEADS), (sub + 1) * TQ)

    w_tile = jnp.maximum(qi - WINDOW // TQ, 0)
    n_win = WIN_T * TQ

    @pl.when(qi <= WINDOW // TQ)
    def _():
        dist_w = t_of((n_win, CQ)) - (w_tile * TQ + lax.broadcasted_iota(jnp.int32, (n_win, CQ), 0))
        negw_scr[...] = jnp.where((dist_w >= 0) & (dist_w < WINDOW), 0.0, NEG)

    lane_w = lax.broadcasted_iota(jnp.int32, (n_win, QA_W), 1)
    slab_off = (lax.broadcasted_iota(jnp.int32, (n_win, QA_W), 0) // TQ * TQ).astype(f32).astype(bf16)
    in_tile_col = (lane_w >= TILE_COL) & (lane_w < TILE_COL + 3)
    kw_all = kwa_ref[0, pl.ds(pl.multiple_of(w_tile * TQ, TQ), n_win), :]
    vw_t = jnp.concatenate([vwt_ref[0, w_tile + i] for i in range(WIN_T)], axis=1)
    gates = jax.nn.sigmoid(bg_ref[0].T)
    blocks = []
    for g in range(KV_HEADS):
        lhs = jnp.where(in_tile_col, slab_off, kw_all[:, g * QA_W:(g + 1) * QA_W])
        st = _dot(lhs, qa_scr[g, 0:QA_W]) + negw_scr[...]
        pt = jnp.exp2((st - jnp.max(st, axis=0, keepdims=True)).astype(bf16))
        win = _dot(vw_t[g * V_ROWS:(g + 1) * V_ROWS], pt)
        o_w = win[0:HEAD_DIM] * (1.0 / win[HEAD_DIM:HEAD_DIM + 1])
        o_s = acc_scr[g, 0:HEAD_DIM] * (1.0 / acc_scr[g, HEAD_DIM:HEAD_DIM + 1])
        o_c = oc_scr[g]
        heads = []
        for r in range(Q_PER_KV):
            h = g * Q_PER_KV + r
            cols = slice(r * TQ, (r + 1) * TQ)
            heads.append(gates[3 * h:3 * h + 1] * o_c[:, cols] + gates[3 * h + 1:3 * h + 2] * o_s[:, cols]
                         + gates[3 * h + 2:3 * h + 3] * o_w[:, cols])
        for r in range(0, Q_PER_KV, 2):
            blocks.append(jnp.concatenate(heads[r:r + 2], axis=0).T)
    y = jnp.concatenate(blocks, axis=-1)
    o_ref[0] = (y * _silu(ng_ref[0])).astype(bf16)


def _nsa_prompt(z3, z16, fk, sk, fv, sv, phik, phivt):
    b, s, _ = z3.shape
    nq = s // TQ
    parts = _slope_parts().reshape(KV_HEADS, Q_PER_KV, POS_ROWS)
    srow = jnp.asarray(np.repeat(parts.transpose(0, 2, 1), TQ, axis=2), bf16)
    slope = jnp.asarray(np.repeat(np.asarray(SLOPES_LOG2, np.float32).reshape(KV_HEADS, 1, Q_PER_KV), TQ, axis=2))
    ksa = _key_aug(_kv_seg(z3, 2), TK)
    kwa = _key_aug(_kv_seg(z3, 4), TQ)
    vst = _value_tiles(_kv_seg(z3, 3), TK)
    vwt = _value_tiles(_kv_seg(z3, 5), TQ)
    cpos = _cmp_pos_cols()

    once = pl.Buffered(1)
    cmp = lambda: pl.BlockSpec((1, 4, N_CMP // 4, KV_W), lambda i, t: (i, 0, 0, 0))
    full = lambda a: pl.BlockSpec(a.shape, lambda i, t: (0,) * a.ndim)
    return pl.pallas_call(
        _nsa_prompt_kernel,
        out_shape=jax.ShapeDtypeStruct((b, s, BRANCH_W), bf16),
        grid=(b, nq),
        in_specs=[pl.BlockSpec((1, TQ, BRANCH_W), lambda i, t: (i, t, C_Q // BRANCH_W)),
                  full(srow), full(slope),
                  pl.BlockSpec((1, TQ, BG_W), lambda i, t: (i, t, C_BG // BG_W)),
                  pl.BlockSpec((1, TQ, BRANCH_W), lambda i, t: (i, t, C_NSA_G // BRANCH_W)),
                  cmp(), cmp(), cmp(), cmp(), full(phik), full(phivt), full(cpos),
                  pl.BlockSpec((1, s, KV_HEADS * QA_W), lambda i, t: (i, 0, 0), pipeline_mode=once),
                  pl.BlockSpec((1, s // TK, KV_HEADS * V_ROWS, TK), lambda i, t: (i, 0, 0, 0), pipeline_mode=once),
                  pl.BlockSpec((1, s, KV_HEADS * QA_W), lambda i, t: (i, 0, 0), pipeline_mode=once),
                  pl.BlockSpec((1, nq, KV_HEADS * V_ROWS, TQ), lambda i, t: (i, 0, 0, 0), pipeline_mode=once)],
        out_specs=pl.BlockSpec((1, TQ, BRANCH_W), lambda i, t: (i, t, 0)),
        scratch_shapes=[pltpu.VMEM((KV_HEADS, N_CMP, QA_W), bf16), pltpu.VMEM((KV_W, N_CMP), bf16),
                        pltpu.VMEM((KV_HEADS, QA_W + N_BLK, CQ), bf16),
                        pltpu.VMEM((KV_HEADS, HEAD_DIM, CQ), f32),
                        pltpu.VMEM((KV_HEADS, 1, CQ), f32),
                        pltpu.VMEM((KV_HEADS, V_ROWS, CQ), f32),
                        pltpu.VMEM((N_CMP, CQ), f32), pltpu.VMEM((TK // TQ, TK, CQ), f32),
                        pltpu.VMEM((WIN_T * TQ, CQ), f32)],
        compiler_params=_params(("parallel", "arbitrary")),
        name="nsa_prompt",
    )(z16, srow, slope, z3, z16, fk, sk, fv, sv, phik, phivt, cpos, ksa, vst, kwa, vwt)


def _by_group(fn):
    hg = lax.broadcasted_iota(jnp.int32, (N_HEADS, 1), 0) >> 2
    out = fn(0)
    for g in range(1, KV_HEADS):
        out = jnp.where(hg == g, fn(g), out)
    return out


def _nsa_sample_kernel(past_len, layer, tab_ref, q_ref, bg_ref, ng_ref, slope_ref,
                       fk_ref, sk_ref, fv_ref, sv_ref, phik_ref, phiv_ref,
                       ksn_ref, vsn_ref, kwn_ref, vwn_ref, bk_ref, bv_ref, kp_hbm, vp_hbm, o_ref,
                       q_scr, sel_scr, oc_scr, ow_scr, m_scr, l_scr, acc_scr, kbuf, vbuf, ksem, vsem):
    slot = _paged_step(tab_ref, layer, PP_SEL, (kp_hbm, vp_hbm), (kbuf, vbuf), (ksem, vsem))
    p = pl.program_id(1)
    slope = slope_ref[:, 0:1]
    gsl = lambda g: slice(g * HEAD_DIM, (g + 1) * HEAD_DIM)

    def per_head(row_ref):
        row = row_ref[0]
        return _by_group(lambda g: jnp.broadcast_to(row[:, gsl(g)], (N_HEADS, HEAD_DIM)))

    @pl.when(p == 0)
    def _():
        qrow = q_ref[0]
        q16 = jnp.concatenate([qrow[:, h * HEAD_DIM:(h + 1) * HEAD_DIM] for h in range(N_HEADS)], axis=0)
        q_scr[...] = q16
        q16f = q16.astype(f32)
        kc = _finish_compress(fk_ref, sk_ref, phik_ref)
        vc = _finish_compress(fv_ref, sv_ref, phiv_ref)

        dist_c = past_len - _cmp_end((1, N_CMP), 1)
        ok_c = dist_c >= 0
        s = _by_group(lambda g: _nt(q16, kc[:, gsl(g)])) - slope * dist_c.astype(f32)
        s = jnp.where(ok_c, s, NEG)
        e = jnp.exp(s - jnp.max(s, axis=-1, keepdims=True))
        pc = jnp.where(ok_c, e * (1.0 / jnp.sum(e, axis=-1, keepdims=True)), 0.0)
        pcb = pc.astype(bf16)
        oc_scr[...] = _by_group(lambda g: _dot(pcb, vc[:, gsl(g)]))
        ps16 = (pc[:, 0:N_BLK] + pc[:, N_BLK:2 * N_BLK]) + (pc[:, 2 * N_BLK:3 * N_BLK] + pc[:, 3 * N_BLK:])
        ps = jnp.concatenate([jnp.sum(ps16[g * Q_PER_KV:(g + 1) * Q_PER_KV], axis=0, keepdims=True)
                              for g in range(KV_HEADS)] + [jnp.zeros((8 - KV_HEADS, N_BLK), f32)], axis=0)
        jl = lax.broadcasted_iota(jnp.int32, (8, N_BLK), 1)
        jb = past_len // SEL_BLK
        forced = jnp.where((jl == 0) | (jl == jb) | (jl == jb - 1), FORCE, 0.0)
        bias8 = _pick_blocks(ps + forced, N_SEL - 1)
        sel_scr[...] = jnp.concatenate(
            [jnp.broadcast_to(bias8[g:g + 1], (Q_PER_KV, N_BLK)) for g in range(KV_HEADS)], axis=0).astype(bf16)

        wb = bk_ref.shape[-1]
        dist_w = wb - lax.broadcasted_iota(jnp.int32, (1, wb), 1)
        ok_w = (dist_w >= 0) & (dist_w < WINDOW)
        bk = bk_ref[0, 0].astype(bf16)
        bv = bv_ref[0, 0].astype(bf16)
        s_buf = _by_group(lambda g: _dot(q16, bk[gsl(g)])) - slope * dist_w.astype(f32)
        s_buf = jnp.where(ok_w, s_buf, NEG)
        s_new = jnp.sum(q16f * per_head(kwn_ref), axis=-1, keepdims=True)
        m_w = jnp.maximum(jnp.max(s_buf, axis=-1, keepdims=True), s_new)
        e_buf = jnp.exp(s_buf - m_w)
        e_new = jnp.exp(s_new - m_w)
        ebb = e_buf.astype(bf16)
        num = _by_group(lambda g: _nt(ebb, bv[gsl(g)])) + e_new * per_head(vwn_ref)
        ow_scr[...] = num * (1.0 / (jnp.sum(e_buf, axis=-1, keepdims=True) + e_new))

        m_scr[...] = jnp.sum(q16f * per_head(ksn_ref), axis=-1, keepdims=True)
        l_scr[...] = jnp.ones_like(l_scr)
        acc_scr[...] = per_head(vsn_ref)

    q16 = q_scr[...]
    n_keys = PP_SEL * PAGE_SIZE
    kp = _side_by_side(kbuf, slot).astype(bf16)
    vp = _side_by_side(vbuf, slot).astype(bf16)
    dist = past_len - (p * n_keys + lax.broadcasted_iota(jnp.int32, (1, n_keys), 1))
    expand = _block_expand(p * (n_keys // SEL_BLK), n_keys)
    s = _by_group(lambda g: _dot(q16, kp[gsl(g)])) - slope * dist.astype(f32) + _dot(sel_scr[...], expand)
    m_old = m_scr[...]
    m_new = jnp.maximum(m_old, jnp.max(s, axis=-1, keepdims=True))
    alpha = jnp.exp(m_old - m_new)
    pr = jnp.exp(s - m_new)
    prb = pr.astype(bf16)
    l_scr[...] = alpha * l_scr[...] + jnp.sum(pr, axis=-1, keepdims=True)
    acc_scr[...] = alpha * acc_scr[...] + _by_group(lambda g: _nt(prb, vp[gsl(g)]))
    m_scr[...] = m_new

    @pl.when(p == pl.num_programs(1) - 1)
    def _():
        o_s = acc_scr[...] * (1.0 / l_scr[...])
        gates = jax.nn.sigmoid(bg_ref[0])
        lane = lax.broadcasted_iota(jnp.int32, (N_HEADS, BG_W), 1)
        h3 = 3 * lax.broadcasted_iota(jnp.int32, (N_HEADS, BG_W), 0)
        gate = lambda n: jnp.sum(jnp.where(lane == h3 + n, gates, 0.0), axis=-1, keepdims=True)
        y16 = gate(0) * oc_scr[...] + gate(1) * o_s + gate(2) * ow_scr[...]
        y = jnp.concatenate([y16[h:h + 1, :] for h in range(N_HEADS)], axis=-1)
        o_ref[0] = (y * _silu(ng_ref[0])).astype(bf16)


def _nsa_sample(table, layer, qb, z2, z2h, slopes, fk, sk, fv, sv, phik, phiv, pool_k, pool_v, buf_k, buf_v,
                past_len):
    nb = qb.shape[0]
    npg = table.shape[0] // nb
    z3 = z2.reshape(nb, 1, Z32_W)
    z3h = z2h.reshape(nb, 1, Z16_W)
    tok = lambda w, c: pl.BlockSpec((1, 1, w), lambda b, p, tab: (b, 0, c))
    cmp = lambda: pl.BlockSpec((1, 4, N_CMP // 4, KV_W), lambda b, p, tab: (b, 0, 0, 0))
    phi = lambda: pl.BlockSpec((KV_W, KV_W), lambda b, p, tab: (0, 0))
    wb = buf_k.shape[-1]
    win = lambda: pl.BlockSpec((1, 1, KV_W, wb), lambda b, p, tab: (layer, b, 0, 0))
    hbm = lambda: pl.BlockSpec(memory_space=pl.ANY)
    page_buf = lambda: pltpu.VMEM((PAGE_SLOTS, PP_SEL, KV_W, PAGE_SIZE), f32)
    kvc = C_KV // KV_W
    return pl.pallas_call(
        functools.partial(_nsa_sample_kernel, past_len, layer),
        out_shape=jax.ShapeDtypeStruct((nb, 1, BRANCH_W), bf16),
        grid_spec=pltpu.PrefetchScalarGridSpec(
            num_scalar_prefetch=1,
            grid=(nb, npg // PP_SEL),
            in_specs=[tok(BRANCH_W, 0), tok(BG_W, C_BG // BG_W), tok(BRANCH_W, C_NSA_G // BRANCH_W),
                      pl.BlockSpec((N_HEADS, 128), lambda b, p, tab: (0, 0)),
                      cmp(), cmp(), cmp(), cmp(), phi(), phi(),
                      tok(KV_W, kvc + 2), tok(KV_W, kvc + 3), tok(KV_W, kvc + 4), tok(KV_W, kvc + 5),
                      win(), win(), hbm(), hbm()],
            out_specs=pl.BlockSpec((1, 1, BRANCH_W), lambda b, p, tab: (b, 0, 0)),
            scratch_shapes=[pltpu.VMEM((N_HEADS, HEAD_DIM), bf16), pltpu.VMEM((N_HEADS, N_BLK), bf16),
                            pltpu.VMEM((N_HEADS, HEAD_DIM), f32), pltpu.VMEM((N_HEADS, HEAD_DIM), f32),
                            pltpu.VMEM((N_HEADS, 1), f32), pltpu.VMEM((N_HEADS, 1), f32),
                            pltpu.VMEM((N_HEADS, HEAD_DIM), f32), page_buf(), page_buf(),
                            pltpu.SemaphoreType.DMA((PAGE_SLOTS,)), pltpu.SemaphoreType.DMA((PAGE_SLOTS,))]),
        compiler_params=_params(("arbitrary", "arbitrary")),
        name="nsa_sample",
    )(table, qb.reshape(nb, 1, BRANCH_W), z3, z3h, slopes, fk, sk, fv, sv, phik, phiv,
      z3, z3, z3, z3, buf_k, buf_v, pool_k, pool_v)


def _merge_kernel(zl_ref, zp_ref, zn_ref, m0_ref, m1_ref, m2_ref, wb_ref, wo_ref, g_ref, x_ref, y_ref):
    acc = None
    for n, (zz, mg) in enumerate(((zl_ref, m0_ref), (zp_ref, m1_ref), (zn_ref, m2_ref))):
        term = jax.nn.sigmoid(mg[...].astype(f32)) * _dot(zz[...], wb_ref[n])
        acc = term if acc is None else acc + term
    out = _dot(acc.astype(bf16), wo_ref[...])
    ms = jnp.mean(out * out, axis=-1, keepdims=True)
    y_ref[...] = x_ref[...] + out * lax.rsqrt(ms + EPS) * g_ref[...]


def _merge(zl, zp, zn, z16, wb, wo, g_row, x2d):
    n = x2d.shape[0]
    tm = min(n, 256)
    rowblk = lambda c: pl.BlockSpec((tm, D_MODEL), lambda i: (i, c))
    return pl.pallas_call(
        _merge_kernel,
        out_shape=jax.ShapeDtypeStruct((n, D_MODEL), f32),
        grid=(n // tm,),
        in_specs=[rowblk(0), rowblk(0), rowblk(0),
                  rowblk(C_MG // D_MODEL), rowblk(C_MG // D_MODEL + 1), rowblk(C_MG // D_MODEL + 2),
                  pl.BlockSpec((N_BRANCH, BRANCH_W, D_MODEL), lambda i: (0, 0, 0)),
                  pl.BlockSpec((D_MODEL, D_MODEL), lambda i: (0, 0)),
                  pl.BlockSpec((1, D_MODEL), lambda i: (0, 0)),
                  rowblk(0)],
        out_specs=rowblk(0),
        compiler_params=_params(("parallel",)),
        name="merge",
    )(zl, zp, zn, z16, z16, z16, wb, wo, g_row, x2d)


def _block_diag(w, per):
    n, d, _ = w.shape
    eye = jnp.eye(per, dtype=w.dtype)
    t = jnp.einsum('cpde,pq->cpdqe', w.reshape(n // per, per, d, d), eye)
    return t.reshape(n // per, per * d, per * d)


def _pack_w_in(w):
    seg = lambda i: w[:, i * BRANCH_W:(i + 1) * BRANCH_W]
    old_kv = 6 * BRANCH_W
    old_bg = old_kv + 6 * KV_W
    old_mg = old_bg + N_BRANCH * N_HEADS
    f32_part = jnp.concatenate([seg(0), seg(2), w[:, old_kv:old_bg], w[:, old_bg:old_mg]], axis=1)
    f32_part = jnp.pad(f32_part, ((0, 0), (0, Z32_W - f32_part.shape[1])))
    bf16_part = jnp.concatenate([seg(1), seg(3), seg(4), seg(5), w[:, old_mg:]], axis=1)
    return jnp.concatenate([f32_part, bf16_part], axis=1).astype(bf16)


def _tile_wpos(w_pos):
    halves = w_pos.reshape(2, CMP_STRIDE, HEAD_DIM)
    return jnp.tile(halves, (1, PAGE_SIZE // CMP_STRIDE, KV_HEADS))


def _tile_wpos_t(w_pos):
    halves = w_pos.reshape(2, CMP_STRIDE, HEAD_DIM).swapaxes(1, 2)
    return jnp.tile(halves, (1, KV_HEADS, PAGE_SIZE // CMP_STRIDE))


def _lanes_last(cache):
    d, n, rows = cache.shape[:3]
    return jnp.transpose(cache, (0, 1, 3, 4, 2)).reshape(d, n, KV_W, rows)


def _slope_parts():
    cols = np.zeros((N_HEADS, POS_ROWS), np.float32)
    rnd = lambda v: np.float32(np.float32(v).astype(bf16))
    for h, s in enumerate(SLOPES_LOG2):
        s1 = rnd(s)
        s2 = rnd(np.float32(s) - s1)
        s3 = rnd(np.float32(s) - s1 - s2)
        cols[h, 0:N_PARTS] = [s1, s2, s3] * 3
    return cols


def _split_pos(pos, shift):
    cols = np.zeros((pos.shape[0], POS_ROWS), np.float32)
    cols[:, 0:3] = ((pos >> shift) << shift)[:, None]
    cols[:, 3:6] = (pos & ((1 << shift) - 1))[:, None]
    return cols


def _key_aug(k_rows, tile):
    b, s, _ = k_rows.shape
    pos = jnp.asarray(_split_pos(np.arange(s) % tile, 4), bf16)
    kg = k_rows.astype(bf16).reshape(b, s, KV_HEADS, HEAD_DIM)
    posb = jnp.broadcast_to(pos[None, :, None, :], (b, s, KV_HEADS, POS_ROWS))
    return jnp.concatenate([kg, posb], axis=-1).reshape(b, s, KV_HEADS * QA_W)


def _value_tiles(v_rows, tile):
    b, s, _ = v_rows.shape
    vt = v_rows.astype(bf16).reshape(b, s // tile, tile, KV_HEADS, HEAD_DIM).transpose(0, 1, 3, 4, 2)
    ones = jnp.ones((b, s // tile, KV_HEADS, V_ROWS - HEAD_DIM, tile), bf16)
    return jnp.concatenate([vt, ones], axis=3).reshape(b, s // tile, KV_HEADS * V_ROWS, tile)


def _cmp_pos_cols():
    slot = np.arange(N_CMP)
    n = ((slot & (N_BLK - 1)) << 2) + (slot >> 7)
    return jnp.asarray(_split_pos(n * CMP_STRIDE + CMP_BLK - 1, 8), bf16)


def _kv_seg(z, i):
    return z[..., C_KV + i * KV_W:C_KV + (i + 1) * KV_W]


def kernel(x_prompt, x_sample, cache_cmp_k, cache_cmp_v, cache_sel_k, cache_sel_v, cache_win_k, cache_win_v, state_conv, state_lru, state_pool, page_table, g_pre, g_post, w_in, conv_w, conv_b, w_rg_a, b_rg_a, w_rg_x, b_rg_x, lru_lambda, w_pool, pool_scale, cmp_pos_k, cmp_phi_k, cmp_pos_v, cmp_phi_v, w_branch, w_out):
    depth = w_in.shape[0]
    bp, seq, _ = x_prompt.shape
    bs = x_sample.shape[0]
    n_pages = page_table.shape[1]
    past_len = n_pages * PAGE_SIZE
    n_phys = cache_cmp_k.shape[1]
    assert seq == N_BLK * SEL_BLK and past_len == N_BLK * SEL_BLK and x_sample.shape[1] == 1
    wb = cache_win_k.shape[2]

    table_s = page_table.reshape(-1).astype(jnp.int32)
    table_p = jnp.arange(bp * (seq // PAGE_SIZE), dtype=jnp.int32)
    slopes = jnp.broadcast_to(jnp.asarray(SLOPES, f32)[:, None], (N_HEADS, 128))
    row = lambda v: v.reshape(1, -1)
    cmp_kt, cmp_vt = _lanes_last(cache_cmp_k), _lanes_last(cache_cmp_v)
    sel_kt, sel_vt = _lanes_last(cache_sel_k), _lanes_last(cache_sel_v)
    win_kt, win_vt = _lanes_last(cache_win_k), _lanes_last(cache_win_v)

    xp = x_prompt.reshape(bp * seq, D_MODEL)
    xs = x_sample.reshape(bs, D_MODEL)
    pr = [[] for _ in range(9)]
    sm = [[] for _ in range(9)]
    for l in range(depth):
        w_packed = _pack_w_in(w_in[l])
        wa = _block_diag(w_rg_a[l], MXU_W // LRU_BD).astype(bf16)
        wx = _block_diag(w_rg_x[l], MXU_W // LRU_BD).astype(bf16)
        wp = w_pool[l].astype(bf16)
        phik = _block_diag(jnp.broadcast_to(cmp_phi_k[l], (KV_HEADS, HEAD_DIM, HEAD_DIM)), KV_HEADS)[0].astype(bf16)
        phiv = _block_diag(jnp.broadcast_to(cmp_phi_v[l], (KV_HEADS, HEAD_DIM, HEAD_DIM)), KV_HEADS)[0].astype(bf16)
        wpos_k = _tile_wpos(cmp_pos_k[l])
        wpos_v = _tile_wpos(cmp_pos_v[l])
        wbr = w_branch[l].astype(bf16)
        wo = w_out[l].astype(bf16)
        lru_w = (conv_w[l], row(conv_b[l]), wa, row(b_rg_a[l]), wx, row(b_rg_x[l]), row(lru_lambda[l]))

        z, z16 = _inproj(xp, row(g_pre[l]), w_packed)
        z3 = z.reshape(bp, seq, Z32_W)
        z16_3 = z16.reshape(bp, seq, Z16_W)
        zl, conv_tail, h_p = _lru_prompt(z3, z16_3, *lru_w)
        zpool, pool_tail = _pool_prompt(z3, z16_3, wp, row(pool_scale[l]))
        zr = z.reshape(bp * seq // PAGE_SIZE, PAGE_SIZE, Z32_W)
        fk, sk, fv, sv = _compress(table_p, zr, zr, C_KV // KV_W, C_KV // KV_W + 1, wpos_k, wpos_v, bp)
        zn = _nsa_prompt(z3, z16_3, fk, sk, fv, sv, phik, phiv.T)
        xp = _merge(zl.reshape(bp * seq, BRANCH_W), zpool.reshape(bp * seq, BRANCH_W),
                    zn.reshape(bp * seq, BRANCH_W), z16, wbr, wo, row(g_post[l]), xp)
        kv_rows = [_kv_seg(z3, i).reshape(bp, seq, KV_HEADS, HEAD_DIM) for i in range(6)]
        wlen = min(WINDOW, seq)
        st_p = kv_rows[:4] + [kv_rows[4][:, -wlen:], kv_rows[5][:, -wlen:],
                              conv_tail[:, -(CONV_W - 1):], h_p[:, 0], pool_tail[:, -POOL_BUF:]]

        zs, zs16 = _inproj(xs, row(g_pre[l]), w_packed)
        zls, zps, h_s = _mix_sample(zs, zs16, state_conv[l].swapaxes(0, 1), state_lru[l],
                                    state_pool[l].swapaxes(0, 1), *lru_w, wp, row(pool_scale[l]), past_len)
        fk, sk, fv, sv = _compress_t(table_s, l, cmp_kt, cmp_vt, _tile_wpos_t(cmp_pos_k[l]),
                                     _tile_wpos_t(cmp_pos_v[l]), bs)
        qs = (zs16[:, C_Q:C_Q + BRANCH_W].astype(f32) * (HEAD_DIM ** -0.5)).astype(bf16)
        zns = _nsa_sample(table_s, l, qs, zs, zs16, slopes, fk, sk, fv, sv, phik, phiv,
                          sel_kt, sel_vt, win_kt, win_vt, past_len)
        xs = _merge(zls, zps, zns.reshape(bs, BRANCH_W), zs16, wbr, wo, row(g_post[l]), xs)
        kv_new = [_kv_seg(zs, i).reshape(bs, 1, KV_HEADS, HEAD_DIM) for i in range(6)]
        st_s = kv_new[:4] + [jnp.concatenate([cache_win_k[l], kv_new[4]], axis=1)[:, -wb:],
                             jnp.concatenate([cache_win_v[l], kv_new[5]], axis=1)[:, -wb:],
                             jnp.concatenate([state_conv[l], zs[:, None, C_LRU_X:C_LRU_X + BRANCH_W]], axis=1)[:, -(CONV_W - 1):],
                             h_s,
                             jnp.concatenate([state_pool[l], zs[:, None, C_POOL_X:C_POOL_X + BRANCH_W]], axis=1)[:, -POOL_BUF:]]
        for i in range(9):
            pr[i].append(st_p[i])
            sm[i].append(st_s[i])

    out = [xp.reshape(bp, seq, D_MODEL), xs.reshape(bs, 1, D_MODEL)]
    for i in range(9):
        out += [jnp.stack(pr[i]), jnp.stack(sm[i])]
    return tuple(out)
```
